```python
import math
import jax, jax.numpy as jnp
from jax import lax
import numpy as np

D_MODEL = 1024
BATCH = 8
SEQ = 2048
DEPTH = 2
DEC_BATCH = 128
DEC_SEQ = 8
PAST_LEN = 16384
PAGE_SIZE = 128

ML_HEADS = 6
ML_DK = 64
ML_DV = 64
ML_W = ML_HEADS * ML_DV
SSD_HEADS = 6
SSD_HEADDIM = 64
SSD_W = SSD_HEADS * SSD_HEADDIM
SSD_GROUPS = 2
SSD_DSTATE = 128
SSD_CONV = 4
SSD_CONV_DIM = SSD_W + 2 * SSD_GROUPS * SSD_DSTATE
HG_HEADS = 4
HG_DK = 64
HG_DV = 64
HG_W = HG_HEADS * HG_DV
MIX_W = ML_W + SSD_W + HG_W
D_FF = 2816
FFN_CONV = 3
CHUNK = 128
HG_CHUNK = 64
EPS = 1e-6
IN_SPLITS = (ML_W, ML_W, ML_W, ML_W, ML_HEADS, ML_HEADS,
             SSD_W, SSD_CONV_DIM, SSD_HEADS,
             HG_W, HG_W, HG_W, HG_W)
IN_DIM = sum(IN_SPLITS)

kernel_name = "hymba_mlstm_ssd_hgrn2_convffn_step"


def _rmsnorm(x, w):
    xf = x.astype(jnp.float32)
    y = xf * lax.rsqrt(jnp.mean(xf * xf, axis=-1, keepdims=True) + EPS)
    return (y * w.astype(jnp.float32)).astype(x.dtype)


def _head_rmsnorm(x, w):
    y = x * lax.rsqrt(jnp.mean(x * x, axis=-1, keepdims=True) + EPS)
    B, T = x.shape[:2]
    return y.reshape(B, T, -1) * w.astype(jnp.float32)


def _causal_dwconv(x, buf, w, b):
    K = w.shape[0]
    T = x.shape[1]
    xp = jnp.concatenate([buf.astype(x.dtype), x], axis=1)
    y = b.astype(jnp.float32)
    for j in range(K):
        y = y + xp[:, j:j + T] * w[j].astype(jnp.float32)
    return y, xp[:, T:]


def _chunks(a, L):
    B, T = a.shape[:2]
    return jnp.swapaxes(a.reshape((B, T // L, L) + a.shape[2:]), 0, 1)


def _unchunk(a):
    NC, B, L = a.shape[:3]
    return jnp.swapaxes(a, 0, 1).reshape((B, NC * L) + a.shape[3:])


def _mlstm(q, k, v, ig, lf, C0, n0, m0):
    T = q.shape[1]
    L = math.gcd(T, CHUNK)
    causal = jnp.tril(jnp.ones((L, L), dtype=bool))

    def step(carry, inp):
        C, n, m = carry
        qc, kc, vc, ic, fc = inp
        b = jnp.cumsum(fc, axis=1)
        Dm = b[:, :, None, :] - b[:, None, :, :] + ic[:, None, :, :]
        Dm = jnp.where(causal[None, :, :, None], Dm, -jnp.inf)
        inter = b + m[:, None, :]
        m_t = jnp.maximum(inter, jnp.max(Dm, axis=2))
        W = jnp.exp(Dm - m_t[:, :, None, :])
        a_int = jnp.exp(inter - m_t)
        s = jnp.einsum('bthd,bshd->btsh', qc, kc) * W
        num = jnp.einsum('btsh,bshv->bthv', s, vc) + a_int[..., None] * jnp.einsum('bthd,bhdv->bthv', qc, C)
        den = jnp.sum(s, axis=2) + a_int * jnp.einsum('bthd,bhd->bth', qc, n)
        h = num / jnp.maximum(jnp.abs(den), jnp.exp(-m_t))[..., None]
        m_new = m_t[:, -1]
        a_st = jnp.exp(b[:, -1] + m - m_new)
        w_s = jnp.exp(b[:, -1:] - b + ic - m_new[:, None])
        C_new = a_st[..., None, None] * C + jnp.einsum('bsh,bshd,bshv->bhdv', w_s, kc, vc)
        n_new = a_st[..., None] * n + jnp.einsum('bsh,bshd->bhd', w_s, kc)
        return (C_new, n_new, m_new), h

    xs = (_chunks(q, L), _chunks(k, L), _chunks(v, L), _chunks(ig, L), _chunks(lf, L))
    (C1, n1, m1), h = lax.scan(step, (C0, n0, m0), xs)
    return _unchunk(h), C1, n1, m1


def _ssd(xh, dt, A, Bh, Ch, h0):
    T = xh.shape[1]
    L = math.gcd(T, CHUNK)
    causal = jnp.tril(jnp.ones((L, L), dtype=bool))

    def step(h, inp):
        xc, dtc, Bc, Cc = inp
        cum = jnp.cumsum(dtc * A, axis=1)
        seg = cum[:, :, None, :] - cum[:, None, :, :]
        Lm = jnp.exp(jnp.where(causal[None, :, :, None], seg, -jnp.inf))
        cb = jnp.einsum('bthn,bshn->btsh', Cc, Bc) * Lm * dtc[:, None, :, :]
        y = jnp.einsum('btsh,bshp->bthp', cb, xc) + jnp.exp(cum)[..., None] * jnp.einsum('bthn,bhpn->bthp', Cc, h)
        w_s = jnp.exp(cum[:, -1:] - cum) * dtc
        h_new = jnp.exp(cum[:, -1])[..., None, None] * h + jnp.einsum('bsh,bshp,bshn->bhpn', w_s, xc, Bc)
        return h_new, y

    xs = (_chunks(xh, L), _chunks(dt, L), _chunks(Bh, L), _chunks(Ch, L))
    h1, y = lax.scan(step, h0, xs)
    return _unchunk(y), h1


def _hgrn2(q, lf, k, v, S0):
    T = q.shape[1]
    L = math.gcd(T, HG_CHUNK)
    causal = jnp.tril(jnp.ones((L, L), dtype=bool))

    def step(S, inp):
        qc, fc, kc, vc = inp
        G = jnp.cumsum(fc, axis=1)
        seg = G[:, :, None] - G[:, None, :]
        dec = jnp.exp(jnp.where(causal[None, :, :, None, None], seg, -jnp.inf))
        A = jnp.einsum('bthk,bshk,btshk->btsh', qc, kc, dec)
        o = jnp.einsum('btsh,bshv->bthv', A, vc) + jnp.einsum('bthk,bhkv->bthv', qc * jnp.exp(G), S)
        S_new = jnp.exp(G[:, -1])[..., None] * S + jnp.einsum('bshk,bshv->bhkv', kc * jnp.exp(G[:, -1:] - G), vc)
        return S_new, o

    xs = (_chunks(q, L), _chunks(lf, L), _chunks(k, L), _chunks(v, L))
    S1, o = lax.scan(step, S0, xs)
    return _unchunk(o), S1


def _layer(x, st, p, lb):
    C0, n0, m0, ssd0, sconv0, hg0, fconv0 = st
    f32 = jnp.float32
    B, T, _ = x.shape
    h = _rmsnorm(x, p['norm1'])
    proj = (h @ p['w_in']).astype(f32)
    offs = np.cumsum(IN_SPLITS)[:-1].tolist()
    (ml_q, ml_k, ml_v, ml_o, ml_i, ml_f, s_z, s_xbc, s_dt, g_q, g_f, g_i, g_g) = jnp.split(proj, offs, axis=-1)

    q = ml_q.reshape(B, T, ML_HEADS, ML_DK)
    k = ml_k.reshape(B, T, ML_HEADS, ML_DK) * (ML_DK ** -0.5)
    v = ml_v.reshape(B, T, ML_HEADS, ML_DV)
    ig = ml_i + p['ml_ig_b'].astype(f32)
    lf = jax.nn.log_sigmoid(ml_f + p['ml_fg_b'].astype(f32))
    hm, C1, n1, m1 = _mlstm(q, k, v, ig, lf, C0.astype(f32), n0.astype(f32), m0.astype(f32))
    out_ml = _head_rmsnorm(hm, p['ml_norm']) * jax.nn.sigmoid(ml_o)

    xbc, sconv1 = _causal_dwconv(s_xbc, sconv0.astype(f32), p['ssd_conv_w'], p['ssd_conv_b'])
    xbc = jax.nn.silu(xbc)
    sx, sB, sC = jnp.split(xbc, [SSD_W, SSD_W + SSD_GROUPS * SSD_DSTATE], axis=-1)
    rep = SSD_HEADS // SSD_GROUPS
    Bh = jnp.repeat(sB.reshape(B, T, SSD_GROUPS, SSD_DSTATE), rep, axis=2)
    Ch = jnp.repeat(sC.reshape(B, T, SSD_GROUPS, SSD_DSTATE), rep, axis=2)
    xh = sx.reshape(B, T, SSD_HEADS, SSD_HEADDIM)
    dt = jax.nn.softplus(s_dt + p['ssd_dt_bias'].astype(f32))
    A = -jnp.exp(p['ssd_A_log'].astype(f32))
    ys, ssd1 = _ssd(xh, dt, A, Bh, Ch, ssd0.astype(f32))
    ys = ys + p['ssd_D'].astype(f32)[:, None] * xh
    out_ssd = _head_rmsnorm(ys * jax.nn.silu(s_z).reshape(B, T, SSD_HEADS, SSD_HEADDIM), p['ssd_norm'])

    lb = lb.astype(f32)
    kg = (1.0 - lb) * jax.nn.sigmoid(-g_f)
    lfg = jnp.log(lb + (1.0 - lb) * jax.nn.sigmoid(g_f))
    qg = g_q * (HG_DK ** -0.5)
    og, hg1 = _hgrn2(qg.reshape(B, T, HG_HEADS, HG_DK), lfg.reshape(B, T, HG_HEADS, HG_DK),
                     kg.reshape(B, T, HG_HEADS, HG_DK), g_i.reshape(B, T, HG_HEADS, HG_DV), hg0.astype(f32))
    out_hg = _head_rmsnorm(og, p['hg_norm']) * jax.nn.silu(g_g)

    mix = jnp.concatenate([out_ml, out_ssd, out_hg], axis=-1).astype(x.dtype)
    x = x + mix @ p['w_out']

    h2 = _rmsnorm(x, p['norm2'])
    u = (h2 @ p['w_up']).astype(f32)
    u, fconv1 = _causal_dwconv(u, fconv0.astype(f32), p['ffn_conv_w'], p['ffn_conv_b'])
    g, val = jnp.split(u, [D_FF], axis=-1)
    x = x + (jax.nn.silu(g) * val).astype(x.dtype) @ p['w_down']
    return x, (C1, n1, m1, ssd1, sconv1, hg1, fconv1)


def setup_inputs(seed: int = 0) -> dict:
    key = jax.random.key(seed)
    ks = jax.random.split(key, 40)
    nrm = lambda i, shape, s=1.0: s * jax.random.normal(ks[i], shape, jnp.float32)
    dt0 = jnp.exp(jax.random.uniform(ks[20], (DEPTH, SSD_HEADS), jnp.float32, math.log(1e-3), math.log(1e-1)))
    return {
        'x_prompt': nrm(0, (BATCH, SEQ, D_MODEL)),
        'x_sample': nrm(1, (DEC_BATCH, DEC_SEQ, D_MODEL)),
        'state_mlstm_C': nrm(2, (DEPTH, DEC_BATCH, ML_HEADS, ML_DK, ML_DV), 0.3),
        'state_mlstm_n': nrm(3, (DEPTH, DEC_BATCH, ML_HEADS, ML_DK), 0.3),
        'state_mlstm_m': jax.random.uniform(ks[4], (DEPTH, DEC_BATCH, ML_HEADS), jnp.float32, 0.0, 4.0),
        'state_ssd': nrm(5, (DEPTH, DEC_BATCH, SSD_HEADS, SSD_HEADDIM, SSD_DSTATE), 0.3),
        'state_ssd_conv': nrm(6, (DEPTH, DEC_BATCH, SSD_CONV - 1, SSD_CONV_DIM)),
        'state_hgrn': nrm(7, (DEPTH, DEC_BATCH, HG_HEADS, HG_DK, HG_DV), 0.3),
        'state_ffn_conv': nrm(8, (DEPTH, DEC_BATCH, FFN_CONV - 1, 2 * D_FF)),
        'norm1_w': 1.0 + nrm(9, (DEPTH, D_MODEL), 0.02),
        'w_in': nrm(10, (DEPTH, D_MODEL, IN_DIM), D_MODEL ** -0.5),
        'ml_ig_b': -1.0 + nrm(11, (DEPTH, ML_HEADS), 0.1),
        'ml_fg_b': jnp.linspace(3.0, 6.0, ML_HEADS, dtype=jnp.float32) + nrm(12, (DEPTH, ML_HEADS), 0.1),
        'ml_norm_w': 1.0 + nrm(13, (DEPTH, ML_W), 0.02),
        'ssd_conv_w': nrm(14, (DEPTH, SSD_CONV, SSD_CONV_DIM), SSD_CONV ** -0.5),
        'ssd_conv_b': nrm(15, (DEPTH, SSD_CONV_DIM), 0.02),
        'ssd_dt_bias': dt0 + jnp.log(-jnp.expm1(-dt0)),
        'ssd_A_log': jnp.log(jax.random.uniform(ks[16], (DEPTH, SSD_HEADS), jnp.float32, 1.0, 16.0)),
        'ssd_D': 1.0 + nrm(17, (DEPTH, SSD_HEADS), 0.1),
        'ssd_norm_w': 1.0 + nrm(18, (DEPTH, SSD_W), 0.02),
        'hg_lower_bounds': nrm(19, (DEPTH, HG_W), 0.1),
        'hg_norm_w': 1.0 + nrm(21, (DEPTH, HG_W), 0.02),
        'w_out': nrm(22, (DEPTH, MIX_W, D_MODEL), MIX_W ** -0.5),
        'norm2_w': 1.0 + nrm(23, (DEPTH, D_MODEL), 0.02),
        'w_up': nrm(24, (DEPTH, D_MODEL, 2 * D_FF), D_MODEL ** -0.5),
        'ffn_conv_w': nrm(25, (DEPTH, FFN_CONV, 2 * D_FF), FFN_CONV ** -0.5),
        'ffn_conv_b': nrm(26, (DEPTH, 2 * D_FF), 0.02),
        'w_down': nrm(27, (DEPTH, D_FF, D_MODEL), D_FF ** -0.5),
        'final_norm_w': 1.0 + nrm(28, (D_MODEL,), 0.02),
    }


def reference(x_prompt, x_sample, state_mlstm_C, state_mlstm_n, state_mlstm_m, state_ssd, state_ssd_conv,
              state_hgrn, state_ffn_conv, norm1_w, w_in, ml_ig_b, ml_fg_b, ml_norm_w, ssd_conv_w, ssd_conv_b,
              ssd_dt_bias, ssd_A_log, ssd_D, ssd_norm_w, hg_lower_bounds, hg_norm_w, w_out, norm2_w, w_up,
              ffn_conv_w, ffn_conv_b, w_down, final_norm_w):
    f32 = jnp.float32
    lbs = jnp.cumsum(jax.nn.softmax(hg_lower_bounds.astype(f32), axis=0), axis=0)
    lbs = lbs - lbs[0]
    Bp = x_prompt.shape[0]
    zero_state = (jnp.zeros((Bp, ML_HEADS, ML_DK, ML_DV), f32), jnp.zeros((Bp, ML_HEADS, ML_DK), f32),
                  jnp.zeros((Bp, ML_HEADS), f32), jnp.zeros((Bp, SSD_HEADS, SSD_HEADDIM, SSD_DSTATE), f32),
                  jnp.zeros((Bp, SSD_CONV - 1, SSD_CONV_DIM), f32), jnp.zeros((Bp, HG_HEADS, HG_DK, HG_DV), f32),
                  jnp.zeros((Bp, FFN_CONV - 1, 2 * D_FF), f32))
    xp, xs = x_prompt, x_sample
    p_states, s_states = [], []
    for l in range(DEPTH):
        p = {'norm1': norm1_w[l], 'w_in': w_in[l], 'ml_ig_b': ml_ig_b[l], 'ml_fg_b': ml_fg_b[l],
             'ml_norm': ml_norm_w[l], 'ssd_conv_w': ssd_conv_w[l], 'ssd_conv_b': ssd_conv_b[l],
             'ssd_dt_bias': ssd_dt_bias[l], 'ssd_A_log': ssd_A_log[l], 'ssd_D': ssd_D[l],
             'ssd_norm': ssd_norm_w[l], 'hg_norm': hg_norm_w[l], 'w_out': w_out[l], 'norm2': norm2_w[l],
             'w_up': w_up[l], 'ffn_conv_w': ffn_conv_w[l], 'ffn_conv_b': ffn_conv_b[l], 'w_down': w_down[l]}
        s_in = (state_mlstm_C[l], state_mlstm_n[l], state_mlstm_m[l], state_ssd[l], state_ssd_conv[l],
                state_hgrn[l], state_ffn_conv[l])
        xp, sp = _layer(xp, zero_state, p, lbs[l])
        xs, ss = _layer(xs, s_in, p, lbs[l])
        p_states.append(sp)
        s_states.append(ss)
    y_prompt = _rmsnorm(xp, final_norm_w)
    y_sample = _rmsnorm(xs, final_norm_w)
    p_C, p_n, p_m, p_ssd, p_sconv, p_hg, p_fconv = [jnp.stack([st[i] for st in p_states]) for i in range(7)]
    s_C, s_n, s_m, s_ssd, s_sconv, s_hg, s_fconv = [jnp.stack([st[i] for st in s_states]) for i in range(7)]
    return (y_prompt, y_sample, p_C, p_n, p_m, p_ssd, p_sconv, p_hg, p_fconv,
            s_C, s_n, s_m, s_ssd, s_sconv, s_hg, s_fconv)
```

```python
import functools

import jax
import jax.numpy as jnp
from jax import lax
from jax.experimental import pallas as pl
from jax.experimental.pallas import tpu as pltpu

f32 = jnp.float32
bf16 = jnp.bfloat16

D_MODEL = 1024
DEPTH = 2
ML_HEADS = 6
HEAD_W = 64
ML_W = 384
SSD_HEADS = 6
SSD_W = 384
SSD_DSTATE = 128
SSD_CONV = 4
SSD_CONV_DIM = 896
HG_HEADS = 4
HG_W = 256
D_FF = 2816
FFN_CONV = 3
EPS = 1e-6
PROMPT_CHUNK = 128
HG_CHUNK = 64
LANES = 128
FF_BLOCK = 256
ROW_TILE = 512
GATE_W = 128
W_IN_COLS = 4 * ML_W + (SSD_W + SSD_CONV_DIM) + 4 * HG_W + GATE_W
VMEM_LIMIT_BYTES = 56 * 1024 * 1024

NEG_INF = float("-inf")


def _cparams(*sem):
    return pltpu.CompilerParams(dimension_semantics=sem, vmem_limit_bytes=VMEM_LIMIT_BYTES)


def _resident(shape):
    nd = len(shape)
    return pl.BlockSpec(shape, lambda *_: (0,) * nd, pipeline_mode=pl.Buffered(1))


def _dot(a, b):
    return jnp.dot(a, b, preferred_element_type=f32)


def _dot_nt(a, b):
    return lax.dot_general(a, b, (((1,), (1,)), ((), ())), preferred_element_type=f32)


def _split_bf16(x, parts):
    out = []
    r = x
    for _ in range(parts):
        h = r.astype(bf16)
        out.append(h)
        r = r - h.astype(f32)
    return out


def _cumsum_rows(x, tri_b):
    acc = None
    for part in _split_bf16(x, 3):
        d = _dot(tri_b, part)
        acc = d if acc is None else acc + d
    return acc


def _group_mean(x, bd_b):
    hi, lo = _split_bf16(x, 2)
    return (_dot(hi, bd_b) + _dot(lo, bd_b)) * (1.0 / HEAD_W)


def _iota2(shape, axis):
    return lax.broadcasted_iota(jnp.int32, shape, axis)


def _ones_where(mask):
    return jnp.where(mask, 1.0, 0.0).astype(bf16)


def _same_head(a, b):
    return jnp.right_shift(a, 6) == jnp.right_shift(b, 6)


def _silu(x):
    return x * jax.nn.sigmoid(x)


ML_COLS = 4 * ML_W
SSD_COLS = SSD_W + SSD_CONV_DIM
HG_COLS = 4 * HG_W


def _inproj_body(x_ref, nw_ref, w_ref, ml_ref, ssd_ref, hg_ref, g_ref):
    x = x_ref[...]
    h = (x * lax.rsqrt(jnp.mean(x * x, axis=-1, keepdims=True) + EPS) * nw_ref[...]).astype(bf16)
    o0, o1, o2 = ML_COLS, ML_COLS + SSD_COLS, ML_COLS + SSD_COLS + HG_COLS
    ml_ref[...] = _dot(h, w_ref[:, 0:o0])
    ssd_ref[...] = _dot(h, w_ref[:, o0:o1])
    hg_ref[...] = _dot(h, w_ref[:, o1:o2])
    g_ref[...] = _dot(h, w_ref[:, o2:W_IN_COLS])


def _inproj(x, norm_w, w_cat):
    n = x.shape[0]
    tm = ROW_TILE
    row = lambda w: pl.BlockSpec((tm, w), lambda i: (i, 0))
    return pl.pallas_call(
        _inproj_body,
        grid=(n // tm,),
        in_specs=[row(D_MODEL), _resident((1, D_MODEL)), _resident((D_MODEL, W_IN_COLS))],
        out_specs=[row(ML_COLS), row(SSD_COLS), row(HG_COLS), row(GATE_W)],
        out_shape=[jax.ShapeDtypeStruct((n, w), f32) for w in (ML_COLS, SSD_COLS, HG_COLS, GATE_W)],
        compiler_params=_cparams("arbitrary"),
        name="inproj",
    )(x, norm_w, w_cat)


def _mlstm_prompt_body(ml_ref, g_ref, gb_ref, nw_ref, out_ref, c1_ref, n1_ref, m1_ref, cext_s, m_s):
    c = pl.program_id(1)
    L = PROMPT_CHUNK

    @pl.when(c == 0)
    def _():
        cext_s[...] = jnp.zeros(cext_s.shape, f32)
        m_s[...] = jnp.zeros(m_s.shape, f32)

    row = _iota2((L, L), 0)
    col = _iota2((L, L), 1)
    tril = col <= row
    lo_half = col < HEAD_W
    tri_b = _ones_where(tril)
    bd_b = _ones_where(_same_head(row, col))
    lane2 = _iota2((L, 2 * LANES), 1)
    row2 = _iota2((L, 2 * LANES), 0)
    rowc = _iota2((L, 1), 0)
    ones01 = jnp.where(col < 2, 1.0, 0.0)
    one0 = jnp.where(col == 0, 1.0, 0.0)
    one1 = jnp.where(col == 1, 1.0, 0.0)
    mask_ext = jnp.where(lane2 < LANES, jnp.where(_same_head(row2, lane2), 1.0, 0.0),
                         jnp.where(lane2 == LANES, jnp.where(row2 < HEAD_W, 1.0, 0.0),
                                   jnp.where(lane2 == LANES + 1, jnp.where(row2 >= HEAD_W, 1.0, 0.0), 0.0)))

    g = g_ref[0] + gb_ref[...]
    bcum = _cumsum_rows(jax.nn.log_sigmoid(g), tri_b)
    gT = g.T
    bT = bcum.T

    for p in range(ML_HEADS // 2):
        sl = slice(p * LANES, (p + 1) * LANES)
        q2 = ml_ref[0, :, sl]
        k2 = ml_ref[0, :, ML_W + p * LANES:ML_W + (p + 1) * LANES] * (HEAD_W ** -0.5)
        v2 = ml_ref[0, :, 2 * ML_W + p * LANES:2 * ML_W + (p + 1) * LANES]
        o2 = ml_ref[0, :, 3 * ML_W + p * LANES:3 * ML_W + (p + 1) * LANES]
        kb = k2.astype(bf16)
        q_sel = (jnp.where(lo_half, q2, 0.0).astype(bf16), jnp.where(lo_half, 0.0, q2).astype(bf16))
        sw, a_int, e_m, a_st, w_s = [], [], [], [], []
        for j in range(2):
            h = 2 * p + j
            b_col = bcum[:, ML_HEADS + h:ML_HEADS + h + 1]
            b_row = bT[ML_HEADS + h:ML_HEADS + h + 1, :]
            i_row = gT[h:h + 1, :]
            i_col = g[:, h:h + 1]
            m_prev = m_s[0:1, h:h + 1]
            dm = jnp.where(tril, b_col - b_row + i_row, NEG_INF)
            inter = b_col + m_prev
            m_t = jnp.maximum(inter, jnp.max(dm, axis=1, keepdims=True))
            w = jnp.exp(dm - m_t)
            s = _dot_nt(q_sel[j], kb)
            sw.append((s * w).astype(bf16))
            a_int.append(jnp.exp(inter - m_t))
            e_m.append(jnp.exp(-m_t))
            m_new = m_t[L - 1:L, :]
            b_last = b_col[L - 1:L, :]
            a_st.append(jnp.exp(b_last + m_prev - m_new))
            w_s.append(jnp.exp(b_last - b_col + i_col - m_new))
            m_s[0:1, h:h + 1] = m_new
        vtop = jnp.concatenate([jnp.where(lo_half, v2, 0.0), one0], axis=1).astype(bf16)
        vbot = jnp.concatenate([jnp.where(lo_half, 0.0, v2), one1], axis=1).astype(bf16)
        cext = cext_s[p]
        a_ext = jnp.where(lane2 < HEAD_W, a_int[0],
                          jnp.where(lane2 < LANES, a_int[1], jnp.where(lane2 == LANES, a_int[0], a_int[1])))
        full = _dot(sw[0], vtop) + _dot(sw[1], vbot) + a_ext * _dot(q2.astype(bf16), cext.astype(bf16))
        num = full[:, 0:LANES]
        den0 = full[:, LANES:LANES + 1]
        den1 = full[:, LANES + 1:LANES + 2]
        dd = jnp.where(lo_half, jnp.maximum(jnp.abs(den0), e_m[0]), jnp.maximum(jnp.abs(den1), e_m[1]))
        hh = num / dd
        ms = _group_mean(hh * hh, bd_b)
        y = hh * lax.rsqrt(ms + EPS) * nw_ref[:, sl] * jax.nn.sigmoid(o2)
        out_ref[0, :, sl] = y.astype(bf16)
        kw_t = (k2 * jnp.where(lo_half, w_s[0], w_s[1])).T.astype(bf16)
        vones = jnp.concatenate([v2, ones01], axis=1).astype(bf16)
        upd = _dot(kw_t, vones)
        a_rows = jnp.where(rowc < HEAD_W, a_st[0], a_st[1])
        cext_s[p] = a_rows * cext + mask_ext * upd

    @pl.when(c == pl.num_programs(1) - 1)
    def _():
        for p in range(ML_HEADS // 2):
            cext = cext_s[p]
            c1_ref[0, 2 * p] = cext[0:HEAD_W, 0:HEAD_W]
            c1_ref[0, 2 * p + 1] = cext[HEAD_W:LANES, HEAD_W:LANES]
            n_t = cext[:, LANES:2 * LANES].T
            n1_ref[0, 2 * p:2 * p + 1, :] = n_t[0:1, 0:HEAD_W]
            n1_ref[0, 2 * p + 1:2 * p + 2, :] = n_t[1:2, HEAD_W:LANES]
        m1_ref[0] = m_s[...]


def _mlstm_prompt(ml, gates, gate_bias, norm_w):
    b, t, _ = ml.shape
    L = PROMPT_CHUNK
    return pl.pallas_call(
        _mlstm_prompt_body,
        grid=(b, t // L),
        in_specs=[pl.BlockSpec((1, L, ML_COLS), lambda i, c: (i, c, 0)),
                  pl.BlockSpec((1, L, GATE_W), lambda i, c: (i, c, 0)),
                  _resident((1, GATE_W)), _resident((1, ML_W))],
        out_specs=[pl.BlockSpec((1, L, ML_W), lambda i, c: (i, c, 0)),
                   pl.BlockSpec((1, ML_HEADS, HEAD_W, HEAD_W), lambda i, c: (i, 0, 0, 0)),
                   pl.BlockSpec((1, ML_HEADS, HEAD_W), lambda i, c: (i, 0, 0)),
                   pl.BlockSpec((1, 1, GATE_W), lambda i, c: (i, 0, 0))],
        out_shape=[jax.ShapeDtypeStruct((b, t, ML_W), bf16),
                   jax.ShapeDtypeStruct((b, ML_HEADS, HEAD_W, HEAD_W), f32),
                   jax.ShapeDtypeStruct((b, ML_HEADS, HEAD_W), f32),
                   jax.ShapeDtypeStruct((b, 1, GATE_W), f32)],
        scratch_shapes=[pltpu.VMEM((ML_HEADS // 2, LANES, 2 * LANES), f32), pltpu.VMEM((1, GATE_W), f32)],
        compiler_params=_cparams("arbitrary", "arbitrary"),
        name="mlstm_prompt",
    )(ml, gates, gate_bias, norm_w)


DT_LANE = 2 * ML_HEADS
HALO = 8


def _ssd_prompt_body(ssd_ref, g_ref, gb_ref, al_ref, cw_ref, cb_ref, d_ref, nw_ref, out_ref, h1_ref,
                     xbuf_s, h_s):
    c = pl.program_id(1)
    L = PROMPT_CHUNK

    @pl.when(c == 0)
    def _():
        xbuf_s[0:HALO, :] = jnp.zeros((HALO, SSD_CONV_DIM), f32)
        h_s[...] = jnp.zeros(h_s.shape, f32)

    xbuf_s[HALO:HALO + L, :] = ssd_ref[0, :, SSD_W:SSD_COLS]
    conv = cb_ref[...]
    for j in range(SSD_CONV):
        o = HALO - (SSD_CONV - 1) + j
        conv = conv + cw_ref[j:j + 1, :] * xbuf_s[o:o + L, :]
    xbuf_s[0:HALO, :] = xbuf_s[L:L + HALO, :]
    xc = _silu(conv)
    nb = SSD_W
    b_b = [xc[:, nb + gi * LANES:nb + (gi + 1) * LANES].astype(bf16) for gi in range(2)]
    c_b = [xc[:, nb + (2 + gi) * LANES:nb + (3 + gi) * LANES].astype(bf16) for gi in range(2)]

    row = _iota2((L, L), 0)
    col = _iota2((L, L), 1)
    tril = col <= row
    lo_half = col < HEAD_W
    tri_b = _ones_where(tril)
    bd_b = _ones_where(_same_head(row, col))
    rowc = _iota2((L, 1), 0)

    g = g_ref[0] + gb_ref[...]
    dt = jax.nn.softplus(g)
    cum = _cumsum_rows(dt * (-jnp.exp(al_ref[...])), tri_b)
    cum_t = cum.T
    dt_t = dt.T
    cb_g = [_dot_nt(c_b[gi], b_b[gi]) for gi in range(2)]

    for p in range(SSD_HEADS // 2):
        sl = slice(p * LANES, (p + 1) * LANES)
        x2 = xc[:, sl]
        z2 = ssd_ref[0, :, sl]
        hp = h_s[p]
        hb = hp.astype(bf16)
        grp = [(2 * p + j) // (SSD_HEADS // 2) for j in range(2)]
        ys = None
        ecum, w_s, a_last, ch = [], [], [], []
        for j in range(2):
            ln = DT_LANE + 2 * p + j
            cum_col = cum[:, ln:ln + 1]
            cum_row = cum_t[ln:ln + 1, :]
            cum_last = cum_col[L - 1:L, :]
            lm = jnp.exp(jnp.where(tril, cum_col - cum_row, NEG_INF))
            m = (cb_g[grp[j]] * lm * dt_t[ln:ln + 1, :]).astype(bf16)
            xm = jnp.where(lo_half, x2, 0.0) if j == 0 else jnp.where(lo_half, 0.0, x2)
            d = _dot(m, xm.astype(bf16))
            ys = d if ys is None else ys + d
            ecum.append(jnp.exp(cum_col))
            w_s.append(jnp.exp(cum_last - cum_col) * dt[:, ln:ln + 1])
            a_last.append(jnp.exp(cum_last))
            if j == 0 or grp[1] != grp[0]:
                ch.append(_dot_nt(c_b[grp[j]], hb))
            else:
                ch.append(ch[0])
        ys = ys + jnp.where(lo_half, ecum[0] * ch[0], ecum[1] * ch[1])
        xw_t = (x2 * jnp.where(lo_half, w_s[0], w_s[1])).T.astype(bf16)
        if grp[0] == grp[1]:
            upd = _dot(xw_t, b_b[grp[0]])
        else:
            upd = jnp.where(row < HEAD_W, _dot(xw_t, b_b[grp[0]]), _dot(xw_t, b_b[grp[1]]))
        h_s[p] = jnp.where(rowc < HEAD_W, a_last[0], a_last[1]) * hp + upd
        yy = (ys + d_ref[:, sl] * x2) * _silu(z2)
        ms = _group_mean(yy * yy, bd_b)
        out_ref[0, :, sl] = (yy * lax.rsqrt(ms + EPS) * nw_ref[:, sl]).astype(bf16)

    @pl.when(c == pl.num_programs(1) - 1)
    def _():
        h1_ref[0] = h_s[...]


def _ssd_prompt(ssd, gates, gate_bias, alog_row, conv_w, conv_b, d_row, norm_w):
    b, t, _ = ssd.shape
    L = PROMPT_CHUNK
    npair = SSD_HEADS // 2
    return pl.pallas_call(
        _ssd_prompt_body,
        grid=(b, t // L),
        in_specs=[pl.BlockSpec((1, L, SSD_COLS), lambda i, c: (i, c, 0)),
                  pl.BlockSpec((1, L, GATE_W), lambda i, c: (i, c, 0)),
                  _resident((1, GATE_W)), _resident((1, GATE_W)),
                  _resident((SSD_CONV, SSD_CONV_DIM)), _resident((1, SSD_CONV_DIM)),
                  _resident((1, SSD_W)), _resident((1, SSD_W))],
        out_specs=[pl.BlockSpec((1, L, SSD_W), lambda i, c: (i, c, 0)),
                   pl.BlockSpec((1, npair, LANES, SSD_DSTATE), lambda i, c: (i, 0, 0, 0))],
        out_shape=[jax.ShapeDtypeStruct((b, t, SSD_W), bf16),
                   jax.ShapeDtypeStruct((b, npair, LANES, SSD_DSTATE), f32)],
        scratch_shapes=[pltpu.VMEM((HALO + L, SSD_CONV_DIM), f32), pltpu.VMEM((npair, LANES, SSD_DSTATE), f32)],
        compiler_params=_cparams("arbitrary", "arbitrary"),
        name="ssd_prompt",
    )(ssd, gates, gate_bias, alog_row, conv_w, conv_b, d_row, norm_w)


def _hg_lower_bound(lbp, layer):
    mx = jnp.max(lbp, axis=0, keepdims=True)
    e = jnp.exp(lbp - mx)
    den = jnp.sum(e, axis=0, keepdims=True)
    lb = jnp.zeros_like(den)
    for j in range(1, layer + 1):
        lb = lb + e[j:j + 1, :] / den
    return lb


def _hg_rows(L):
    offs, off = [], 0
    for s in range(L):
        offs.append(off)
        off += L - 8 * (s // 8)
    return offs, off


def _hgrn_prompt_body(hg_ref, lbp_ref, nw_ref, out_ref, s1_ref, sbd_s, p_s, r_s, o_s, g_s, k_s, v_s, q_s, *, layer):
    c = pl.program_id(1)
    L = HG_CHUNK

    @pl.when(c == 0)
    def _():
        sbd_s[...] = jnp.zeros(sbd_s.shape, f32)

    lb = _hg_lower_bound(lbp_ref[...], layer)
    ff = hg_ref[0, :, HG_W:2 * HG_W]
    sig = jax.nn.sigmoid(ff)
    k_s[...] = (1.0 - lb) * jax.nn.sigmoid(-ff)
    v_s[...] = hg_ref[0, :, 2 * HG_W:3 * HG_W]
    q_s[...] = hg_ref[0, :, 0:HG_W] * (HEAD_W ** -0.5)
    row = _iota2((L, L), 0)
    col = _iota2((L, L), 1)
    tri_b = _ones_where(col <= row)
    g_s[...] = _cumsum_rows(jnp.log(lb + (1.0 - lb) * sig), tri_b)

    r2 = _iota2((2 * LANES, 2 * LANES), 0)
    c2 = _iota2((2 * LANES, 2 * LANES), 1)
    bd4_b = _ones_where(_same_head(r2, c2))
    r1 = _iota2((LANES, LANES), 0)
    c1 = _iota2((LANES, LANES), 1)
    bd_mask = _same_head(r1, c1)
    bd_b = _ones_where(bd_mask)

    offs, total = _hg_rows(L)
    for s in range(L):
        t0 = 8 * (s // 8)
        n = L - t0
        tt = _iota2((n, HG_W), 0) + t0
        e = jnp.exp(jnp.where(tt >= s, g_s[t0:L, :] - g_s[s:s + 1, :], NEG_INF))
        p_s[offs[s]:offs[s] + n, :] = q_s[t0:L, :] * k_s[s:s + 1, :] * e
    step = 576
    for r0 in range(0, total, step):
        r_s[r0:r0 + step, :] = _dot(p_s[r0:r0 + step, :].astype(bf16), bd4_b)

    gg = g_s[...]
    qd = q_s[...] * jnp.exp(gg)
    for p in range(HG_HEADS // 2):
        sl = slice(p * LANES, (p + 1) * LANES)
        o_s[:, sl] = _dot(qd[:, sl].astype(bf16), sbd_s[p].astype(bf16))
    for s in range(L):
        t0 = 8 * (s // 8)
        n = L - t0
        o_s[t0:L, :] = o_s[t0:L, :] + r_s[offs[s]:offs[s] + n, :] * v_s[s:s + 1, :]

    g_last = gg[L - 1:L, :]
    kd = k_s[...] * jnp.exp(g_last - gg)
    vv = v_s[...]
    zpad = jnp.zeros((LANES - L, LANES), f32)
    og = o_s[...]
    gate = hg_ref[0, :, 3 * HG_W:4 * HG_W]
    for p in range(HG_HEADS // 2):
        sl = slice(p * LANES, (p + 1) * LANES)
        kd_t = jnp.concatenate([kd[:, sl], zpad], axis=0).T.astype(bf16)
        v_pad = jnp.concatenate([vv[:, sl], zpad], axis=0).astype(bf16)
        g_t = jnp.concatenate([gg[:, sl], zpad], axis=0).T
        dec = jnp.exp(g_t[:, L - 1:L])
        sbd_s[p] = dec * sbd_s[p] + jnp.where(bd_mask, _dot(kd_t, v_pad), 0.0)
        oo = og[:, sl]
        ms = _group_mean(oo * oo, bd_b)
        out_ref[0, :, sl] = (oo * lax.rsqrt(ms + EPS) * nw_ref[:, sl] * _silu(gate[:, sl])).astype(bf16)

    @pl.when(c == pl.num_programs(1) - 1)
    def _():
        for p in range(HG_HEADS // 2):
            sbd = sbd_s[p]
            s1_ref[0, 2 * p] = sbd[0:HEAD_W, 0:HEAD_W]
            s1_ref[0, 2 * p + 1] = sbd[HEAD_W:LANES, HEAD_W:LANES]


def _hgrn_prompt(hg, lbp, norm_w, layer):
    b, t, _ = hg.shape
    L = HG_CHUNK
    _, total = _hg_rows(L)
    return pl.pallas_call(
        functools.partial(_hgrn_prompt_body, layer=layer),
        grid=(b, t // L),
        in_specs=[pl.BlockSpec((1, L, HG_COLS), lambda i, c: (i, c, 0)),
                  _resident((DEPTH, HG_W)), _resident((1, HG_W))],
        out_specs=[pl.BlockSpec((1, L, HG_W), lambda i, c: (i, c, 0)),
                   pl.BlockSpec((1, HG_HEADS, HEAD_W, HEAD_W), lambda i, c: (i, 0, 0, 0))],
        out_shape=[jax.ShapeDtypeStruct((b, t, HG_W), bf16),
                   jax.ShapeDtypeStruct((b, HG_HEADS, HEAD_W, HEAD_W), f32)],
        scratch_shapes=[pltpu.VMEM((HG_HEADS // 2, LANES, LANES), f32),
                        pltpu.VMEM((total, HG_W), f32), pltpu.VMEM((total, HG_W), f32),
                        pltpu.VMEM((L, HG_W), f32), pltpu.VMEM((L, HG_W), f32), pltpu.VMEM((L, HG_W), f32),
                        pltpu.VMEM((L, HG_W), f32), pltpu.VMEM((L, HG_W), f32)],
        compiler_params=_cparams("arbitrary", "arbitrary"),
        name="hgrn_prompt",
    )(hg, lbp, norm_w)


NB = 128
TS = 8


def _load_state_t(src_ref, dst_s):
    per_head = dst_s.shape[1] * dst_s.shape[2] // LANES
    rows = LANES // dst_s.shape[2]
    for j in range(2 * per_head):
        tile = src_ref[:, j * LANES:(j + 1) * LANES].T
        hl, r0 = j // per_head, (j % per_head) * rows
        for r in range(rows):
            dst_s[hl, r0 + r] = tile[r * dst_s.shape[2]:(r + 1) * dst_s.shape[2], :]


def _store_state_t(src_s, dst_ref):
    per_head = src_s.shape[1] * src_s.shape[2] // LANES
    rows = LANES // src_s.shape[2]
    for j in range(2 * per_head):
        hl, r0 = j // per_head, (j % per_head) * rows
        tile = jnp.concatenate([src_s[hl, r0 + r] for r in range(rows)], axis=0) if rows > 1 else src_s[hl, r0]
        dst_ref[:, j * LANES:(j + 1) * LANES] = tile.T


def _mlstm_sample_body(q_ref, k_ref, v_ref, o_ref, g_ref, gb_ref, nw_ref, c0_ref, n0_ref, m0_ref,
                       out_ref, c1_ref, n1_ref, m1_ref, ct_s, qt_s, kt_s, vt_s, gt_s, ht_s):
    p = pl.program_id(0)

    @pl.when(p == 0)
    def _():
        m1_ref[...] = jnp.zeros(m1_ref.shape, f32)

    _load_state_t(c0_ref, ct_s)
    n_all = n0_ref[...].T
    n_cur = [n_all[0:HEAD_W, :], n_all[HEAD_W:LANES, :]]
    m_cur = [m0_ref[pl.ds(2 * p + hl, 1), :] for hl in range(2)]

    for t in range(TS):
        rows = slice(t * NB, (t + 1) * NB)
        qt_s[...] = q_ref[rows, :].T
        kt_s[...] = k_ref[rows, :].T * (HEAD_W ** -0.5)
        vt_s[...] = v_ref[rows, :].T
        gt_s[...] = (g_ref[rows, :] + gb_ref[...]).T
        for hl in range(2):
            h = 2 * p + hl
            hs = slice(hl * HEAD_W, (hl + 1) * HEAD_W)
            i_t = gt_s[pl.ds(h, 1), :]
            lf = jax.nn.log_sigmoid(gt_s[pl.ds(ML_HEADS + h, 1), :])
            m_new = jnp.maximum(lf + m_cur[hl], i_t)
            a = jnp.exp(lf + m_cur[hl] - m_new)
            w = jnp.exp(i_t - m_new)
            n_new = a * n_cur[hl] + w * kt_s[hs, :]
            v_h = vt_s[hs, :]

            def body(d, num, hl=hl, a=a, w=w, v_h=v_h):
                kd = kt_s[pl.ds(hl * HEAD_W + d, 1), :] * w
                qd = qt_s[pl.ds(hl * HEAD_W + d, 1), :]
                cd = a * ct_s[hl, d] + kd * v_h
                ct_s[hl, d] = cd
                return num + qd * cd

            num = lax.fori_loop(0, HEAD_W, body, jnp.zeros((HEAD_W, NB), f32), unroll=4)
            den = jnp.sum(qt_s[hs, :] * n_new, axis=0, keepdims=True)
            hh = num / jnp.maximum(jnp.abs(den), jnp.exp(-m_new))
            ms = jnp.mean(hh * hh, axis=0, keepdims=True)
            ht_s[hs, :] = hh * lax.rsqrt(ms + EPS)
            n_cur[hl] = n_new
            m_cur[hl] = m_new
        out_ref[rows, :] = (ht_s[...].T * nw_ref[...] * jax.nn.sigmoid(o_ref[rows, :])).astype(bf16)

    _store_state_t(ct_s, c1_ref)
    n1_ref[...] = jnp.concatenate(n_cur, axis=0).T
    for hl in range(2):
        m1_ref[pl.ds(2 * p + hl, 1), :] = m_cur[hl]


def _mlstm_sample(ml, gates, gate_bias, norm_w, c0, n0, m0):
    n = ml.shape[0]
    npair = ML_HEADS // 2
    blk = lambda off: pl.BlockSpec((n, LANES), lambda p, off=off: (0, off + p))
    st_w = 2 * HEAD_W * HEAD_W
    return pl.pallas_call(
        _mlstm_sample_body,
        grid=(npair,),
        in_specs=[blk(0), blk(npair), blk(2 * npair), blk(3 * npair),
                  _resident((n, GATE_W)), _resident((1, GATE_W)),
                  pl.BlockSpec((1, LANES), lambda p: (0, p)),
                  pl.BlockSpec((NB, st_w), lambda p: (0, p)),
                  pl.BlockSpec((NB, LANES), lambda p: (0, p)),
                  _resident((8, NB))],
        out_specs=[pl.BlockSpec((n, LANES), lambda p: (0, p)),
                   pl.BlockSpec((NB, st_w), lambda p: (0, p)),
                   pl.BlockSpec((NB, LANES), lambda p: (0, p)),
                   pl.BlockSpec((8, NB), lambda p: (0, 0))],
        out_shape=[jax.ShapeDtypeStruct((n, ML_W), bf16),
                   jax.ShapeDtypeStruct(c0.shape, f32), jax.ShapeDtypeStruct(n0.shape, f32),
                   jax.ShapeDtypeStruct((8, NB), f32)],
        scratch_shapes=[pltpu.VMEM((2, HEAD_W, HEAD_W, NB), f32)] + [pltpu.VMEM((LANES, NB), f32)] * 5,
        compiler_params=_cparams("arbitrary"),
        name="mlstm_sample",
    )(ml, ml, ml, ml, gates, gate_bias, norm_w, c0, n0, m0)


def _ssd_conv_sample_body(ssd_ref, halo_ref, cw_ref, cb_ref, out_ref, xp_s):
    nh = (SSD_CONV - 1) * NB
    n = TS * NB
    xp_s[0:nh, :] = halo_ref[...]
    xp_s[nh:nh + n, :] = ssd_ref[:, SSD_W:SSD_COLS]
    conv = cb_ref[...]
    for j in range(SSD_CONV):
        conv = conv + cw_ref[j:j + 1, :] * xp_s[j * NB:j * NB + n, :]
    out_ref[...] = _silu(conv)


def _ssd_conv_sample(ssd, halo, conv_w, conv_b):
    n = ssd.shape[0]
    nh = (SSD_CONV - 1) * NB
    return pl.pallas_call(
        _ssd_conv_sample_body,
        grid=(1,),
        in_specs=[_resident((n, SSD_COLS)), _resident((nh, SSD_CONV_DIM)),
                  _resident((SSD_CONV, SSD_CONV_DIM)), _resident((1, SSD_CONV_DIM))],
        out_specs=pl.BlockSpec((n, SSD_CONV_DIM), lambda i: (0, 0)),
        out_shape=jax.ShapeDtypeStruct((n, SSD_CONV_DIM), f32),
        scratch_shapes=[pltpu.VMEM((nh + n, SSD_CONV_DIM), f32)],
        compiler_params=_cparams("arbitrary"),
        name="ssd_conv_sample",
    )(ssd, halo, conv_w, conv_b)


def _ssd_sample_body(z_ref, x_ref, b0_ref, b1_ref, c0_ref, c1_ref, g_ref, gb_ref, hp_ref, nw_ref, h0_ref,
                     out_ref, h1_ref, ht_s, xt_s, bt_s, ct_s, gt_s, yt_s):
    p = pl.program_id(0)
    _load_state_t(h0_ref, ht_s)
    b_refs = (b0_ref, b1_ref)
    c_refs = (c0_ref, c1_ref)

    for t in range(TS):
        rows = slice(t * NB, (t + 1) * NB)
        xt_s[...] = x_ref[rows, :].T
        gt_s[...] = (g_ref[rows, :] + gb_ref[...]).T
        zt = z_ref[rows, :].T
        for hl in range(2):
            bt_s[hl] = b_refs[hl][rows, :].T
            ct_s[hl] = c_refs[hl][rows, :].T
        for hl in range(2):
            h = 2 * p + hl
            hs = slice(hl * HEAD_W, (hl + 1) * HEAD_W)
            dt = jax.nn.softplus(gt_s[pl.ds(DT_LANE + h, 1), :])
            da = jnp.exp(dt * (-jnp.exp(hp_ref[0, 0, hl:hl + 1, :])))

            def body(pp, carry, hl=hl, dt=dt, da=da):
                xrow = xt_s[pl.ds(hl * HEAD_W + pp, 1), :] * dt
                hn = da * ht_s[hl, pp] + xrow * bt_s[hl]
                ht_s[hl, pp] = hn
                yt_s[pl.ds(hl * HEAD_W + pp, 1), :] = jnp.sum(ct_s[hl] * hn, axis=0, keepdims=True)
                return carry

            lax.fori_loop(0, HEAD_W, body, 0, unroll=2)
            yy = (yt_s[hs, :] + hp_ref[1, 0, hl:hl + 1, :] * xt_s[hs, :]) * _silu(zt[hs, :])
            ms = jnp.mean(yy * yy, axis=0, keepdims=True)
            yt_s[hs, :] = yy * lax.rsqrt(ms + EPS)
        out_ref[rows, :] = (yt_s[...].T * nw_ref[...]).astype(bf16)

    _store_state_t(ht_s, h1_ref)


def _ssd_sample(ssd, xc, gates, gate_bias, head_params, norm_w, h0):
    n = ssd.shape[0]
    npair = SSD_HEADS // 2
    hpg = SSD_HEADS // 2
    nxb = SSD_W // LANES
    st_w = 2 * HEAD_W * SSD_DSTATE
    blk = lambda f: pl.BlockSpec((n, LANES), f)
    return pl.pallas_call(
        _ssd_sample_body,
        grid=(npair,),
        in_specs=[blk(lambda p: (0, p)), blk(lambda p: (0, p)),
                  blk(lambda p: (0, nxb + (2 * p) // hpg)), blk(lambda p: (0, nxb + (2 * p + 1) // hpg)),
                  blk(lambda p: (0, nxb + 2 + (2 * p) // hpg)), blk(lambda p: (0, nxb + 2 + (2 * p + 1) // hpg)),
                  _resident((n, GATE_W)), _resident((1, GATE_W)),
                  pl.BlockSpec((2, 1, 2, LANES), lambda p: (0, p, 0, 0)),
                  pl.BlockSpec((1, LANES), lambda p: (0, p)),
                  pl.BlockSpec((NB, st_w), lambda p: (0, p))],
        out_specs=[pl.BlockSpec((n, LANES), lambda p: (0, p)),
                   pl.BlockSpec((NB, st_w), lambda p: (0, p))],
        out_shape=[jax.ShapeDtypeStruct((n, SSD_W), bf16), jax.ShapeDtypeStruct(h0.shape, f32)],
        scratch_shapes=[pltpu.VMEM((2, HEAD_W, SSD_DSTATE, NB), f32), pltpu.VMEM((LANES, NB), f32),
                        pltpu.VMEM((2, SSD_DSTATE, NB), f32), pltpu.VMEM((2, SSD_DSTATE, NB), f32),
                        pltpu.VMEM((LANES, NB), f32), pltpu.VMEM((LANES, NB), f32)],
        compiler_params=_cparams("arbitrary"),
        name="ssd_sample",
    )(ssd, xc, xc, xc, xc, xc, gates, gate_bias, head_params, norm_w, h0)


def _hgrn_sample_body(q_ref, f_ref, i_ref, gate_ref, lbp_ref, nw_ref, s0_ref, out_ref, s1_ref,
                      st_s, qt_s, kt_s, ft_s, vt_s, ot_s, *, layer):
    _load_state_t(s0_ref, st_s)
    lb = _hg_lower_bound(lbp_ref[...], layer)

    for t in range(TS):
        rows = slice(t * NB, (t + 1) * NB)
        ff = f_ref[rows, :]
        qt_s[...] = (q_ref[rows, :] * (HEAD_W ** -0.5)).T
        kt_s[...] = ((1.0 - lb) * jax.nn.sigmoid(-ff)).T
        ft_s[...] = (lb + (1.0 - lb) * jax.nn.sigmoid(ff)).T
        vt_s[...] = i_ref[rows, :].T
        for hl in range(2):
            hs = slice(hl * HEAD_W, (hl + 1) * HEAD_W)
            v_h = vt_s[hs, :]

            def body(k, o, hl=hl, v_h=v_h):
                r = pl.ds(hl * HEAD_W + k, 1)
                sk = ft_s[r, :] * st_s[hl, k] + kt_s[r, :] * v_h
                st_s[hl, k] = sk
                return o + qt_s[r, :] * sk

            o = lax.fori_loop(0, HEAD_W, body, jnp.zeros((HEAD_W, NB), f32), unroll=4)
            ms = jnp.mean(o * o, axis=0, keepdims=True)
            ot_s[hs, :] = o * lax.rsqrt(ms + EPS)
        out_ref[rows, :] = (ot_s[...].T * nw_ref[...] * _silu(gate_ref[rows, :])).astype(bf16)

    _store_state_t(st_s, s1_ref)


def _hgrn_sample(hg, lbp, norm_w, s0, layer):
    n = hg.shape[0]
    npair = HG_HEADS // 2
    blk = lambda off: pl.BlockSpec((n, LANES), lambda p, off=off: (0, off + p))
    st_w = 2 * HEAD_W * HEAD_W
    return pl.pallas_call(
        functools.partial(_hgrn_sample_body, layer=layer),
        grid=(npair,),
        in_specs=[blk(0), blk(npair), blk(2 * npair), blk(3 * npair),
                  pl.BlockSpec((DEPTH, LANES), lambda p: (0, p)),
                  pl.BlockSpec((1, LANES), lambda p: (0, p)),
                  pl.BlockSpec((NB, st_w), lambda p: (0, p))],
        out_specs=[pl.BlockSpec((n, LANES), lambda p: (0, p)),
                   pl.BlockSpec((NB, st_w), lambda p: (0, p))],
        out_shape=[jax.ShapeDtypeStruct((n, HG_W), bf16), jax.ShapeDtypeStruct(s0.shape, f32)],
        scratch_shapes=[pltpu.VMEM((2, HEAD_W, HEAD_W, NB), f32)] + [pltpu.VMEM((LANES, NB), f32)] * 5,
        compiler_params=_cparams("arbitrary"),
        name="hgrn_sample",
    )(hg, hg, hg, hg, lbp, norm_w, s0)


def _ffn_body(x_ref, mml_ref, mssd_ref, mhg_ref, wout_ref, n2_ref, wug_ref, wuv_ref, cwg_ref, cwv_ref,
              cbg_ref, cbv_ref, wdn_ref, f0g_ref, f0v_ref, fn_ref, y_ref, f1g_ref, f1v_ref,
              x1_s, h2_s, acc_s, ubuf_s, *, tm, shift, jb, final):
    t = pl.program_id(1)
    f = pl.program_id(2)
    halo = max(HALO, (FFN_CONV - 1) * shift)
    keep = (FFN_CONV - 1) * shift

    @pl.when(f == 0)
    def _():
        x = x_ref[...]
        x1 = (x + _dot(mml_ref[...], wout_ref[0:ML_W, :]) + _dot(mssd_ref[...], wout_ref[ML_W:ML_W + SSD_W, :])
              + _dot(mhg_ref[...], wout_ref[ML_W + SSD_W:D_MODEL, :]))
        x1_s[...] = x1
        h2_s[...] = (x1 * lax.rsqrt(jnp.mean(x1 * x1, axis=-1, keepdims=True) + EPS) * n2_ref[...]).astype(bf16)

    @pl.when(t == 0)
    def _():
        f1g_ref[0] = f0g_ref[0]
        f1v_ref[0] = f0v_ref[0]

    h2 = h2_s[...]

    def conv(wu_ref, cw_ref, cb_ref, f1_ref, cols):
        u = _dot(h2, wu_ref[:, cols])
        ubuf_s[halo - keep:halo, :] = f1_ref[0, :, cols]
        ubuf_s[halo:halo + tm, :] = u
        yv = (cb_ref[:, cols] + cw_ref[2:3, cols] * u
              + cw_ref[1:2, cols] * ubuf_s[halo - shift:halo - shift + tm, :]
              + cw_ref[0:1, cols] * ubuf_s[halo - 2 * shift:halo - 2 * shift + tm, :])
        f1_ref[0, :, cols] = ubuf_s[halo + tm - keep:halo + tm, :]
        return yv

    for j in range(jb):
        cols = slice(j * FF_BLOCK, (j + 1) * FF_BLOCK)
        gq = conv(wug_ref, cwg_ref, cbg_ref, f1g_ref, cols)
        vq = conv(wuv_ref, cwv_ref, cbv_ref, f1v_ref, cols)
        contrib = _dot((_silu(gq) * vq).astype(bf16), wdn_ref[cols, :])
        if j == 0:
            @pl.when(f == 0)
            def _():
                acc_s[...] = contrib

            @pl.when(f != 0)
            def _():
                acc_s[...] = acc_s[...] + contrib
        else:
            acc_s[...] = acc_s[...] + contrib

    @pl.when(f == pl.num_programs(2) - 1)
    def _():
        x2 = x1_s[...] + acc_s[...]
        if final:
            x2 = x2 * lax.rsqrt(jnp.mean(x2 * x2, axis=-1, keepdims=True) + EPS) * fn_ref[...]
        y_ref[...] = x2


def _ffn(x, mml, mssd, mhg, w_out, norm2, w_up, conv_w, conv_b, w_down, f0, final_w, *, groups, tm, shift, jb,
         final):
    n = x.shape[0]
    nt = n // (groups * tm)
    fw = jb * FF_BLOCK
    nff = D_FF // fw
    halo = max(HALO, (FFN_CONV - 1) * shift)
    keep = (FFN_CONV - 1) * shift
    rows = lambda w: pl.BlockSpec((tm, w), lambda g, t, f: (g * nt + t, 0))
    single = nff == 1
    wspec = lambda shape, imap: pl.BlockSpec(shape, imap, pipeline_mode=pl.Buffered(1)) if single else pl.BlockSpec(shape, imap)
    return pl.pallas_call(
        functools.partial(_ffn_body, tm=tm, shift=shift, jb=jb, final=final),
        grid=(groups, nt, nff),
        in_specs=[rows(D_MODEL), rows(ML_W), rows(SSD_W), rows(HG_W),
                  _resident((D_MODEL, D_MODEL)), _resident((1, D_MODEL)),
                  wspec((D_MODEL, fw), lambda g, t, f: (0, f)), wspec((D_MODEL, fw), lambda g, t, f: (0, nff + f)),
                  wspec((FFN_CONV, fw), lambda g, t, f: (0, f)), wspec((FFN_CONV, fw), lambda g, t, f: (0, nff + f)),
                  wspec((1, fw), lambda g, t, f: (0, f)), wspec((1, fw), lambda g, t, f: (0, nff + f)),
                  wspec((fw, D_MODEL), lambda g, t, f: (f, 0)),
                  pl.BlockSpec((1, keep, fw), lambda g, t, f: (g, 0, f)),
                  pl.BlockSpec((1, keep, fw), lambda g, t, f: (g, 0, nff + f)),
                  _resident((1, D_MODEL))],
        out_specs=[rows(D_MODEL),
                   pl.BlockSpec((1, keep, fw), lambda g, t, f: (g, 0, f)),
                   pl.BlockSpec((1, keep, fw), lambda g, t, f: (g, 0, f))],
        out_shape=[jax.ShapeDtypeStruct((n, D_MODEL), f32),
                   jax.ShapeDtypeStruct((groups, keep, D_FF), f32),
                   jax.ShapeDtypeStruct((groups, keep, D_FF), f32)],
        scratch_shapes=[pltpu.VMEM((tm, D_MODEL), f32), pltpu.VMEM((tm, D_MODEL), bf16),
                        pltpu.VMEM((tm, D_MODEL), f32), pltpu.VMEM((halo + tm, FF_BLOCK), f32)],
        compiler_params=_cparams("arbitrary", "arbitrary", "arbitrary"),
        name="ffn",
    )(x, mml, mssd, mhg, w_out, norm2, w_up, w_up, conv_w, conv_w, conv_b, conv_b, w_down, f0, f0, final_w)


def _regroup_w_in(w):
    o = [0]
    for s in (ML_W, ML_W, ML_W, ML_W, ML_HEADS, ML_HEADS, SSD_W, SSD_CONV_DIM, SSD_HEADS, HG_W, HG_W, HG_W, HG_W):
        o.append(o[-1] + s)
    gates = jnp.concatenate([w[:, o[4]:o[6]], w[:, o[8]:o[9]]], axis=1)
    gates = jnp.pad(gates, ((0, 0), (0, GATE_W - gates.shape[1])))
    return jnp.concatenate([w[:, o[0]:o[4]], w[:, o[6]:o[8]], w[:, o[9]:o[13]], gates], axis=1).astype(bf16)


def _gate_row(*parts):
    r = jnp.concatenate([p.astype(f32) for p in parts])
    return jnp.pad(r, (0, GATE_W - r.shape[0]))[None, :]


def kernel(x_prompt, x_sample, state_mlstm_C, state_mlstm_n, state_mlstm_m, state_ssd, state_ssd_conv,
           state_hgrn, state_ffn_conv, norm1_w, w_in, ml_ig_b, ml_fg_b, ml_norm_w, ssd_conv_w, ssd_conv_b,
           ssd_dt_bias, ssd_A_log, ssd_D, ssd_norm_w, hg_lower_bounds, hg_norm_w, w_out, norm2_w, w_up,
           ffn_conv_w, ffn_conv_b, w_down, final_norm_w):
    bp, tp, _ = x_prompt.shape
    nb, ts, _ = x_sample.shape
    assert nb == NB and ts == TS and tp % ROW_TILE == 0
    depth = w_in.shape[0]
    xp = x_prompt.reshape(bp * tp, D_MODEL)
    xs = jnp.transpose(x_sample, (1, 0, 2)).reshape(ts * nb, D_MODEL)
    lbp = hg_lower_bounds.astype(f32)
    fin_w = final_norm_w[None, :]
    p_states, s_states = [], []
    zero_f = jnp.zeros((bp, FFN_CONV - 1, 2 * D_FF), f32)

    for l in range(depth):
        w_cat = _regroup_w_in(w_in[l])
        wo_b, wu_b, wd_b = w_out[l].astype(bf16), w_up[l].astype(bf16), w_down[l].astype(bf16)
        n1, n2 = norm1_w[l][None, :], norm2_w[l][None, :]
        gate_bias = _gate_row(ml_ig_b[l], ml_fg_b[l], ssd_dt_bias[l])
        alog_row = _gate_row(jnp.zeros((DT_LANE,), f32), ssd_A_log[l])
        d_row = jnp.repeat(ssd_D[l].astype(f32), HEAD_W)[None, :]
        mlw, ssw, hgw = ml_norm_w[l][None, :], ssd_norm_w[l][None, :], hg_norm_w[l][None, :]
        cw, cb = ssd_conv_w[l], ssd_conv_b[l][None, :]
        fcw, fcb = ffn_conv_w[l], ffn_conv_b[l][None, :]
        last = l == depth - 1

        ml, ssd, hg, gates = _inproj(xp, n1, w_cat)
        gates3 = gates.reshape(bp, tp, GATE_W)
        ssd3 = ssd.reshape(bp, tp, SSD_COLS)
        hm, p_c, p_n, p_m = _mlstm_prompt(ml.reshape(bp, tp, ML_COLS), gates3, gate_bias, mlw)
        hs, p_h = _ssd_prompt(ssd3, gates3, gate_bias, alog_row, cw, cb, d_row, ssw)
        ho, p_s = _hgrn_prompt(hg.reshape(bp, tp, HG_COLS), lbp, hgw, l)
        xp, p_fg, p_fv = _ffn(xp, hm.reshape(bp * tp, ML_W), hs.reshape(bp * tp, SSD_W), ho.reshape(bp * tp, HG_W),
                              wo_b, n2, wu_b, fcw, fcb, wd_b, zero_f, fin_w,
                              groups=bp, tm=ROW_TILE, shift=1, jb=D_FF // FF_BLOCK, final=last)
        p_states.append((p_c, p_n, p_m[:, 0, 0:ML_HEADS],
                         p_h.reshape(bp, SSD_HEADS, HEAD_W, SSD_DSTATE),
                         ssd3[:, tp - (SSD_CONV - 1):, SSD_W:], p_s,
                         jnp.concatenate([p_fg, p_fv], axis=-1)))

        ml, ssd, hg, gates = _inproj(xs, n1, w_cat)
        c0 = state_mlstm_C[l].astype(f32).reshape(nb, ML_HEADS * HEAD_W * HEAD_W)
        n0 = state_mlstm_n[l].astype(f32).reshape(nb, ML_W)
        m0 = jnp.pad(state_mlstm_m[l].astype(f32).T, ((0, 8 - ML_HEADS), (0, 0)))
        hm, s_c, s_n, s_m = _mlstm_sample(ml, gates, gate_bias, mlw, c0, n0, m0)
        halo = jnp.transpose(state_ssd_conv[l].astype(f32), (1, 0, 2)).reshape((SSD_CONV - 1) * nb, SSD_CONV_DIM)
        xc = _ssd_conv_sample(ssd, halo, cw, cb)
        head_params = jnp.broadcast_to(
            jnp.stack([ssd_A_log[l], ssd_D[l]]).astype(f32).reshape(2, SSD_HEADS // 2, 2, 1),
            (2, SSD_HEADS // 2, 2, LANES))
        h0 = state_ssd[l].astype(f32).reshape(nb, SSD_HEADS * HEAD_W * SSD_DSTATE)
        hs, s_h = _ssd_sample(ssd, xc, gates, gate_bias, head_params, ssw, h0)
        s0 = state_hgrn[l].astype(f32).reshape(nb, HG_HEADS * HEAD_W * HEAD_W)
        ho, s_s = _hgrn_sample(hg, lbp, hgw, s0, l)
        f0 = jnp.transpose(state_ffn_conv[l].astype(f32), (1, 0, 2)).reshape(1, (FFN_CONV - 1) * nb, 2 * D_FF)
        xs, s_fg, s_fv = _ffn(xs, hm, hs, ho, wo_b, n2, wu_b, fcw, fcb, wd_b, f0, fin_w,
                              groups=1, tm=ts * nb, shift=nb, jb=1, final=last)
        s_f = jnp.concatenate([s_fg, s_fv], axis=-1).reshape(FFN_CONV - 1, nb, 2 * D_FF)
        raw_xbc = ssd[:, SSD_W:].reshape(ts, nb, SSD_CONV_DIM)
        s_states.append((s_c.reshape(nb, ML_HEADS, HEAD_W, HEAD_W), s_n.reshape(nb, ML_HEADS, HEAD_W),
                         s_m[0:ML_HEADS].T,
                         s_h.reshape(nb, SSD_HEADS, HEAD_W, SSD_DSTATE),
                         jnp.transpose(raw_xbc[ts - (SSD_CONV - 1):], (1, 0, 2)),
                         s_s.reshape(nb, HG_HEADS, HEAD_W, HEAD_W),
                         jnp.transpose(s_f, (1, 0, 2))))

    y_prompt = xp.reshape(bp, tp, D_MODEL)
    y_sample = jnp.transpose(xs.reshape(ts, nb, D_MODEL), (1, 0, 2))
    p_out = [jnp.stack([st[i] for st in p_states]) for i in range(7)]
    s_out = [jnp.stack([st[i] for st in s_states]) for i in range(7)]
    return (y_prompt, y_sample, *p_out, *s_out)
```

```python
import functools

import jax
import jax.numpy as jnp
from jax import lax
from jax.experimental import pallas as pl
from jax.experimental.pallas import tpu as pltpu

f32 = jnp.float32
bf16 = jnp.bfloat16

D_MODEL = 1024
DEPTH = 2
ML_HEADS = 6
HEAD_W = 64
ML_W = 384
SSD_HEADS = 6
SSD_W = 384
SSD_DSTATE = 128
SSD_CONV = 4
SSD_CONV_DIM = 896
HG_HEADS = 4
HG_W = 256
D_FF = 2816
FFN_CONV = 3
EPS = 1e-6
PROMPT_CHUNK = 128
HG_CHUNK = 64
LANES = 128
FF_BLOCK = 256
ROW_TILE = 512
GATE_W = 128
W_IN_COLS = 4 * ML_W + (SSD_W + SSD_CONV_DIM) + 4 * HG_W + GATE_W
VMEM_LIMIT_BYTES = 56 * 1024 * 1024

NEG_INF = float("-inf")


def _cparams(*sem):
    return pltpu.CompilerParams(dimension_semantics=sem, vmem_limit_bytes=VMEM_LIMIT_BYTES)


def _resident(shape):
    nd = len(shape)
    return pl.BlockSpec(shape, lambda *_: (0,) * nd, pipeline_mode=pl.Buffered(1))


def _layer_resident(shape, layer):
    nd = len(shape)
    return pl.BlockSpec((None,) + tuple(shape), lambda *_: (layer,) + (0,) * nd, pipeline_mode=pl.Buffered(1))


def _unfetched():
    return pl.BlockSpec(memory_space=pl.ANY)


def _dot(a, b):
    return jnp.dot(a, b, preferred_element_type=f32)


def _dot_nt(a, b):
    return lax.dot_general(a, b, (((1,), (1,)), ((), ())), preferred_element_type=f32)


def _split_bf16(x, parts):
    out = []
    r = x
    for _ in range(parts):
        h = r.astype(bf16)
        out.append(h)
        r = r - h.astype(f32)
    return out


def _cumsum_rows(x, tri_b):
    acc = None
    for part in _split_bf16(x, 3):
        d = _dot(tri_b, part)
        acc = d if acc is None else acc + d
    return acc


def _group_mean(x, bd_b):
    hi, lo = _split_bf16(x, 2)
    return (_dot(hi, bd_b) + _dot(lo, bd_b)) * (1.0 / HEAD_W)


def _iota2(shape, axis):
    return lax.broadcasted_iota(jnp.int32, shape, axis)


def _ones_where(mask):
    return jnp.where(mask, 1.0, 0.0).astype(bf16)


def _same_head(a, b):
    return jnp.right_shift(a, 6) == jnp.right_shift(b, 6)


def _silu(x):
    return x * jax.nn.sigmoid(x)


ML_COLS = 4 * ML_W
SSD_COLS = SSD_W + SSD_CONV_DIM
HG_COLS = 4 * HG_W


def _inproj_body(x_ref, nw_ref, w_ref, ml_ref, ssd_ref, hg_ref, g_ref):
    x = x_ref[...]
    h = (x * lax.rsqrt(jnp.mean(x * x, axis=-1, keepdims=True) + EPS) * nw_ref[...]).astype(bf16)
    o0, o1, o2 = ML_COLS, ML_COLS + SSD_COLS, ML_COLS + SSD_COLS + HG_COLS
    ml_ref[...] = _dot(h, w_ref[:, 0:o0])
    ssd_ref[...] = _dot(h, w_ref[:, o0:o1])
    hg_ref[...] = _dot(h, w_ref[:, o1:o2])
    g_ref[...] = _dot(h, w_ref[:, o2:W_IN_COLS])


def _inproj(x, norm_w, w_cat, layer):
    n = x.shape[0]
    tm = ROW_TILE
    row = lambda w: pl.BlockSpec((tm, w), lambda i: (i, 0))
    return pl.pallas_call(
        _inproj_body,
        grid=(n // tm,),
        in_specs=[row(D_MODEL), _layer_resident((1, D_MODEL), layer), _layer_resident((D_MODEL, W_IN_COLS), layer)],
        out_specs=[row(ML_COLS), row(SSD_COLS), row(HG_COLS), row(GATE_W)],
        out_shape=[jax.ShapeDtypeStruct((n, w), f32) for w in (ML_COLS, SSD_COLS, HG_COLS, GATE_W)],
        compiler_params=_cparams("arbitrary"),
        name="inproj",
    )(x, norm_w, w_cat)


W_IN_GATE_IF = ML_COLS
W_IN_SSD = W_IN_GATE_IF + 2 * ML_HEADS
W_IN_DT = W_IN_SSD + SSD_COLS
W_IN_HG = W_IN_DT + SSD_HEADS
W_IN_DIM = W_IN_HG + HG_COLS
W_PREP_ROWS = 256


def _prep_w_in_body(w_ref, out_ref):
    lane = _iota2((W_PREP_ROWS, LANES), 1)
    tile = lambda k: w_ref[:, k * LANES:(k + 1) * LANES]
    out_ref[:, 0:ML_COLS] = w_ref[:, 0:ML_COLS].astype(bf16)

    def shifted(dst0, src0, ntiles):
        t0, sh = src0 // LANES, src0 % LANES
        rolled = [pltpu.roll(tile(t0 + k), LANES - sh, axis=1) for k in range(ntiles + 1)]
        for k in range(ntiles):
            out_ref[:, dst0 + k * LANES:dst0 + (k + 1) * LANES] = jnp.where(
                lane < LANES - sh, rolled[k], rolled[k + 1]).astype(bf16)

    shifted(ML_COLS, W_IN_SSD, SSD_COLS // LANES)
    shifted(ML_COLS + SSD_COLS, W_IN_HG, HG_COLS // LANES)
    assert W_IN_GATE_IF % LANES == 0 and W_IN_DT % LANES == DT_LANE
    gates = jnp.where(lane < DT_LANE, tile(W_IN_GATE_IF // LANES),
                      jnp.where(lane < DT_LANE + SSD_HEADS, tile(W_IN_DT // LANES), 0.0))
    out_ref[:, ML_COLS + SSD_COLS + HG_COLS:W_IN_COLS] = gates.astype(bf16)


def _prep_w_in(w_in):
    depth, d, n = w_in.shape
    assert n == W_IN_DIM and d % W_PREP_ROWS == 0
    return pl.pallas_call(
        _prep_w_in_body,
        grid=(depth, d // W_PREP_ROWS),
        in_specs=[pl.BlockSpec((None, W_PREP_ROWS, W_IN_COLS), lambda l, i: (l, i, 0))],
        out_specs=pl.BlockSpec((None, W_PREP_ROWS, W_IN_COLS), lambda l, i: (l, i, 0)),
        out_shape=jax.ShapeDtypeStruct((depth, d, W_IN_COLS), bf16),
        compiler_params=_cparams("arbitrary", "arbitrary"),
        name="prep_w_in",
    )(w_in)


def _mlstm_prompt_body(ml_ref, g_ref, gb_ref, nw_ref, out_ref, c1_ref, n1_ref, m1_ref, cext_s, m_s):
    c = pl.program_id(1)
    L = PROMPT_CHUNK

    @pl.when(c == 0)
    def _():
        cext_s[...] = jnp.zeros(cext_s.shape, f32)
        m_s[...] = jnp.zeros(m_s.shape, f32)

    row = _iota2((L, L), 0)
    col = _iota2((L, L), 1)
    tril = col <= row
    lo_half = col < HEAD_W
    tri_b = _ones_where(tril)
    bd_b = _ones_where(_same_head(row, col))
    lane2 = _iota2((L, 2 * LANES), 1)
    row2 = _iota2((L, 2 * LANES), 0)
    rowc = _iota2((L, 1), 0)
    ones01 = jnp.where(col < 2, 1.0, 0.0)
    one0 = jnp.where(col == 0, 1.0, 0.0)
    one1 = jnp.where(col == 1, 1.0, 0.0)
    mask_ext = jnp.where(lane2 < LANES, jnp.where(_same_head(row2, lane2), 1.0, 0.0),
                         jnp.where(lane2 == LANES, jnp.where(row2 < HEAD_W, 1.0, 0.0),
                                   jnp.where(lane2 == LANES + 1, jnp.where(row2 >= HEAD_W, 1.0, 0.0), 0.0)))

    g = g_ref[0] + gb_ref[...]
    bcum = _cumsum_rows(jax.nn.log_sigmoid(g), tri_b)
    gT = g.T
    bT = bcum.T

    for p in range(ML_HEADS // 2):
        sl = slice(p * LANES, (p + 1) * LANES)
        q2 = ml_ref[0, :, sl]
        k2 = ml_ref[0, :, ML_W + p * LANES:ML_W + (p + 1) * LANES] * (HEAD_W ** -0.5)
        v2 = ml_ref[0, :, 2 * ML_W + p * LANES:2 * ML_W + (p + 1) * LANES]
        o2 = ml_ref[0, :, 3 * ML_W + p * LANES:3 * ML_W + (p + 1) * LANES]
        kb = k2.astype(bf16)
        q_sel = (jnp.where(lo_half, q2, 0.0).astype(bf16), jnp.where(lo_half, 0.0, q2).astype(bf16))
        sw, a_int, e_m, a_st, w_s = [], [], [], [], []
        for j in range(2):
            h = 2 * p + j
            b_col = bcum[:, ML_HEADS + h:ML_HEADS + h + 1]
            b_row = bT[ML_HEADS + h:ML_HEADS + h + 1, :]
            i_row = gT[h:h + 1, :]
            i_col = g[:, h:h + 1]
            m_prev = m_s[0:1, h:h + 1]
            dm = jnp.where(tril, b_col - b_row + i_row, NEG_INF)
            inter = b_col + m_prev
            m_t = jnp.maximum(inter, jnp.max(dm, axis=1, keepdims=True))
            w = jnp.exp(dm - m_t)
            s = _dot_nt(q_sel[j], kb)
            sw.append((s * w).astype(bf16))
            a_int.append(jnp.exp(inter - m_t))
            e_m.append(jnp.exp(-m_t))
            m_new = m_t[L - 1:L, :]
            b_last = b_col[L - 1:L, :]
            a_st.append(jnp.exp(b_last + m_prev - m_new))
            w_s.append(jnp.exp(b_last - b_col + i_col - m_new))
            m_s[0:1, h:h + 1] = m_new
        vtop = jnp.concatenate([jnp.where(lo_half, v2, 0.0), one0], axis=1).astype(bf16)
        vbot = jnp.concatenate([jnp.where(lo_half, 0.0, v2), one1], axis=1).astype(bf16)
        cext = cext_s[p]
        a_ext = jnp.where(lane2 < HEAD_W, a_int[0],
                          jnp.where(lane2 < LANES, a_int[1], jnp.where(lane2 == LANES, a_int[0], a_int[1])))
        full = _dot(sw[0], vtop) + _dot(sw[1], vbot) + a_ext * _dot(q2.astype(bf16), cext.astype(bf16))
        num = full[:, 0:LANES]
        den0 = full[:, LANES:LANES + 1]
        den1 = full[:, LANES + 1:LANES + 2]
        dd = jnp.where(lo_half, jnp.maximum(jnp.abs(den0), e_m[0]), jnp.maximum(jnp.abs(den1), e_m[1]))
        hh = num / dd
        ms = _group_mean(hh * hh, bd_b)
        y = hh * lax.rsqrt(ms + EPS) * nw_ref[:, sl] * jax.nn.sigmoid(o2)
        out_ref[0, :, sl] = y.astype(bf16)
        kw_t = (k2 * jnp.where(lo_half, w_s[0], w_s[1])).T.astype(bf16)
        vones = jnp.concatenate([v2, ones01], axis=1).astype(bf16)
        upd = _dot(kw_t, vones)
        a_rows = jnp.where(rowc < HEAD_W, a_st[0], a_st[1])
        cext_s[p] = a_rows * cext + mask_ext * upd

    @pl.when(c == pl.num_programs(1) - 1)
    def _():
        for p in range(ML_HEADS // 2):
            cext = cext_s[p]
            c1_ref[0, 2 * p] = cext[0:HEAD_W, 0:HEAD_W]
            c1_ref[0, 2 * p + 1] = cext[HEAD_W:LANES, HEAD_W:LANES]
            n_t = cext[:, LANES:2 * LANES].T
            n1_ref[0, 2 * p:2 * p + 1, :] = n_t[0:1, 0:HEAD_W]
            n1_ref[0, 2 * p + 1:2 * p + 2, :] = n_t[1:2, HEAD_W:LANES]
        m1_ref[0] = m_s[...]


def _mlstm_prompt(ml, gates, gate_bias, norm_w, layer):
    b, t, _ = ml.shape
    L = PROMPT_CHUNK
    return pl.pallas_call(
        _mlstm_prompt_body,
        grid=(b, t // L),
        in_specs=[pl.BlockSpec((1, L, ML_COLS), lambda i, c: (i, c, 0)),
                  pl.BlockSpec((1, L, GATE_W), lambda i, c: (i, c, 0)),
                  _layer_resident((1, GATE_W), layer), _layer_resident((1, ML_W), layer)],
        out_specs=[pl.BlockSpec((1, L, ML_W), lambda i, c: (i, c, 0)),
                   pl.BlockSpec((1, ML_HEADS, HEAD_W, HEAD_W), lambda i, c: (i, 0, 0, 0)),
                   pl.BlockSpec((1, ML_HEADS, HEAD_W), lambda i, c: (i, 0, 0)),
                   pl.BlockSpec((1, 1, GATE_W), lambda i, c: (i, 0, 0))],
        out_shape=[jax.ShapeDtypeStruct((b, t, ML_W), bf16),
                   jax.ShapeDtypeStruct((b, ML_HEADS, HEAD_W, HEAD_W), f32),
                   jax.ShapeDtypeStruct((b, ML_HEADS, HEAD_W), f32),
                   jax.ShapeDtypeStruct((b, 1, GATE_W), f32)],
        scratch_shapes=[pltpu.VMEM((ML_HEADS // 2, LANES, 2 * LANES), f32), pltpu.VMEM((1, GATE_W), f32)],
        compiler_params=_cparams("arbitrary", "arbitrary"),
        name="mlstm_prompt",
    )(ml, gates, gate_bias, norm_w)


DT_LANE = 2 * ML_HEADS
HALO = 8


def _ssd_prompt_body(ssd_ref, g_ref, gb_ref, al_ref, cw_ref, cb_ref, d_ref, nw_ref, out_ref, h1_ref,
                     xbuf_s, h_s):
    c = pl.program_id(1)
    L = PROMPT_CHUNK

    @pl.when(c == 0)
    def _():
        xbuf_s[0:HALO, :] = jnp.zeros((HALO, SSD_CONV_DIM), f32)
        h_s[...] = jnp.zeros(h_s.shape, f32)

    xbuf_s[HALO:HALO + L, :] = ssd_ref[0, :, SSD_W:SSD_COLS]
    conv = cb_ref[...]
    for j in range(SSD_CONV):
        o = HALO - (SSD_CONV - 1) + j
        conv = conv + cw_ref[j:j + 1, :] * xbuf_s[o:o + L, :]
    xbuf_s[0:HALO, :] = xbuf_s[L:L + HALO, :]
    xc = _silu(conv)
    nb = SSD_W
    b_b = [xc[:, nb + gi * LANES:nb + (gi + 1) * LANES].astype(bf16) for gi in range(2)]
    c_b = [xc[:, nb + (2 + gi) * LANES:nb + (3 + gi) * LANES].astype(bf16) for gi in range(2)]

    row = _iota2((L, L), 0)
    col = _iota2((L, L), 1)
    tril = col <= row
    lo_half = col < HEAD_W
    tri_b = _ones_where(tril)
    bd_b = _ones_where(_same_head(row, col))
    rowc = _iota2((L, 1), 0)

    g = g_ref[0] + gb_ref[...]
    dt = jax.nn.softplus(g)
    cum = _cumsum_rows(dt * (-jnp.exp(al_ref[...])), tri_b)
    cum_t = cum.T
    dt_t = dt.T
    cb_g = [_dot_nt(c_b[gi], b_b[gi]) for gi in range(2)]

    for p in range(SSD_HEADS // 2):
        sl = slice(p * LANES, (p + 1) * LANES)
        x2 = xc[:, sl]
        z2 = ssd_ref[0, :, sl]
        hp = h_s[p]
        hb = hp.astype(bf16)
        grp = [(2 * p + j) // (SSD_HEADS // 2) for j in range(2)]
        ys = None
        ecum, w_s, a_last, ch = [], [], [], []
        for j in range(2):
            ln = DT_LANE + 2 * p + j
            cum_col = cum[:, ln:ln + 1]
            cum_row = cum_t[ln:ln + 1, :]
            cum_last = cum_col[L - 1:L, :]
            lm = jnp.exp(jnp.where(tril, cum_col - cum_row, NEG_INF))
            m = (cb_g[grp[j]] * lm * dt_t[ln:ln + 1, :]).astype(bf16)
            xm = jnp.where(lo_half, x2, 0.0) if j == 0 else jnp.where(lo_half, 0.0, x2)
            d = _dot(m, xm.astype(bf16))
            ys = d if ys is None else ys + d
            ecum.append(jnp.exp(cum_col))
            w_s.append(jnp.exp(cum_last - cum_col) * dt[:, ln:ln + 1])
            a_last.append(jnp.exp(cum_last))
            if j == 0 or grp[1] != grp[0]:
                ch.append(_dot_nt(c_b[grp[j]], hb))
            else:
                ch.append(ch[0])
        ys = ys + jnp.where(lo_half, ecum[0] * ch[0], ecum[1] * ch[1])
        xw_t = (x2 * jnp.where(lo_half, w_s[0], w_s[1])).T.astype(bf16)
        if grp[0] == grp[1]:
            upd = _dot(xw_t, b_b[grp[0]])
        else:
            upd = jnp.where(row < HEAD_W, _dot(xw_t, b_b[grp[0]]), _dot(xw_t, b_b[grp[1]]))
        h_s[p] = jnp.where(rowc < HEAD_W, a_last[0], a_last[1]) * hp + upd
        yy = (ys + d_ref[:, sl] * x2) * _silu(z2)
        ms = _group_mean(yy * yy, bd_b)
        out_ref[0, :, sl] = (yy * lax.rsqrt(ms + EPS) * nw_ref[:, sl]).astype(bf16)

    @pl.when(c == pl.num_programs(1) - 1)
    def _():
        h1_ref[0] = h_s[...]


def _ssd_prompt(ssd, gates, gate_bias, alog_row, conv_w, conv_b, d_row, norm_w, layer):
    b, t, _ = ssd.shape
    L = PROMPT_CHUNK
    npair = SSD_HEADS // 2
    return pl.pallas_call(
        _ssd_prompt_body,
        grid=(b, t // L),
        in_specs=[pl.BlockSpec((1, L, SSD_COLS), lambda i, c: (i, c, 0)),
                  pl.BlockSpec((1, L, GATE_W), lambda i, c: (i, c, 0)),
                  _layer_resident((1, GATE_W), layer), _layer_resident((1, GATE_W), layer),
                  _layer_resident((SSD_CONV, SSD_CONV_DIM), layer), _layer_resident((1, SSD_CONV_DIM), layer),
                  _layer_resident((1, SSD_W), layer), _layer_resident((1, SSD_W), layer)],
        out_specs=[pl.BlockSpec((1, L, SSD_W), lambda i, c: (i, c, 0)),
                   pl.BlockSpec((1, npair, LANES, SSD_DSTATE), lambda i, c: (i, 0, 0, 0))],
        out_shape=[jax.ShapeDtypeStruct((b, t, SSD_W), bf16),
                   jax.ShapeDtypeStruct((b, npair, LANES, SSD_DSTATE), f32)],
        scratch_shapes=[pltpu.VMEM((HALO + L, SSD_CONV_DIM), f32), pltpu.VMEM((npair, LANES, SSD_DSTATE), f32)],
        compiler_params=_cparams("arbitrary", "arbitrary"),
        name="ssd_prompt",
    )(ssd, gates, gate_bias, alog_row, conv_w, conv_b, d_row, norm_w)


def _hg_lower_bound(lbp, layer):
    mx = jnp.max(lbp, axis=0, keepdims=True)
    e = jnp.exp(lbp - mx)
    den = jnp.sum(e, axis=0, keepdims=True)
    lb = jnp.zeros_like(den)
    for j in range(1, layer + 1):
        lb = lb + e[j:j + 1, :] / den
    return lb


def _hg_rows(L):
    offs, off = [], 0
    for s in range(L):
        offs.append(off)
        off += L - 8 * (s // 8)
    return offs, off


def _hgrn_prompt_body(hg_ref, lbp_ref, nw_ref, out_ref, s1_ref, sbd_s, p_s, r_s, o_s, g_s, k_s, v_s, q_s, *, layer):
    c = pl.program_id(1)
    L = HG_CHUNK

    @pl.when(c == 0)
    def _():
        sbd_s[...] = jnp.zeros(sbd_s.shape, f32)

    lb = _hg_lower_bound(lbp_ref[...], layer)
    ff = hg_ref[0, :, HG_W:2 * HG_W]
    sig = jax.nn.sigmoid(ff)
    k_s[...] = (1.0 - lb) * jax.nn.sigmoid(-ff)
    v_s[...] = hg_ref[0, :, 2 * HG_W:3 * HG_W]
    q_s[...] = hg_ref[0, :, 0:HG_W] * (HEAD_W ** -0.5)
    row = _iota2((L, L), 0)
    col = _iota2((L, L), 1)
    tri_b = _ones_where(col <= row)
    g_s[...] = _cumsum_rows(jnp.log(lb + (1.0 - lb) * sig), tri_b)

    r2 = _iota2((2 * LANES, 2 * LANES), 0)
    c2 = _iota2((2 * LANES, 2 * LANES), 1)
    bd4_b = _ones_where(_same_head(r2, c2))
    r1 = _iota2((LANES, LANES), 0)
    c1 = _iota2((LANES, LANES), 1)
    bd_mask = _same_head(r1, c1)
    bd_b = _ones_where(bd_mask)

    offs, total = _hg_rows(L)
    for s in range(L):
        t0 = 8 * (s // 8)
        n = L - t0
        tt = _iota2((n, HG_W), 0) + t0
        e = jnp.exp(jnp.where(tt >= s, g_s[t0:L, :] - g_s[s:s + 1, :], NEG_INF))
        p_s[offs[s]:offs[s] + n, :] = q_s[t0:L, :] * k_s[s:s + 1, :] * e
    step = 576
    for r0 in range(0, total, step):
        r_s[r0:r0 + step, :] = _dot(p_s[r0:r0 + step, :].astype(bf16), bd4_b)

    gg = g_s[...]
    qd = q_s[...] * jnp.exp(gg)
    for p in range(HG_HEADS // 2):
        sl = slice(p * LANES, (p + 1) * LANES)
        o_s[:, sl] = _dot(qd[:, sl].astype(bf16), sbd_s[p].astype(bf16))
    for s in range(L):
        t0 = 8 * (s // 8)
        n = L - t0
        o_s[t0:L, :] = o_s[t0:L, :] + r_s[offs[s]:offs[s] + n, :] * v_s[s:s + 1, :]

    g_last = gg[L - 1:L, :]
    kd = k_s[...] * jnp.exp(g_last - gg)
    vv = v_s[...]
    zpad = jnp.zeros((LANES - L, LANES), f32)
    og = o_s[...]
    gate = hg_ref[0, :, 3 * HG_W:4 * HG_W]
    for p in range(HG_HEADS // 2):
        sl = slice(p * LANES, (p + 1) * LANES)
        kd_t = jnp.concatenate([kd[:, sl], zpad], axis=0).T.astype(bf16)
        v_pad = jnp.concatenate([vv[:, sl], zpad], axis=0).astype(bf16)
        g_t = jnp.concatenate([gg[:, sl], zpad], axis=0).T
        dec = jnp.exp(g_t[:, L - 1:L])
        sbd_s[p] = dec * sbd_s[p] + jnp.where(bd_mask, _dot(kd_t, v_pad), 0.0)
        oo = og[:, sl]
        ms = _group_mean(oo * oo, bd_b)
        out_ref[0, :, sl] = (oo * lax.rsqrt(ms + EPS) * nw_ref[:, sl] * _silu(gate[:, sl])).astype(bf16)

    @pl.when(c == pl.num_programs(1) - 1)
    def _():
        for p in range(HG_HEADS // 2):
            sbd = sbd_s[p]
            s1_ref[0, 2 * p] = sbd[0:HEAD_W, 0:HEAD_W]
            s1_ref[0, 2 * p + 1] = sbd[HEAD_W:LANES, HEAD_W:LANES]


def _hgrn_prompt(hg, lbp, norm_w, layer):
    b, t, _ = hg.shape
    L = HG_CHUNK
    _, total = _hg_rows(L)
    return pl.pallas_call(
        functools.partial(_hgrn_prompt_body, layer=layer),
        grid=(b, t // L),
        in_specs=[pl.BlockSpec((1, L, HG_COLS), lambda i, c: (i, c, 0)),
                  _resident((DEPTH, HG_W)), _layer_resident((1, HG_W), layer)],
        out_specs=[pl.BlockSpec((1, L, HG_W), lambda i, c: (i, c, 0)),
                   pl.BlockSpec((1, HG_HEADS, HEAD_W, HEAD_W), lambda i, c: (i, 0, 0, 0))],
        out_shape=[jax.ShapeDtypeStruct((b, t, HG_W), bf16),
                   jax.ShapeDtypeStruct((b, HG_HEADS, HEAD_W, HEAD_W), f32)],
        scratch_shapes=[pltpu.VMEM((HG_HEADS // 2, LANES, LANES), f32),
                        pltpu.VMEM((total, HG_W), f32), pltpu.VMEM((total, HG_W), f32),
                        pltpu.VMEM((L, HG_W), f32), pltpu.VMEM((L, HG_W), f32), pltpu.VMEM((L, HG_W), f32),
                        pltpu.VMEM((L, HG_W), f32), pltpu.VMEM((L, HG_W), f32)],
        compiler_params=_cparams("arbitrary", "arbitrary"),
        name="hgrn_prompt",
    )(hg, lbp, norm_w)


NB = 128
TS = 8


def _sample_call(body, name, grid, in_specs, operands, out_specs, out_shapes, scratch, prev):
    n_in = len(operands)
    aliases = {}
    if prev is not None:
        in_specs = list(in_specs) + [_unfetched() for _ in prev]
        operands = list(operands) + list(prev)
        aliases = {n_in + i: 1 + i for i in range(len(prev))}
    n_prev = 0 if prev is None else len(prev)

    def wrapped(*refs):
        body(*refs[:n_in], *refs[n_in + n_prev:])

    return pl.pallas_call(
        wrapped, grid=grid, in_specs=in_specs, out_specs=out_specs, out_shape=out_shapes,
        scratch_shapes=scratch, input_output_aliases=aliases,
        compiler_params=_cparams(*(["arbitrary"] * len(grid))), name=name,
    )(*operands)


def _mlstm_sample_body(q_ref, k_ref, v_ref, o_ref, g_ref, gb_ref, nw_ref, c0_ref, n0_ref, m0_ref,
                       out_ref, c1_ref, n1_ref, m1_ref, qt_s, kt_s, vt_s, gt_s, ht_s):
    p = pl.program_id(0)
    n_cur = [n0_ref[hl] for hl in range(2)]
    m_cur = [m0_ref[pl.ds(2 * p + hl, 1), :] for hl in range(2)]

    for t in range(TS):
        c_src = c0_ref if t == 0 else c1_ref
        rows = slice(t * NB, (t + 1) * NB)
        qt_s[...] = q_ref[rows, :].T
        kt_s[...] = k_ref[rows, :].T * (HEAD_W ** -0.5)
        vt_s[...] = v_ref[rows, :].T
        gt_s[...] = (g_ref[rows, :] + gb_ref[...]).T
        for hl in range(2):
            h = 2 * p + hl
            hs = slice(hl * HEAD_W, (hl + 1) * HEAD_W)
            i_t = gt_s[pl.ds(h, 1), :]
            lf = jax.nn.log_sigmoid(gt_s[pl.ds(ML_HEADS + h, 1), :])
            m_new = jnp.maximum(lf + m_cur[hl], i_t)
            a = jnp.exp(lf + m_cur[hl] - m_new)
            w = jnp.exp(i_t - m_new)
            n_new = a * n_cur[hl] + w * kt_s[hs, :]
            v_h = vt_s[hs, :]

            def body(d, num, hl=hl, a=a, w=w, v_h=v_h, c_src=c_src):
                kd = kt_s[pl.ds(hl * HEAD_W + d, 1), :] * w
                qd = qt_s[pl.ds(hl * HEAD_W + d, 1), :]
                cd = a * c_src[hl, d] + kd * v_h
                c1_ref[hl, d] = cd
                return num + qd * cd

            num = lax.fori_loop(0, HEAD_W, body, jnp.zeros((HEAD_W, NB), f32), unroll=4)
            den = jnp.sum(qt_s[hs, :] * n_new, axis=0, keepdims=True)
            hh = num / jnp.maximum(jnp.abs(den), jnp.exp(-m_new))
            ms = jnp.mean(hh * hh, axis=0, keepdims=True)
            ht_s[hs, :] = hh * lax.rsqrt(ms + EPS)
            n_cur[hl] = n_new
            m_cur[hl] = m_new
        out_ref[rows, :] = (ht_s[...].T * nw_ref[...] * jax.nn.sigmoid(o_ref[rows, :])).astype(bf16)

    for hl in range(2):
        n1_ref[hl] = n_cur[hl]
        m1_ref[pl.ds(2 * p + hl, 1), :] = m_cur[hl]


def _mlstm_sample(ml, gates, gate_bias, norm_w, c0t, n0t, m0t, layer, prev):
    n = ml.shape[0]
    npair = ML_HEADS // 2
    blk = lambda off: pl.BlockSpec((n, LANES), lambda p, off=off: (0, off + p))
    c_spec = pl.BlockSpec((None, 2, HEAD_W, HEAD_W, NB), lambda p: (layer, p, 0, 0, 0))
    n_spec = pl.BlockSpec((None, 2, HEAD_W, NB), lambda p: (layer, p, 0, 0))
    m_spec = pl.BlockSpec((None, ML_HEADS, NB), lambda p: (layer, 0, 0))
    return _sample_call(
        _mlstm_sample_body, "mlstm_sample", (npair,),
        [blk(0), blk(npair), blk(2 * npair), blk(3 * npair), _resident((n, GATE_W)),
         _layer_resident((1, GATE_W), layer), pl.BlockSpec((None, 1, LANES), lambda p: (layer, 0, p)),
         c_spec, n_spec, m_spec],
        [ml, ml, ml, ml, gates, gate_bias, norm_w, c0t, n0t, m0t],
        [pl.BlockSpec((n, LANES), lambda p: (0, p)), c_spec, n_spec, m_spec],
        [jax.ShapeDtypeStruct((n, ML_W), bf16), jax.ShapeDtypeStruct(c0t.shape, f32),
         jax.ShapeDtypeStruct(n0t.shape, f32), jax.ShapeDtypeStruct(m0t.shape, f32)],
        [pltpu.VMEM((LANES, NB), f32)] * 5, prev)


def _ssd_conv_sample_body(ssd_ref, halo_ref, cw_ref, cb_ref, out_ref, xp_s):
    nh = (SSD_CONV - 1) * NB
    n = TS * NB
    xp_s[0:nh, :] = halo_ref[...]
    xp_s[nh:nh + n, :] = ssd_ref[:, SSD_W:SSD_COLS]
    conv = cb_ref[...]
    for j in range(SSD_CONV):
        conv = conv + cw_ref[j:j + 1, :] * xp_s[j * NB:j * NB + n, :]
    out_ref[...] = _silu(conv)


def _ssd_conv_sample(ssd, halo, conv_w, conv_b, layer):
    n = ssd.shape[0]
    nh = (SSD_CONV - 1) * NB
    return pl.pallas_call(
        _ssd_conv_sample_body,
        grid=(1,),
        in_specs=[_resident((n, SSD_COLS)), _layer_resident((nh, SSD_CONV_DIM), layer),
                  _layer_resident((SSD_CONV, SSD_CONV_DIM), layer), _layer_resident((1, SSD_CONV_DIM), layer)],
        out_specs=pl.BlockSpec((n, SSD_CONV_DIM), lambda i: (0, 0)),
        out_shape=jax.ShapeDtypeStruct((n, SSD_CONV_DIM), f32),
        scratch_shapes=[pltpu.VMEM((nh + n, SSD_CONV_DIM), f32)],
        compiler_params=_cparams("arbitrary"),
        name="ssd_conv_sample",
    )(ssd, halo, conv_w, conv_b)


def _ssd_sample_body(z_ref, x_ref, b0_ref, b1_ref, c0_ref, c1_ref, g_ref, gb_ref, hp_ref, nw_ref, h0_ref,
                     out_ref, h1_ref, xt_s, bt_s, ct_s, gt_s, yt_s):
    p = pl.program_id(0)
    b_refs = (b0_ref, b1_ref)
    c_refs = (c0_ref, c1_ref)

    for t in range(TS):
        h_src = h0_ref if t == 0 else h1_ref
        rows = slice(t * NB, (t + 1) * NB)
        xt_s[...] = x_ref[rows, :].T
        gt_s[...] = (g_ref[rows, :] + gb_ref[...]).T
        zt = z_ref[rows, :].T
        for hl in range(2):
            bt_s[hl] = b_refs[hl][rows, :].T
            ct_s[hl] = c_refs[hl][rows, :].T
        for hl in range(2):
            h = 2 * p + hl
            hs = slice(hl * HEAD_W, (hl + 1) * HEAD_W)
            dt = jax.nn.softplus(gt_s[pl.ds(DT_LANE + h, 1), :])
            da = jnp.exp(dt * (-jnp.exp(hp_ref[0, hl:hl + 1, :])))

            def body(pp, carry, hl=hl, dt=dt, da=da, h_src=h_src):
                xrow = xt_s[pl.ds(hl * HEAD_W + pp, 1), :] * dt
                hn = da * h_src[hl, pp] + xrow * bt_s[hl]
                h1_ref[hl, pp] = hn
                yt_s[pl.ds(hl * HEAD_W + pp, 1), :] = jnp.sum(ct_s[hl] * hn, axis=0, keepdims=True)
                return carry

            lax.fori_loop(0, HEAD_W, body, 0, unroll=2)
            yy = (yt_s[hs, :] + hp_ref[1, hl:hl + 1, :] * xt_s[hs, :]) * _silu(zt[hs, :])
            ms = jnp.mean(yy * yy, axis=0, keepdims=True)
            yt_s[hs, :] = yy * lax.rsqrt(ms + EPS)
        out_ref[rows, :] = (yt_s[...].T * nw_ref[...]).astype(bf16)


def _ssd_sample(ssd, xc, gates, gate_bias, head_params, norm_w, h0t, layer, prev):
    n = ssd.shape[0]
    npair = SSD_HEADS // 2
    hpg = SSD_HEADS // 2
    nxb = SSD_W // LANES
    blk = lambda f: pl.BlockSpec((n, LANES), f)
    h_spec = pl.BlockSpec((None, 2, HEAD_W, SSD_DSTATE, NB), lambda p: (layer, p, 0, 0, 0))
    return _sample_call(
        _ssd_sample_body, "ssd_sample", (npair,),
        [blk(lambda p: (0, p)), blk(lambda p: (0, p)),
         blk(lambda p: (0, nxb + (2 * p) // hpg)), blk(lambda p: (0, nxb + (2 * p + 1) // hpg)),
         blk(lambda p: (0, nxb + 2 + (2 * p) // hpg)), blk(lambda p: (0, nxb + 2 + (2 * p + 1) // hpg)),
         _resident((n, GATE_W)), _layer_resident((1, GATE_W), layer),
         pl.BlockSpec((None, 2, None, 2, LANES), lambda p: (layer, 0, p, 0, 0)),
         pl.BlockSpec((None, 1, LANES), lambda p: (layer, 0, p)), h_spec],
        [ssd, xc, xc, xc, xc, xc, gates, gate_bias, head_params, norm_w, h0t],
        [pl.BlockSpec((n, LANES), lambda p: (0, p)), h_spec],
        [jax.ShapeDtypeStruct((n, SSD_W), bf16), jax.ShapeDtypeStruct(h0t.shape, f32)],
        [pltpu.VMEM((LANES, NB), f32), pltpu.VMEM((2, SSD_DSTATE, NB), f32), pltpu.VMEM((2, SSD_DSTATE, NB), f32),
         pltpu.VMEM((LANES, NB), f32), pltpu.VMEM((LANES, NB), f32)], prev)


def _hgrn_sample_body(q_ref, f_ref, i_ref, gate_ref, lbp_ref, nw_ref, s0_ref, out_ref, s1_ref,
                      qt_s, kt_s, ft_s, vt_s, ot_s, *, layer):
    lb = _hg_lower_bound(lbp_ref[...], layer)

    for t in range(TS):
        s_src = s0_ref if t == 0 else s1_ref
        rows = slice(t * NB, (t + 1) * NB)
        ff = f_ref[rows, :]
        qt_s[...] = (q_ref[rows, :] * (HEAD_W ** -0.5)).T
        kt_s[...] = ((1.0 - lb) * jax.nn.sigmoid(-ff)).T
        ft_s[...] = (lb + (1.0 - lb) * jax.nn.sigmoid(ff)).T
        vt_s[...] = i_ref[rows, :].T
        for hl in range(2):
            hs = slice(hl * HEAD_W, (hl + 1) * HEAD_W)
            v_h = vt_s[hs, :]

            def body(k, o, hl=hl, v_h=v_h, s_src=s_src):
                r = pl.ds(hl * HEAD_W + k, 1)
                sk = ft_s[r, :] * s_src[hl, k] + kt_s[r, :] * v_h
                s1_ref[hl, k] = sk
                return o + qt_s[r, :] * sk

            o = lax.fori_loop(0, HEAD_W, body, jnp.zeros((HEAD_W, NB), f32), unroll=4)
            ms = jnp.mean(o * o, axis=0, keepdims=True)
            ot_s[hs, :] = o * lax.rsqrt(ms + EPS)
        out_ref[rows, :] = (ot_s[...].T * nw_ref[...] * _silu(gate_ref[rows, :])).astype(bf16)


def _hgrn_sample(hg, lbp, norm_w, s0t, layer, prev):
    n = hg.shape[0]
    npair = HG_HEADS // 2
    blk = lambda off: pl.BlockSpec((n, LANES), lambda p, off=off: (0, off + p))
    s_spec = pl.BlockSpec((None, 2, HEAD_W, HEAD_W, NB), lambda p: (layer, p, 0, 0, 0))
    return _sample_call(
        functools.partial(_hgrn_sample_body, layer=layer), "hgrn_sample", (npair,),
        [blk(0), blk(npair), blk(2 * npair), blk(3 * npair),
         pl.BlockSpec((DEPTH, LANES), lambda p: (0, p)),
         pl.BlockSpec((None, 1, LANES), lambda p: (layer, 0, p)), s_spec],
        [hg, hg, hg, hg, lbp, norm_w, s0t],
        [pl.BlockSpec((n, LANES), lambda p: (0, p)), s_spec],
        [jax.ShapeDtypeStruct((n, HG_W), bf16), jax.ShapeDtypeStruct(s0t.shape, f32)],
        [pltpu.VMEM((LANES, NB), f32)] * 5, prev)


def _ffn_body(x_ref, mml_ref, mssd_ref, mhg_ref, wout_ref, n2_ref, wug_ref, wuv_ref, cwg_ref, cwv_ref,
              cbg_ref, cbv_ref, wdn_ref, f0g_ref, f0v_ref, fn_ref, y_ref, f1g_ref, f1v_ref,
              x1_s, h2_s, acc_s, ubuf_s, *, tm, shift, jb, final):
    t = pl.program_id(1)
    f = pl.program_id(2)
    halo = max(HALO, (FFN_CONV - 1) * shift)
    keep = (FFN_CONV - 1) * shift

    @pl.when(f == 0)
    def _():
        x = x_ref[...]
        x1 = (x + _dot(mml_ref[...], wout_ref[0:ML_W, :]) + _dot(mssd_ref[...], wout_ref[ML_W:ML_W + SSD_W, :])
              + _dot(mhg_ref[...], wout_ref[ML_W + SSD_W:D_MODEL, :]))
        x1_s[...] = x1
        h2_s[...] = (x1 * lax.rsqrt(jnp.mean(x1 * x1, axis=-1, keepdims=True) + EPS) * n2_ref[...]).astype(bf16)

    @pl.when(t == 0)
    def _():
        f1g_ref[0] = f0g_ref[0]
        f1v_ref[0] = f0v_ref[0]

    h2 = h2_s[...]

    def conv(wu_ref, cw_ref, cb_ref, f1_ref, cols):
        u = _dot(h2, wu_ref[:, cols])
        ubuf_s[halo - keep:halo, :] = f1_ref[0, :, cols]
        ubuf_s[halo:halo + tm, :] = u
        yv = (cb_ref[:, cols] + cw_ref[2:3, cols] * u
              + cw_ref[1:2, cols] * ubuf_s[halo - shift:halo - shift + tm, :]
              + cw_ref[0:1, cols] * ubuf_s[halo - 2 * shift:halo - 2 * shift + tm, :])
        f1_ref[0, :, cols] = ubuf_s[halo + tm - keep:halo + tm, :]
        return yv

    for j in range(jb):
        cols = slice(j * FF_BLOCK, (j + 1) * FF_BLOCK)
        gq = conv(wug_ref, cwg_ref, cbg_ref, f1g_ref, cols)
        vq = conv(wuv_ref, cwv_ref, cbv_ref, f1v_ref, cols)
        contrib = _dot((_silu(gq) * vq).astype(bf16), wdn_ref[cols, :])
        if j == 0:
            @pl.when(f == 0)
            def _():
                acc_s[...] = contrib

            @pl.when(f != 0)
            def _():
                acc_s[...] = acc_s[...] + contrib
        else:
            acc_s[...] = acc_s[...] + contrib

    @pl.when(f == pl.num_programs(2) - 1)
    def _():
        x2 = x1_s[...] + acc_s[...]
        if final:
            x2 = x2 * lax.rsqrt(jnp.mean(x2 * x2, axis=-1, keepdims=True) + EPS) * fn_ref[...]
        y_ref[...] = x2


def _ffn(x, mml, mssd, mhg, w_out, norm2, w_up, conv_w, conv_b, w_down, f0, final_w, *, layer, groups, tm, shift, jb,
         final, f0_base=0, f1_groups=None, f1_base=0, prev=None):
    n = x.shape[0]
    nt = n // (groups * tm)
    fw = jb * FF_BLOCK
    nff = D_FF // fw
    assert nt == 1 or nff == 1
    f1_groups = groups if f1_groups is None else f1_groups
    halo = max(HALO, (FFN_CONV - 1) * shift)
    keep = (FFN_CONV - 1) * shift
    rows = lambda w: pl.BlockSpec((tm, w), lambda g, t, f: (g * nt + t, 0))
    single = nff == 1
    wspec = lambda shape, imap: pl.BlockSpec(shape, imap, pipeline_mode=pl.Buffered(1)) if single else pl.BlockSpec(shape, imap)
    f1_spec = pl.BlockSpec((1, keep, fw), lambda g, t, f: (f1_base + g, 0, f))
    in_specs = [rows(D_MODEL), rows(ML_W), rows(SSD_W), rows(HG_W),
                _layer_resident((D_MODEL, D_MODEL), layer), _layer_resident((1, D_MODEL), layer),
                wspec((None, D_MODEL, fw), lambda g, t, f: (layer, 0, f)),
                wspec((None, D_MODEL, fw), lambda g, t, f: (layer, 0, nff + f)),
                wspec((None, FFN_CONV, fw), lambda g, t, f: (layer, 0, f)),
                wspec((None, FFN_CONV, fw), lambda g, t, f: (layer, 0, nff + f)),
                wspec((None, 1, fw), lambda g, t, f: (layer, 0, f)),
                wspec((None, 1, fw), lambda g, t, f: (layer, 0, nff + f)),
                wspec((None, fw, D_MODEL), lambda g, t, f: (layer, f, 0)),
                pl.BlockSpec((1, keep, fw), lambda g, t, f: (f0_base + g, 0, f)),
                pl.BlockSpec((1, keep, fw), lambda g, t, f: (f0_base + g, 0, nff + f)),
                _resident((1, D_MODEL))]
    operands = [x, mml, mssd, mhg, w_out, norm2, w_up, w_up, conv_w, conv_w, conv_b, conv_b, w_down, f0, f0, final_w]
    n_in = len(operands)
    aliases = {}
    if prev is not None:
        in_specs += [_unfetched(), _unfetched()]
        operands += list(prev)
        aliases = {n_in: 1, n_in + 1: 2}
    n_prev = len(operands) - n_in
    body = functools.partial(_ffn_body, tm=tm, shift=shift, jb=jb, final=final)

    def wrapped(*refs):
        body(*refs[:n_in], *refs[n_in + n_prev:])

    return pl.pallas_call(
        wrapped,
        grid=(groups, nt, nff),
        in_specs=in_specs,
        out_specs=[rows(D_MODEL), f1_spec, f1_spec],
        out_shape=[jax.ShapeDtypeStruct((n, D_MODEL), f32),
                   jax.ShapeDtypeStruct((f1_groups, keep, D_FF), f32),
                   jax.ShapeDtypeStruct((f1_groups, keep, D_FF), f32)],
        scratch_shapes=[pltpu.VMEM((tm, D_MODEL), f32), pltpu.VMEM((tm, D_MODEL), bf16),
                        pltpu.VMEM((tm, D_MODEL), f32), pltpu.VMEM((halo + tm, FF_BLOCK), f32)],
        input_output_aliases=aliases,
        compiler_params=_cparams("arbitrary", "arbitrary", "arbitrary"),
        name="ffn",
    )(*operands)


def _gate_rows(*parts):
    r = jnp.concatenate([p.astype(f32) for p in parts], axis=1)
    return jnp.pad(r, ((0, 0), (0, GATE_W - r.shape[1])))[:, None, :]


def kernel(x_prompt, x_sample, state_mlstm_C, state_mlstm_n, state_mlstm_m, state_ssd, state_ssd_conv,
           state_hgrn, state_ffn_conv, norm1_w, w_in, ml_ig_b, ml_fg_b, ml_norm_w, ssd_conv_w, ssd_conv_b,
           ssd_dt_bias, ssd_A_log, ssd_D, ssd_norm_w, hg_lower_bounds, hg_norm_w, w_out, norm2_w, w_up,
           ffn_conv_w, ffn_conv_b, w_down, final_norm_w):
    bp, tp, _ = x_prompt.shape
    nb, ts, _ = x_sample.shape
    assert nb == NB and ts == TS and tp % ROW_TILE == 0
    depth = w_in.shape[0]
    assert depth == DEPTH
    xp = x_prompt.reshape(bp * tp, D_MODEL)
    xs = jnp.transpose(x_sample, (1, 0, 2)).reshape(ts * nb, D_MODEL)
    lbp = hg_lower_bounds.astype(f32)
    fin_w = final_norm_w[None, :]
    zero_f = jnp.zeros((bp, FFN_CONV - 1, 2 * D_FF), f32)

    w_cat = _prep_w_in(w_in)
    wo_b, wu_b, wd_b = w_out.astype(bf16), w_up.astype(bf16), w_down.astype(bf16)
    n1, n2 = norm1_w[:, None, :], norm2_w[:, None, :]
    gate_bias = _gate_rows(ml_ig_b, ml_fg_b, ssd_dt_bias)
    alog_row = _gate_rows(jnp.zeros((depth, DT_LANE), f32), ssd_A_log)
    d_row = jnp.repeat(ssd_D.astype(f32), HEAD_W, axis=1)[:, None, :]
    mlw, ssw, hgw = ml_norm_w[:, None, :], ssd_norm_w[:, None, :], hg_norm_w[:, None, :]
    cw, cb = ssd_conv_w, ssd_conv_b[:, None, :]
    fcw, fcb = ffn_conv_w, ffn_conv_b[:, None, :]
    head_params = jnp.broadcast_to(
        jnp.stack([ssd_A_log, ssd_D], axis=1).astype(f32).reshape(depth, 2, SSD_HEADS // 2, 2, 1),
        (depth, 2, SSD_HEADS // 2, 2, LANES))

    c_t = jnp.transpose(state_mlstm_C.astype(f32), (0, 2, 3, 4, 1))
    n_t = jnp.transpose(state_mlstm_n.astype(f32), (0, 2, 3, 1))
    m_t = jnp.transpose(state_mlstm_m.astype(f32), (0, 2, 1))
    h_t = jnp.transpose(state_ssd.astype(f32), (0, 2, 3, 4, 1))
    s_t = jnp.transpose(state_hgrn.astype(f32), (0, 2, 3, 4, 1))
    conv_halo = jnp.transpose(state_ssd_conv.astype(f32), (0, 2, 1, 3)).reshape(depth, (SSD_CONV - 1) * nb, SSD_CONV_DIM)
    f_t = jnp.transpose(state_ffn_conv.astype(f32), (0, 2, 1, 3)).reshape(depth, (FFN_CONV - 1) * nb, 2 * D_FF)

    p_states, s_sconv = [], []
    ml_prev = ssd_prev = hg_prev = ffn_prev = None
    for l in range(depth):
        last = l == depth - 1

        ml, ssd, hg, gates = _inproj(xp, n1, w_cat, l)
        gates3 = gates.reshape(bp, tp, GATE_W)
        ssd3 = ssd.reshape(bp, tp, SSD_COLS)
        hm, p_c, p_n, p_m = _mlstm_prompt(ml.reshape(bp, tp, ML_COLS), gates3, gate_bias, mlw, l)
        hs, p_h = _ssd_prompt(ssd3, gates3, gate_bias, alog_row, cw, cb, d_row, ssw, l)
        ho, p_s = _hgrn_prompt(hg.reshape(bp, tp, HG_COLS), lbp, hgw, l)
        xp, p_fg, p_fv = _ffn(xp, hm.reshape(bp * tp, ML_W), hs.reshape(bp * tp, SSD_W), ho.reshape(bp * tp, HG_W),
                              wo_b, n2, wu_b, fcw, fcb, wd_b, zero_f, fin_w, layer=l,
                              groups=bp, tm=ROW_TILE, shift=1, jb=D_FF // FF_BLOCK, final=last)
        p_states.append((p_c, p_n, p_m[:, 0, 0:ML_HEADS],
                         p_h.reshape(bp, SSD_HEADS, HEAD_W, SSD_DSTATE),
                         ssd3[:, tp - (SSD_CONV - 1):, SSD_W:], p_s,
                         jnp.concatenate([p_fg, p_fv], axis=-1)))

        ml, ssd, hg, gates = _inproj(xs, n1, w_cat, l)
        hm, *ml_prev = _mlstm_sample(ml, gates, gate_bias, mlw, c_t, n_t, m_t, l, ml_prev)
        xc = _ssd_conv_sample(ssd, conv_halo, cw, cb, l)
        hs, *ssd_prev = _ssd_sample(ssd, xc, gates, gate_bias, head_params, ssw, h_t, l, ssd_prev)
        ho, *hg_prev = _hgrn_sample(hg, lbp, hgw, s_t, l, hg_prev)
        xs, *ffn_prev = _ffn(xs, hm, hs, ho, wo_b, n2, wu_b, fcw, fcb, wd_b, f_t, fin_w, layer=l,
                             groups=1, tm=ts * nb, shift=nb, jb=1, final=last,
                             f0_base=l, f1_groups=depth, f1_base=l, prev=ffn_prev)
        raw_xbc = ssd[:, SSD_W:].reshape(ts, nb, SSD_CONV_DIM)
        s_sconv.append(jnp.transpose(raw_xbc[ts - (SSD_CONV - 1):], (1, 0, 2)))

    y_prompt = xp.reshape(bp, tp, D_MODEL)
    y_sample = jnp.transpose(xs.reshape(ts, nb, D_MODEL), (1, 0, 2))
    p_out = [jnp.stack([st[i] for st in p_states]) for i in range(7)]
    s_c, s_n, s_m = ml_prev
    s_f = jnp.concatenate(ffn_prev, axis=-1).reshape(depth, FFN_CONV - 1, nb, 2 * D_FF)
    s_out = [jnp.transpose(s_c, (0, 4, 1, 2, 3)), jnp.transpose(s_n, (0, 3, 1, 2)), jnp.transpose(s_m, (0, 2, 1)),
             jnp.transpose(ssd_prev[0], (0, 4, 1, 2, 3)), jnp.stack(s_sconv),
             jnp.transpose(hg_prev[0], (0, 4, 1, 2, 3)), jnp.transpose(s_f, (0, 2, 1, 3))]
    return (y_prompt, y_sample, *p_out, *s_out)
```

```python
import functools

import jax
import jax.numpy as jnp
from jax import lax
from jax.experimental import pallas as pl
from jax.experimental.pallas import tpu as pltpu

f32 = jnp.float32
bf16 = jnp.bfloat16

D_MODEL = 1024
DEPTH = 2
ML_HEADS = 6
HEAD_W = 64
ML_W = 384
SSD_HEADS = 6
SSD_W = 384
SSD_DSTATE = 128
SSD_CONV = 4
SSD_CONV_DIM = 896
HG_HEADS = 4
HG_W = 256
D_FF = 2816
FFN_CONV = 3
EPS = 1e-6
PROMPT_CHUNK = 128
HG_CHUNK = 64
LANES = 128
FF_BLOCK = 256
ROW_TILE = 512
SEQS_PER_STEP = 2
GATE_W = 128
W_IN_COLS = 4 * ML_W + (SSD_W + SSD_CONV_DIM) + 4 * HG_W + GATE_W
VMEM_LIMIT_BYTES = 56 * 1024 * 1024

NEG_INF = float("-inf")


def _cparams(*sem):
    return pltpu.CompilerParams(dimension_semantics=sem, vmem_limit_bytes=VMEM_LIMIT_BYTES)


def _resident(shape):
    nd = len(shape)
    return pl.BlockSpec(shape, lambda *_: (0,) * nd, pipeline_mode=pl.Buffered(1))


def _layer_resident(shape, layer):
    nd = len(shape)
    return pl.BlockSpec((None,) + tuple(shape), lambda *_: (layer,) + (0,) * nd, pipeline_mode=pl.Buffered(1))


def _unfetched():
    return pl.BlockSpec(memory_space=pl.ANY)


def _dot(a, b):
    return jnp.dot(a, b, preferred_element_type=f32)


def _dot_nt(a, b):
    return lax.dot_general(a, b, (((1,), (1,)), ((), ())), preferred_element_type=f32)


def _split_bf16(x, parts):
    out = []
    r = x
    for _ in range(parts):
        h = r.astype(bf16)
        out.append(h)
        r = r - h.astype(f32)
    return out


def _cumsum_rows(x, tri_b):
    acc = None
    for part in _split_bf16(x, 3):
        d = _dot(tri_b, part)
        acc = d if acc is None else acc + d
    return acc


def _group_mean(x, bd_b):
    hi, lo = _split_bf16(x, 2)
    return (_dot(hi, bd_b) + _dot(lo, bd_b)) * (1.0 / HEAD_W)


def _iota2(shape, axis):
    return lax.broadcasted_iota(jnp.int32, shape, axis)


def _ones_where(mask):
    return jnp.where(mask, 1.0, 0.0).astype(bf16)


def _same_head(a, b):
    return jnp.right_shift(a, 6) == jnp.right_shift(b, 6)


def _silu(x):
    return x * jax.nn.sigmoid(x)


ML_COLS = 4 * ML_W
SSD_COLS = SSD_W + SSD_CONV_DIM
HG_COLS = 4 * HG_W


def _inproj_body(x_ref, nw_ref, w_ref, ml_ref, ssd_ref, hg_ref, g_ref):
    x = x_ref[...]
    h = (x * lax.rsqrt(jnp.mean(x * x, axis=-1, keepdims=True) + EPS) * nw_ref[...]).astype(bf16)
    o0, o1, o2 = ML_COLS, ML_COLS + SSD_COLS, ML_COLS + SSD_COLS + HG_COLS
    ml_ref[...] = _dot(h, w_ref[:, 0:o0])
    ssd_ref[...] = _dot(h, w_ref[:, o0:o1])
    hg_ref[...] = _dot(h, w_ref[:, o1:o2])
    g_ref[...] = _dot(h, w_ref[:, o2:W_IN_COLS])


def _inproj(x, norm_w, w_cat, layer):
    n = x.shape[0]
    tm = ROW_TILE
    row = lambda w: pl.BlockSpec((tm, w), lambda i: (i, 0))
    return pl.pallas_call(
        _inproj_body,
        grid=(n // tm,),
        in_specs=[row(D_MODEL), _layer_resident((1, D_MODEL), layer), _layer_resident((D_MODEL, W_IN_COLS), layer)],
        out_specs=[row(ML_COLS), row(SSD_COLS), row(HG_COLS), row(GATE_W)],
        out_shape=[jax.ShapeDtypeStruct((n, w), f32) for w in (ML_COLS, SSD_COLS, HG_COLS, GATE_W)],
        compiler_params=_cparams("arbitrary"),
        name="inproj",
    )(x, norm_w, w_cat)


W_IN_GATE_IF = ML_COLS
W_IN_SSD = W_IN_GATE_IF + 2 * ML_HEADS
W_IN_DT = W_IN_SSD + SSD_COLS
W_IN_HG = W_IN_DT + SSD_HEADS
W_IN_DIM = W_IN_HG + HG_COLS
W_PREP_ROWS = 256


def _prep_w_in_body(w_ref, out_ref):
    lane = _iota2((W_PREP_ROWS, LANES), 1)
    tile = lambda k: w_ref[:, k * LANES:(k + 1) * LANES]
    out_ref[:, 0:ML_COLS] = w_ref[:, 0:ML_COLS].astype(bf16)

    def shifted(dst0, src0, ntiles):
        t0, sh = src0 // LANES, src0 % LANES
        rolled = [pltpu.roll(tile(t0 + k), LANES - sh, axis=1) for k in range(ntiles + 1)]
        for k in range(ntiles):
            out_ref[:, dst0 + k * LANES:dst0 + (k + 1) * LANES] = jnp.where(
                lane < LANES - sh, rolled[k], rolled[k + 1]).astype(bf16)

    shifted(ML_COLS, W_IN_SSD, SSD_COLS // LANES)
    shifted(ML_COLS + SSD_COLS, W_IN_HG, HG_COLS // LANES)
    assert W_IN_GATE_IF % LANES == 0 and W_IN_DT % LANES == DT_LANE
    gates = jnp.where(lane < DT_LANE, tile(W_IN_GATE_IF // LANES),
                      jnp.where(lane < DT_LANE + SSD_HEADS, tile(W_IN_DT // LANES), 0.0))
    out_ref[:, ML_COLS + SSD_COLS + HG_COLS:W_IN_COLS] = gates.astype(bf16)


def _prep_w_in(w_in):
    depth, d, n = w_in.shape
    assert n == W_IN_DIM and d % W_PREP_ROWS == 0
    return pl.pallas_call(
        _prep_w_in_body,
        grid=(depth, d // W_PREP_ROWS),
        in_specs=[pl.BlockSpec((None, W_PREP_ROWS, W_IN_COLS), lambda l, i: (l, i, 0))],
        out_specs=pl.BlockSpec((None, W_PREP_ROWS, W_IN_COLS), lambda l, i: (l, i, 0)),
        out_shape=jax.ShapeDtypeStruct((depth, d, W_IN_COLS), bf16),
        compiler_params=_cparams("arbitrary", "arbitrary"),
        name="prep_w_in",
    )(w_in)


def _one_seq(ref, bi):
    return ref.at[pl.ds(bi, 1)]


def _mlstm_prompt_body(ml_ref, g_ref, gb_ref, nw_ref, out_ref, c1_ref, n1_ref, m1_ref, cext_s, m_s):
    c = pl.program_id(1)
    seqs = range(ml_ref.shape[0])

    @pl.when(c == 0)
    def _():
        cext_s[...] = jnp.zeros(cext_s.shape, f32)
        m_s[...] = jnp.zeros(m_s.shape, f32)

    for bi in seqs:
        _mlstm_prompt_chunk(_one_seq(ml_ref, bi), _one_seq(g_ref, bi), gb_ref, nw_ref, _one_seq(out_ref, bi),
                            cext_s.at[bi], m_s.at[bi])

    @pl.when(c == pl.num_programs(1) - 1)
    def _():
        for bi in seqs:
            _mlstm_prompt_final(_one_seq(c1_ref, bi), _one_seq(n1_ref, bi), _one_seq(m1_ref, bi),
                                cext_s.at[bi], m_s.at[bi])


def _mlstm_prompt_final(c1_ref, n1_ref, m1_ref, cext_s, m_s):
    for p in range(ML_HEADS // 2):
        cext = cext_s[p]
        c1_ref[0, 2 * p] = cext[0:HEAD_W, 0:HEAD_W]
        c1_ref[0, 2 * p + 1] = cext[HEAD_W:LANES, HEAD_W:LANES]
        n_t = cext[:, LANES:2 * LANES].T
        n1_ref[0, 2 * p:2 * p + 1, :] = n_t[0:1, 0:HEAD_W]
        n1_ref[0, 2 * p + 1:2 * p + 2, :] = n_t[1:2, HEAD_W:LANES]
    m1_ref[0] = m_s[...]


def _mlstm_prompt_chunk(ml_ref, g_ref, gb_ref, nw_ref, out_ref, cext_s, m_s):
    L = PROMPT_CHUNK
    row = _iota2((L, L), 0)
    col = _iota2((L, L), 1)
    tril = col <= row
    lo_half = col < HEAD_W
    tri_b = _ones_where(tril)
    bd_b = _ones_where(_same_head(row, col))
    lane2 = _iota2((L, 2 * LANES), 1)
    row2 = _iota2((L, 2 * LANES), 0)
    rowc = _iota2((L, 1), 0)
    ones01 = jnp.where(col < 2, 1.0, 0.0)
    one0 = jnp.where(col == 0, 1.0, 0.0)
    one1 = jnp.where(col == 1, 1.0, 0.0)
    mask_ext = jnp.where(lane2 < LANES, jnp.where(_same_head(row2, lane2), 1.0, 0.0),
                         jnp.where(lane2 == LANES, jnp.where(row2 < HEAD_W, 1.0, 0.0),
                                   jnp.where(lane2 == LANES + 1, jnp.where(row2 >= HEAD_W, 1.0, 0.0), 0.0)))

    g = g_ref[0] + gb_ref[...]
    bcum = _cumsum_rows(jax.nn.log_sigmoid(g), tri_b)
    gT = g.T
    bT = bcum.T
    m_old = m_s[...]
    m_next = m_old
    lane1 = _iota2((1, GATE_W), 1)

    for p in range(ML_HEADS // 2):
        sl = slice(p * LANES, (p + 1) * LANES)
        q2 = ml_ref[0, :, sl]
        k2 = ml_ref[0, :, ML_W + p * LANES:ML_W + (p + 1) * LANES] * (HEAD_W ** -0.5)
        v2 = ml_ref[0, :, 2 * ML_W + p * LANES:2 * ML_W + (p + 1) * LANES]
        o2 = ml_ref[0, :, 3 * ML_W + p * LANES:3 * ML_W + (p + 1) * LANES]
        kb = k2.astype(bf16)
        q_sel = (jnp.where(lo_half, q2, 0.0).astype(bf16), jnp.where(lo_half, 0.0, q2).astype(bf16))
        sw, a_int, e_m, a_st, w_s = [], [], [], [], []
        for j in range(2):
            h = 2 * p + j
            b_col = bcum[:, ML_HEADS + h:ML_HEADS + h + 1]
            b_row = bT[ML_HEADS + h:ML_HEADS + h + 1, :]
            i_row = gT[h:h + 1, :]
            i_col = g[:, h:h + 1]
            m_prev = m_old[:, h:h + 1]
            dm = jnp.where(tril, b_col - b_row + i_row, NEG_INF)
            inter = b_col + m_prev
            m_t = jnp.maximum(inter, jnp.max(dm, axis=1, keepdims=True))
            w = jnp.exp(dm - m_t)
            s = _dot_nt(q_sel[j], kb)
            sw.append((s * w).astype(bf16))
            a_int.append(jnp.exp(inter - m_t))
            e_m.append(jnp.exp(-m_t))
            m_new = m_t[L - 1:L, :]
            b_last = b_col[L - 1:L, :]
            a_st.append(jnp.exp(b_last + m_prev - m_new))
            w_s.append(jnp.exp(b_last - b_col + i_col - m_new))
            m_next = jnp.where(lane1 == h, m_new, m_next)
        vtop = jnp.concatenate([jnp.where(lo_half, v2, 0.0), one0], axis=1).astype(bf16)
        vbot = jnp.concatenate([jnp.where(lo_half, 0.0, v2), one1], axis=1).astype(bf16)
        cext = cext_s[p]
        a_ext = jnp.where(lane2 < HEAD_W, a_int[0],
                          jnp.where(lane2 < LANES, a_int[1], jnp.where(lane2 == LANES, a_int[0], a_int[1])))
        full = _dot(sw[0], vtop) + _dot(sw[1], vbot) + a_ext * _dot(q2.astype(bf16), cext.astype(bf16))
        num = full[:, 0:LANES]
        den0 = full[:, LANES:LANES + 1]
        den1 = full[:, LANES + 1:LANES + 2]
        dd = jnp.where(lo_half, jnp.maximum(jnp.abs(den0), e_m[0]), jnp.maximum(jnp.abs(den1), e_m[1]))
        hh = num / dd
        ms = _group_mean(hh * hh, bd_b)
        y = hh * lax.rsqrt(ms + EPS) * nw_ref[:, sl] * jax.nn.sigmoid(o2)
        out_ref[0, :, sl] = y.astype(bf16)
        kw_t = (k2 * jnp.where(lo_half, w_s[0], w_s[1])).T.astype(bf16)
        vones = jnp.concatenate([v2, ones01], axis=1).astype(bf16)
        upd = _dot(kw_t, vones)
        a_rows = jnp.where(rowc < HEAD_W, a_st[0], a_st[1])
        cext_s[p] = a_rows * cext + mask_ext * upd
    m_s[...] = m_next


def _mlstm_prompt(ml, gates, gate_bias, norm_w, layer):
    b, t, _ = ml.shape
    L = PROMPT_CHUNK
    ns = SEQS_PER_STEP
    assert b % ns == 0
    return pl.pallas_call(
        _mlstm_prompt_body,
        grid=(b // ns, t // L),
        in_specs=[pl.BlockSpec((ns, L, ML_COLS), lambda i, c: (i, c, 0)),
                  pl.BlockSpec((ns, L, GATE_W), lambda i, c: (i, c, 0)),
                  _layer_resident((1, GATE_W), layer), _layer_resident((1, ML_W), layer)],
        out_specs=[pl.BlockSpec((ns, L, ML_W), lambda i, c: (i, c, 0)),
                   pl.BlockSpec((ns, ML_HEADS, HEAD_W, HEAD_W), lambda i, c: (i, 0, 0, 0)),
                   pl.BlockSpec((ns, ML_HEADS, HEAD_W), lambda i, c: (i, 0, 0)),
                   pl.BlockSpec((ns, 1, GATE_W), lambda i, c: (i, 0, 0))],
        out_shape=[jax.ShapeDtypeStruct((b, t, ML_W), bf16),
                   jax.ShapeDtypeStruct((b, ML_HEADS, HEAD_W, HEAD_W), f32),
                   jax.ShapeDtypeStruct((b, ML_HEADS, HEAD_W), f32),
                   jax.ShapeDtypeStruct((b, 1, GATE_W), f32)],
        scratch_shapes=[pltpu.VMEM((ns, ML_HEADS // 2, LANES, 2 * LANES), f32), pltpu.VMEM((ns, 1, GATE_W), f32)],
        compiler_params=_cparams("arbitrary", "arbitrary"),
        name="mlstm_prompt",
    )(ml, gates, gate_bias, norm_w)


DT_LANE = 2 * ML_HEADS
HALO = 8


def _ssd_prompt_body(ssd_ref, g_ref, gb_ref, al_ref, cw_ref, cb_ref, d_ref, nw_ref, out_ref, h1_ref,
                     xbuf_s, h_s):
    c = pl.program_id(1)
    seqs = range(ssd_ref.shape[0])

    @pl.when(c == 0)
    def _():
        xbuf_s[:, 0:HALO, :] = jnp.zeros((xbuf_s.shape[0], HALO, SSD_CONV_DIM), f32)
        h_s[...] = jnp.zeros(h_s.shape, f32)

    for bi in seqs:
        _ssd_prompt_chunk(_one_seq(ssd_ref, bi), _one_seq(g_ref, bi), gb_ref, al_ref, cw_ref, cb_ref, d_ref, nw_ref,
                          _one_seq(out_ref, bi), xbuf_s.at[bi], h_s.at[bi])

    @pl.when(c == pl.num_programs(1) - 1)
    def _():
        h1_ref[...] = h_s[...]


def _ssd_prompt_chunk(ssd_ref, g_ref, gb_ref, al_ref, cw_ref, cb_ref, d_ref, nw_ref, out_ref, xbuf_s, h_s):
    L = PROMPT_CHUNK
    xbuf_s[HALO:HALO + L, :] = ssd_ref[0, :, SSD_W:SSD_COLS]
    conv = cb_ref[...]
    for j in range(SSD_CONV):
        o = HALO - (SSD_CONV - 1) + j
        conv = conv + cw_ref[j:j + 1, :] * xbuf_s[o:o + L, :]
    xbuf_s[0:HALO, :] = xbuf_s[L:L + HALO, :]
    xc = _silu(conv)
    nb = SSD_W
    b_b = [xc[:, nb + gi * LANES:nb + (gi + 1) * LANES].astype(bf16) for gi in range(2)]
    c_b = [xc[:, nb + (2 + gi) * LANES:nb + (3 + gi) * LANES].astype(bf16) for gi in range(2)]

    row = _iota2((L, L), 0)
    col = _iota2((L, L), 1)
    tril = col <= row
    lo_half = col < HEAD_W
    tri_b = _ones_where(tril)
    bd_b = _ones_where(_same_head(row, col))
    rowc = _iota2((L, 1), 0)

    g = g_ref[0] + gb_ref[...]
    dt = jax.nn.softplus(g)
    cum = _cumsum_rows(dt * (-jnp.exp(al_ref[...])), tri_b)
    cum_t = cum.T
    dt_t = dt.T
    cb_g = [_dot_nt(c_b[gi], b_b[gi]) for gi in range(2)]

    for p in range(SSD_HEADS // 2):
        sl = slice(p * LANES, (p + 1) * LANES)
        x2 = xc[:, sl]
        z2 = ssd_ref[0, :, sl]
        hp = h_s[p]
        hb = hp.astype(bf16)
        grp = [(2 * p + j) // (SSD_HEADS // 2) for j in range(2)]
        ys = None
        ecum, w_s, a_last, ch = [], [], [], []
        for j in range(2):
            ln = DT_LANE + 2 * p + j
            cum_col = cum[:, ln:ln + 1]
            cum_row = cum_t[ln:ln + 1, :]
            cum_last = cum_col[L - 1:L, :]
            lm = jnp.exp(jnp.where(tril, cum_col - cum_row, NEG_INF))
            m = (cb_g[grp[j]] * lm * dt_t[ln:ln + 1, :]).astype(bf16)
            xm = jnp.where(lo_half, x2, 0.0) if j == 0 else jnp.where(lo_half, 0.0, x2)
            d = _dot(m, xm.astype(bf16))
            ys = d if ys is None else ys + d
            ecum.append(jnp.exp(cum_col))
            w_s.append(jnp.exp(cum_last - cum_col) * dt[:, ln:ln + 1])
            a_last.append(jnp.exp(cum_last))
            if j == 0 or grp[1] != grp[0]:
                ch.append(_dot_nt(c_b[grp[j]], hb))
            else:
                ch.append(ch[0])
        ys = ys + jnp.where(lo_half, ecum[0] * ch[0], ecum[1] * ch[1])
        xw_t = (x2 * jnp.where(lo_half, w_s[0], w_s[1])).T.astype(bf16)
        if grp[0] == grp[1]:
            upd = _dot(xw_t, b_b[grp[0]])
        else:
            upd = jnp.where(row < HEAD_W, _dot(xw_t, b_b[grp[0]]), _dot(xw_t, b_b[grp[1]]))
        h_s[p] = jnp.where(rowc < HEAD_W, a_last[0], a_last[1]) * hp + upd
        yy = (ys + d_ref[:, sl] * x2) * _silu(z2)
        ms = _group_mean(yy * yy, bd_b)
        out_ref[0, :, sl] = (yy * lax.rsqrt(ms + EPS) * nw_ref[:, sl]).astype(bf16)


def _ssd_prompt(ssd, gates, gate_bias, alog_row, conv_w, conv_b, d_row, norm_w, layer):
    b, t, _ = ssd.shape
    L = PROMPT_CHUNK
    npair = SSD_HEADS // 2
    ns = SEQS_PER_STEP
    assert b % ns == 0
    return pl.pallas_call(
        _ssd_prompt_body,
        grid=(b // ns, t // L),
        in_specs=[pl.BlockSpec((ns, L, SSD_COLS), lambda i, c: (i, c, 0)),
                  pl.BlockSpec((ns, L, GATE_W), lambda i, c: (i, c, 0)),
                  _layer_resident((1, GATE_W), layer), _layer_resident((1, GATE_W), layer),
                  _layer_resident((SSD_CONV, SSD_CONV_DIM), layer), _layer_resident((1, SSD_CONV_DIM), layer),
                  _layer_resident((1, SSD_W), layer), _layer_resident((1, SSD_W), layer)],
        out_specs=[pl.BlockSpec((ns, L, SSD_W), lambda i, c: (i, c, 0)),
                   pl.BlockSpec((ns, npair, LANES, SSD_DSTATE), lambda i, c: (i, 0, 0, 0))],
        out_shape=[jax.ShapeDtypeStruct((b, t, SSD_W), bf16),
                   jax.ShapeDtypeStruct((b, npair, LANES, SSD_DSTATE), f32)],
        scratch_shapes=[pltpu.VMEM((ns, HALO + L, SSD_CONV_DIM), f32),
                        pltpu.VMEM((ns, npair, LANES, SSD_DSTATE), f32)],
        compiler_params=_cparams("arbitrary", "arbitrary"),
        name="ssd_prompt",
    )(ssd, gates, gate_bias, alog_row, conv_w, conv_b, d_row, norm_w)


def _hg_lower_bound(lbp, layer):
    mx = jnp.max(lbp, axis=0, keepdims=True)
    e = jnp.exp(lbp - mx)
    den = jnp.sum(e, axis=0, keepdims=True)
    lb = jnp.zeros_like(den)
    for j in range(1, layer + 1):
        lb = lb + e[j:j + 1, :] / den
    return lb


def _hg_rows(L):
    offs, off = [], 0
    for s in range(L):
        offs.append(off)
        off += L - 8 * (s // 8)
    return offs, off


def _hgrn_prompt_body(hg_ref, lbp_ref, nw_ref, out_ref, s1_ref, sbd_s, p_s, r_s, o_s, g_s, k_s, v_s, q_s,
                      kt_s, qt_s, *, layer):
    c = pl.program_id(1)
    seqs = range(hg_ref.shape[0])

    @pl.when(c == 0)
    def _():
        sbd_s[...] = jnp.zeros(sbd_s.shape, f32)

    for bi in seqs:
        _hgrn_prompt_chunk(_one_seq(hg_ref, bi), lbp_ref, nw_ref, _one_seq(out_ref, bi), sbd_s.at[bi], p_s.at[bi],
                           r_s.at[bi], o_s.at[bi], g_s.at[bi], k_s.at[bi], v_s.at[bi], q_s.at[bi], kt_s.at[bi],
                           qt_s.at[bi], layer=layer)

    @pl.when(c == pl.num_programs(1) - 1)
    def _():
        for bi in seqs:
            for p in range(HG_HEADS // 2):
                sbd = sbd_s[bi, p]
                s1_ref[bi, 2 * p] = sbd[0:HEAD_W, 0:HEAD_W]
                s1_ref[bi, 2 * p + 1] = sbd[HEAD_W:LANES, HEAD_W:LANES]


def _hgrn_prompt_chunk(hg_ref, lbp_ref, nw_ref, out_ref, sbd_s, p_s, r_s, o_s, g_s, k_s, v_s, q_s, kt_s, qt_s, *,
                       layer):
    L = HG_CHUNK
    nblk = L // 8
    lb = _hg_lower_bound(lbp_ref[...], layer)
    ff = hg_ref[0, :, HG_W:2 * HG_W]
    sig = jax.nn.sigmoid(ff)
    k_s[...] = (1.0 - lb) * jax.nn.sigmoid(-ff)
    v_s[...] = hg_ref[0, :, 2 * HG_W:3 * HG_W]
    q_s[...] = hg_ref[0, :, 0:HG_W] * (HEAD_W ** -0.5)
    row = _iota2((L, L), 0)
    col = _iota2((L, L), 1)
    tri_b = _ones_where(col <= row)
    g_s[...] = _cumsum_rows(jnp.log(lb + (1.0 - lb) * sig), tri_b)

    r2 = _iota2((2 * LANES, 2 * LANES), 0)
    c2 = _iota2((2 * LANES, 2 * LANES), 1)
    bd4_b = _ones_where(_same_head(r2, c2))
    r1 = _iota2((LANES, LANES), 0)
    c1 = _iota2((LANES, LANES), 1)
    bd_mask = _same_head(r1, c1)
    bd_b = _ones_where(bd_mask)

    for i in range(1, nblk):
        blk = slice(8 * i, 8 * i + 8)
        r_i = g_s[8 * i - 1:8 * i, :]
        kt_s[i - 1, 0:8 * i, :] = k_s[0:8 * i, :] * jnp.exp(r_i - g_s[0:8 * i, :])
        qt_s[blk, :] = q_s[blk, :] * jnp.exp(g_s[blk, :] - r_i)

    offs, total = _hg_rows(L)
    sub = _iota2((8, HG_W), 0)
    for s in range(L):
        j = s // 8
        blk = slice(8 * j, 8 * j + 8)
        e = jnp.exp(jnp.where(sub >= s - 8 * j, g_s[blk, :] - g_s[s:s + 1, :], NEG_INF))
        p_s[offs[s]:offs[s] + 8, :] = q_s[blk, :] * k_s[s:s + 1, :] * e
        for i in range(j + 1, nblk):
            o = offs[s] + 8 * (i - j)
            p_s[o:o + 8, :] = qt_s[8 * i:8 * i + 8, :] * kt_s[i - 1, s:s + 1, :]
    step = 576
    for r0 in range(0, total, step):
        r_s[r0:r0 + step, :] = _dot(p_s[r0:r0 + step, :].astype(bf16), bd4_b)

    gg = g_s[...]
    qd = q_s[...] * jnp.exp(gg)
    for p in range(HG_HEADS // 2):
        sl = slice(p * LANES, (p + 1) * LANES)
        o_s[:, sl] = _dot(qd[:, sl].astype(bf16), sbd_s[p].astype(bf16))
    for i in range(nblk):
        blk = slice(8 * i, 8 * i + 8)
        acc = o_s[blk, :]
        for s in range(8 * (i + 1)):
            o = offs[s] + 8 * (i - s // 8)
            acc = acc + r_s[o:o + 8, :] * v_s[s:s + 1, :]
        o_s[blk, :] = acc

    g_last = gg[L - 1:L, :]
    kd = k_s[...] * jnp.exp(g_last - gg)
    vv = v_s[...]
    zpad = jnp.zeros((LANES - L, LANES), f32)
    og = o_s[...]
    gate = hg_ref[0, :, 3 * HG_W:4 * HG_W]
    for p in range(HG_HEADS // 2):
        sl = slice(p * LANES, (p + 1) * LANES)
        kd_t = jnp.concatenate([kd[:, sl], zpad], axis=0).T.astype(bf16)
        v_pad = jnp.concatenate([vv[:, sl], zpad], axis=0).astype(bf16)
        g_t = jnp.concatenate([gg[:, sl], zpad], axis=0).T
        dec = jnp.exp(g_t[:, L - 1:L])
        sbd_s[p] = dec * sbd_s[p] + jnp.where(bd_mask, _dot(kd_t, v_pad), 0.0)
        oo = og[:, sl]
        ms = _group_mean(oo * oo, bd_b)
        out_ref[0, :, sl] = (oo * lax.rsqrt(ms + EPS) * nw_ref[:, sl] * _silu(gate[:, sl])).astype(bf16)


def _hgrn_prompt(hg, lbp, norm_w, layer):
    b, t, _ = hg.shape
    L = HG_CHUNK
    _, total = _hg_rows(L)
    ns = SEQS_PER_STEP
    assert b % ns == 0
    row_buf = pltpu.VMEM((ns, L, HG_W), f32)
    return pl.pallas_call(
        functools.partial(_hgrn_prompt_body, layer=layer),
        grid=(b // ns, t // L),
        in_specs=[pl.BlockSpec((ns, L, HG_COLS), lambda i, c: (i, c, 0)),
                  _resident((DEPTH, HG_W)), _layer_resident((1, HG_W), layer)],
        out_specs=[pl.BlockSpec((ns, L, HG_W), lambda i, c: (i, c, 0)),
                   pl.BlockSpec((ns, HG_HEADS, HEAD_W, HEAD_W), lambda i, c: (i, 0, 0, 0))],
        out_shape=[jax.ShapeDtypeStruct((b, t, HG_W), bf16),
                   jax.ShapeDtypeStruct((b, HG_HEADS, HEAD_W, HEAD_W), f32)],
        scratch_shapes=[pltpu.VMEM((ns, HG_HEADS // 2, LANES, LANES), f32),
                        pltpu.VMEM((ns, total, HG_W), f32), pltpu.VMEM((ns, total, HG_W), f32),
                        row_buf, row_buf, row_buf, row_buf, row_buf,
                        pltpu.VMEM((ns, L // 8 - 1, L, HG_W), f32), row_buf],
        compiler_params=_cparams("arbitrary", "arbitrary"),
        name="hgrn_prompt",
    )(hg, lbp, norm_w)


NB = 128
TS = 8


def _sample_call(body, name, grid, in_specs, operands, out_specs, out_shapes, scratch, prev):
    n_in = len(operands)
    aliases = {}
    if prev is not None:
        in_specs = list(in_specs) + [_unfetched() for _ in prev]
        operands = list(operands) + list(prev)
        aliases = {n_in + i: 1 + i for i in range(len(prev))}
    n_prev = 0 if prev is None else len(prev)

    def wrapped(*refs):
        body(*refs[:n_in], *refs[n_in + n_prev:])

    return pl.pallas_call(
        wrapped, grid=grid, in_specs=in_specs, out_specs=out_specs, out_shape=out_shapes,
        scratch_shapes=scratch, input_output_aliases=aliases,
        compiler_params=_cparams(*(["arbitrary"] * len(grid))), name=name,
    )(*operands)


def _mlstm_sample_body(q_ref, k_ref, v_ref, o_ref, g_ref, gb_ref, nw_ref, c0_ref, n0_ref, m0_ref,
                       out_ref, c1_ref, n1_ref, m1_ref, qt_s, kt_s, vt_s, gt_s, ht_s):
    p = pl.program_id(0)
    n_cur = [n0_ref[hl] for hl in range(2)]
    m_cur = [m0_ref[pl.ds(2 * p + hl, 1), :] for hl in range(2)]

    for t in range(TS):
        c_src = c0_ref if t == 0 else c1_ref
        rows = slice(t * NB, (t + 1) * NB)
        qt_s[...] = q_ref[rows, :].T
        kt_s[...] = k_ref[rows, :].T * (HEAD_W ** -0.5)
        vt_s[...] = v_ref[rows, :].T
        gt_s[...] = (g_ref[rows, :] + gb_ref[...]).T
        for hl in range(2):
            h = 2 * p + hl
            hs = slice(hl * HEAD_W, (hl + 1) * HEAD_W)
            i_t = gt_s[pl.ds(h, 1), :]
            lf = jax.nn.log_sigmoid(gt_s[pl.ds(ML_HEADS + h, 1), :])
            m_new = jnp.maximum(lf + m_cur[hl], i_t)
            a = jnp.exp(lf + m_cur[hl] - m_new)
            w = jnp.exp(i_t - m_new)
            n_new = a * n_cur[hl] + w * kt_s[hs, :]
            v_h = vt_s[hs, :]

            def body(d, num, hl=hl, a=a, w=w, v_h=v_h, c_src=c_src):
                kd = kt_s[pl.ds(hl * HEAD_W + d, 1), :] * w
                qd = qt_s[pl.ds(hl * HEAD_W + d, 1), :]
                cd = a * c_src[hl, d] + kd * v_h
                c1_ref[hl, d] = cd
                return num + qd * cd

            num = lax.fori_loop(0, HEAD_W, body, jnp.zeros((HEAD_W, NB), f32), unroll=4)
            den = jnp.sum(qt_s[hs, :] * n_new, axis=0, keepdims=True)
            hh = num / jnp.maximum(jnp.abs(den), jnp.exp(-m_new))
            ms = jnp.mean(hh * hh, axis=0, keepdims=True)
            ht_s[hs, :] = hh * lax.rsqrt(ms + EPS)
            n_cur[hl] = n_new
            m_cur[hl] = m_new
        out_ref[rows, :] = (ht_s[...].T * nw_ref[...] * jax.nn.sigmoid(o_ref[rows, :])).astype(bf16)

    for hl in range(2):
        n1_ref[hl] = n_cur[hl]
        m1_ref[pl.ds(2 * p + hl, 1), :] = m_cur[hl]


def _mlstm_sample(ml, gates, gate_bias, norm_w, c0t, n0t, m0t, layer, prev):
    n = ml.shape[0]
    npair = ML_HEADS // 2
    blk = lambda off: pl.BlockSpec((n, LANES), lambda p, off=off: (0, off + p))
    c_spec = pl.BlockSpec((None, 2, HEAD_W, HEAD_W, NB), lambda p: (layer, p, 0, 0, 0))
    n_spec = pl.BlockSpec((None, 2, HEAD_W, NB), lambda p: (layer, p, 0, 0))
    m_spec = pl.BlockSpec((None, ML_HEADS, NB), lambda p: (layer, 0, 0))
    return _sample_call(
        _mlstm_sample_body, "mlstm_sample", (npair,),
        [blk(0), blk(npair), blk(2 * npair), blk(3 * npair), _resident((n, GATE_W)),
         _layer_resident((1, GATE_W), layer), pl.BlockSpec((None, 1, LANES), lambda p: (layer, 0, p)),
         c_spec, n_spec, m_spec],
        [ml, ml, ml, ml, gates, gate_bias, norm_w, c0t, n0t, m0t],
        [pl.BlockSpec((n, LANES), lambda p: (0, p)), c_spec, n_spec, m_spec],
        [jax.ShapeDtypeStruct((n, ML_W), bf16), jax.ShapeDtypeStruct(c0t.shape, f32),
         jax.ShapeDtypeStruct(n0t.shape, f32), jax.ShapeDtypeStruct(m0t.shape, f32)],
        [pltpu.VMEM((LANES, NB), f32)] * 5, prev)


def _ssd_conv_sample_body(ssd_ref, halo_ref, cw_ref, cb_ref, out_ref, xp_s):
    nh = (SSD_CONV - 1) * NB
    n = TS * NB
    xp_s[0:nh, :] = halo_ref[...]
    xp_s[nh:nh + n, :] = ssd_ref[:, SSD_W:SSD_COLS]
    conv = cb_ref[...]
    for j in range(SSD_CONV):
        conv = conv + cw_ref[j:j + 1, :] * xp_s[j * NB:j * NB + n, :]
    out_ref[...] = _silu(conv)


def _ssd_conv_sample(ssd, halo, conv_w, conv_b, layer):
    n = ssd.shape[0]
    nh = (SSD_CONV - 1) * NB
    return pl.pallas_call(
        _ssd_conv_sample_body,
        grid=(1,),
        in_specs=[_resident((n, SSD_COLS)), _layer_resident((nh, SSD_CONV_DIM), layer),
                  _layer_resident((SSD_CONV, SSD_CONV_DIM), layer), _layer_resident((1, SSD_CONV_DIM), layer)],
        out_specs=pl.BlockSpec((n, SSD_CONV_DIM), lambda i: (0, 0)),
        out_shape=jax.ShapeDtypeStruct((n, SSD_CONV_DIM), f32),
        scratch_shapes=[pltpu.VMEM((nh + n, SSD_CONV_DIM), f32)],
        compiler_params=_cparams("arbitrary"),
        name="ssd_conv_sample",
    )(ssd, halo, conv_w, conv_b)


def _ssd_sample_body(z_ref, x_ref, b0_ref, b1_ref, c0_ref, c1_ref, g_ref, gb_ref, hp_ref, nw_ref, h0_ref,
                     out_ref, h1_ref, xt_s, bt_s, ct_s, gt_s, yt_s):
    p = pl.program_id(0)
    b_refs = (b0_ref, b1_ref)
    c_refs = (c0_ref, c1_ref)

    for t in range(TS):
        h_src = h0_ref if t == 0 else h1_ref
        rows = slice(t * NB, (t + 1) * NB)
        xt_s[...] = x_ref[rows, :].T
        gt_s[...] = (g_ref[rows, :] + gb_ref[...]).T
        zt = z_ref[rows, :].T
        for hl in range(2):
            bt_s[hl] = b_refs[hl][rows, :].T
            ct_s[hl] = c_refs[hl][rows, :].T
        for hl in range(2):
            h = 2 * p + hl
            hs = slice(hl * HEAD_W, (hl + 1) * HEAD_W)
            dt = jax.nn.softplus(gt_s[pl.ds(DT_LANE + h, 1), :])
            da = jnp.exp(dt * (-jnp.exp(hp_ref[0, hl:hl + 1, :])))

            def body(pp, carry, hl=hl, dt=dt, da=da, h_src=h_src):
                xrow = xt_s[pl.ds(hl * HEAD_W + pp, 1), :] * dt
                hn = da * h_src[hl, pp] + xrow * bt_s[hl]
                h1_ref[hl, pp] = hn
                yt_s[pl.ds(hl * HEAD_W + pp, 1), :] = jnp.sum(ct_s[hl] * hn, axis=0, keepdims=True)
                return carry

            lax.fori_loop(0, HEAD_W, body, 0, unroll=2)
            yy = (yt_s[hs, :] + hp_ref[1, hl:hl + 1, :] * xt_s[hs, :]) * _silu(zt[hs, :])
            ms = jnp.mean(yy * yy, axis=0, keepdims=True)
            yt_s[hs, :] = yy * lax.rsqrt(ms + EPS)
        out_ref[rows, :] = (yt_s[...].T * nw_ref[...]).astype(bf16)


def _ssd_sample(ssd, xc, gates, gate_bias, head_params, norm_w, h0t, layer, prev):
    n = ssd.shape[0]
    npair = SSD_HEADS // 2
    hpg = SSD_HEADS // 2
    nxb = SSD_W // LANES
    blk = lambda f: pl.BlockSpec((n, LANES), f)
    h_spec = pl.BlockSpec((None, 2, HEAD_W, SSD_DSTATE, NB), lambda p: (layer, p, 0, 0, 0))
    return _sample_call(
        _ssd_sample_body, "ssd_sample", (npair,),
        [blk(lambda p: (0, p)), blk(lambda p: (0, p)),
         blk(lambda p: (0, nxb + (2 * p) // hpg)), blk(lambda p: (0, nxb + (2 * p + 1) // hpg)),
         blk(lambda p: (0, nxb + 2 + (2 * p) // hpg)), blk(lambda p: (0, nxb + 2 + (2 * p + 1) // hpg)),
         _resident((n, GATE_W)), _layer_resident((1, GATE_W), layer),
         pl.BlockSpec((None, 2, None, 2, LANES), lambda p: (layer, 0, p, 0, 0)),
         pl.BlockSpec((None, 1, LANES), lambda p: (layer, 0, p)), h_spec],
        [ssd, xc, xc, xc, xc, xc, gates, gate_bias, head_params, norm_w, h0t],
        [pl.BlockSpec((n, LANES), lambda p: (0, p)), h_spec],
        [jax.ShapeDtypeStruct((n, SSD_W), bf16), jax.ShapeDtypeStruct(h0t.shape, f32)],
        [pltpu.VMEM((LANES, NB), f32), pltpu.VMEM((2, SSD_DSTATE, NB), f32), pltpu.VMEM((2, SSD_DSTATE, NB), f32),
         pltpu.VMEM((LANES, NB), f32), pltpu.VMEM((LANES, NB), f32)], prev)


def _hgrn_sample_body(q_ref, f_ref, i_ref, gate_ref, lbp_ref, nw_ref, s0_ref, out_ref, s1_ref,
                      qt_s, kt_s, ft_s, vt_s, ot_s, *, layer):
    lb = _hg_lower_bound(lbp_ref[...], layer)

    for t in range(TS):
        s_src = s0_ref if t == 0 else s1_ref
        rows = slice(t * NB, (t + 1) * NB)
        ff = f_ref[rows, :]
        qt_s[...] = (q_ref[rows, :] * (HEAD_W ** -0.5)).T
        kt_s[...] = ((1.0 - lb) * jax.nn.sigmoid(-ff)).T
        ft_s[...] = (lb + (1.0 - lb) * jax.nn.sigmoid(ff)).T
        vt_s[...] = i_ref[rows, :].T
        for hl in range(2):
            hs = slice(hl * HEAD_W, (hl + 1) * HEAD_W)
            v_h = vt_s[hs, :]

            def body(k, o, hl=hl, v_h=v_h, s_src=s_src):
                r = pl.ds(hl * HEAD_W + k, 1)
                sk = ft_s[r, :] * s_src[hl, k] + kt_s[r, :] * v_h
                s1_ref[hl, k] = sk
                return o + qt_s[r, :] * sk

            o = lax.fori_loop(0, HEAD_W, body, jnp.zeros((HEAD_W, NB), f32), unroll=4)
            ms = jnp.mean(o * o, axis=0, keepdims=True)
            ot_s[hs, :] = o * lax.rsqrt(ms + EPS)
        out_ref[rows, :] = (ot_s[...].T * nw_ref[...] * _silu(gate_ref[rows, :])).astype(bf16)


def _hgrn_sample(hg, lbp, norm_w, s0t, layer, prev):
    n = hg.shape[0]
    npair = HG_HEADS // 2
    blk = lambda off: pl.BlockSpec((n, LANES), lambda p, off=off: (0, off + p))
    s_spec = pl.BlockSpec((None, 2, HEAD_W, HEAD_W, NB), lambda p: (layer, p, 0, 0, 0))
    return _sample_call(
        functools.partial(_hgrn_sample_body, layer=layer), "hgrn_sample", (npair,),
        [blk(0), blk(npair), blk(2 * npair), blk(3 * npair),
         pl.BlockSpec((DEPTH, LANES), lambda p: (0, p)),
         pl.BlockSpec((None, 1, LANES), lambda p: (layer, 0, p)), s_spec],
        [hg, hg, hg, hg, lbp, norm_w, s0t],
        [pl.BlockSpec((n, LANES), lambda p: (0, p)), s_spec],
        [jax.ShapeDtypeStruct((n, HG_W), bf16), jax.ShapeDtypeStruct(s0t.shape, f32)],
        [pltpu.VMEM((LANES, NB), f32)] * 5, prev)


def _ffn_body(x_ref, mml_ref, mssd_ref, mhg_ref, wout_ref, n2_ref, wug_ref, wuv_ref, cwg_ref, cwv_ref,
              cbg_ref, cbv_ref, wdn_ref, f0g_ref, f0v_ref, fn_ref, y_ref, f1g_ref, f1v_ref,
              x1_s, h2_s, acc_s, act_s, ubuf_s, *, tm, shift, jb, nff, final):
    t = pl.program_id(1)
    f = pl.program_id(2)
    halo = max(HALO, (FFN_CONV - 1) * shift)
    keep = (FFN_CONV - 1) * shift

    def when(cond, fn):
        if nff == 1:
            fn()
        else:
            pl.when(cond)(fn)

    @pl.when(t == 0)
    def _():
        f1g_ref[0] = f0g_ref[0]
        f1v_ref[0] = f0v_ref[0]

    def first():
        x = x_ref[...]
        x1 = (x + _dot(mml_ref[...], wout_ref[0:ML_W, :]) + _dot(mssd_ref[...], wout_ref[ML_W:ML_W + SSD_W, :])
              + _dot(mhg_ref[...], wout_ref[ML_W + SSD_W:D_MODEL, :]))
        x1_s[...] = x1
        h2_s[...] = (x1 * lax.rsqrt(jnp.mean(x1 * x1, axis=-1, keepdims=True) + EPS) * n2_ref[...]).astype(bf16)

    when(f == 0, first)
    h2 = h2_s[...]

    def conv(wu_ref, cw_ref, cb_ref, f1_ref, cols, ubuf):
        u = _dot(h2, wu_ref[:, cols])
        ubuf[halo - keep:halo, :] = f1_ref[0, :, cols]
        ubuf[halo:halo + tm, :] = u
        yv = (cb_ref[:, cols] + cw_ref[2:3, cols] * u
              + cw_ref[1:2, cols] * ubuf[halo - shift:halo - shift + tm, :]
              + cw_ref[0:1, cols] * ubuf[halo - 2 * shift:halo - 2 * shift + tm, :])
        f1_ref[0, :, cols] = ubuf[halo + tm - keep:halo + tm, :]
        return yv

    nbuf = ubuf_s.shape[0]
    for j in range(jb):
        cols = slice(j * FF_BLOCK, (j + 1) * FF_BLOCK)
        gq = conv(wug_ref, cwg_ref, cbg_ref, f1g_ref, cols, ubuf_s.at[(2 * j) % nbuf])
        vq = conv(wuv_ref, cwv_ref, cbv_ref, f1v_ref, cols, ubuf_s.at[(2 * j + 1) % nbuf])
        act_s[:, cols] = (_silu(gq) * vq).astype(bf16)
    contrib = _dot(act_s[...], wdn_ref[...])

    def finish(down):
        x2 = x1_s[...] + down
        if final:
            x2 = x2 * lax.rsqrt(jnp.mean(x2 * x2, axis=-1, keepdims=True) + EPS) * fn_ref[...]
        y_ref[...] = x2

    if nff == 1:
        finish(contrib)
    else:
        @pl.when(f == 0)
        def _():
            acc_s[...] = contrib

        @pl.when(f != 0)
        def _():
            acc_s[...] = acc_s[...] + contrib

        pl.when(f == nff - 1)(lambda: finish(acc_s[...]))


def _ffn(x, mml, mssd, mhg, w_out, norm2, w_up, conv_w, conv_b, w_down, f0, final_w, *, layer, groups, tm, shift, jb,
         final, f0_base=0, f1_groups=None, f1_base=0, prev=None):
    n = x.shape[0]
    nt = n // (groups * tm)
    fw = jb * FF_BLOCK
    nff = D_FF // fw
    assert nt == 1 or nff == 1
    f1_groups = groups if f1_groups is None else f1_groups
    halo = max(HALO, (FFN_CONV - 1) * shift)
    keep = (FFN_CONV - 1) * shift
    rows = lambda w: pl.BlockSpec((tm, w), lambda g, t, f: (g * nt + t, 0))
    single = nff == 1
    wspec = lambda shape, imap: pl.BlockSpec(shape, imap, pipeline_mode=pl.Buffered(1)) if single else pl.BlockSpec(shape, imap)
    f1_spec = pl.BlockSpec((1, keep, fw), lambda g, t, f: (f1_base + g, 0, f))
    in_specs = [rows(D_MODEL), rows(ML_W), rows(SSD_W), rows(HG_W),
                _layer_resident((D_MODEL, D_MODEL), layer), _layer_resident((1, D_MODEL), layer),
                wspec((None, D_MODEL, fw), lambda g, t, f: (layer, 0, f)),
                wspec((None, D_MODEL, fw), lambda g, t, f: (layer, 0, nff + f)),
                wspec((None, FFN_CONV, fw), lambda g, t, f: (layer, 0, f)),
                wspec((None, FFN_CONV, fw), lambda g, t, f: (layer, 0, nff + f)),
                wspec((None, 1, fw), lambda g, t, f: (layer, 0, f)),
                wspec((None, 1, fw), lambda g, t, f: (layer, 0, nff + f)),
                wspec((None, fw, D_MODEL), lambda g, t, f: (layer, f, 0)),
                pl.BlockSpec((1, keep, fw), lambda g, t, f: (f0_base + g, 0, f)),
                pl.BlockSpec((1, keep, fw), lambda g, t, f: (f0_base + g, 0, nff + f)),
                _resident((1, D_MODEL))]
    operands = [x, mml, mssd, mhg, w_out, norm2, w_up, w_up, conv_w, conv_w, conv_b, conv_b, w_down, f0, f0, final_w]
    n_in = len(operands)
    aliases = {}
    if prev is not None:
        in_specs += [_unfetched(), _unfetched()]
        operands += list(prev)
        aliases = {n_in: 1, n_in + 1: 2}
    n_prev = len(operands) - n_in
    body = functools.partial(_ffn_body, tm=tm, shift=shift, jb=jb, nff=nff, final=final)
    acc_rows = tm if nff > 1 else 8

    def wrapped(*refs):
        body(*refs[:n_in], *refs[n_in + n_prev:])

    return pl.pallas_call(
        wrapped,
        grid=(groups, nt, nff),
        in_specs=in_specs,
        out_specs=[rows(D_MODEL), f1_spec, f1_spec],
        out_shape=[jax.ShapeDtypeStruct((n, D_MODEL), f32),
                   jax.ShapeDtypeStruct((f1_groups, keep, D_FF), f32),
                   jax.ShapeDtypeStruct((f1_groups, keep, D_FF), f32)],
        scratch_shapes=[pltpu.VMEM((tm, D_MODEL), f32), pltpu.VMEM((tm, D_MODEL), bf16),
                        pltpu.VMEM((acc_rows, D_MODEL), f32), pltpu.VMEM((tm, fw), bf16),
                        pltpu.VMEM((min(4, 2 * jb), halo + tm, FF_BLOCK), f32)],
        input_output_aliases=aliases,
        compiler_params=_cparams("arbitrary", "arbitrary", "arbitrary"),
        name="ffn",
    )(*operands)


def _gate_rows(*parts):
    r = jnp.concatenate([p.astype(f32) for p in parts], axis=1)
    return jnp.pad(r, ((0, 0), (0, GATE_W - r.shape[1])))[:, None, :]


def kernel(x_prompt, x_sample, state_mlstm_C, state_mlstm_n, state_mlstm_m, state_ssd, state_ssd_conv,
           state_hgrn, state_ffn_conv, norm1_w, w_in, ml_ig_b, ml_fg_b, ml_norm_w, ssd_conv_w, ssd_conv_b,
           ssd_dt_bias, ssd_A_log, ssd_D, ssd_norm_w, hg_lower_bounds, hg_norm_w, w_out, norm2_w, w_up,
           ffn_conv_w, ffn_conv_b, w_down, final_norm_w):
    bp, tp, _ = x_prompt.shape
    nb, ts, _ = x_sample.shape
    assert nb == NB and ts == TS and tp % ROW_TILE == 0
    depth = w_in.shape[0]
    assert depth == DEPTH
    xp = x_prompt.reshape(bp * tp, D_MODEL)
    xs = jnp.transpose(x_sample, (1, 0, 2)).reshape(ts * nb, D_MODEL)
    lbp = hg_lower_bounds.astype(f32)
    fin_w = final_norm_w[None, :]
    zero_f = jnp.zeros((bp, FFN_CONV - 1, 2 * D_FF), f32)

    w_cat = _prep_w_in(w_in)
    wo_b, wu_b, wd_b = w_out.astype(bf16), w_up.astype(bf16), w_down.astype(bf16)
    n1, n2 = norm1_w[:, None, :], norm2_w[:, None, :]
    gate_bias = _gate_rows(ml_ig_b, ml_fg_b, ssd_dt_bias)
    alog_row = _gate_rows(jnp.zeros((depth, DT_LANE), f32), ssd_A_log)
    d_row = jnp.repeat(ssd_D.astype(f32), HEAD_W, axis=1)[:, None, :]
    mlw, ssw, hgw = ml_norm_w[:, None, :], ssd_norm_w[:, None, :], hg_norm_w[:, None, :]
    cw, cb = ssd_conv_w, ssd_conv_b[:, None, :]
    fcw, fcb = ffn_conv_w, ffn_conv_b[:, None, :]
    head_params = jnp.broadcast_to(
        jnp.stack([ssd_A_log, ssd_D], axis=1).astype(f32).reshape(depth, 2, SSD_HEADS // 2, 2, 1),
        (depth, 2, SSD_HEADS // 2, 2, LANES))

    c_t = jnp.transpose(state_mlstm_C.astype(f32), (0, 2, 3, 4, 1))
    n_t = jnp.transpose(state_mlstm_n.astype(f32), (0, 2, 3, 1))
    m_t = jnp.transpose(state_mlstm_m.astype(f32), (0, 2, 1))
    h_t = jnp.transpose(state_ssd.astype(f32), (0, 2, 3, 4, 1))
    s_t = jnp.transpose(state_hgrn.astype(f32), (0, 2, 3, 4, 1))
    conv_halo = jnp.transpose(state_ssd_conv.astype(f32), (0, 2, 1, 3)).reshape(depth, (SSD_CONV - 1) * nb, SSD_CONV_DIM)
    f_t = jnp.transpose(state_ffn_conv.astype(f32), (0, 2, 1, 3)).reshape(depth, (FFN_CONV - 1) * nb, 2 * D_FF)

    p_states, s_sconv = [], []
    ml_prev = ssd_prev = hg_prev = ffn_prev = None
    for l in range(depth):
        last = l == depth - 1

        ml, ssd, hg, gates = _inproj(xp, n1, w_cat, l)
        gates3 = gates.reshape(bp, tp, GATE_W)
        ssd3 = ssd.reshape(bp, tp, SSD_COLS)
        hm, p_c, p_n, p_m = _mlstm_prompt(ml.reshape(bp, tp, ML_COLS), gates3, gate_bias, mlw, l)
        hs, p_h = _ssd_prompt(ssd3, gates3, gate_bias, alog_row, cw, cb, d_row, ssw, l)
        ho, p_s = _hgrn_prompt(hg.reshape(bp, tp, HG_COLS), lbp, hgw, l)
        xp, p_fg, p_fv = _ffn(xp, hm.reshape(bp * tp, ML_W), hs.reshape(bp * tp, SSD_W), ho.reshape(bp * tp, HG_W),
                              wo_b, n2, wu_b, fcw, fcb, wd_b, zero_f, fin_w, layer=l,
                              groups=bp, tm=ROW_TILE, shift=1, jb=D_FF // FF_BLOCK, final=last)
        p_states.append((p_c, p_n, p_m[:, 0, 0:ML_HEADS],
                         p_h.reshape(bp, SSD_HEADS, HEAD_W, SSD_DSTATE),
                         ssd3[:, tp - (SSD_CONV - 1):, SSD_W:], p_s,
                         jnp.concatenate([p_fg, p_fv], axis=-1)))

        ml, ssd, hg, gates = _inproj(xs, n1, w_cat, l)
        hm, *ml_prev = _mlstm_sample(ml, gates, gate_bias, mlw, c_t, n_t, m_t, l, ml_prev)
        xc = _ssd_conv_sample(ssd, conv_halo, cw, cb, l)
        hs, *ssd_prev = _ssd_sample(ssd, xc, gates, gate_bias, head_params, ssw, h_t, l, ssd_prev)
        ho, *hg_prev = _hgrn_sample(hg, lbp, hgw, s_t, l, hg_prev)
        xs, *ffn_prev = _ffn(xs, hm, hs, ho, wo_b, n2, wu_b, fcw, fcb, wd_b, f_t, fin_w, layer=l,
                             groups=1, tm=ts * nb, shift=nb, jb=1, final=last,
                             f0_base=l, f1_groups=depth, f1_base=l, prev=ffn_prev)
        raw_xbc = ssd[:, SSD_W:].reshape(ts, nb, SSD_CONV_DIM)
        s_sconv.append(jnp.transpose(raw_xbc[ts - (SSD_CONV - 1):], (1, 0, 2)))

    y_prompt = xp.reshape(bp, tp, D_MODEL)
    y_sample = jnp.transpose(xs.reshape(ts, nb, D_MODEL), (1, 0, 2))
    p_out = [jnp.stack([st[i] for st in p_states]) for i in range(7)]
    s_c, s_n, s_m = ml_prev
    s_f = jnp.concatenate(ffn_prev, axis=-1).reshape(depth, FFN_CONV - 1, nb, 2 * D_FF)
    s_out = [jnp.transpose(s_c, (0, 4, 1, 2, 3)), jnp.transpose(s_n, (0, 3, 1, 2)), jnp.transpose(s_m, (0, 2, 1)),
             jnp.transpose(ssd_prev[0], (0, 4, 1, 2, 3)), jnp.stack(s_sconv),
             jnp.transpose(hg_prev[0], (0, 4, 1, 2, 3)), jnp.transpose(s_f, (0, 2, 1, 3))]
    return (y_prompt, y_sample, *p_out, *s_out)
```

```python
import functools

import jax
import jax.numpy as jnp
from jax import lax
from jax.experimental import pallas as pl
from jax.experimental.pallas import tpu as pltpu

f32 = jnp.float32
bf16 = jnp.bfloat16

D_MODEL = 1024
DEPTH = 2
ML_HEADS = 6
HEAD_W = 64
ML_W = 384
SSD_HEADS = 6
SSD_W = 384
SSD_DSTATE = 128
SSD_CONV = 4
SSD_CONV_DIM = 896
HG_HEADS = 4
HG_W = 256
D_FF = 2816
FFN_CONV = 3
EPS = 1e-6
PROMPT_CHUNK = 128
HG_CHUNK = 64
LANES = 128
FF_BLOCK = 256
ROW_TILE = 512
SEQS_PER_STEP = 2
GATE_W = 128
W_IN_COLS = 4 * ML_W + (SSD_W + SSD_CONV_DIM) + 4 * HG_W + GATE_W
VMEM_LIMIT_BYTES = 56 * 1024 * 1024

NEG_INF = float("-inf")


def _cparams(*sem):
    return pltpu.CompilerParams(dimension_semantics=sem, vmem_limit_bytes=VMEM_LIMIT_BYTES)


def _resident(shape):
    nd = len(shape)
    return pl.BlockSpec(shape, lambda *_: (0,) * nd, pipeline_mode=pl.Buffered(1))


def _layer_resident(shape, layer):
    nd = len(shape)
    return pl.BlockSpec((None,) + tuple(shape), lambda *_: (layer,) + (0,) * nd, pipeline_mode=pl.Buffered(1))


def _dot(a, b):
    return jnp.dot(a, b, preferred_element_type=f32)


def _dot_nt(a, b):
    return lax.dot_general(a, b, (((1,), (1,)), ((), ())), preferred_element_type=f32)


def _split_bf16(x, parts):
    out = []
    r = x
    for _ in range(parts):
        h = r.astype(bf16)
        out.append(h)
        r = r - h.astype(f32)
    return out


def _cumsum_rows(x, tri_b):
    acc = None
    for part in _split_bf16(x, 3):
        d = _dot(tri_b, part)
        acc = d if acc is None else acc + d
    return acc


def _group_mean(x, bd_b):
    hi, lo = _split_bf16(x, 2)
    return (_dot(hi, bd_b) + _dot(lo, bd_b)) * (1.0 / HEAD_W)


def _iota2(shape, axis):
    return lax.broadcasted_iota(jnp.int32, shape, axis)


def _ones_where(mask):
    return jnp.where(mask, 1.0, 0.0).astype(bf16)


def _same_head(a, b):
    return jnp.right_shift(a, 6) == jnp.right_shift(b, 6)


def _silu(x):
    return x * jax.nn.sigmoid(x)


ML_COLS = 4 * ML_W
SSD_COLS = SSD_W + SSD_CONV_DIM
HG_COLS = 4 * HG_W


def _inproj_body(x_ref, nw_ref, w_ref, ml_ref, ssd_ref, hg_ref, g_ref):
    x = x_ref[...]
    h = (x * lax.rsqrt(jnp.mean(x * x, axis=-1, keepdims=True) + EPS) * nw_ref[...]).astype(bf16)
    o0, o1, o2 = ML_COLS, ML_COLS + SSD_COLS, ML_COLS + SSD_COLS + HG_COLS
    ml_ref[...] = _dot(h, w_ref[:, 0:o0])
    ssd_ref[...] = _dot(h, w_ref[:, o0:o1])
    hg_ref[...] = _dot(h, w_ref[:, o1:o2])
    g_ref[...] = _dot(h, w_ref[:, o2:W_IN_COLS])


def _inproj(x, norm_w, w_cat, layer):
    n = x.shape[0]
    tm = ROW_TILE
    row = lambda w: pl.BlockSpec((tm, w), lambda i: (i, 0))
    return pl.pallas_call(
        _inproj_body,
        grid=(n // tm,),
        in_specs=[row(D_MODEL), _layer_resident((1, D_MODEL), layer), _layer_resident((D_MODEL, W_IN_COLS), layer)],
        out_specs=[row(ML_COLS), row(SSD_COLS), row(HG_COLS), row(GATE_W)],
        out_shape=[jax.ShapeDtypeStruct((n, w), f32) for w in (ML_COLS, SSD_COLS, HG_COLS, GATE_W)],
        compiler_params=_cparams("arbitrary"),
        name="inproj",
    )(x, norm_w, w_cat)


W_IN_GATE_IF = ML_COLS
W_IN_SSD = W_IN_GATE_IF + 2 * ML_HEADS
W_IN_DT = W_IN_SSD + SSD_COLS
W_IN_HG = W_IN_DT + SSD_HEADS
W_IN_DIM = W_IN_HG + HG_COLS
W_PREP_ROWS = 256


def _prep_w_in_body(w_ref, out_ref):
    lane = _iota2((W_PREP_ROWS, LANES), 1)
    tile = lambda k: w_ref[:, k * LANES:(k + 1) * LANES]
    out_ref[:, 0:ML_COLS] = w_ref[:, 0:ML_COLS].astype(bf16)

    def shifted(dst0, src0, ntiles):
        t0, sh = src0 // LANES, src0 % LANES
        rolled = [pltpu.roll(tile(t0 + k), LANES - sh, axis=1) for k in range(ntiles + 1)]
        for k in range(ntiles):
            out_ref[:, dst0 + k * LANES:dst0 + (k + 1) * LANES] = jnp.where(
                lane < LANES - sh, rolled[k], rolled[k + 1]).astype(bf16)

    shifted(ML_COLS, W_IN_SSD, SSD_COLS // LANES)
    shifted(ML_COLS + SSD_COLS, W_IN_HG, HG_COLS // LANES)
    assert W_IN_GATE_IF % LANES == 0 and W_IN_DT % LANES == DT_LANE
    gates = jnp.where(lane < DT_LANE, tile(W_IN_GATE_IF // LANES),
                      jnp.where(lane < DT_LANE + SSD_HEADS, tile(W_IN_DT // LANES), 0.0))
    out_ref[:, ML_COLS + SSD_COLS + HG_COLS:W_IN_COLS] = gates.astype(bf16)


def _prep_w_in(w_in):
    depth, d, n = w_in.shape
    assert n == W_IN_DIM and d % W_PREP_ROWS == 0
    return pl.pallas_call(
        _prep_w_in_body,
        grid=(depth, d // W_PREP_ROWS),
        in_specs=[pl.BlockSpec((None, W_PREP_ROWS, W_IN_COLS), lambda l, i: (l, i, 0))],
        out_specs=pl.BlockSpec((None, W_PREP_ROWS, W_IN_COLS), lambda l, i: (l, i, 0)),
        out_shape=jax.ShapeDtypeStruct((depth, d, W_IN_COLS), bf16),
        compiler_params=_cparams("arbitrary", "arbitrary"),
        name="prep_w_in",
    )(w_in)


def _one_seq(ref, bi):
    return ref.at[pl.ds(bi, 1)]


def _mlstm_prompt_body(ml_ref, g_ref, gb_ref, nw_ref, out_ref, c1_ref, n1_ref, m1_ref, ct_s, nr_s, m_s):
    c = pl.program_id(1)
    seqs = range(ml_ref.shape[0])

    @pl.when(c == 0)
    def _():
        ct_s[...] = jnp.zeros(ct_s.shape, f32)
        nr_s[...] = jnp.zeros(nr_s.shape, f32)
        m_s[...] = jnp.zeros(m_s.shape, f32)

    for bi in seqs:
        _mlstm_prompt_chunk(_one_seq(ml_ref, bi), _one_seq(g_ref, bi), gb_ref, nw_ref, _one_seq(out_ref, bi),
                            ct_s.at[bi], nr_s.at[bi], m_s.at[bi])

    @pl.when(c == pl.num_programs(1) - 1)
    def _():
        for bi in seqs:
            for p in range(ML_HEADS // 2):
                c_pair = ct_s[bi, p].T
                c1_ref[bi, 2 * p] = c_pair[0:HEAD_W, 0:HEAD_W]
                c1_ref[bi, 2 * p + 1] = c_pair[HEAD_W:LANES, HEAD_W:LANES]
                n1_ref[bi, 2 * p:2 * p + 1, :] = nr_s[bi, p, 0:1, 0:HEAD_W]
                n1_ref[bi, 2 * p + 1:2 * p + 2, :] = nr_s[bi, p, 1:2, HEAD_W:LANES]
            m1_ref[bi] = m_s[bi]


def _mlstm_prompt_chunk(ml_ref, g_ref, gb_ref, nw_ref, out_ref, ct_s, nr_s, m_s):
    L = PROMPT_CHUNK
    row = _iota2((L, L), 0)
    col = _iota2((L, L), 1)
    causal = row <= col
    tri_b = _ones_where(causal)
    top = row < HEAD_W
    lo_half = col < HEAD_W
    bd_mask = _same_head(row, col)
    sub8 = _iota2((8, L), 0)
    lo8 = _iota2((8, L), 1) < HEAD_W
    n_mask = jnp.where(sub8 == 0, jnp.where(lo8, 1.0, 0.0), jnp.where(sub8 == 1, jnp.where(lo8, 0.0, 1.0), 0.0))

    g_t = (g_ref[0] + gb_ref[...]).T
    b_t = None
    for part in _split_bf16(jax.nn.log_sigmoid(g_t), 3):
        d = _dot(part, tri_b)
        b_t = d if b_t is None else b_t + d
    z_nat = (pltpu.roll(b_t, L - ML_HEADS, axis=0) - g_t).T
    m_old = m_s[...]
    m_next = m_old

    for p in range(ML_HEADS // 2):
        sl = slice(p * LANES, (p + 1) * LANES)
        q2 = ml_ref[0, :, sl]
        k2 = ml_ref[0, :, ML_W + p * LANES:ML_W + (p + 1) * LANES] * (HEAD_W ** -0.5)
        v2 = ml_ref[0, :, 2 * ML_W + p * LANES:2 * ML_W + (p + 1) * LANES]
        o2 = ml_ref[0, :, 3 * ML_W + p * LANES:3 * ML_W + (p + 1) * LANES]
        q_t = q2.T.astype(bf16)
        v_t = v2.T
        kb = k2.astype(bf16)
        k_sel = (jnp.where(lo_half, k2, 0.0).astype(bf16), jnp.where(lo_half, 0.0, k2).astype(bf16))
        sw, den, a_int, e_m, a_st, w_s = [], [], [], [], [], []
        for j in range(2):
            h = 2 * p + j
            b_row = b_t[ML_HEADS + h:ML_HEADS + h + 1, :]
            i_row = g_t[h:h + 1, :]
            m_prev = m_old[h:h + 1, :]
            dm = jnp.where(causal, b_row - z_nat[:, h:h + 1], NEG_INF)
            inter = b_row + m_prev
            m_t = jnp.maximum(inter, jnp.max(dm, axis=0, keepdims=True))
            s_w = _dot(k_sel[j], q_t) * jnp.exp(dm - m_t)
            den.append(jnp.sum(s_w, axis=0, keepdims=True))
            sw.append(s_w.astype(bf16))
            a_int.append(jnp.exp(inter - m_t))
            e_m.append(jnp.exp(-m_t))
            m_new = jnp.broadcast_to(m_t[:, L - 1:L], (1, L))
            b_last = jnp.broadcast_to(b_row[:, L - 1:L], (1, L))
            a_st.append(jnp.exp(b_last + m_prev - m_new))
            w_s.append(jnp.exp(b_last - b_row + i_row - m_new))
            m_next = jnp.where(sub8 == h, m_new, m_next)
        ct = ct_s[p]
        nr = nr_s[p]
        qn = _dot(nr.astype(bf16), q_t)
        num = (_dot(jnp.where(top, v_t, 0.0).astype(bf16), sw[0]) + _dot(jnp.where(top, 0.0, v_t).astype(bf16), sw[1])
               + jnp.where(top, a_int[0], a_int[1]) * _dot(ct.astype(bf16), q_t))
        den0 = den[0] + a_int[0] * qn[0:1, :]
        den1 = den[1] + a_int[1] * qn[1:2, :]
        hh = num / jnp.where(top, jnp.maximum(jnp.abs(den0), e_m[0]), jnp.maximum(jnp.abs(den1), e_m[1]))
        sq = hh * hh
        r0 = lax.rsqrt(jnp.sum(sq[0:HEAD_W, :], axis=0, keepdims=True) * (1.0 / HEAD_W) + EPS)
        r1 = lax.rsqrt(jnp.sum(sq[HEAD_W:LANES, :], axis=0, keepdims=True) * (1.0 / HEAD_W) + EPS)
        y = (hh * jnp.where(top, r0, r1)).T * nw_ref[:, sl] * jax.nn.sigmoid(o2)
        out_ref[0, :, sl] = y.astype(bf16)
        vw = (v_t * jnp.where(top, w_s[0], w_s[1])).astype(bf16)
        ct_s[p] = jnp.where(top, a_st[0], a_st[1]) * ct + jnp.where(bd_mask, _dot(vw, kb), 0.0)
        w8 = jnp.where(sub8 == 0, w_s[0], jnp.where(sub8 == 1, w_s[1], 0.0)).astype(bf16)
        nr_s[p] = jnp.where(sub8 == 0, a_st[0], a_st[1]) * nr + n_mask * _dot(w8, kb)
    m_s[...] = m_next


def _mlstm_prompt(ml, gates, gate_bias, norm_w, layer):
    b, t, _ = ml.shape
    L = PROMPT_CHUNK
    ns = SEQS_PER_STEP
    npair = ML_HEADS // 2
    assert b % ns == 0
    return pl.pallas_call(
        _mlstm_prompt_body,
        grid=(b // ns, t // L),
        in_specs=[pl.BlockSpec((ns, L, ML_COLS), lambda i, c: (i, c, 0)),
                  pl.BlockSpec((ns, L, GATE_W), lambda i, c: (i, c, 0)),
                  _layer_resident((1, GATE_W), layer), _layer_resident((1, ML_W), layer)],
        out_specs=[pl.BlockSpec((ns, L, ML_W), lambda i, c: (i, c, 0)),
                   pl.BlockSpec((ns, ML_HEADS, HEAD_W, HEAD_W), lambda i, c: (i, 0, 0, 0)),
                   pl.BlockSpec((ns, ML_HEADS, HEAD_W), lambda i, c: (i, 0, 0)),
                   pl.BlockSpec((ns, 8, LANES), lambda i, c: (i, 0, 0))],
        out_shape=[jax.ShapeDtypeStruct((b, t, ML_W), bf16),
                   jax.ShapeDtypeStruct((b, ML_HEADS, HEAD_W, HEAD_W), f32),
                   jax.ShapeDtypeStruct((b, ML_HEADS, HEAD_W), f32),
                   jax.ShapeDtypeStruct((b, 8, LANES), f32)],
        scratch_shapes=[pltpu.VMEM((ns, npair, LANES, LANES), f32), pltpu.VMEM((ns, npair, 8, LANES), f32),
                        pltpu.VMEM((ns, 8, LANES), f32)],
        compiler_params=_cparams("arbitrary", "arbitrary"),
        name="mlstm_prompt",
    )(ml, gates, gate_bias, norm_w)


DT_LANE = 2 * ML_HEADS
HALO = 8


def _ssd_prompt_body(ssd_ref, g_ref, gb_ref, al_ref, cw_ref, cb_ref, d_ref, nw_ref, out_ref, h1_ref,
                     xbuf_s, h_s):
    c = pl.program_id(1)
    seqs = range(ssd_ref.shape[0])

    @pl.when(c == 0)
    def _():
        xbuf_s[:, 0:HALO, :] = jnp.zeros((xbuf_s.shape[0], HALO, SSD_CONV_DIM), f32)
        h_s[...] = jnp.zeros(h_s.shape, f32)

    for bi in seqs:
        _ssd_prompt_chunk(_one_seq(ssd_ref, bi), _one_seq(g_ref, bi), gb_ref, al_ref, cw_ref, cb_ref, d_ref, nw_ref,
                          _one_seq(out_ref, bi), xbuf_s.at[bi], h_s.at[bi])

    @pl.when(c == pl.num_programs(1) - 1)
    def _():
        h1_ref[...] = h_s[...]


def _ssd_prompt_chunk(ssd_ref, g_ref, gb_ref, al_ref, cw_ref, cb_ref, d_ref, nw_ref, out_ref, xbuf_s, h_s):
    L = PROMPT_CHUNK
    xbuf_s[HALO:HALO + L, :] = ssd_ref[0, :, SSD_W:SSD_COLS]
    conv = cb_ref[...]
    for j in range(SSD_CONV):
        o = HALO - (SSD_CONV - 1) + j
        conv = conv + cw_ref[j:j + 1, :] * xbuf_s[o:o + L, :]
    xbuf_s[0:HALO, :] = xbuf_s[L:L + HALO, :]
    xc = _silu(conv)
    nb = SSD_W
    b_b = [xc[:, nb + gi * LANES:nb + (gi + 1) * LANES].astype(bf16) for gi in range(2)]
    c_b = [xc[:, nb + (2 + gi) * LANES:nb + (3 + gi) * LANES].astype(bf16) for gi in range(2)]

    row = _iota2((L, L), 0)
    col = _iota2((L, L), 1)
    tril = col <= row
    lo_half = col < HEAD_W
    tri_b = _ones_where(tril)
    bd_b = _ones_where(_same_head(row, col))
    rowc = _iota2((L, 1), 0)

    g = g_ref[0] + gb_ref[...]
    dt = jax.nn.softplus(g)
    cum = _cumsum_rows(dt * (-jnp.exp(al_ref[...])), tri_b)
    cum_t = cum.T
    dt_t = dt.T
    cb_g = [_dot_nt(c_b[gi], b_b[gi]) for gi in range(2)]

    for p in range(SSD_HEADS // 2):
        sl = slice(p * LANES, (p + 1) * LANES)
        x2 = xc[:, sl]
        z2 = ssd_ref[0, :, sl]
        hp = h_s[p]
        hb = hp.astype(bf16)
        grp = [(2 * p + j) // (SSD_HEADS // 2) for j in range(2)]
        ys = None
        ecum, w_s, a_last, ch = [], [], [], []
        for j in range(2):
            ln = DT_LANE + 2 * p + j
            cum_col = cum[:, ln:ln + 1]
            cum_row = cum_t[ln:ln + 1, :]
            cum_last = cum_col[L - 1:L, :]
            lm = jnp.exp(jnp.where(tril, cum_col - cum_row, NEG_INF))
            m = (cb_g[grp[j]] * lm * dt_t[ln:ln + 1, :]).astype(bf16)
            xm = jnp.where(lo_half, x2, 0.0) if j == 0 else jnp.where(lo_half, 0.0, x2)
            d = _dot(m, xm.astype(bf16))
            ys = d if ys is None else ys + d
            ecum.append(jnp.exp(cum_col))
            w_s.append(jnp.exp(cum_last - cum_col) * dt[:, ln:ln + 1])
            a_last.append(jnp.exp(cum_last))
            if j == 0 or grp[1] != grp[0]:
                ch.append(_dot_nt(c_b[grp[j]], hb))
            else:
                ch.append(ch[0])
        ys = ys + jnp.where(lo_half, ecum[0] * ch[0], ecum[1] * ch[1])
        xw_t = (x2 * jnp.where(lo_half, w_s[0], w_s[1])).T.astype(bf16)
        if grp[0] == grp[1]:
            upd = _dot(xw_t, b_b[grp[0]])
        else:
            upd = jnp.where(row < HEAD_W, _dot(xw_t, b_b[grp[0]]), _dot(xw_t, b_b[grp[1]]))
        h_s[p] = jnp.where(rowc < HEAD_W, a_last[0], a_last[1]) * hp + upd
        yy = (ys + d_ref[:, sl] * x2) * _silu(z2)
        ms = _group_mean(yy * yy, bd_b)
        out_ref[0, :, sl] = (yy * lax.rsqrt(ms + EPS) * nw_ref[:, sl]).astype(bf16)


def _ssd_prompt(ssd, gates, gate_bias, alog_row, conv_w, conv_b, d_row, norm_w, layer):
    b, t, _ = ssd.shape
    L = PROMPT_CHUNK
    npair = SSD_HEADS // 2
    ns = SEQS_PER_STEP
    assert b % ns == 0
    return pl.pallas_call(
        _ssd_prompt_body,
        grid=(b // ns, t // L),
        in_specs=[pl.BlockSpec((ns, L, SSD_COLS), lambda i, c: (i, c, 0)),
                  pl.BlockSpec((ns, L, GATE_W), lambda i, c: (i, c, 0)),
                  _layer_resident((1, GATE_W), layer), _layer_resident((1, GATE_W), layer),
                  _layer_resident((SSD_CONV, SSD_CONV_DIM), layer), _layer_resident((1, SSD_CONV_DIM), layer),
                  _layer_resident((1, SSD_W), layer), _layer_resident((1, SSD_W), layer)],
        out_specs=[pl.BlockSpec((ns, L, SSD_W), lambda i, c: (i, c, 0)),
                   pl.BlockSpec((ns, npair, LANES, SSD_DSTATE), lambda i, c: (i, 0, 0, 0))],
        out_shape=[jax.ShapeDtypeStruct((b, t, SSD_W), bf16),
                   jax.ShapeDtypeStruct((b, npair, LANES, SSD_DSTATE), f32)],
        scratch_shapes=[pltpu.VMEM((ns, HALO + L, SSD_CONV_DIM), f32),
                        pltpu.VMEM((ns, npair, LANES, SSD_DSTATE), f32)],
        compiler_params=_cparams("arbitrary", "arbitrary"),
        name="ssd_prompt",
    )(ssd, gates, gate_bias, alog_row, conv_w, conv_b, d_row, norm_w)


def _hg_lower_bound(lbp, layer):
    mx = jnp.max(lbp, axis=0, keepdims=True)
    e = jnp.exp(lbp - mx)
    den = jnp.sum(e, axis=0, keepdims=True)
    lb = jnp.zeros_like(den)
    for j in range(1, layer + 1):
        lb = lb + e[j:j + 1, :] / den
    return lb


def _hg_rows(L):
    offs, off = [], 0
    for s in range(L):
        offs.append(off)
        off += L - 8 * (s // 8)
    return offs, off


def _hgrn_prompt_body(hg_ref, lbp_ref, nw_ref, out_ref, s1_ref, sbd_s, p_s, r_s, o_s, g_s, k_s, v_s, q_s,
                      kt_s, qt_s, vb_s, *, layer):
    c = pl.program_id(1)
    seqs = range(hg_ref.shape[0])

    @pl.when(c == 0)
    def _():
        sbd_s[...] = jnp.zeros(sbd_s.shape, f32)

    for bi in seqs:
        _hgrn_prompt_chunk(_one_seq(hg_ref, bi), lbp_ref, nw_ref, _one_seq(out_ref, bi), sbd_s.at[bi], p_s.at[bi],
                           r_s.at[bi], o_s.at[bi], g_s.at[bi], k_s.at[bi], v_s.at[bi], q_s.at[bi], kt_s.at[bi],
                           qt_s.at[bi], vb_s.at[bi], layer=layer)

    @pl.when(c == pl.num_programs(1) - 1)
    def _():
        for bi in seqs:
            for p in range(HG_HEADS // 2):
                sbd = sbd_s[bi, p]
                s1_ref[bi, 2 * p] = sbd[0:HEAD_W, 0:HEAD_W]
                s1_ref[bi, 2 * p + 1] = sbd[HEAD_W:LANES, HEAD_W:LANES]


def _hgrn_prompt_chunk(hg_ref, lbp_ref, nw_ref, out_ref, sbd_s, p_s, r_s, o_s, g_s, k_s, v_s, q_s, kt_s, qt_s, vb_s,
                       *, layer):
    L = HG_CHUNK
    nblk = L // 8
    lb = _hg_lower_bound(lbp_ref[...], layer)
    ff = hg_ref[0, :, HG_W:2 * HG_W]
    sig = jax.nn.sigmoid(ff)
    k_s[...] = (1.0 - lb) * jax.nn.sigmoid(-ff)
    v_s[...] = hg_ref[0, :, 2 * HG_W:3 * HG_W]
    q_s[...] = hg_ref[0, :, 0:HG_W] * (HEAD_W ** -0.5)
    row = _iota2((L, L), 0)
    col = _iota2((L, L), 1)
    tri_b = _ones_where(col <= row)
    g_s[...] = _cumsum_rows(jnp.log(lb + (1.0 - lb) * sig), tri_b)

    r2 = _iota2((2 * LANES, 2 * LANES), 0)
    c2 = _iota2((2 * LANES, 2 * LANES), 1)
    bd4_b = _ones_where(_same_head(r2, c2))
    r1 = _iota2((LANES, LANES), 0)
    c1 = _iota2((LANES, LANES), 1)
    bd_mask = _same_head(r1, c1)
    bd_b = _ones_where(bd_mask)

    for i in range(1, nblk):
        blk = slice(8 * i, 8 * i + 8)
        r_i = g_s[8 * i - 1:8 * i, :]
        kt_s[i - 1, 0:8 * i, :] = k_s[0:8 * i, :] * jnp.exp(r_i - g_s[0:8 * i, :])
        qt_s[blk, :] = q_s[blk, :] * jnp.exp(g_s[blk, :] - r_i)

    offs, total = _hg_rows(L)
    sub = _iota2((8, HG_W), 0)
    for j in range(nblk):
        blk = slice(8 * j, 8 * j + 8)
        g_blk, k_blk, q_blk, v_blk = g_s[blk, :], k_s[blk, :], q_s[blk, :], v_s[blk, :]
        kt_blk = [kt_s[i - 1, blk, :] for i in range(j + 1, nblk)]
        for r in range(8):
            s = 8 * j + r
            vb_s[8 * s:8 * s + 8, :] = jnp.broadcast_to(v_blk[r:r + 1, :], (8, HG_W))
            e = jnp.exp(jnp.where(sub >= r, g_blk - g_blk[r:r + 1, :], NEG_INF))
            p_s[offs[s]:offs[s] + 8, :] = q_blk * k_blk[r:r + 1, :] * e
            for i in range(j + 1, nblk):
                o = offs[s] + 8 * (i - j)
                p_s[o:o + 8, :] = qt_s[8 * i:8 * i + 8, :] * kt_blk[i - j - 1][r:r + 1, :]
    step = 576
    for r0 in range(0, total, step):
        r_s[r0:r0 + step, :] = _dot(p_s[r0:r0 + step, :].astype(bf16), bd4_b)

    gg = g_s[...]
    qd = q_s[...] * jnp.exp(gg)
    for p in range(HG_HEADS // 2):
        sl = slice(p * LANES, (p + 1) * LANES)
        o_s[:, sl] = _dot(qd[:, sl].astype(bf16), sbd_s[p].astype(bf16))
    for i in range(nblk):
        blk = slice(8 * i, 8 * i + 8)
        acc = o_s[blk, :]
        for s in range(8 * (i + 1)):
            o = offs[s] + 8 * (i - s // 8)
            acc = acc + r_s[o:o + 8, :] * vb_s[8 * s:8 * s + 8, :]
        o_s[blk, :] = acc

    g_last = gg[L - 1:L, :]
    kd = k_s[...] * jnp.exp(g_last - gg)
    vv = v_s[...]
    zpad = jnp.zeros((LANES - L, LANES), f32)
    og = o_s[...]
    gate = hg_ref[0, :, 3 * HG_W:4 * HG_W]
    for p in range(HG_HEADS // 2):
        sl = slice(p * LANES, (p + 1) * LANES)
        kd_t = jnp.concatenate([kd[:, sl], zpad], axis=0).T.astype(bf16)
        v_pad = jnp.concatenate([vv[:, sl], zpad], axis=0).astype(bf16)
        g_t = jnp.concatenate([gg[:, sl], zpad], axis=0).T
        dec = jnp.exp(g_t[:, L - 1:L])
        sbd_s[p] = dec * sbd_s[p] + jnp.where(bd_mask, _dot(kd_t, v_pad), 0.0)
        oo = og[:, sl]
        ms = _group_mean(oo * oo, bd_b)
        out_ref[0, :, sl] = (oo * lax.rsqrt(ms + EPS) * nw_ref[:, sl] * _silu(gate[:, sl])).astype(bf16)


def _hgrn_prompt(hg, lbp, norm_w, layer):
    b, t, _ = hg.shape
    L = HG_CHUNK
    _, total = _hg_rows(L)
    ns = SEQS_PER_STEP
    assert b % ns == 0
    row_buf = pltpu.VMEM((ns, L, HG_W), f32)
    return pl.pallas_call(
        functools.partial(_hgrn_prompt_body, layer=layer),
        grid=(b // ns, t // L),
        in_specs=[pl.BlockSpec((ns, L, HG_COLS), lambda i, c: (i, c, 0)),
                  _resident((DEPTH, HG_W)), _layer_resident((1, HG_W), layer)],
        out_specs=[pl.BlockSpec((ns, L, HG_W), lambda i, c: (i, c, 0)),
                   pl.BlockSpec((ns, HG_HEADS, HEAD_W, HEAD_W), lambda i, c: (i, 0, 0, 0))],
        out_shape=[jax.ShapeDtypeStruct((b, t, HG_W), bf16),
                   jax.ShapeDtypeStruct((b, HG_HEADS, HEAD_W, HEAD_W), f32)],
        scratch_shapes=[pltpu.VMEM((ns, HG_HEADS // 2, LANES, LANES), f32),
                        pltpu.VMEM((ns, total, HG_W), f32), pltpu.VMEM((ns, total, HG_W), f32),
                        row_buf, row_buf, row_buf, row_buf, row_buf,
                        pltpu.VMEM((ns, L // 8 - 1, L, HG_W), f32), row_buf, pltpu.VMEM((ns, 8 * L, HG_W), f32)],
        compiler_params=_cparams("arbitrary", "arbitrary"),
        name="hgrn_prompt",
    )(hg, lbp, norm_w)


NB = 128
TS = 8


def _sample_call(body, name, grid, in_specs, operands, out_specs, out_shapes, scratch, n_state):
    first = len(operands) - n_state
    return pl.pallas_call(
        body, grid=grid, in_specs=in_specs, out_specs=out_specs, out_shape=out_shapes,
        scratch_shapes=scratch, input_output_aliases={first + i: 1 + i for i in range(n_state)},
        compiler_params=_cparams(*(["arbitrary"] * len(grid))), name=name,
    )(*operands)


def _mlstm_sample_body(q_ref, k_ref, v_ref, o_ref, g_ref, gb_ref, nw_ref, c0_ref, n0_ref, m0_ref,
                       out_ref, c1_ref, n1_ref, m1_ref, qt_s, kt_s, vt_s, gt_s, ht_s):
    p = pl.program_id(0)
    n_cur = [n0_ref[hl] for hl in range(2)]
    m_cur = [m0_ref[pl.ds(2 * p + hl, 1), :] for hl in range(2)]

    for t in range(TS):
        c_src = c0_ref if t == 0 else c1_ref
        rows = slice(t * NB, (t + 1) * NB)
        qt_s[...] = q_ref[rows, :].T
        kt_s[...] = k_ref[rows, :].T * (HEAD_W ** -0.5)
        vt_s[...] = v_ref[rows, :].T
        gt_s[...] = (g_ref[rows, :] + gb_ref[...]).T
        for hl in range(2):
            h = 2 * p + hl
            hs = slice(hl * HEAD_W, (hl + 1) * HEAD_W)
            i_t = gt_s[pl.ds(h, 1), :]
            lf = jax.nn.log_sigmoid(gt_s[pl.ds(ML_HEADS + h, 1), :])
            m_new = jnp.maximum(lf + m_cur[hl], i_t)
            a = jnp.exp(lf + m_cur[hl] - m_new)
            w = jnp.exp(i_t - m_new)
            n_new = a * n_cur[hl] + w * kt_s[hs, :]
            v_h = vt_s[hs, :]

            def body(d, num, hl=hl, a=a, w=w, v_h=v_h, c_src=c_src):
                kd = kt_s[pl.ds(hl * HEAD_W + d, 1), :] * w
                qd = qt_s[pl.ds(hl * HEAD_W + d, 1), :]
                cd = a * c_src[hl, d] + kd * v_h
                c1_ref[hl, d] = cd
                return num + qd * cd

            num = lax.fori_loop(0, HEAD_W, body, jnp.zeros((HEAD_W, NB), f32), unroll=4)
            den = jnp.sum(qt_s[hs, :] * n_new, axis=0, keepdims=True)
            hh = num / jnp.maximum(jnp.abs(den), jnp.exp(-m_new))
            ms = jnp.mean(hh * hh, axis=0, keepdims=True)
            ht_s[hs, :] = hh * lax.rsqrt(ms + EPS)
            n_cur[hl] = n_new
            m_cur[hl] = m_new
        out_ref[rows, :] = (ht_s[...].T * nw_ref[...] * jax.nn.sigmoid(o_ref[rows, :])).astype(bf16)

    for hl in range(2):
        n1_ref[hl] = n_cur[hl]
        m1_ref[pl.ds(2 * p + hl, 1), :] = m_cur[hl]


def _mlstm_sample(ml, gates, gate_bias, norm_w, c0t, n0t, m0t, layer):
    n = ml.shape[0]
    npair = ML_HEADS // 2
    blk = lambda off: pl.BlockSpec((n, LANES), lambda p, off=off: (0, off + p))
    c_spec = pl.BlockSpec((None, 2, HEAD_W, HEAD_W, NB), lambda p: (layer, p, 0, 0, 0))
    n_spec = pl.BlockSpec((None, 2, HEAD_W, NB), lambda p: (layer, p, 0, 0))
    m_spec = pl.BlockSpec((None, ML_HEADS, NB), lambda p: (layer, 0, 0))
    return _sample_call(
        _mlstm_sample_body, "mlstm_sample", (npair,),
        [blk(0), blk(npair), blk(2 * npair), blk(3 * npair), _resident((n, GATE_W)),
         _layer_resident((1, GATE_W), layer), pl.BlockSpec((None, 1, LANES), lambda p: (layer, 0, p)),
         c_spec, n_spec, m_spec],
        [ml, ml, ml, ml, gates, gate_bias, norm_w, c0t, n0t, m0t],
        [pl.BlockSpec((n, LANES), lambda p: (0, p)), c_spec, n_spec, m_spec],
        [jax.ShapeDtypeStruct((n, ML_W), bf16), jax.ShapeDtypeStruct(c0t.shape, f32),
         jax.ShapeDtypeStruct(n0t.shape, f32), jax.ShapeDtypeStruct(m0t.shape, f32)],
        [pltpu.VMEM((LANES, NB), f32)] * 5, 3)


def _ssd_conv_sample_body(ssd_ref, halo_ref, cw_ref, cb_ref, out_ref, xp_s):
    nh = (SSD_CONV - 1) * NB
    n = TS * NB
    xp_s[0:nh, :] = halo_ref[...]
    xp_s[nh:nh + n, :] = ssd_ref[:, SSD_W:SSD_COLS]
    conv = cb_ref[...]
    for j in range(SSD_CONV):
        conv = conv + cw_ref[j:j + 1, :] * xp_s[j * NB:j * NB + n, :]
    out_ref[...] = _silu(conv)


def _ssd_conv_sample(ssd, halo, conv_w, conv_b, layer):
    n = ssd.shape[0]
    nh = (SSD_CONV - 1) * NB
    return pl.pallas_call(
        _ssd_conv_sample_body,
        grid=(1,),
        in_specs=[_resident((n, SSD_COLS)), _layer_resident((nh, SSD_CONV_DIM), layer),
                  _layer_resident((SSD_CONV, SSD_CONV_DIM), layer), _layer_resident((1, SSD_CONV_DIM), layer)],
        out_specs=pl.BlockSpec((n, SSD_CONV_DIM), lambda i: (0, 0)),
        out_shape=jax.ShapeDtypeStruct((n, SSD_CONV_DIM), f32),
        scratch_shapes=[pltpu.VMEM((nh + n, SSD_CONV_DIM), f32)],
        compiler_params=_cparams("arbitrary"),
        name="ssd_conv_sample",
    )(ssd, halo, conv_w, conv_b)


def _ssd_sample_body(z_ref, x_ref, b0_ref, b1_ref, c0_ref, c1_ref, g_ref, gb_ref, hp_ref, nw_ref, h0_ref,
                     out_ref, h1_ref, xt_s, bt_s, ct_s, gt_s, yt_s):
    p = pl.program_id(0)
    b_refs = (b0_ref, b1_ref)
    c_refs = (c0_ref, c1_ref)

    for t in range(TS):
        h_src = h0_ref if t == 0 else h1_ref
        rows = slice(t * NB, (t + 1) * NB)
        xt_s[...] = x_ref[rows, :].T
        gt_s[...] = (g_ref[rows, :] + gb_ref[...]).T
        zt = z_ref[rows, :].T
        for hl in range(2):
            bt_s[hl] = b_refs[hl][rows, :].T
            ct_s[hl] = c_refs[hl][rows, :].T
        for hl in range(2):
            h = 2 * p + hl
            hs = slice(hl * HEAD_W, (hl + 1) * HEAD_W)
            dt = jax.nn.softplus(gt_s[pl.ds(DT_LANE + h, 1), :])
            da = jnp.exp(dt * (-jnp.exp(hp_ref[0, hl:hl + 1, :])))

            def body(pp, carry, hl=hl, dt=dt, da=da, h_src=h_src):
                xrow = xt_s[pl.ds(hl * HEAD_W + pp, 1), :] * dt
                hn = da * h_src[hl, pp] + xrow * bt_s[hl]
                h1_ref[hl, pp] = hn
                yt_s[pl.ds(hl * HEAD_W + pp, 1), :] = jnp.sum(ct_s[hl] * hn, axis=0, keepdims=True)
                return carry

            lax.fori_loop(0, HEAD_W, body, 0, unroll=2)
            yy = (yt_s[hs, :] + hp_ref[1, hl:hl + 1, :] * xt_s[hs, :]) * _silu(zt[hs, :])
            ms = jnp.mean(yy * yy, axis=0, keepdims=True)
            yt_s[hs, :] = yy * lax.rsqrt(ms + EPS)
        out_ref[rows, :] = (yt_s[...].T * nw_ref[...]).astype(bf16)


def _ssd_sample(ssd, xc, gates, gate_bias, head_params, norm_w, h0t, layer):
    n = ssd.shape[0]
    npair = SSD_HEADS // 2
    hpg = SSD_HEADS // 2
    nxb = SSD_W // LANES
    blk = lambda f: pl.BlockSpec((n, LANES), f)
    h_spec = pl.BlockSpec((None, 2, HEAD_W, SSD_DSTATE, NB), lambda p: (layer, p, 0, 0, 0))
    return _sample_call(
        _ssd_sample_body, "ssd_sample", (npair,),
        [blk(lambda p: (0, p)), blk(lambda p: (0, p)),
         blk(lambda p: (0, nxb + (2 * p) // hpg)), blk(lambda p: (0, nxb + (2 * p + 1) // hpg)),
         blk(lambda p: (0, nxb + 2 + (2 * p) // hpg)), blk(lambda p: (0, nxb + 2 + (2 * p + 1) // hpg)),
         _resident((n, GATE_W)), _layer_resident((1, GATE_W), layer),
         pl.BlockSpec((None, 2, None, 2, LANES), lambda p: (layer, 0, p, 0, 0)),
         pl.BlockSpec((None, 1, LANES), lambda p: (layer, 0, p)), h_spec],
        [ssd, xc, xc, xc, xc, xc, gates, gate_bias, head_params, norm_w, h0t],
        [pl.BlockSpec((n, LANES), lambda p: (0, p)), h_spec],
        [jax.ShapeDtypeStruct((n, SSD_W), bf16), jax.ShapeDtypeStruct(h0t.shape, f32)],
        [pltpu.VMEM((LANES, NB), f32), pltpu.VMEM((2, SSD_DSTATE, NB), f32), pltpu.VMEM((2, SSD_DSTATE, NB), f32),
         pltpu.VMEM((LANES, NB), f32), pltpu.VMEM((LANES, NB), f32)], 1)


def _hgrn_sample_body(q_ref, f_ref, i_ref, gate_ref, lbp_ref, nw_ref, s0_ref, out_ref, s1_ref,
                      qt_s, kt_s, ft_s, vt_s, ot_s, *, layer):
    lb = _hg_lower_bound(lbp_ref[...], layer)

    for t in range(TS):
        s_src = s0_ref if t == 0 else s1_ref
        rows = slice(t * NB, (t + 1) * NB)
        ff = f_ref[rows, :]
        qt_s[...] = (q_ref[rows, :] * (HEAD_W ** -0.5)).T
        kt_s[...] = ((1.0 - lb) * jax.nn.sigmoid(-ff)).T
        ft_s[...] = (lb + (1.0 - lb) * jax.nn.sigmoid(ff)).T
        vt_s[...] = i_ref[rows, :].T
        for hl in range(2):
            hs = slice(hl * HEAD_W, (hl + 1) * HEAD_W)
            v_h = vt_s[hs, :]

            def body(k, o, hl=hl, v_h=v_h, s_src=s_src):
                r = pl.ds(hl * HEAD_W + k, 1)
                sk = ft_s[r, :] * s_src[hl, k] + kt_s[r, :] * v_h
                s1_ref[hl, k] = sk
                return o + qt_s[r, :] * sk

            o = lax.fori_loop(0, HEAD_W, body, jnp.zeros((HEAD_W, NB), f32), unroll=4)
            ms = jnp.mean(o * o, axis=0, keepdims=True)
            ot_s[hs, :] = o * lax.rsqrt(ms + EPS)
        out_ref[rows, :] = (ot_s[...].T * nw_ref[...] * _silu(gate_ref[rows, :])).astype(bf16)


def _hgrn_sample(hg, lbp, norm_w, s0t, layer):
    n = hg.shape[0]
    npair = HG_HEADS // 2
    blk = lambda off: pl.BlockSpec((n, LANES), lambda p, off=off: (0, off + p))
    s_spec = pl.BlockSpec((None, 2, HEAD_W, HEAD_W, NB), lambda p: (layer, p, 0, 0, 0))
    return _sample_call(
        functools.partial(_hgrn_sample_body, layer=layer), "hgrn_sample", (npair,),
        [blk(0), blk(npair), blk(2 * npair), blk(3 * npair),
         pl.BlockSpec((DEPTH, LANES), lambda p: (0, p)),
         pl.BlockSpec((None, 1, LANES), lambda p: (layer, 0, p)), s_spec],
        [hg, hg, hg, hg, lbp, norm_w, s0t],
        [pl.BlockSpec((n, LANES), lambda p: (0, p)), s_spec],
        [jax.ShapeDtypeStruct((n, HG_W), bf16), jax.ShapeDtypeStruct(s0t.shape, f32)],
        [pltpu.VMEM((LANES, NB), f32)] * 5, 1)


def _ffn_body(x_ref, mml_ref, mssd_ref, mhg_ref, wout_ref, n2_ref, wug_ref, wuv_ref, cwg_ref, cwv_ref,
              cbg_ref, cbv_ref, wdn_ref, f0g_ref, f0v_ref, fn_ref, y_ref, f1g_ref, f1v_ref,
              x1_s, h2_s, acc_s, act_s, ubuf_s, *, tm, shift, jb, nff, final):
    t = pl.program_id(1)
    f = pl.program_id(2)
    halo = max(HALO, (FFN_CONV - 1) * shift)
    keep = (FFN_CONV - 1) * shift

    def when(cond, fn):
        if nff == 1:
            fn()
        else:
            pl.when(cond)(fn)

    @pl.when(t == 0)
    def _():
        f1g_ref[0] = f0g_ref[0]
        f1v_ref[0] = f0v_ref[0]

    def first():
        x = x_ref[...]
        x1 = (x + _dot(mml_ref[...], wout_ref[0:ML_W, :]) + _dot(mssd_ref[...], wout_ref[ML_W:ML_W + SSD_W, :])
              + _dot(mhg_ref[...], wout_ref[ML_W + SSD_W:D_MODEL, :]))
        x1_s[...] = x1
        h2_s[...] = (x1 * lax.rsqrt(jnp.mean(x1 * x1, axis=-1, keepdims=True) + EPS) * n2_ref[...]).astype(bf16)

    when(f == 0, first)
    h2 = h2_s[...]

    def conv(wu_ref, cw_ref, cb_ref, f1_ref, cols, ubuf):
        u = _dot(h2, wu_ref[:, cols])
        ubuf[halo - keep:halo, :] = f1_ref[0, :, cols]
        ubuf[halo:halo + tm, :] = u
        yv = (cb_ref[:, cols] + cw_ref[2:3, cols] * u
              + cw_ref[1:2, cols] * ubuf[halo - shift:halo - shift + tm, :]
              + cw_ref[0:1, cols] * ubuf[halo - 2 * shift:halo - 2 * shift + tm, :])
        f1_ref[0, :, cols] = ubuf[halo + tm - keep:halo + tm, :]
        return yv

    nbuf = ubuf_s.shape[0]
    for j in range(jb):
        cols = slice(j * FF_BLOCK, (j + 1) * FF_BLOCK)
        gq = conv(wug_ref, cwg_ref, cbg_ref, f1g_ref, cols, ubuf_s.at[(2 * j) % nbuf])
        vq = conv(wuv_ref, cwv_ref, cbv_ref, f1v_ref, cols, ubuf_s.at[(2 * j + 1) % nbuf])
        act_s[:, cols] = (_silu(gq) * vq).astype(bf16)
    contrib = _dot(act_s[...], wdn_ref[...])

    def finish(down):
        x2 = x1_s[...] + down
        if final:
            x2 = x2 * lax.rsqrt(jnp.mean(x2 * x2, axis=-1, keepdims=True) + EPS) * fn_ref[...]
        y_ref[...] = x2

    if nff == 1:
        finish(contrib)
    else:
        @pl.when(f == 0)
        def _():
            acc_s[...] = contrib

        @pl.when(f != 0)
        def _():
            acc_s[...] = acc_s[...] + contrib

        pl.when(f == nff - 1)(lambda: finish(acc_s[...]))


def _ffn(x, mml, mssd, mhg, w_out, norm2, w_up, conv_w, conv_b, w_down, f0, final_w, *, layer, groups, tm, shift, jb,
         final, f0_base=0):
    n = x.shape[0]
    nt = n // (groups * tm)
    fw = jb * FF_BLOCK
    nff = D_FF // fw
    assert nt == 1 or nff == 1
    halo = max(HALO, (FFN_CONV - 1) * shift)
    keep = (FFN_CONV - 1) * shift
    rows = lambda w: pl.BlockSpec((tm, w), lambda g, t, f: (g * nt + t, 0))
    single = nff == 1
    wspec = lambda shape, imap: pl.BlockSpec(shape, imap, pipeline_mode=pl.Buffered(1)) if single else pl.BlockSpec(shape, imap)
    f1_spec = pl.BlockSpec((1, keep, fw), lambda g, t, f: (g, 0, f))
    in_specs = [rows(D_MODEL), rows(ML_W), rows(SSD_W), rows(HG_W),
                _layer_resident((D_MODEL, D_MODEL), layer), _layer_resident((1, D_MODEL), layer),
                wspec((None, D_MODEL, fw), lambda g, t, f: (layer, 0, f)),
                wspec((None, D_MODEL, fw), lambda g, t, f: (layer, 0, nff + f)),
                wspec((None, FFN_CONV, fw), lambda g, t, f: (layer, 0, f)),
                wspec((None, FFN_CONV, fw), lambda g, t, f: (layer, 0, nff + f)),
                wspec((None, 1, fw), lambda g, t, f: (layer, 0, f)),
                wspec((None, 1, fw), lambda g, t, f: (layer, 0, nff + f)),
                wspec((None, fw, D_MODEL), lambda g, t, f: (layer, f, 0)),
                pl.BlockSpec((1, keep, fw), lambda g, t, f: (f0_base + g, 0, f)),
                pl.BlockSpec((1, keep, fw), lambda g, t, f: (f0_base + g, 0, nff + f)),
                _resident((1, D_MODEL))]
    operands = [x, mml, mssd, mhg, w_out, norm2, w_up, w_up, conv_w, conv_w, conv_b, conv_b, w_down, f0, f0, final_w]
    acc_rows = tm if nff > 1 else 8
    return pl.pallas_call(
        functools.partial(_ffn_body, tm=tm, shift=shift, jb=jb, nff=nff, final=final),
        grid=(groups, nt, nff),
        in_specs=in_specs,
        out_specs=[rows(D_MODEL), f1_spec, f1_spec],
        out_shape=[jax.ShapeDtypeStruct((n, D_MODEL), f32),
                   jax.ShapeDtypeStruct((groups, keep, D_FF), f32),
                   jax.ShapeDtypeStruct((groups, keep, D_FF), f32)],
        scratch_shapes=[pltpu.VMEM((tm, D_MODEL), f32), pltpu.VMEM((tm, D_MODEL), bf16),
                        pltpu.VMEM((acc_rows, D_MODEL), f32), pltpu.VMEM((tm, fw), bf16),
                        pltpu.VMEM((min(4, 2 * jb), halo + tm, FF_BLOCK), f32)],
        compiler_params=_cparams("arbitrary", "arbitrary", "arbitrary"),
        name="ffn",
    )(*operands)


def _gate_rows(*parts):
    r = jnp.concatenate([p.astype(f32) for p in parts], axis=1)
    return jnp.pad(r, ((0, 0), (0, GATE_W - r.shape[1])))[:, None, :]


def kernel(x_prompt, x_sample, state_mlstm_C, state_mlstm_n, state_mlstm_m, state_ssd, state_ssd_conv,
           state_hgrn, state_ffn_conv, norm1_w, w_in, ml_ig_b, ml_fg_b, ml_norm_w, ssd_conv_w, ssd_conv_b,
           ssd_dt_bias, ssd_A_log, ssd_D, ssd_norm_w, hg_lower_bounds, hg_norm_w, w_out, norm2_w, w_up,
           ffn_conv_w, ffn_conv_b, w_down, final_norm_w):
    bp, tp, _ = x_prompt.shape
    nb, ts, _ = x_sample.shape
    assert nb == NB and ts == TS and tp % ROW_TILE == 0
    depth = w_in.shape[0]
    assert depth == DEPTH
    xp = x_prompt.reshape(bp * tp, D_MODEL)
    xs = jnp.transpose(x_sample, (1, 0, 2)).reshape(ts * nb, D_MODEL)
    lbp = hg_lower_bounds.astype(f32)
    fin_w = final_norm_w[None, :]
    zero_f = jnp.zeros((bp, FFN_CONV - 1, 2 * D_FF), f32)

    w_cat = _prep_w_in(w_in)
    wo_b, wu_b, wd_b = w_out.astype(bf16), w_up.astype(bf16), w_down.astype(bf16)
    n1, n2 = norm1_w[:, None, :], norm2_w[:, None, :]
    gate_bias = _gate_rows(ml_ig_b, ml_fg_b, ssd_dt_bias)
    alog_row = _gate_rows(jnp.zeros((depth, DT_LANE), f32), ssd_A_log)
    d_row = jnp.repeat(ssd_D.astype(f32), HEAD_W, axis=1)[:, None, :]
    mlw, ssw, hgw = ml_norm_w[:, None, :], ssd_norm_w[:, None, :], hg_norm_w[:, None, :]
    cw, cb = ssd_conv_w, ssd_conv_b[:, None, :]
    fcw, fcb = ffn_conv_w, ffn_conv_b[:, None, :]
    head_params = jnp.broadcast_to(
        jnp.stack([ssd_A_log, ssd_D], axis=1).astype(f32).reshape(depth, 2, SSD_HEADS // 2, 2, 1),
        (depth, 2, SSD_HEADS // 2, 2, LANES))

    c_t = jnp.transpose(state_mlstm_C.astype(f32), (0, 2, 3, 4, 1))
    n_t = jnp.transpose(state_mlstm_n.astype(f32), (0, 2, 3, 1))
    m_t = jnp.transpose(state_mlstm_m.astype(f32), (0, 2, 1))
    h_t = jnp.transpose(state_ssd.astype(f32), (0, 2, 3, 4, 1))
    s_t = jnp.transpose(state_hgrn.astype(f32), (0, 2, 3, 4, 1))
    conv_halo = jnp.transpose(state_ssd_conv.astype(f32), (0, 2, 1, 3)).reshape(depth, (SSD_CONV - 1) * nb, SSD_CONV_DIM)
    f_t = jnp.transpose(state_ffn_conv.astype(f32), (0, 2, 1, 3)).reshape(depth, (FFN_CONV - 1) * nb, 2 * D_FF)

    p_states, s_sconv, s_ffn = [], [], []
    for l in range(depth):
        last = l == depth - 1

        ml, ssd, hg, gates = _inproj(xp, n1, w_cat, l)
        gates3 = gates.reshape(bp, tp, GATE_W)
        ssd3 = ssd.reshape(bp, tp, SSD_COLS)
        hm, p_c, p_n, p_m = _mlstm_prompt(ml.reshape(bp, tp, ML_COLS), gates3, gate_bias, mlw, l)
        hs, p_h = _ssd_prompt(ssd3, gates3, gate_bias, alog_row, cw, cb, d_row, ssw, l)
        ho, p_s = _hgrn_prompt(hg.reshape(bp, tp, HG_COLS), lbp, hgw, l)
        xp, p_fg, p_fv = _ffn(xp, hm.reshape(bp * tp, ML_W), hs.reshape(bp * tp, SSD_W), ho.reshape(bp * tp, HG_W),
                              wo_b, n2, wu_b, fcw, fcb, wd_b, zero_f, fin_w, layer=l,
                              groups=bp, tm=ROW_TILE, shift=1, jb=D_FF // FF_BLOCK, final=last)
        p_states.append((p_c, p_n, p_m[:, 0:ML_HEADS, 0],
                         p_h.reshape(bp, SSD_HEADS, HEAD_W, SSD_DSTATE),
                         ssd3[:, tp - (SSD_CONV - 1):, SSD_W:], p_s,
                         jnp.concatenate([p_fg, p_fv], axis=-1)))

        ml, ssd, hg, gates = _inproj(xs, n1, w_cat, l)
        hm, c_t, n_t, m_t = _mlstm_sample(ml, gates, gate_bias, mlw, c_t, n_t, m_t, l)
        xc = _ssd_conv_sample(ssd, conv_halo, cw, cb, l)
        hs, h_t = _ssd_sample(ssd, xc, gates, gate_bias, head_params, ssw, h_t, l)
        ho, s_t = _hgrn_sample(hg, lbp, hgw, s_t, l)
        xs, s_fg, s_fv = _ffn(xs, hm, hs, ho, wo_b, n2, wu_b, fcw, fcb, wd_b, f_t, fin_w, layer=l,
                              groups=1, tm=ts * nb, shift=nb, jb=1, final=last, f0_base=l)
        s_ffn.append(jnp.concatenate([s_fg, s_fv], axis=-1).reshape(FFN_CONV - 1, nb, 2 * D_FF))
        raw_xbc = ssd[:, SSD_W:].reshape(ts, nb, SSD_CONV_DIM)
        s_sconv.append(jnp.transpose(raw_xbc[ts - (SSD_CONV - 1):], (1, 0, 2)))

    y_prompt = xp.reshape(bp, tp, D_MODEL)
    y_sample = jnp.transpose(xs.reshape(ts, nb, D_MODEL), (1, 0, 2))
    p_out = [jnp.stack([st[i] for st in p_states]) for i in range(7)]
    s_out = [jnp.transpose(c_t, (0, 4, 1, 2, 3)), jnp.transpose(n_t, (0, 3, 1, 2)), jnp.transpose(m_t, (0, 2, 1)),
             jnp.transpose(h_t, (0, 4, 1, 2, 3)), jnp.stack(s_sconv),
             jnp.transpose(s_t, (0, 4, 1, 2, 3)), jnp.transpose(jnp.stack(s_ffn), (0, 2, 1, 3))]
    return (y_prompt, y_sample, *p_out, *s_out)
```

```python
import functools

import jax
import jax.numpy as jnp
from jax import lax
from jax.experimental import pallas as pl
from jax.experimental.pallas import tpu as pltpu

f32 = jnp.float32
bf16 = jnp.bfloat16

D_MODEL = 1024
DEPTH = 2
ML_HEADS = 6
HEAD_W = 64
ML_W = 384
SSD_HEADS = 6
SSD_W = 384
SSD_DSTATE = 128
SSD_CONV = 4
SSD_CONV_DIM = 896
HG_HEADS = 4
HG_W = 256
D_FF = 2816
FFN_CONV = 3
EPS = 1e-6
PROMPT_CHUNK = 128
HG_CHUNK = 64
LANES = 128
FF_BLOCK = 256
ROW_TILE = 512
SEQS_PER_STEP = 2
GATE_W = 128
W_IN_COLS = 4 * ML_W + (SSD_W + SSD_CONV_DIM) + 4 * HG_W + GATE_W
VMEM_LIMIT_BYTES = 56 * 1024 * 1024

NEG_INF = float("-inf")


def _cparams(*sem):
    return pltpu.CompilerParams(dimension_semantics=sem, vmem_limit_bytes=VMEM_LIMIT_BYTES)


def _resident(shape):
    nd = len(shape)
    return pl.BlockSpec(shape, lambda *_: (0,) * nd, pipeline_mode=pl.Buffered(1))


def _layer_resident(shape, layer):
    nd = len(shape)
    return pl.BlockSpec((None,) + tuple(shape), lambda *_: (layer,) + (0,) * nd, pipeline_mode=pl.Buffered(1))


def _dot(a, b):
    return jnp.dot(a, b, preferred_element_type=f32)


def _dot_nt(a, b):
    return lax.dot_general(a, b, (((1,), (1,)), ((), ())), preferred_element_type=f32)


def _split_bf16(x, parts):
    out = []
    r = x
    for _ in range(parts):
        h = r.astype(bf16)
        out.append(h)
        r = r - h.astype(f32)
    return out


def _cumsum_rows(x, tri_b):
    acc = None
    for part in _split_bf16(x, 3):
        d = _dot(tri_b, part)
        acc = d if acc is None else acc + d
    return acc


def _group_mean(x, bd_b):
    hi, lo = _split_bf16(x, 2)
    return (_dot(hi, bd_b) + _dot(lo, bd_b)) * (1.0 / HEAD_W)


def _iota2(shape, axis):
    return lax.broadcasted_iota(jnp.int32, shape, axis)


def _ones_where(mask):
    return jnp.where(mask, 1.0, 0.0).astype(bf16)


def _same_head(a, b):
    return jnp.right_shift(a, 6) == jnp.right_shift(b, 6)


def _silu(x):
    return x * jax.nn.sigmoid(x)


ML_COLS = 4 * ML_W
SSD_COLS = SSD_W + SSD_CONV_DIM
HG_COLS = 4 * HG_W


def _inproj_body(x_ref, nw_ref, w_ref, ml_ref, ssd_ref, hg_ref, g_ref):
    x = x_ref[...]
    h = (x * lax.rsqrt(jnp.mean(x * x, axis=-1, keepdims=True) + EPS) * nw_ref[...]).astype(bf16)
    o0, o1, o2 = ML_COLS, ML_COLS + SSD_COLS, ML_COLS + SSD_COLS + HG_COLS
    ml_ref[...] = _dot(h, w_ref[:, 0:o0])
    ssd_ref[...] = _dot(h, w_ref[:, o0:o1])
    hg_ref[...] = _dot(h, w_ref[:, o1:o2])
    g_ref[...] = _dot(h, w_ref[:, o2:W_IN_COLS])


def _inproj(x, norm_w, w_cat, layer):
    n = x.shape[0]
    tm = ROW_TILE
    row = lambda w: pl.BlockSpec((tm, w), lambda i: (i, 0))
    return pl.pallas_call(
        _inproj_body,
        grid=(n // tm,),
        in_specs=[row(D_MODEL), _layer_resident((1, D_MODEL), layer), _layer_resident((D_MODEL, W_IN_COLS), layer)],
        out_specs=[row(ML_COLS), row(SSD_COLS), row(HG_COLS), row(GATE_W)],
        out_shape=[jax.ShapeDtypeStruct((n, w), f32) for w in (ML_COLS, SSD_COLS, HG_COLS, GATE_W)],
        compiler_params=_cparams("arbitrary"),
        name="inproj",
    )(x, norm_w, w_cat)


W_IN_GATE_IF = ML_COLS
W_IN_SSD = W_IN_GATE_IF + 2 * ML_HEADS
W_IN_DT = W_IN_SSD + SSD_COLS
W_IN_HG = W_IN_DT + SSD_HEADS
W_IN_DIM = W_IN_HG + HG_COLS
W_PREP_ROWS = 256


def _prep_w_in_body(w_ref, out_ref):
    lane = _iota2((W_PREP_ROWS, LANES), 1)
    tile = lambda k: w_ref[:, k * LANES:(k + 1) * LANES]
    out_ref[:, 0:ML_COLS] = w_ref[:, 0:ML_COLS].astype(bf16)

    def shifted(dst0, src0, ntiles):
        t0, sh = src0 // LANES, src0 % LANES
        rolled = [pltpu.roll(tile(t0 + k), LANES - sh, axis=1) for k in range(ntiles + 1)]
        for k in range(ntiles):
            out_ref[:, dst0 + k * LANES:dst0 + (k + 1) * LANES] = jnp.where(
                lane < LANES - sh, rolled[k], rolled[k + 1]).astype(bf16)

    shifted(ML_COLS, W_IN_SSD, SSD_COLS // LANES)
    shifted(ML_COLS + SSD_COLS, W_IN_HG, HG_COLS // LANES)
    assert W_IN_GATE_IF % LANES == 0 and W_IN_DT % LANES == DT_LANE
    gates = jnp.where(lane < DT_LANE, tile(W_IN_GATE_IF // LANES),
                      jnp.where(lane < DT_LANE + SSD_HEADS, tile(W_IN_DT // LANES), 0.0))
    out_ref[:, ML_COLS + SSD_COLS + HG_COLS:W_IN_COLS] = gates.astype(bf16)


def _prep_w_in(w_in):
    depth, d, n = w_in.shape
    assert n == W_IN_DIM and d % W_PREP_ROWS == 0
    return pl.pallas_call(
        _prep_w_in_body,
        grid=(depth, d // W_PREP_ROWS),
        in_specs=[pl.BlockSpec((None, W_PREP_ROWS, W_IN_COLS), lambda l, i: (l, i, 0))],
        out_specs=pl.BlockSpec((None, W_PREP_ROWS, W_IN_COLS), lambda l, i: (l, i, 0)),
        out_shape=jax.ShapeDtypeStruct((depth, d, W_IN_COLS), bf16),
        compiler_params=_cparams("arbitrary", "arbitrary"),
        name="prep_w_in",
    )(w_in)


def _one_seq(ref, bi):
    return ref.at[pl.ds(bi, 1)]


def _mlstm_prompt_body(ml_ref, g_ref, gb_ref, nw_ref, out_ref, c1_ref, n1_ref, m1_ref, ct_s, nr_s, m_s):
    c = pl.program_id(1)
    seqs = range(ml_ref.shape[0])

    @pl.when(c == 0)
    def _():
        ct_s[...] = jnp.zeros(ct_s.shape, f32)
        nr_s[...] = jnp.zeros(nr_s.shape, f32)
        m_s[...] = jnp.zeros(m_s.shape, f32)

    for bi in seqs:
        _mlstm_prompt_chunk(_one_seq(ml_ref, bi), _one_seq(g_ref, bi), gb_ref, nw_ref, _one_seq(out_ref, bi),
                            ct_s.at[bi], nr_s.at[bi], m_s.at[bi])

    @pl.when(c == pl.num_programs(1) - 1)
    def _():
        for bi in seqs:
            for p in range(ML_HEADS // 2):
                c_pair = ct_s[bi, p].T
                c1_ref[bi, 2 * p] = c_pair[0:HEAD_W, 0:HEAD_W]
                c1_ref[bi, 2 * p + 1] = c_pair[HEAD_W:LANES, HEAD_W:LANES]
                n1_ref[bi, 2 * p:2 * p + 1, :] = nr_s[bi, p, 0:1, 0:HEAD_W]
                n1_ref[bi, 2 * p + 1:2 * p + 2, :] = nr_s[bi, p, 1:2, HEAD_W:LANES]
            m1_ref[bi] = m_s[bi]


def _mlstm_prompt_chunk(ml_ref, g_ref, gb_ref, nw_ref, out_ref, ct_s, nr_s, m_s):
    L = PROMPT_CHUNK
    row = _iota2((L, L), 0)
    col = _iota2((L, L), 1)
    causal = row <= col
    tri_b = _ones_where(causal)
    top = row < HEAD_W
    lo_half = col < HEAD_W
    bd_mask = _same_head(row, col)
    sub8 = _iota2((8, L), 0)
    lo8 = _iota2((8, L), 1) < HEAD_W
    n_mask = jnp.where(sub8 == 0, jnp.where(lo8, 1.0, 0.0), jnp.where(sub8 == 1, jnp.where(lo8, 0.0, 1.0), 0.0))

    g_t = (g_ref[0] + gb_ref[...]).T
    b_t = None
    for part in _split_bf16(jax.nn.log_sigmoid(g_t), 3):
        d = _dot(part, tri_b)
        b_t = d if b_t is None else b_t + d
    z_nat = (pltpu.roll(b_t, L - ML_HEADS, axis=0) - g_t).T
    m_old = m_s[...]
    m_next = m_old

    for p in range(ML_HEADS // 2):
        sl = slice(p * LANES, (p + 1) * LANES)
        q2 = ml_ref[0, :, sl]
        k2 = ml_ref[0, :, ML_W + p * LANES:ML_W + (p + 1) * LANES] * (HEAD_W ** -0.5)
        v2 = ml_ref[0, :, 2 * ML_W + p * LANES:2 * ML_W + (p + 1) * LANES]
        o2 = ml_ref[0, :, 3 * ML_W + p * LANES:3 * ML_W + (p + 1) * LANES]
        q_t = q2.T.astype(bf16)
        v_t = v2.T
        kb = k2.astype(bf16)
        k_sel = (jnp.where(lo_half, k2, 0.0).astype(bf16), jnp.where(lo_half, 0.0, k2).astype(bf16))
        sw, den, a_int, e_m, a_st, w_s = [], [], [], [], [], []
        for j in range(2):
            h = 2 * p + j
            b_row = b_t[ML_HEADS + h:ML_HEADS + h + 1, :]
            i_row = g_t[h:h + 1, :]
            m_prev = m_old[h:h + 1, :]
            dm = jnp.where(causal, b_row - z_nat[:, h:h + 1], NEG_INF)
            inter = b_row + m_prev
            m_t = jnp.maximum(inter, jnp.max(dm, axis=0, keepdims=True))
            s_w = _dot(k_sel[j], q_t) * jnp.exp(dm - m_t)
            den.append(jnp.sum(s_w, axis=0, keepdims=True))
            sw.append(s_w.astype(bf16))
            a_int.append(jnp.exp(inter - m_t))
            e_m.append(jnp.exp(-m_t))
            m_new = jnp.broadcast_to(m_t[:, L - 1:L], (1, L))
            b_last = jnp.broadcast_to(b_row[:, L - 1:L], (1, L))
            a_st.append(jnp.exp(b_last + m_prev - m_new))
            w_s.append(jnp.exp(b_last - b_row + i_row - m_new))
            m_next = jnp.where(sub8 == h, m_new, m_next)
        ct = ct_s[p]
        nr = nr_s[p]
        qn = _dot(nr.astype(bf16), q_t)
        num = (_dot(jnp.where(top, v_t, 0.0).astype(bf16), sw[0]) + _dot(jnp.where(top, 0.0, v_t).astype(bf16), sw[1])
               + jnp.where(top, a_int[0], a_int[1]) * _dot(ct.astype(bf16), q_t))
        den0 = den[0] + a_int[0] * qn[0:1, :]
        den1 = den[1] + a_int[1] * qn[1:2, :]
        hh = num / jnp.where(top, jnp.maximum(jnp.abs(den0), e_m[0]), jnp.maximum(jnp.abs(den1), e_m[1]))
        sq = hh * hh
        r0 = lax.rsqrt(jnp.sum(sq[0:HEAD_W, :], axis=0, keepdims=True) * (1.0 / HEAD_W) + EPS)
        r1 = lax.rsqrt(jnp.sum(sq[HEAD_W:LANES, :], axis=0, keepdims=True) * (1.0 / HEAD_W) + EPS)
        y = (hh * jnp.where(top, r0, r1)).T * nw_ref[:, sl] * jax.nn.sigmoid(o2)
        out_ref[0, :, sl] = y.astype(bf16)
        vw = (v_t * jnp.where(top, w_s[0], w_s[1])).astype(bf16)
        ct_s[p] = jnp.where(top, a_st[0], a_st[1]) * ct + jnp.where(bd_mask, _dot(vw, kb), 0.0)
        w8 = jnp.where(sub8 == 0, w_s[0], jnp.where(sub8 == 1, w_s[1], 0.0)).astype(bf16)
        nr_s[p] = jnp.where(sub8 == 0, a_st[0], a_st[1]) * nr + n_mask * _dot(w8, kb)
    m_s[...] = m_next


def _mlstm_prompt(ml, gates, gate_bias, norm_w, layer):
    b, t, _ = ml.shape
    L = PROMPT_CHUNK
    ns = SEQS_PER_STEP
    npair = ML_HEADS // 2
    assert b % ns == 0
    return pl.pallas_call(
        _mlstm_prompt_body,
        grid=(b // ns, t // L),
        in_specs=[pl.BlockSpec((ns, L, ML_COLS), lambda i, c: (i, c, 0)),
                  pl.BlockSpec((ns, L, GATE_W), lambda i, c: (i, c, 0)),
                  _layer_resident((1, GATE_W), layer), _layer_resident((1, ML_W), layer)],
        out_specs=[pl.BlockSpec((ns, L, ML_W), lambda i, c: (i, c, 0)),
                   pl.BlockSpec((ns, ML_HEADS, HEAD_W, HEAD_W), lambda i, c: (i, 0, 0, 0)),
                   pl.BlockSpec((ns, ML_HEADS, HEAD_W), lambda i, c: (i, 0, 0)),
                   pl.BlockSpec((ns, 8, LANES), lambda i, c: (i, 0, 0))],
        out_shape=[jax.ShapeDtypeStruct((b, t, ML_W), bf16),
                   jax.ShapeDtypeStruct((b, ML_HEADS, HEAD_W, HEAD_W), f32),
                   jax.ShapeDtypeStruct((b, ML_HEADS, HEAD_W), f32),
                   jax.ShapeDtypeStruct((b, 8, LANES), f32)],
        scratch_shapes=[pltpu.VMEM((ns, npair, LANES, LANES), f32), pltpu.VMEM((ns, npair, 8, LANES), f32),
                        pltpu.VMEM((ns, 8, LANES), f32)],
        compiler_params=_cparams("arbitrary", "arbitrary"),
        name="mlstm_prompt",
    )(ml, gates, gate_bias, norm_w)


DT_LANE = 2 * ML_HEADS
HALO = 8


def _ssd_prompt_body(ssd_ref, g_ref, gb_ref, al_ref, cw_ref, cb_ref, d_ref, nw_ref, out_ref, h1_ref,
                     xbuf_s, h_s):
    c = pl.program_id(1)
    seqs = range(ssd_ref.shape[0])

    @pl.when(c == 0)
    def _():
        xbuf_s[:, 0:HALO, :] = jnp.zeros((xbuf_s.shape[0], HALO, SSD_CONV_DIM), f32)
        h_s[...] = jnp.zeros(h_s.shape, f32)

    for bi in seqs:
        _ssd_prompt_chunk(_one_seq(ssd_ref, bi), _one_seq(g_ref, bi), gb_ref, al_ref, cw_ref, cb_ref, d_ref, nw_ref,
                          _one_seq(out_ref, bi), xbuf_s.at[bi], h_s.at[bi])

    @pl.when(c == pl.num_programs(1) - 1)
    def _():
        h1_ref[...] = h_s[...]


def _ssd_prompt_chunk(ssd_ref, g_ref, gb_ref, al_ref, cw_ref, cb_ref, d_ref, nw_ref, out_ref, xbuf_s, h_s):
    L = PROMPT_CHUNK
    xbuf_s[HALO:HALO + L, :] = ssd_ref[0, :, SSD_W:SSD_COLS]
    conv = cb_ref[...]
    for j in range(SSD_CONV):
        o = HALO - (SSD_CONV - 1) + j
        conv = conv + cw_ref[j:j + 1, :] * xbuf_s[o:o + L, :]
    xbuf_s[0:HALO, :] = xbuf_s[L:L + HALO, :]
    xc = _silu(conv)
    nb = SSD_W
    b_b = [xc[:, nb + gi * LANES:nb + (gi + 1) * LANES].astype(bf16) for gi in range(2)]
    c_b = [xc[:, nb + (2 + gi) * LANES:nb + (3 + gi) * LANES].astype(bf16) for gi in range(2)]

    row = _iota2((L, L), 0)
    col = _iota2((L, L), 1)
    tril = col <= row
    lo_half = col < HEAD_W
    tri_b = _ones_where(tril)
    bd_b = _ones_where(_same_head(row, col))
    rowc = _iota2((L, 1), 0)

    g = g_ref[0] + gb_ref[...]
    dt = jax.nn.softplus(g)
    cum = _cumsum_rows(dt * (-jnp.exp(al_ref[...])), tri_b)
    cum_t = cum.T
    dt_t = dt.T
    cb_g = [_dot_nt(c_b[gi], b_b[gi]) for gi in range(2)]

    for p in range(SSD_HEADS // 2):
        sl = slice(p * LANES, (p + 1) * LANES)
        x2 = xc[:, sl]
        z2 = ssd_ref[0, :, sl]
        hp = h_s[p]
        hb = hp.astype(bf16)
        grp = [(2 * p + j) // (SSD_HEADS // 2) for j in range(2)]
        ys = None
        ecum, w_s, a_last, ch = [], [], [], []
        for j in range(2):
            ln = DT_LANE + 2 * p + j
            cum_col = cum[:, ln:ln + 1]
            cum_row = cum_t[ln:ln + 1, :]
            cum_last = cum_col[L - 1:L, :]
            lm = jnp.exp(jnp.where(tril, cum_col - cum_row, NEG_INF))
            m = (cb_g[grp[j]] * lm * dt_t[ln:ln + 1, :]).astype(bf16)
            xm = jnp.where(lo_half, x2, 0.0) if j == 0 else jnp.where(lo_half, 0.0, x2)
            d = _dot(m, xm.astype(bf16))
            ys = d if ys is None else ys + d
            ecum.append(jnp.exp(cum_col))
            w_s.append(jnp.exp(cum_last - cum_col) * dt[:, ln:ln + 1])
            a_last.append(jnp.exp(cum_last))
            if j == 0 or grp[1] != grp[0]:
                ch.append(_dot_nt(c_b[grp[j]], hb))
            else:
                ch.append(ch[0])
        ys = ys + jnp.where(lo_half, ecum[0] * ch[0], ecum[1] * ch[1])
        xw_t = (x2 * jnp.where(lo_half, w_s[0], w_s[1])).T.astype(bf16)
        if grp[0] == grp[1]:
            upd = _dot(xw_t, b_b[grp[0]])
        else:
            upd = jnp.where(row < HEAD_W, _dot(xw_t, b_b[grp[0]]), _dot(xw_t, b_b[grp[1]]))
        h_s[p] = jnp.where(rowc < HEAD_W, a_last[0], a_last[1]) * hp + upd
        yy = (ys + d_ref[:, sl] * x2) * _silu(z2)
        ms = _group_mean(yy * yy, bd_b)
        out_ref[0, :, sl] = (yy * lax.rsqrt(ms + EPS) * nw_ref[:, sl]).astype(bf16)


def _ssd_prompt(ssd, gates, gate_bias, alog_row, conv_w, conv_b, d_row, norm_w, layer):
    b, t, _ = ssd.shape
    L = PROMPT_CHUNK
    npair = SSD_HEADS // 2
    ns = SEQS_PER_STEP
    assert b % ns == 0
    return pl.pallas_call(
        _ssd_prompt_body,
        grid=(b // ns, t // L),
        in_specs=[pl.BlockSpec((ns, L, SSD_COLS), lambda i, c: (i, c, 0)),
                  pl.BlockSpec((ns, L, GATE_W), lambda i, c: (i, c, 0)),
                  _layer_resident((1, GATE_W), layer), _layer_resident((1, GATE_W), layer),
                  _layer_resident((SSD_CONV, SSD_CONV_DIM), layer), _layer_resident((1, SSD_CONV_DIM), layer),
                  _layer_resident((1, SSD_W), layer), _layer_resident((1, SSD_W), layer)],
        out_specs=[pl.BlockSpec((ns, L, SSD_W), lambda i, c: (i, c, 0)),
                   pl.BlockSpec((ns, npair, LANES, SSD_DSTATE), lambda i, c: (i, 0, 0, 0))],
        out_shape=[jax.ShapeDtypeStruct((b, t, SSD_W), bf16),
                   jax.ShapeDtypeStruct((b, npair, LANES, SSD_DSTATE), f32)],
        scratch_shapes=[pltpu.VMEM((ns, HALO + L, SSD_CONV_DIM), f32),
                        pltpu.VMEM((ns, npair, LANES, SSD_DSTATE), f32)],
        compiler_params=_cparams("arbitrary", "arbitrary"),
        name="ssd_prompt",
    )(ssd, gates, gate_bias, alog_row, conv_w, conv_b, d_row, norm_w)


def _hg_lower_bound(lbp, layer):
    mx = jnp.max(lbp, axis=0, keepdims=True)
    e = jnp.exp(lbp - mx)
    den = jnp.sum(e, axis=0, keepdims=True)
    lb = jnp.zeros_like(den)
    for j in range(1, layer + 1):
        lb = lb + e[j:j + 1, :] / den
    return lb


def _hgrn_prompt_body(hg_ref, lbp_ref, nw_ref, out_ref, s1_ref, sbd_s, p_s, r_s, o_s, g_s, k_s, v_s, q_s,
                      kt_s, a_s, *, layer):
    c = pl.program_id(1)
    seqs = range(hg_ref.shape[0])

    @pl.when(c == 0)
    def _():
        sbd_s[...] = jnp.zeros(sbd_s.shape, f32)

    for bi in seqs:
        _hgrn_prompt_chunk(_one_seq(hg_ref, bi), lbp_ref, nw_ref, _one_seq(out_ref, bi), sbd_s.at[bi], p_s.at[bi],
                           r_s.at[bi], o_s.at[bi], g_s.at[bi], k_s.at[bi], v_s.at[bi], q_s.at[bi], kt_s.at[bi],
                           a_s.at[bi], layer=layer)

    @pl.when(c == pl.num_programs(1) - 1)
    def _():
        for bi in seqs:
            for p in range(HG_HEADS // 2):
                sbd = sbd_s[bi, p]
                s1_ref[bi, 2 * p] = sbd[0:HEAD_W, 0:HEAD_W]
                s1_ref[bi, 2 * p + 1] = sbd[HEAD_W:LANES, HEAD_W:LANES]


def _hgrn_prompt_chunk(hg_ref, lbp_ref, nw_ref, out_ref, sbd_s, p_s, r_s, o_s, g_s, k_s, v_s, q_s, kt_s, a_s, *, layer):
    L = HG_CHUNK
    nblk = L // 8
    lb = _hg_lower_bound(lbp_ref[...], layer)
    ff = hg_ref[0, :, HG_W:2 * HG_W]
    sig = jax.nn.sigmoid(ff)
    k_s[...] = (1.0 - lb) * jax.nn.sigmoid(-ff)
    v_s[...] = hg_ref[0, :, 2 * HG_W:3 * HG_W]
    q_s[...] = hg_ref[0, :, 0:HG_W] * (HEAD_W ** -0.5)
    row = _iota2((L, L), 0)
    col = _iota2((L, L), 1)
    tri_b = _ones_where(col <= row)
    g_s[...] = _cumsum_rows(jnp.log(lb + (1.0 - lb) * sig), tri_b)

    r2 = _iota2((2 * LANES, 2 * LANES), 0)
    c2 = _iota2((2 * LANES, 2 * LANES), 1)
    bd4_b = _ones_where(_same_head(r2, c2))
    r1 = _iota2((LANES, LANES), 0)
    c1 = _iota2((LANES, LANES), 1)
    bd_mask = _same_head(r1, c1)
    bd_b = _ones_where(bd_mask)

    a_s[0:8, :] = jnp.zeros((8, HG_W), f32)
    for i in range(1, nblk):
        blk = slice(8 * i, 8 * i + 8)
        n = 8 * i
        r_i = g_s[n - 1:n, :]
        kt = k_s[0:n, :] * jnp.exp(r_i - g_s[0:n, :])
        first_head = jnp.bitwise_and(_iota2((n, HG_W), 1), LANES - 1) < HEAD_W
        kt_s[i - 1, 0:n, :] = jnp.where(first_head, kt, 0.0)
        kt_s[i - 1, n:L, :] = jnp.zeros((L - n, HG_W), f32)
        kt_s[i - 1, L:L + n, :] = jnp.where(first_head, 0.0, kt)
        kt_s[i - 1, L + n:2 * L, :] = jnp.zeros((L - n, HG_W), f32)
        qt = q_s[blk, :] * jnp.exp(g_s[blk, :] - r_i)
        qt = jnp.concatenate([qt, jnp.zeros_like(qt)], axis=0).astype(bf16)
        for p in range(HG_HEADS // 2):
            sl = slice(p * LANES, (p + 1) * LANES)
            a_s[blk, sl] = _dot_nt(qt[:, sl], kt_s[i - 1, :, sl].astype(bf16))[0:8, :]

    sub = _iota2((8, HG_W), 0)
    for j in range(nblk):
        blk = slice(8 * j, 8 * j + 8)
        g_blk, k_blk, q_blk = g_s[blk, :], k_s[blk, :], q_s[blk, :]
        for r in range(8):
            s = 8 * j + r
            e = jnp.exp(jnp.where(sub >= r, g_blk - g_blk[r:r + 1, :], NEG_INF))
            p_s[8 * s:8 * s + 8, :] = q_blk * k_blk[r:r + 1, :] * e
    r_s[...] = _dot(p_s[...].astype(bf16), bd4_b)

    gg = g_s[...]
    vv = v_s[...]
    qd = q_s[...] * jnp.exp(gg)
    lo_half = _iota2((L, LANES), 1) < HEAD_W
    for p in range(HG_HEADS // 2):
        sl = slice(p * LANES, (p + 1) * LANES)
        v_cat = jnp.concatenate([jnp.where(lo_half, vv[:, sl], 0.0), jnp.where(lo_half, 0.0, vv[:, sl])], axis=0)
        o_s[:, sl] = (_dot(qd[:, sl].astype(bf16), sbd_s[p].astype(bf16))
                      + _dot(a_s[:, sl].astype(bf16), v_cat.astype(bf16)))
    for j in range(nblk):
        blk = slice(8 * j, 8 * j + 8)
        acc = o_s[blk, :]
        v_blk = v_s[blk, :]
        for r in range(8):
            s = 8 * j + r
            acc = acc + r_s[8 * s:8 * s + 8, :] * v_blk[r:r + 1, :]
        o_s[blk, :] = acc

    g_last = gg[L - 1:L, :]
    kd = k_s[...] * jnp.exp(g_last - gg)
    zpad = jnp.zeros((LANES - L, LANES), f32)
    og = o_s[...]
    gate = hg_ref[0, :, 3 * HG_W:4 * HG_W]
    for p in range(HG_HEADS // 2):
        sl = slice(p * LANES, (p + 1) * LANES)
        kd_t = jnp.concatenate([kd[:, sl], zpad], axis=0).T.astype(bf16)
        v_pad = jnp.concatenate([vv[:, sl], zpad], axis=0).astype(bf16)
        g_t = jnp.concatenate([gg[:, sl], zpad], axis=0).T
        dec = jnp.exp(g_t[:, L - 1:L])
        sbd_s[p] = dec * sbd_s[p] + jnp.where(bd_mask, _dot(kd_t, v_pad), 0.0)
        oo = og[:, sl]
        ms = _group_mean(oo * oo, bd_b)
        out_ref[0, :, sl] = (oo * lax.rsqrt(ms + EPS) * nw_ref[:, sl] * _silu(gate[:, sl])).astype(bf16)


def _hgrn_prompt(hg, lbp, norm_w, layer):
    b, t, _ = hg.shape
    L = HG_CHUNK
    ns = SEQS_PER_STEP
    assert b % ns == 0
    row_buf = pltpu.VMEM((ns, L, HG_W), f32)
    pair_buf = pltpu.VMEM((ns, 8 * L, HG_W), f32)
    return pl.pallas_call(
        functools.partial(_hgrn_prompt_body, layer=layer),
        grid=(b // ns, t // L),
        in_specs=[pl.BlockSpec((ns, L, HG_COLS), lambda i, c: (i, c, 0)),
                  _resident((DEPTH, HG_W)), _layer_resident((1, HG_W), layer)],
        out_specs=[pl.BlockSpec((ns, L, HG_W), lambda i, c: (i, c, 0)),
                   pl.BlockSpec((ns, HG_HEADS, HEAD_W, HEAD_W), lambda i, c: (i, 0, 0, 0))],
        out_shape=[jax.ShapeDtypeStruct((b, t, HG_W), bf16),
                   jax.ShapeDtypeStruct((b, HG_HEADS, HEAD_W, HEAD_W), f32)],
        scratch_shapes=[pltpu.VMEM((ns, HG_HEADS // 2, LANES, LANES), f32), pair_buf, pair_buf,
                        row_buf, row_buf, row_buf, row_buf, row_buf,
                        pltpu.VMEM((ns, L // 8 - 1, 2 * L, HG_W), f32), row_buf],
        compiler_params=_cparams("arbitrary", "arbitrary"),
        name="hgrn_prompt",
    )(hg, lbp, norm_w)


NB = 128
TS = 8


def _sample_call(body, name, grid, in_specs, operands, out_specs, out_shapes, scratch, n_state):
    first = len(operands) - n_state
    return pl.pallas_call(
        body, grid=grid, in_specs=in_specs, out_specs=out_specs, out_shape=out_shapes,
        scratch_shapes=scratch, input_output_aliases={first + i: 1 + i for i in range(n_state)},
        compiler_params=_cparams(*(["arbitrary"] * len(grid))), name=name,
    )(*operands)


def _mlstm_sample_body(q_ref, k_ref, v_ref, o_ref, g_ref, gb_ref, nw_ref, c0_ref, n0_ref, m0_ref,
                       out_ref, c1_ref, n1_ref, m1_ref, qt_s, kt_s, vt_s, gt_s, ht_s):
    p = pl.program_id(0)
    n_cur = [n0_ref[hl] for hl in range(2)]
    m_cur = [m0_ref[pl.ds(2 * p + hl, 1), :] for hl in range(2)]

    for t in range(TS):
        c_src = c0_ref if t == 0 else c1_ref
        rows = slice(t * NB, (t + 1) * NB)
        qt_s[...] = q_ref[rows, :].T
        kt_s[...] = k_ref[rows, :].T * (HEAD_W ** -0.5)
        vt_s[...] = v_ref[rows, :].T
        gt_s[...] = (g_ref[rows, :] + gb_ref[...]).T
        for hl in range(2):
            h = 2 * p + hl
            hs = slice(hl * HEAD_W, (hl + 1) * HEAD_W)
            i_t = gt_s[pl.ds(h, 1), :]
            lf = jax.nn.log_sigmoid(gt_s[pl.ds(ML_HEADS + h, 1), :])
            m_new = jnp.maximum(lf + m_cur[hl], i_t)
            a = jnp.exp(lf + m_cur[hl] - m_new)
            w = jnp.exp(i_t - m_new)
            n_new = a * n_cur[hl] + w * kt_s[hs, :]
            v_h = vt_s[hs, :]

            def body(d, num, hl=hl, a=a, w=w, v_h=v_h, c_src=c_src):
                kd = kt_s[pl.ds(hl * HEAD_W + d, 1), :] * w
                qd = qt_s[pl.ds(hl * HEAD_W + d, 1), :]
                cd = a * c_src[hl, d] + kd * v_h
                c1_ref[hl, d] = cd
                return num + qd * cd

            num = lax.fori_loop(0, HEAD_W, body, jnp.zeros((HEAD_W, NB), f32), unroll=4)
            den = jnp.sum(qt_s[hs, :] * n_new, axis=0, keepdims=True)
            hh = num / jnp.maximum(jnp.abs(den), jnp.exp(-m_new))
            ms = jnp.mean(hh * hh, axis=0, keepdims=True)
            ht_s[hs, :] = hh * lax.rsqrt(ms + EPS)
            n_cur[hl] = n_new
            m_cur[hl] = m_new
        out_ref[rows, :] = (ht_s[...].T * nw_ref[...] * jax.nn.sigmoid(o_ref[rows, :])).astype(bf16)

    for hl in range(2):
        n1_ref[hl] = n_cur[hl]
        m1_ref[pl.ds(2 * p + hl, 1), :] = m_cur[hl]


def _mlstm_sample(ml, gates, gate_bias, norm_w, c0t, n0t, m0t, layer):
    n = ml.shape[0]
    npair = ML_HEADS // 2
    blk = lambda off: pl.BlockSpec((n, LANES), lambda p, off=off: (0, off + p))
    c_spec = pl.BlockSpec((None, 2, HEAD_W, HEAD_W, NB), lambda p: (layer, p, 0, 0, 0))
    n_spec = pl.BlockSpec((None, 2, HEAD_W, NB), lambda p: (layer, p, 0, 0))
    m_spec = pl.BlockSpec((None, ML_HEADS, NB), lambda p: (layer, 0, 0))
    return _sample_call(
        _mlstm_sample_body, "mlstm_sample", (npair,),
        [blk(0), blk(npair), blk(2 * npair), blk(3 * npair), _resident((n, GATE_W)),
         _layer_resident((1, GATE_W), layer), pl.BlockSpec((None, 1, LANES), lambda p: (layer, 0, p)),
         c_spec, n_spec, m_spec],
        [ml, ml, ml, ml, gates, gate_bias, norm_w, c0t, n0t, m0t],
        [pl.BlockSpec((n, LANES), lambda p: (0, p)), c_spec, n_spec, m_spec],
        [jax.ShapeDtypeStruct((n, ML_W), bf16), jax.ShapeDtypeStruct(c0t.shape, f32),
         jax.ShapeDtypeStruct(n0t.shape, f32), jax.ShapeDtypeStruct(m0t.shape, f32)],
        [pltpu.VMEM((LANES, NB), f32)] * 5, 3)


def _ssd_conv_sample_body(ssd_ref, halo_ref, cw_ref, cb_ref, out_ref, xp_s):
    nh = (SSD_CONV - 1) * NB
    n = TS * NB
    xp_s[0:nh, :] = halo_ref[...]
    xp_s[nh:nh + n, :] = ssd_ref[:, SSD_W:SSD_COLS]
    conv = cb_ref[...]
    for j in range(SSD_CONV):
        conv = conv + cw_ref[j:j + 1, :] * xp_s[j * NB:j * NB + n, :]
    out_ref[...] = _silu(conv)


def _ssd_conv_sample(ssd, halo, conv_w, conv_b, layer):
    n = ssd.shape[0]
    nh = (SSD_CONV - 1) * NB
    return pl.pallas_call(
        _ssd_conv_sample_body,
        grid=(1,),
        in_specs=[_resident((n, SSD_COLS)), _layer_resident((nh, SSD_CONV_DIM), layer),
                  _layer_resident((SSD_CONV, SSD_CONV_DIM), layer), _layer_resident((1, SSD_CONV_DIM), layer)],
        out_specs=pl.BlockSpec((n, SSD_CONV_DIM), lambda i: (0, 0)),
        out_shape=jax.ShapeDtypeStruct((n, SSD_CONV_DIM), f32),
        scratch_shapes=[pltpu.VMEM((nh + n, SSD_CONV_DIM), f32)],
        compiler_params=_cparams("arbitrary"),
        name="ssd_conv_sample",
    )(ssd, halo, conv_w, conv_b)


def _ssd_sample_body(z_ref, x_ref, b0_ref, b1_ref, c0_ref, c1_ref, g_ref, gb_ref, hp_ref, nw_ref, h0_ref,
                     out_ref, h1_ref, xt_s, bt_s, ct_s, gt_s, yt_s):
    p = pl.program_id(0)
    b_refs = (b0_ref, b1_ref)
    c_refs = (c0_ref, c1_ref)

    for t in range(TS):
        h_src = h0_ref if t == 0 else h1_ref
        rows = slice(t * NB, (t + 1) * NB)
        xt_s[...] = x_ref[rows, :].T
        gt_s[...] = (g_ref[rows, :] + gb_ref[...]).T
        zt = z_ref[rows, :].T
        for hl in range(2):
            bt_s[hl] = b_refs[hl][rows, :].T
            ct_s[hl] = c_refs[hl][rows, :].T
        for hl in range(2):
            h = 2 * p + hl
            hs = slice(hl * HEAD_W, (hl + 1) * HEAD_W)
            dt = jax.nn.softplus(gt_s[pl.ds(DT_LANE + h, 1), :])
            da = jnp.exp(dt * (-jnp.exp(hp_ref[0, hl:hl + 1, :])))

            def body(pp, carry, hl=hl, dt=dt, da=da, h_src=h_src):
                xrow = xt_s[pl.ds(hl * HEAD_W + pp, 1), :] * dt
                hn = da * h_src[hl, pp] + xrow * bt_s[hl]
                h1_ref[hl, pp] = hn
                yt_s[pl.ds(hl * HEAD_W + pp, 1), :] = jnp.sum(ct_s[hl] * hn, axis=0, keepdims=True)
                return carry

            lax.fori_loop(0, HEAD_W, body, 0, unroll=2)
            yy = (yt_s[hs, :] + hp_ref[1, hl:hl + 1, :] * xt_s[hs, :]) * _silu(zt[hs, :])
            ms = jnp.mean(yy * yy, axis=0, keepdims=True)
            yt_s[hs, :] = yy * lax.rsqrt(ms + EPS)
        out_ref[rows, :] = (yt_s[...].T * nw_ref[...]).astype(bf16)


def _ssd_sample(ssd, xc, gates, gate_bias, head_params, norm_w, h0t, layer):
    n = ssd.shape[0]
    npair = SSD_HEADS // 2
    hpg = SSD_HEADS // 2
    nxb = SSD_W // LANES
    blk = lambda f: pl.BlockSpec((n, LANES), f)
    h_spec = pl.BlockSpec((None, 2, HEAD_W, SSD_DSTATE, NB), lambda p: (layer, p, 0, 0, 0))
    return _sample_call(
        _ssd_sample_body, "ssd_sample", (npair,),
        [blk(lambda p: (0, p)), blk(lambda p: (0, p)),
         blk(lambda p: (0, nxb + (2 * p) // hpg)), blk(lambda p: (0, nxb + (2 * p + 1) // hpg)),
         blk(lambda p: (0, nxb + 2 + (2 * p) // hpg)), blk(lambda p: (0, nxb + 2 + (2 * p + 1) // hpg)),
         _resident((n, GATE_W)), _layer_resident((1, GATE_W), layer),
         pl.BlockSpec((None, 2, None, 2, LANES), lambda p: (layer, 0, p, 0, 0)),
         pl.BlockSpec((None, 1, LANES), lambda p: (layer, 0, p)), h_spec],
        [ssd, xc, xc, xc, xc, xc, gates, gate_bias, head_params, norm_w, h0t],
        [pl.BlockSpec((n, LANES), lambda p: (0, p)), h_spec],
        [jax.ShapeDtypeStruct((n, SSD_W), bf16), jax.ShapeDtypeStruct(h0t.shape, f32)],
        [pltpu.VMEM((LANES, NB), f32), pltpu.VMEM((2, SSD_DSTATE, NB), f32), pltpu.VMEM((2, SSD_DSTATE, NB), f32),
         pltpu.VMEM((LANES, NB), f32), pltpu.VMEM((LANES, NB), f32)], 1)


def _hgrn_sample_body(q_ref, f_ref, i_ref, gate_ref, lbp_ref, nw_ref, s0_ref, out_ref, s1_ref,
                      qt_s, kt_s, ft_s, vt_s, ot_s, *, layer):
    lb = _hg_lower_bound(lbp_ref[...], layer)

    for t in range(TS):
        s_src = s0_ref if t == 0 else s1_ref
        rows = slice(t * NB, (t + 1) * NB)
        ff = f_ref[rows, :]
        qt_s[...] = (q_ref[rows, :] * (HEAD_W ** -0.5)).T
        kt_s[...] = ((1.0 - lb) * jax.nn.sigmoid(-ff)).T
        ft_s[...] = (lb + (1.0 - lb) * jax.nn.sigmoid(ff)).T
        vt_s[...] = i_ref[rows, :].T
        for hl in range(2):
            hs = slice(hl * HEAD_W, (hl + 1) * HEAD_W)
            v_h = vt_s[hs, :]

            def body(k, o, hl=hl, v_h=v_h, s_src=s_src):
                r = pl.ds(hl * HEAD_W + k, 1)
                sk = ft_s[r, :] * s_src[hl, k] + kt_s[r, :] * v_h
                s1_ref[hl, k] = sk
                return o + qt_s[r, :] * sk

            o = lax.fori_loop(0, HEAD_W, body, jnp.zeros((HEAD_W, NB), f32), unroll=4)
            ms = jnp.mean(o * o, axis=0, keepdims=True)
            ot_s[hs, :] = o * lax.rsqrt(ms + EPS)
        out_ref[rows, :] = (ot_s[...].T * nw_ref[...] * _silu(gate_ref[rows, :])).astype(bf16)


def _hgrn_sample(hg, lbp, norm_w, s0t, layer):
    n = hg.shape[0]
    npair = HG_HEADS // 2
    blk = lambda off: pl.BlockSpec((n, LANES), lambda p, off=off: (0, off + p))
    s_spec = pl.BlockSpec((None, 2, HEAD_W, HEAD_W, NB), lambda p: (layer, p, 0, 0, 0))
    return _sample_call(
        functools.partial(_hgrn_sample_body, layer=layer), "hgrn_sample", (npair,),
        [blk(0), blk(npair), blk(2 * npair), blk(3 * npair),
         pl.BlockSpec((DEPTH, LANES), lambda p: (0, p)),
         pl.BlockSpec((None, 1, LANES), lambda p: (layer, 0, p)), s_spec],
        [hg, hg, hg, hg, lbp, norm_w, s0t],
        [pl.BlockSpec((n, LANES), lambda p: (0, p)), s_spec],
        [jax.ShapeDtypeStruct((n, HG_W), bf16), jax.ShapeDtypeStruct(s0t.shape, f32)],
        [pltpu.VMEM((LANES, NB), f32)] * 5, 1)


def _ffn_body(x_ref, mml_ref, mssd_ref, mhg_ref, wout_ref, n2_ref, wug_ref, wuv_ref, cwg_ref, cwv_ref,
              cbg_ref, cbv_ref, wdn_ref, f0g_ref, f0v_ref, fn_ref, y_ref, f1g_ref, f1v_ref,
              x1_s, h2_s, acc_s, act_s, ubuf_s, *, tm, shift, jb, nff, final):
    t = pl.program_id(1)
    f = pl.program_id(2)
    halo = max(HALO, (FFN_CONV - 1) * shift)
    keep = (FFN_CONV - 1) * shift

    def when(cond, fn):
        if nff == 1:
            fn()
        else:
            pl.when(cond)(fn)

    @pl.when(t == 0)
    def _():
        f1g_ref[0] = f0g_ref[0]
        f1v_ref[0] = f0v_ref[0]

    def first():
        x = x_ref[...]
        x1 = (x + _dot(mml_ref[...], wout_ref[0:ML_W, :]) + _dot(mssd_ref[...], wout_ref[ML_W:ML_W + SSD_W, :])
              + _dot(mhg_ref[...], wout_ref[ML_W + SSD_W:D_MODEL, :]))
        x1_s[...] = x1
        h2_s[...] = (x1 * lax.rsqrt(jnp.mean(x1 * x1, axis=-1, keepdims=True) + EPS) * n2_ref[...]).astype(bf16)

    when(f == 0, first)
    h2 = h2_s[...]

    def conv(wu_ref, cw_ref, cb_ref, f1_ref, cols, ubuf):
        u = _dot(h2, wu_ref[:, cols])
        ubuf[halo - keep:halo, :] = f1_ref[0, :, cols]
        ubuf[halo:halo + tm, :] = u
        yv = (cb_ref[:, cols] + cw_ref[2:3, cols] * u
              + cw_ref[1:2, cols] * ubuf[halo - shift:halo - shift + tm, :]
              + cw_ref[0:1, cols] * ubuf[halo - 2 * shift:halo - 2 * shift + tm, :])
        f1_ref[0, :, cols] = ubuf[halo + tm - keep:halo + tm, :]
        return yv

    nbuf = ubuf_s.shape[0]
    for j in range(jb):
        cols = slice(j * FF_BLOCK, (j + 1) * FF_BLOCK)
        gq = conv(wug_ref, cwg_ref, cbg_ref, f1g_ref, cols, ubuf_s.at[(2 * j) % nbuf])
        vq = conv(wuv_ref, cwv_ref, cbv_ref, f1v_ref, cols, ubuf_s.at[(2 * j + 1) % nbuf])
        act_s[:, cols] = (_silu(gq) * vq).astype(bf16)
    contrib = _dot(act_s[...], wdn_ref[...])

    def finish(down):
        x2 = x1_s[...] + down
        if final:
            x2 = x2 * lax.rsqrt(jnp.mean(x2 * x2, axis=-1, keepdims=True) + EPS) * fn_ref[...]
        y_ref[...] = x2

    if nff == 1:
        finish(contrib)
    else:
        @pl.when(f == 0)
        def _():
            acc_s[...] = contrib

        @pl.when(f != 0)
        def _():
            acc_s[...] = acc_s[...] + contrib

        pl.when(f == nff - 1)(lambda: finish(acc_s[...]))


def _ffn(x, mml, mssd, mhg, w_out, norm2, w_up, conv_w, conv_b, w_down, f0, final_w, *, layer, groups, tm, shift, jb,
         final, f0_base=0):
    n = x.shape[0]
    nt = n // (groups * tm)
    fw = jb * FF_BLOCK
    nff = D_FF // fw
    assert nt == 1 or nff == 1
    halo = max(HALO, (FFN_CONV - 1) * shift)
    keep = (FFN_CONV - 1) * shift
    rows = lambda w: pl.BlockSpec((tm, w), lambda g, t, f: (g * nt + t, 0))
    single = nff == 1
    wspec = lambda shape, imap: pl.BlockSpec(shape, imap, pipeline_mode=pl.Buffered(1)) if single else pl.BlockSpec(shape, imap)
    f1_spec = pl.BlockSpec((1, keep, fw), lambda g, t, f: (g, 0, f))
    in_specs = [rows(D_MODEL), rows(ML_W), rows(SSD_W), rows(HG_W),
                _layer_resident((D_MODEL, D_MODEL), layer), _layer_resident((1, D_MODEL), layer),
                wspec((None, D_MODEL, fw), lambda g, t, f: (layer, 0, f)),
                wspec((None, D_MODEL, fw), lambda g, t, f: (layer, 0, nff + f)),
                wspec((None, FFN_CONV, fw), lambda g, t, f: (layer, 0, f)),
                wspec((None, FFN_CONV, fw), lambda g, t, f: (layer, 0, nff + f)),
                wspec((None, 1, fw), lambda g, t, f: (layer, 0, f)),
                wspec((None, 1, fw), lambda g, t, f: (layer, 0, nff + f)),
                wspec((None, fw, D_MODEL), lambda g, t, f: (layer, f, 0)),
                pl.BlockSpec((1, keep, fw), lambda g, t, f: (f0_base + g, 0, f)),
                pl.BlockSpec((1, keep, fw), lambda g, t, f: (f0_base + g, 0, nff + f)),
                _resident((1, D_MODEL))]
    operands = [x, mml, mssd, mhg, w_out, norm2, w_up, w_up, conv_w, conv_w, conv_b, conv_b, w_down, f0, f0, final_w]
    acc_rows = tm if nff > 1 else 8
    return pl.pallas_call(
        functools.partial(_ffn_body, tm=tm, shift=shift, jb=jb, nff=nff, final=final),
        grid=(groups, nt, nff),
        in_specs=in_specs,
        out_specs=[rows(D_MODEL), f1_spec, f1_spec],
        out_shape=[jax.ShapeDtypeStruct((n, D_MODEL), f32),
                   jax.ShapeDtypeStruct((groups, keep, D_FF), f32),
                   jax.ShapeDtypeStruct((groups, keep, D_FF), f32)],
        scratch_shapes=[pltpu.VMEM((tm, D_MODEL), f32), pltpu.VMEM((tm, D_MODEL), bf16),
                        pltpu.VMEM((acc_rows, D_MODEL), f32), pltpu.VMEM((tm, fw), bf16),
                        pltpu.VMEM((min(4, 2 * jb), halo + tm, FF_BLOCK), f32)],
        compiler_params=_cparams("arbitrary", "arbitrary", "arbitrary"),
        name="ffn",
    )(*operands)


def _gate_rows(*parts):
    r = jnp.concatenate([p.astype(f32) for p in parts], axis=1)
    return jnp.pad(r, ((0, 0), (0, GATE_W - r.shape[1])))[:, None, :]


def kernel(x_prompt, x_sample, state_mlstm_C, state_mlstm_n, state_mlstm_m, state_ssd, state_ssd_conv,
           state_hgrn, state_ffn_conv, norm1_w, w_in, ml_ig_b, ml_fg_b, ml_norm_w, ssd_conv_w, ssd_conv_b,
           ssd_dt_bias, ssd_A_log, ssd_D, ssd_norm_w, hg_lower_bounds, hg_norm_w, w_out, norm2_w, w_up,
           ffn_conv_w, ffn_conv_b, w_down, final_norm_w):
    bp, tp, _ = x_prompt.shape
    nb, ts, _ = x_sample.shape
    assert nb == NB and ts == TS and tp % ROW_TILE == 0
    depth = w_in.shape[0]
    assert depth == DEPTH
    xp = x_prompt.reshape(bp * tp, D_MODEL)
    xs = jnp.transpose(x_sample, (1, 0, 2)).reshape(ts * nb, D_MODEL)
    lbp = hg_lower_bounds.astype(f32)
    fin_w = final_norm_w[None, :]
    zero_f = jnp.zeros((bp, FFN_CONV - 1, 2 * D_FF), f32)

    w_cat = _prep_w_in(w_in)
    wo_b, wu_b, wd_b = w_out.astype(bf16), w_up.astype(bf16), w_down.astype(bf16)
    n1, n2 = norm1_w[:, None, :], norm2_w[:, None, :]
    gate_bias = _gate_rows(ml_ig_b, ml_fg_b, ssd_dt_bias)
    alog_row = _gate_rows(jnp.zeros((depth, DT_LANE), f32), ssd_A_log)
    d_row = jnp.repeat(ssd_D.astype(f32), HEAD_W, axis=1)[:, None, :]
    mlw, ssw, hgw = ml_norm_w[:, None, :], ssd_norm_w[:, None, :], hg_norm_w[:, None, :]
    cw, cb = ssd_conv_w, ssd_conv_b[:, None, :]
    fcw, fcb = ffn_conv_w, ffn_conv_b[:, None, :]
    head_params = jnp.broadcast_to(
        jnp.stack([ssd_A_log, ssd_D], axis=1).astype(f32).reshape(depth, 2, SSD_HEADS // 2, 2, 1),
        (depth, 2, SSD_HEADS // 2, 2, LANES))

    c_t = jnp.transpose(state_mlstm_C.astype(f32), (0, 2, 3, 4, 1))
    n_t = jnp.transpose(state_mlstm_n.astype(f32), (0, 2, 3, 1))
    m_t = jnp.transpose(state_mlstm_m.astype(f32), (0, 2, 1))
    h_t = jnp.transpose(state_ssd.astype(f32), (0, 2, 3, 4, 1))
    s_t = jnp.transpose(state_hgrn.astype(f32), (0, 2, 3, 4, 1))
    conv_halo = jnp.transpose(state_ssd_conv.astype(f32), (0, 2, 1, 3)).reshape(depth, (SSD_CONV - 1) * nb, SSD_CONV_DIM)
    f_t = jnp.transpose(state_ffn_conv.astype(f32), (0, 2, 1, 3)).reshape(depth, (FFN_CONV - 1) * nb, 2 * D_FF)

    p_states, s_sconv, s_ffn = [], [], []
    for l in range(depth):
        last = l == depth - 1

        ml, ssd, hg, gates = _inproj(xp, n1, w_cat, l)
        gates3 = gates.reshape(bp, tp, GATE_W)
        ssd3 = ssd.reshape(bp, tp, SSD_COLS)
        hm, p_c, p_n, p_m = _mlstm_prompt(ml.reshape(bp, tp, ML_COLS), gates3, gate_bias, mlw, l)
        hs, p_h = _ssd_prompt(ssd3, gates3, gate_bias, alog_row, cw, cb, d_row, ssw, l)
        ho, p_s = _hgrn_prompt(hg.reshape(bp, tp, HG_COLS), lbp, hgw, l)
        xp, p_fg, p_fv = _ffn(xp, hm.reshape(bp * tp, ML_W), hs.reshape(bp * tp, SSD_W), ho.reshape(bp * tp, HG_W),
                              wo_b, n2, wu_b, fcw, fcb, wd_b, zero_f, fin_w, layer=l,
                              groups=bp, tm=ROW_TILE, shift=1, jb=D_FF // FF_BLOCK, final=last)
        p_states.append((p_c, p_n, p_m[:, 0:ML_HEADS, 0],
                         p_h.reshape(bp, SSD_HEADS, HEAD_W, SSD_DSTATE),
                         ssd3[:, tp - (SSD_CONV - 1):, SSD_W:], p_s,
                         jnp.concatenate([p_fg, p_fv], axis=-1)))

        ml, ssd, hg, gates = _inproj(xs, n1, w_cat, l)
        hm, c_t, n_t, m_t = _mlstm_sample(ml, gates, gate_bias, mlw, c_t, n_t, m_t, l)
        xc = _ssd_conv_sample(ssd, conv_halo, cw, cb, l)
        hs, h_t = _ssd_sample(ssd, xc, gates, gate_bias, head_params, ssw, h_t, l)
        ho, s_t = _hgrn_sample(hg, lbp, hgw, s_t, l)
        xs, s_fg, s_fv = _ffn(xs, hm, hs, ho, wo_b, n2, wu_b, fcw, fcb, wd_b, f_t, fin_w, layer=l,
                              groups=1, tm=ts * nb, shift=nb, jb=1, final=last, f0_base=l)
        s_ffn.append(jnp.concatenate([s_fg, s_fv], axis=-1).reshape(FFN_CONV - 1, nb, 2 * D_FF))
        raw_xbc = ssd[:, SSD_W:].reshape(ts, nb, SSD_CONV_DIM)
        s_sconv.append(jnp.transpose(raw_xbc[ts - (SSD_CONV - 1):], (1, 0, 2)))

    y_prompt = xp.reshape(bp, tp, D_MODEL)
    y_sample = jnp.transpose(xs.reshape(ts, nb, D_MODEL), (1, 0, 2))
    p_out = [jnp.stack([st[i] for st in p_states]) for i in range(7)]
    s_out = [jnp.transpose(c_t, (0, 4, 1, 2, 3)), jnp.transpose(n_t, (0, 3, 1, 2)), jnp.transpose(m_t, (0, 2, 1)),
             jnp.transpose(h_t, (0, 4, 1, 2, 3)), jnp.stack(s_sconv),
             jnp.transpose(s_t, (0, 4, 1, 2, 3)), jnp.transpose(jnp.stack(s_ffn), (0, 2, 1, 3))]
    return (y_prompt, y_sample, *p_out, *s_out)
```

```python
import functools

import jax
import jax.numpy as jnp
from jax import lax
from jax.experimental import pallas as pl
from jax.experimental.pallas import tpu as pltpu

f32 = jnp.float32
bf16 = jnp.bfloat16

D_MODEL = 1024
DEPTH = 2
ML_HEADS = 6
HEAD_W = 64
ML_W = 384
SSD_HEADS = 6
SSD_W = 384
SSD_DSTATE = 128
SSD_CONV = 4
SSD_CONV_DIM = 896
HG_HEADS = 4
HG_W = 256
D_FF = 2816
FFN_CONV = 3
EPS = 1e-6
PROMPT_CHUNK = 128
HG_CHUNK = 64
LANES = 128
FF_BLOCK = 256
ROW_TILE = 512
SEQS_PER_STEP = 4
GATE_W = 128
W_IN_COLS = 4 * ML_W + (SSD_W + SSD_CONV_DIM) + 4 * HG_W + GATE_W
VMEM_LIMIT_BYTES = 56 * 1024 * 1024

NEG_INF = float("-inf")


def _cparams(*sem):
    return pltpu.CompilerParams(dimension_semantics=sem, vmem_limit_bytes=VMEM_LIMIT_BYTES)


def _resident(shape):
    nd = len(shape)
    return pl.BlockSpec(shape, lambda *_: (0,) * nd, pipeline_mode=pl.Buffered(1))


def _layer_resident(shape, layer):
    nd = len(shape)
    return pl.BlockSpec((None,) + tuple(shape), lambda *_: (layer,) + (0,) * nd, pipeline_mode=pl.Buffered(1))


def _dot(a, b):
    return jnp.dot(a, b, preferred_element_type=f32)


def _dot_nt(a, b):
    return lax.dot_general(a, b, (((1,), (1,)), ((), ())), preferred_element_type=f32)


def _split_bf16(x, parts):
    out = []
    r = x
    for _ in range(parts):
        h = r.astype(bf16)
        out.append(h)
        r = r - h.astype(f32)
    return out


def _cumsum_rows(x, tri_b):
    acc = None
    for part in _split_bf16(x, 3):
        d = _dot(tri_b, part)
        acc = d if acc is None else acc + d
    return acc


def _group_mean(x, bd_b):
    hi, lo = _split_bf16(x, 2)
    return (_dot(hi, bd_b) + _dot(lo, bd_b)) * (1.0 / HEAD_W)


def _iota2(shape, axis):
    return lax.broadcasted_iota(jnp.int32, shape, axis)


def _ones_where(mask):
    return jnp.where(mask, 1.0, 0.0).astype(bf16)


def _same_head(a, b):
    return jnp.right_shift(a, 6) == jnp.right_shift(b, 6)


def _silu(x):
    return x * jax.nn.sigmoid(x)


ML_COLS = 4 * ML_W
SSD_COLS = SSD_W + SSD_CONV_DIM
HG_COLS = 4 * HG_W


def _inproj_body(x_ref, nw_ref, w_ref, ml_ref, ssd_ref, hg_ref, g_ref):
    x = x_ref[...]
    h = (x * lax.rsqrt(jnp.mean(x * x, axis=-1, keepdims=True) + EPS) * nw_ref[...]).astype(bf16)
    o0, o1, o2 = ML_COLS, ML_COLS + SSD_COLS, ML_COLS + SSD_COLS + HG_COLS
    ml_ref[...] = _dot(h, w_ref[:, 0:o0])
    ssd_ref[...] = _dot(h, w_ref[:, o0:o1])
    hg_ref[...] = _dot(h, w_ref[:, o1:o2])
    g_ref[...] = _dot(h, w_ref[:, o2:W_IN_COLS])


def _inproj(x, norm_w, w_cat, layer):
    n = x.shape[0]
    tm = ROW_TILE
    row = lambda w: pl.BlockSpec((tm, w), lambda i: (i, 0))
    return pl.pallas_call(
        _inproj_body,
        grid=(n // tm,),
        in_specs=[row(D_MODEL), _layer_resident((1, D_MODEL), layer), _layer_resident((D_MODEL, W_IN_COLS), layer)],
        out_specs=[row(ML_COLS), row(SSD_COLS), row(HG_COLS), row(GATE_W)],
        out_shape=[jax.ShapeDtypeStruct((n, w), f32) for w in (ML_COLS, SSD_COLS, HG_COLS, GATE_W)],
        compiler_params=_cparams("arbitrary"),
        name="inproj",
    )(x, norm_w, w_cat)


W_IN_GATE_IF = ML_COLS
W_IN_SSD = W_IN_GATE_IF + 2 * ML_HEADS
W_IN_DT = W_IN_SSD + SSD_COLS
W_IN_HG = W_IN_DT + SSD_HEADS
W_IN_DIM = W_IN_HG + HG_COLS
W_PREP_ROWS = 256


def _prep_w_in_body(w_ref, out_ref):
    lane = _iota2((W_PREP_ROWS, LANES), 1)
    tile = lambda k: w_ref[:, k * LANES:(k + 1) * LANES]
    out_ref[:, 0:ML_COLS] = w_ref[:, 0:ML_COLS].astype(bf16)

    def shifted(dst0, src0, ntiles):
        t0, sh = src0 // LANES, src0 % LANES
        rolled = [pltpu.roll(tile(t0 + k), LANES - sh, axis=1) for k in range(ntiles + 1)]
        for k in range(ntiles):
            out_ref[:, dst0 + k * LANES:dst0 + (k + 1) * LANES] = jnp.where(
                lane < LANES - sh, rolled[k], rolled[k + 1]).astype(bf16)

    shifted(ML_COLS, W_IN_SSD, SSD_COLS // LANES)
    shifted(ML_COLS + SSD_COLS, W_IN_HG, HG_COLS // LANES)
    assert W_IN_GATE_IF % LANES == 0 and W_IN_DT % LANES == DT_LANE
    gates = jnp.where(lane < DT_LANE, tile(W_IN_GATE_IF // LANES),
                      jnp.where(lane < DT_LANE + SSD_HEADS, tile(W_IN_DT // LANES), 0.0))
    out_ref[:, ML_COLS + SSD_COLS + HG_COLS:W_IN_COLS] = gates.astype(bf16)


def _prep_w_in(w_in):
    depth, d, n = w_in.shape
    assert n == W_IN_DIM and d % W_PREP_ROWS == 0
    return pl.pallas_call(
        _prep_w_in_body,
        grid=(depth, d // W_PREP_ROWS),
        in_specs=[pl.BlockSpec((None, W_PREP_ROWS, W_IN_COLS), lambda l, i: (l, i, 0))],
        out_specs=pl.BlockSpec((None, W_PREP_ROWS, W_IN_COLS), lambda l, i: (l, i, 0)),
        out_shape=jax.ShapeDtypeStruct((depth, d, W_IN_COLS), bf16),
        compiler_params=_cparams("arbitrary", "arbitrary"),
        name="prep_w_in",
    )(w_in)


def _one_seq(ref, bi):
    return ref.at[pl.ds(bi, 1)]


def _interleave(chunks):
    live = list(chunks)
    while live:
        nxt = []
        for g in live:
            try:
                next(g)
                nxt.append(g)
            except StopIteration:
                pass
        live = nxt


def _mlstm_prompt_body(ml_ref, g_ref, gb_ref, nw_ref, out_ref, c1_ref, n1_ref, m1_ref, ct_s, nr_s, m_s):
    c = pl.program_id(1)
    seqs = range(ml_ref.shape[0])

    @pl.when(c == 0)
    def _():
        ct_s[...] = jnp.zeros(ct_s.shape, f32)
        nr_s[...] = jnp.zeros(nr_s.shape, f32)
        m_s[...] = jnp.zeros(m_s.shape, f32)

    _interleave([_mlstm_prompt_chunk(_one_seq(ml_ref, bi), _one_seq(g_ref, bi), gb_ref, nw_ref, _one_seq(out_ref, bi),
                                     ct_s.at[bi], nr_s.at[bi], m_s.at[bi]) for bi in seqs])

    @pl.when(c == pl.num_programs(1) - 1)
    def _():
        for bi in seqs:
            for p in range(ML_HEADS // 2):
                c_pair = ct_s[bi, p].T
                c1_ref[bi, 2 * p] = c_pair[0:HEAD_W, 0:HEAD_W]
                c1_ref[bi, 2 * p + 1] = c_pair[HEAD_W:LANES, HEAD_W:LANES]
                n1_ref[bi, 2 * p:2 * p + 1, :] = nr_s[bi, p, 0:1, 0:HEAD_W]
                n1_ref[bi, 2 * p + 1:2 * p + 2, :] = nr_s[bi, p, 1:2, HEAD_W:LANES]
            m1_ref[bi] = m_s[bi]


def _mlstm_prompt_chunk(ml_ref, g_ref, gb_ref, nw_ref, out_ref, ct_s, nr_s, m_s):
    L = PROMPT_CHUNK
    row = _iota2((L, L), 0)
    col = _iota2((L, L), 1)
    causal = row <= col
    tri_b = _ones_where(causal)
    top = row < HEAD_W
    lo_half = col < HEAD_W
    bd_mask = _same_head(row, col)
    sub8 = _iota2((8, L), 0)
    lo8 = _iota2((8, L), 1) < HEAD_W
    n_mask = jnp.where(sub8 == 0, jnp.where(lo8, 1.0, 0.0), jnp.where(sub8 == 1, jnp.where(lo8, 0.0, 1.0), 0.0))

    g_t = (g_ref[0] + gb_ref[...]).T
    b_t = None
    parts = _split_bf16(jax.nn.log_sigmoid(g_t), 3)
    yield
    for part in parts:
        d = _dot(part, tri_b)
        b_t = d if b_t is None else b_t + d
        yield
    z_nat = (pltpu.roll(b_t, L - ML_HEADS, axis=0) - g_t).T
    m_old = m_s[...]
    m_next = m_old
    yield

    for p in range(ML_HEADS // 2):
        sl = slice(p * LANES, (p + 1) * LANES)
        q2 = ml_ref[0, :, sl]
        k2 = ml_ref[0, :, ML_W + p * LANES:ML_W + (p + 1) * LANES] * (HEAD_W ** -0.5)
        v2 = ml_ref[0, :, 2 * ML_W + p * LANES:2 * ML_W + (p + 1) * LANES]
        o2 = ml_ref[0, :, 3 * ML_W + p * LANES:3 * ML_W + (p + 1) * LANES]
        q_t = q2.T.astype(bf16)
        v_t = v2.T
        kb = k2.astype(bf16)
        k_sel = (jnp.where(lo_half, k2, 0.0).astype(bf16), jnp.where(lo_half, 0.0, k2).astype(bf16))
        sw, den, a_int, e_m, a_st, w_s = [], [], [], [], [], []
        yield
        for j in range(2):
            h = 2 * p + j
            b_row = b_t[ML_HEADS + h:ML_HEADS + h + 1, :]
            i_row = g_t[h:h + 1, :]
            m_prev = m_old[h:h + 1, :]
            dm = jnp.where(causal, b_row - z_nat[:, h:h + 1], NEG_INF)
            inter = b_row + m_prev
            raw = _dot(k_sel[j], q_t)
            yield
            m_t = jnp.maximum(inter, jnp.max(dm, axis=0, keepdims=True))
            yield
            s_w = raw * jnp.exp(dm - m_t)
            den.append(jnp.sum(s_w, axis=0, keepdims=True))
            sw.append(s_w.astype(bf16))
            yield
            a_int.append(jnp.exp(inter - m_t))
            e_m.append(jnp.exp(-m_t))
            m_new = jnp.broadcast_to(m_t[:, L - 1:L], (1, L))
            b_last = jnp.broadcast_to(b_row[:, L - 1:L], (1, L))
            a_st.append(jnp.exp(b_last + m_prev - m_new))
            w_s.append(jnp.exp(b_last - b_row + i_row - m_new))
            m_next = jnp.where(sub8 == h, m_new, m_next)
            yield
        ct = ct_s[p]
        nr = nr_s[p]
        qn = _dot(nr.astype(bf16), q_t)
        num0 = _dot(jnp.where(top, v_t, 0.0).astype(bf16), sw[0])
        yield
        num1 = _dot(jnp.where(top, 0.0, v_t).astype(bf16), sw[1])
        qc = _dot(ct.astype(bf16), q_t)
        yield
        num = num0 + num1 + jnp.where(top, a_int[0], a_int[1]) * qc
        den0 = den[0] + a_int[0] * qn[0:1, :]
        den1 = den[1] + a_int[1] * qn[1:2, :]
        hh = num / jnp.where(top, jnp.maximum(jnp.abs(den0), e_m[0]), jnp.maximum(jnp.abs(den1), e_m[1]))
        yield
        sq = hh * hh
        r0 = lax.rsqrt(jnp.sum(sq[0:HEAD_W, :], axis=0, keepdims=True) * (1.0 / HEAD_W) + EPS)
        r1 = lax.rsqrt(jnp.sum(sq[HEAD_W:LANES, :], axis=0, keepdims=True) * (1.0 / HEAD_W) + EPS)
        y = (hh * jnp.where(top, r0, r1)).T * nw_ref[:, sl] * jax.nn.sigmoid(o2)
        out_ref[0, :, sl] = y.astype(bf16)
        yield
        vw = (v_t * jnp.where(top, w_s[0], w_s[1])).astype(bf16)
        ct_s[p] = jnp.where(top, a_st[0], a_st[1]) * ct + jnp.where(bd_mask, _dot(vw, kb), 0.0)
        w8 = jnp.where(sub8 == 0, w_s[0], jnp.where(sub8 == 1, w_s[1], 0.0)).astype(bf16)
        nr_s[p] = jnp.where(sub8 == 0, a_st[0], a_st[1]) * nr + n_mask * _dot(w8, kb)
        yield
    m_s[...] = m_next


def _mlstm_prompt(ml, gates, gate_bias, norm_w, layer):
    b, t, _ = ml.shape
    L = PROMPT_CHUNK
    ns = SEQS_PER_STEP
    npair = ML_HEADS // 2
    assert b % ns == 0
    return pl.pallas_call(
        _mlstm_prompt_body,
        grid=(b // ns, t // L),
        in_specs=[pl.BlockSpec((ns, L, ML_COLS), lambda i, c: (i, c, 0)),
                  pl.BlockSpec((ns, L, GATE_W), lambda i, c: (i, c, 0)),
                  _layer_resident((1, GATE_W), layer), _layer_resident((1, ML_W), layer)],
        out_specs=[pl.BlockSpec((ns, L, ML_W), lambda i, c: (i, c, 0)),
                   pl.BlockSpec((ns, ML_HEADS, HEAD_W, HEAD_W), lambda i, c: (i, 0, 0, 0)),
                   pl.BlockSpec((ns, ML_HEADS, HEAD_W), lambda i, c: (i, 0, 0)),
                   pl.BlockSpec((ns, 8, LANES), lambda i, c: (i, 0, 0))],
        out_shape=[jax.ShapeDtypeStruct((b, t, ML_W), bf16),
                   jax.ShapeDtypeStruct((b, ML_HEADS, HEAD_W, HEAD_W), f32),
                   jax.ShapeDtypeStruct((b, ML_HEADS, HEAD_W), f32),
                   jax.ShapeDtypeStruct((b, 8, LANES), f32)],
        scratch_shapes=[pltpu.VMEM((ns, npair, LANES, LANES), f32), pltpu.VMEM((ns, npair, 8, LANES), f32),
                        pltpu.VMEM((ns, 8, LANES), f32)],
        compiler_params=_cparams("arbitrary", "arbitrary"),
        name="mlstm_prompt",
    )(ml, gates, gate_bias, norm_w)


DT_LANE = 2 * ML_HEADS
HALO = 8


def _ssd_prompt_body(ssd_ref, g_ref, gb_ref, al_ref, cw_ref, cb_ref, d_ref, nw_ref, out_ref, h1_ref,
                     xbuf_s, h_s):
    c = pl.program_id(1)
    seqs = range(ssd_ref.shape[0])

    @pl.when(c == 0)
    def _():
        xbuf_s[:, 0:HALO, :] = jnp.zeros((xbuf_s.shape[0], HALO, SSD_CONV_DIM), f32)
        h_s[...] = jnp.zeros(h_s.shape, f32)

    _interleave([_ssd_prompt_chunk(_one_seq(ssd_ref, bi), _one_seq(g_ref, bi), gb_ref, al_ref, cw_ref, cb_ref, d_ref,
                                   nw_ref, _one_seq(out_ref, bi), xbuf_s.at[bi], h_s.at[bi]) for bi in seqs])

    @pl.when(c == pl.num_programs(1) - 1)
    def _():
        h1_ref[...] = h_s[...]


def _ssd_prompt_chunk(ssd_ref, g_ref, gb_ref, al_ref, cw_ref, cb_ref, d_ref, nw_ref, out_ref, xbuf_s, h_s):
    L = PROMPT_CHUNK
    xbuf_s[HALO:HALO + L, :] = ssd_ref[0, :, SSD_W:SSD_COLS]
    conv = cb_ref[...]
    for j in range(SSD_CONV):
        o = HALO - (SSD_CONV - 1) + j
        conv = conv + cw_ref[j:j + 1, :] * xbuf_s[o:o + L, :]
    xbuf_s[0:HALO, :] = xbuf_s[L:L + HALO, :]
    yield
    xc = _silu(conv)
    nb = SSD_W
    b_b = [xc[:, nb + gi * LANES:nb + (gi + 1) * LANES].astype(bf16) for gi in range(2)]
    c_b = [xc[:, nb + (2 + gi) * LANES:nb + (3 + gi) * LANES].astype(bf16) for gi in range(2)]

    row = _iota2((L, L), 0)
    col = _iota2((L, L), 1)
    tril = col <= row
    lo_half = col < HEAD_W
    tri_b = _ones_where(tril)
    bd_b = _ones_where(_same_head(row, col))
    rowc = _iota2((L, 1), 0)

    g = g_ref[0] + gb_ref[...]
    dt = jax.nn.softplus(g)
    yield
    cum = _cumsum_rows(dt * (-jnp.exp(al_ref[...])), tri_b)
    yield
    cum_t = cum.T
    dt_t = dt.T
    cb_g = [_dot_nt(c_b[gi], b_b[gi]) for gi in range(2)]
    yield

    for p in range(SSD_HEADS // 2):
        sl = slice(p * LANES, (p + 1) * LANES)
        x2 = xc[:, sl]
        z2 = ssd_ref[0, :, sl]
        hp = h_s[p]
        hb = hp.astype(bf16)
        grp = [(2 * p + j) // (SSD_HEADS // 2) for j in range(2)]
        ys = None
        ecum, w_s, a_last, ch = [], [], [], []
        for j in range(2):
            ln = DT_LANE + 2 * p + j
            cum_col = cum[:, ln:ln + 1]
            cum_row = cum_t[ln:ln + 1, :]
            cum_last = cum_col[L - 1:L, :]
            lm = jnp.exp(jnp.where(tril, cum_col - cum_row, NEG_INF))
            m = (cb_g[grp[j]] * lm * dt_t[ln:ln + 1, :]).astype(bf16)
            yield
            xm = jnp.where(lo_half, x2, 0.0) if j == 0 else jnp.where(lo_half, 0.0, x2)
            d = _dot(m, xm.astype(bf16))
            ys = d if ys is None else ys + d
            ecum.append(jnp.exp(cum_col))
            w_s.append(jnp.exp(cum_last - cum_col) * dt[:, ln:ln + 1])
            a_last.append(jnp.exp(cum_last))
            if j == 0 or grp[1] != grp[0]:
                ch.append(_dot_nt(c_b[grp[j]], hb))
            else:
                ch.append(ch[0])
            yield
        ys = ys + jnp.where(lo_half, ecum[0] * ch[0], ecum[1] * ch[1])
        xw_t = (x2 * jnp.where(lo_half, w_s[0], w_s[1])).T.astype(bf16)
        yield
        if grp[0] == grp[1]:
            upd = _dot(xw_t, b_b[grp[0]])
        else:
            upd = jnp.where(row < HEAD_W, _dot(xw_t, b_b[grp[0]]), _dot(xw_t, b_b[grp[1]]))
        h_s[p] = jnp.where(rowc < HEAD_W, a_last[0], a_last[1]) * hp + upd
        yy = (ys + d_ref[:, sl] * x2) * _silu(z2)
        yield
        ms = _group_mean(yy * yy, bd_b)
        yield
        out_ref[0, :, sl] = (yy * lax.rsqrt(ms + EPS) * nw_ref[:, sl]).astype(bf16)


def _ssd_prompt(ssd, gates, gate_bias, alog_row, conv_w, conv_b, d_row, norm_w, layer):
    b, t, _ = ssd.shape
    L = PROMPT_CHUNK
    npair = SSD_HEADS // 2
    ns = SEQS_PER_STEP
    assert b % ns == 0
    return pl.pallas_call(
        _ssd_prompt_body,
        grid=(b // ns, t // L),
        in_specs=[pl.BlockSpec((ns, L, SSD_COLS), lambda i, c: (i, c, 0)),
                  pl.BlockSpec((ns, L, GATE_W), lambda i, c: (i, c, 0)),
                  _layer_resident((1, GATE_W), layer), _layer_resident((1, GATE_W), layer),
                  _layer_resident((SSD_CONV, SSD_CONV_DIM), layer), _layer_resident((1, SSD_CONV_DIM), layer),
                  _layer_resident((1, SSD_W), layer), _layer_resident((1, SSD_W), layer)],
        out_specs=[pl.BlockSpec((ns, L, SSD_W), lambda i, c: (i, c, 0)),
                   pl.BlockSpec((ns, npair, LANES, SSD_DSTATE), lambda i, c: (i, 0, 0, 0))],
        out_shape=[jax.ShapeDtypeStruct((b, t, SSD_W), bf16),
                   jax.ShapeDtypeStruct((b, npair, LANES, SSD_DSTATE), f32)],
        scratch_shapes=[pltpu.VMEM((ns, HALO + L, SSD_CONV_DIM), f32),
                        pltpu.VMEM((ns, npair, LANES, SSD_DSTATE), f32)],
        compiler_params=_cparams("arbitrary", "arbitrary"),
        name="ssd_prompt",
    )(ssd, gates, gate_bias, alog_row, conv_w, conv_b, d_row, norm_w)


def _hg_lower_bound(lbp, layer):
    mx = jnp.max(lbp, axis=0, keepdims=True)
    e = jnp.exp(lbp - mx)
    den = jnp.sum(e, axis=0, keepdims=True)
    lb = jnp.zeros_like(den)
    for j in range(1, layer + 1):
        lb = lb + e[j:j + 1, :] / den
    return lb


def _hgrn_prompt_body(hg_ref, lbp_ref, nw_ref, out_ref, s1_ref, sbd_s, p_s, r_s, o_s, g_s, k_s, v_s, q_s,
                      kt_s, a_s, *, layer):
    c = pl.program_id(1)
    seqs = range(hg_ref.shape[0])

    @pl.when(c == 0)
    def _():
        sbd_s[...] = jnp.zeros(sbd_s.shape, f32)

    _interleave([_hgrn_prompt_chunk(_one_seq(hg_ref, bi), lbp_ref, nw_ref, _one_seq(out_ref, bi), sbd_s.at[bi],
                                    p_s.at[bi], r_s.at[bi], o_s.at[bi], g_s.at[bi], k_s.at[bi], v_s.at[bi], q_s.at[bi],
                                    kt_s.at[bi], a_s.at[bi], layer=layer) for bi in seqs])

    @pl.when(c == pl.num_programs(1) - 1)
    def _():
        for bi in seqs:
            for p in range(HG_HEADS // 2):
                sbd = sbd_s[bi, p]
                s1_ref[bi, 2 * p] = sbd[0:HEAD_W, 0:HEAD_W]
                s1_ref[bi, 2 * p + 1] = sbd[HEAD_W:LANES, HEAD_W:LANES]


def _hgrn_prompt_chunk(hg_ref, lbp_ref, nw_ref, out_ref, sbd_s, p_s, r_s, o_s, g_s, k_s, v_s, q_s, kt_s, a_s, *, layer):
    L = HG_CHUNK
    nblk = L // 8
    lb = _hg_lower_bound(lbp_ref[...], layer)
    ff = hg_ref[0, :, HG_W:2 * HG_W]
    sig = jax.nn.sigmoid(ff)
    k_s[...] = (1.0 - lb) * jax.nn.sigmoid(-ff)
    v_s[...] = hg_ref[0, :, 2 * HG_W:3 * HG_W]
    q_s[...] = hg_ref[0, :, 0:HG_W] * (HEAD_W ** -0.5)
    row = _iota2((L, L), 0)
    col = _iota2((L, L), 1)
    tri_b = _ones_where(col <= row)
    yield
    g_s[...] = _cumsum_rows(jnp.log(lb + (1.0 - lb) * sig), tri_b)
    yield

    r2 = _iota2((2 * LANES, 2 * LANES), 0)
    c2 = _iota2((2 * LANES, 2 * LANES), 1)
    bd4_b = _ones_where(_same_head(r2, c2))
    r1 = _iota2((LANES, LANES), 0)
    c1 = _iota2((LANES, LANES), 1)
    bd_mask = _same_head(r1, c1)
    bd_b = _ones_where(bd_mask)

    a_s[0:8, :] = jnp.zeros((8, HG_W), f32)
    for i in range(1, nblk):
        blk = slice(8 * i, 8 * i + 8)
        n = 8 * i
        r_i = g_s[n - 1:n, :]
        kt = k_s[0:n, :] * jnp.exp(r_i - g_s[0:n, :])
        first_head = jnp.bitwise_and(_iota2((n, HG_W), 1), LANES - 1) < HEAD_W
        kt_s[i - 1, 0:n, :] = jnp.where(first_head, kt, 0.0)
        kt_s[i - 1, n:L, :] = jnp.zeros((L - n, HG_W), f32)
        kt_s[i - 1, L:L + n, :] = jnp.where(first_head, 0.0, kt)
        kt_s[i - 1, L + n:2 * L, :] = jnp.zeros((L - n, HG_W), f32)
        qt = q_s[blk, :] * jnp.exp(g_s[blk, :] - r_i)
        qt = jnp.concatenate([qt, jnp.zeros_like(qt)], axis=0).astype(bf16)
        for p in range(HG_HEADS // 2):
            sl = slice(p * LANES, (p + 1) * LANES)
            a_s[blk, sl] = _dot_nt(qt[:, sl], kt_s[i - 1, :, sl].astype(bf16))[0:8, :]
        yield

    sub = _iota2((8, HG_W), 0)
    for j in range(nblk):
        blk = slice(8 * j, 8 * j + 8)
        g_blk, k_blk, q_blk = g_s[blk, :], k_s[blk, :], q_s[blk, :]
        for r in range(8):
            s = 8 * j + r
            e = jnp.exp(jnp.where(sub >= r, g_blk - g_blk[r:r + 1, :], NEG_INF))
            p_s[8 * s:8 * s + 8, :] = q_blk * k_blk[r:r + 1, :] * e
        yield
    r_s[...] = _dot(p_s[...].astype(bf16), bd4_b)
    yield

    gg = g_s[...]
    vv = v_s[...]
    qd = q_s[...] * jnp.exp(gg)
    lo_half = _iota2((L, LANES), 1) < HEAD_W
    for p in range(HG_HEADS // 2):
        sl = slice(p * LANES, (p + 1) * LANES)
        v_cat = jnp.concatenate([jnp.where(lo_half, vv[:, sl], 0.0), jnp.where(lo_half, 0.0, vv[:, sl])], axis=0)
        o_s[:, sl] = (_dot(qd[:, sl].astype(bf16), sbd_s[p].astype(bf16))
                      + _dot(a_s[:, sl].astype(bf16), v_cat.astype(bf16)))
        yield
    for j in range(nblk):
        blk = slice(8 * j, 8 * j + 8)
        acc = o_s[blk, :]
        v_blk = v_s[blk, :]
        for r in range(8):
            s = 8 * j + r
            acc = acc + r_s[8 * s:8 * s + 8, :] * v_blk[r:r + 1, :]
        o_s[blk, :] = acc
        yield

    g_last = gg[L - 1:L, :]
    kd = k_s[...] * jnp.exp(g_last - gg)
    zpad = jnp.zeros((LANES - L, LANES), f32)
    og = o_s[...]
    gate = hg_ref[0, :, 3 * HG_W:4 * HG_W]
    for p in range(HG_HEADS // 2):
        sl = slice(p * LANES, (p + 1) * LANES)
        kd_t = jnp.concatenate([kd[:, sl], zpad], axis=0).T.astype(bf16)
        v_pad = jnp.concatenate([vv[:, sl], zpad], axis=0).astype(bf16)
        g_t = jnp.concatenate([gg[:, sl], zpad], axis=0).T
        dec = jnp.exp(g_t[:, L - 1:L])
        sbd_s[p] = dec * sbd_s[p] + jnp.where(bd_mask, _dot(kd_t, v_pad), 0.0)
        yield
        oo = og[:, sl]
        ms = _group_mean(oo * oo, bd_b)
        out_ref[0, :, sl] = (oo * lax.rsqrt(ms + EPS) * nw_ref[:, sl] * _silu(gate[:, sl])).astype(bf16)


def _hgrn_prompt(hg, lbp, norm_w, layer):
    b, t, _ = hg.shape
    L = HG_CHUNK
    ns = SEQS_PER_STEP
    assert b % ns == 0
    row_buf = pltpu.VMEM((ns, L, HG_W), f32)
    pair_buf = pltpu.VMEM((ns, 8 * L, HG_W), f32)
    return pl.pallas_call(
        functools.partial(_hgrn_prompt_body, layer=layer),
        grid=(b // ns, t // L),
        in_specs=[pl.BlockSpec((ns, L, HG_COLS), lambda i, c: (i, c, 0)),
                  _resident((DEPTH, HG_W)), _layer_resident((1, HG_W), layer)],
        out_specs=[pl.BlockSpec((ns, L, HG_W), lambda i, c: (i, c, 0)),
                   pl.BlockSpec((ns, HG_HEADS, HEAD_W, HEAD_W), lambda i, c: (i, 0, 0, 0))],
        out_shape=[jax.ShapeDtypeStruct((b, t, HG_W), bf16),
                   jax.ShapeDtypeStruct((b, HG_HEADS, HEAD_W, HEAD_W), f32)],
        scratch_shapes=[pltpu.VMEM((ns, HG_HEADS // 2, LANES, LANES), f32), pair_buf, pair_buf,
                        row_buf, row_buf, row_buf, row_buf, row_buf,
                        pltpu.VMEM((ns, L // 8 - 1, 2 * L, HG_W), f32), row_buf],
        compiler_params=_cparams("arbitrary", "arbitrary"),
        name="hgrn_prompt",
    )(hg, lbp, norm_w)


NB = 128
TS = 8


def _sample_call(body, name, grid, in_specs, operands, out_specs, out_shapes, scratch, n_state):
    first = len(operands) - n_state
    return pl.pallas_call(
        body, grid=grid, in_specs=in_specs, out_specs=out_specs, out_shape=out_shapes,
        scratch_shapes=scratch, input_output_aliases={first + i: 1 + i for i in range(n_state)},
        compiler_params=_cparams(*(["arbitrary"] * len(grid))), name=name,
    )(*operands)


def _mlstm_sample_body(q_ref, k_ref, v_ref, o_ref, g_ref, gb_ref, nw_ref, c0_ref, n0_ref, m0_ref,
                       out_ref, c1_ref, n1_ref, m1_ref, qt_s, kt_s, vt_s, gt_s, ht_s):
    p = pl.program_id(0)
    n_cur = [n0_ref[hl] for hl in range(2)]
    m_cur = [m0_ref[pl.ds(2 * p + hl, 1), :] for hl in range(2)]

    for t in range(TS):
        c_src = c0_ref if t == 0 else c1_ref
        rows = slice(t * NB, (t + 1) * NB)
        qt_s[...] = q_ref[rows, :].T
        kt_s[...] = k_ref[rows, :].T * (HEAD_W ** -0.5)
        vt_s[...] = v_ref[rows, :].T
        gt_s[...] = (g_ref[rows, :] + gb_ref[...]).T
        for hl in range(2):
            h = 2 * p + hl
            hs = slice(hl * HEAD_W, (hl + 1) * HEAD_W)
            i_t = gt_s[pl.ds(h, 1), :]
            lf = jax.nn.log_sigmoid(gt_s[pl.ds(ML_HEADS + h, 1), :])
            m_new = jnp.maximum(lf + m_cur[hl], i_t)
            a = jnp.exp(lf + m_cur[hl] - m_new)
            w = jnp.exp(i_t - m_new)
            n_new = a * n_cur[hl] + w * kt_s[hs, :]
            v_h = vt_s[hs, :]

            def body(d, num, hl=hl, a=a, w=w, v_h=v_h, c_src=c_src):
                kd = kt_s[pl.ds(hl * HEAD_W + d, 1), :] * w
                qd = qt_s[pl.ds(hl * HEAD_W + d, 1), :]
                cd = a * c_src[hl, d] + kd * v_h
                c1_ref[hl, d] = cd
                return num + qd * cd

            num = lax.fori_loop(0, HEAD_W, body, jnp.zeros((HEAD_W, NB), f32), unroll=4)
            den = jnp.sum(qt_s[hs, :] * n_new, axis=0, keepdims=True)
            hh = num / jnp.maximum(jnp.abs(den), jnp.exp(-m_new))
            ms = jnp.mean(hh * hh, axis=0, keepdims=True)
            ht_s[hs, :] = hh * lax.rsqrt(ms + EPS)
            n_cur[hl] = n_new
            m_cur[hl] = m_new
        out_ref[rows, :] = (ht_s[...].T * nw_ref[...] * jax.nn.sigmoid(o_ref[rows, :])).astype(bf16)

    for hl in range(2):
        n1_ref[hl] = n_cur[hl]
        m1_ref[pl.ds(2 * p + hl, 1), :] = m_cur[hl]


def _mlstm_sample(ml, gates, gate_bias, norm_w, c0t, n0t, m0t, layer):
    n = ml.shape[0]
    npair = ML_HEADS // 2
    blk = lambda off: pl.BlockSpec((n, LANES), lambda p, off=off: (0, off + p))
    c_spec = pl.BlockSpec((None, 2, HEAD_W, HEAD_W, NB), lambda p: (layer, p, 0, 0, 0))
    n_spec = pl.BlockSpec((None, 2, HEAD_W, NB), lambda p: (layer, p, 0, 0))
    m_spec = pl.BlockSpec((None, ML_HEADS, NB), lambda p: (layer, 0, 0))
    return _sample_call(
        _mlstm_sample_body, "mlstm_sample", (npair,),
        [blk(0), blk(npair), blk(2 * npair), blk(3 * npair), _resident((n, GATE_W)),
         _layer_resident((1, GATE_W), layer), pl.BlockSpec((None, 1, LANES), lambda p: (layer, 0, p)),
         c_spec, n_spec, m_spec],
        [ml, ml, ml, ml, gates, gate_bias, norm_w, c0t, n0t, m0t],
        [pl.BlockSpec((n, LANES), lambda p: (0, p)), c_spec, n_spec, m_spec],
        [jax.ShapeDtypeStruct((n, ML_W), bf16), jax.ShapeDtypeStruct(c0t.shape, f32),
         jax.ShapeDtypeStruct(n0t.shape, f32), jax.ShapeDtypeStruct(m0t.shape, f32)],
        [pltpu.VMEM((LANES, NB), f32)] * 5, 3)


def _ssd_conv_sample_body(ssd_ref, halo_ref, cw_ref, cb_ref, out_ref, xp_s):
    nh = (SSD_CONV - 1) * NB
    n = TS * NB
    xp_s[0:nh, :] = halo_ref[...]
    xp_s[nh:nh + n, :] = ssd_ref[:, SSD_W:SSD_COLS]
    conv = cb_ref[...]
    for j in range(SSD_CONV):
        conv = conv + cw_ref[j:j + 1, :] * xp_s[j * NB:j * NB + n, :]
    out_ref[...] = _silu(conv)


def _ssd_conv_sample(ssd, halo, conv_w, conv_b, layer):
    n = ssd.shape[0]
    nh = (SSD_CONV - 1) * NB
    return pl.pallas_call(
        _ssd_conv_sample_body,
        grid=(1,),
        in_specs=[_resident((n, SSD_COLS)), _layer_resident((nh, SSD_CONV_DIM), layer),
                  _layer_resident((SSD_CONV, SSD_CONV_DIM), layer), _layer_resident((1, SSD_CONV_DIM), layer)],
        out_specs=pl.BlockSpec((n, SSD_CONV_DIM), lambda i: (0, 0)),
        out_shape=jax.ShapeDtypeStruct((n, SSD_CONV_DIM), f32),
        scratch_shapes=[pltpu.VMEM((nh + n, SSD_CONV_DIM), f32)],
        compiler_params=_cparams("arbitrary"),
        name="ssd_conv_sample",
    )(ssd, halo, conv_w, conv_b)


def _ssd_sample_body(z_ref, x_ref, b0_ref, b1_ref, c0_ref, c1_ref, g_ref, gb_ref, hp_ref, nw_ref, h0_ref,
                     out_ref, h1_ref, xt_s, bt_s, ct_s, gt_s, yt_s):
    p = pl.program_id(0)
    b_refs = (b0_ref, b1_ref)
    c_refs = (c0_ref, c1_ref)

    for t in range(TS):
        h_src = h0_ref if t == 0 else h1_ref
        rows = slice(t * NB, (t + 1) * NB)
        xt_s[...] = x_ref[rows, :].T
        gt_s[...] = (g_ref[rows, :] + gb_ref[...]).T
        zt = z_ref[rows, :].T
        for hl in range(2):
            bt_s[hl] = b_refs[hl][rows, :].T
            ct_s[hl] = c_refs[hl][rows, :].T
        for hl in range(2):
            h = 2 * p + hl
            hs = slice(hl * HEAD_W, (hl + 1) * HEAD_W)
            dt = jax.nn.softplus(gt_s[pl.ds(DT_LANE + h, 1), :])
            da = jnp.exp(dt * (-jnp.exp(hp_ref[0, hl:hl + 1, :])))

            def body(pp, carry, hl=hl, dt=dt, da=da, h_src=h_src):
                xrow = xt_s[pl.ds(hl * HEAD_W + pp, 1), :] * dt
                hn = da * h_src[hl, pp] + xrow * bt_s[hl]
                h1_ref[hl, pp] = hn
                yt_s[pl.ds(hl * HEAD_W + pp, 1), :] = jnp.sum(ct_s[hl] * hn, axis=0, keepdims=True)
                return carry

            lax.fori_loop(0, HEAD_W, body, 0, unroll=8)
            yy = (yt_s[hs, :] + hp_ref[1, hl:hl + 1, :] * xt_s[hs, :]) * _silu(zt[hs, :])
            ms = jnp.mean(yy * yy, axis=0, keepdims=True)
            yt_s[hs, :] = yy * lax.rsqrt(ms + EPS)
        out_ref[rows, :] = (yt_s[...].T * nw_ref[...]).astype(bf16)


def _ssd_sample(ssd, xc, gates, gate_bias, head_params, norm_w, h0t, layer):
    n = ssd.shape[0]
    npair = SSD_HEADS // 2
    hpg = SSD_HEADS // 2
    nxb = SSD_W // LANES
    blk = lambda f: pl.BlockSpec((n, LANES), f)
    h_spec = pl.BlockSpec((None, 2, HEAD_W, SSD_DSTATE, NB), lambda p: (layer, p, 0, 0, 0))
    return _sample_call(
        _ssd_sample_body, "ssd_sample", (npair,),
        [blk(lambda p: (0, p)), blk(lambda p: (0, p)),
         blk(lambda p: (0, nxb + (2 * p) // hpg)), blk(lambda p: (0, nxb + (2 * p + 1) // hpg)),
         blk(lambda p: (0, nxb + 2 + (2 * p) // hpg)), blk(lambda p: (0, nxb + 2 + (2 * p + 1) // hpg)),
         _resident((n, GATE_W)), _layer_resident((1, GATE_W), layer),
         pl.BlockSpec((None, 2, None, 2, LANES), lambda p: (layer, 0, p, 0, 0)),
         pl.BlockSpec((None, 1, LANES), lambda p: (layer, 0, p)), h_spec],
        [ssd, xc, xc, xc, xc, xc, gates, gate_bias, head_params, norm_w, h0t],
        [pl.BlockSpec((n, LANES), lambda p: (0, p)), h_spec],
        [jax.ShapeDtypeStruct((n, SSD_W), bf16), jax.ShapeDtypeStruct(h0t.shape, f32)],
        [pltpu.VMEM((LANES, NB), f32), pltpu.VMEM((2, SSD_DSTATE, NB), f32), pltpu.VMEM((2, SSD_DSTATE, NB), f32),
         pltpu.VMEM((LANES, NB), f32), pltpu.VMEM((LANES, NB), f32)], 1)


def _hgrn_sample_body(q_ref, f_ref, i_ref, gate_ref, lbp_ref, nw_ref, s0_ref, out_ref, s1_ref,
                      qt_s, kt_s, ft_s, vt_s, ot_s, *, layer):
    lb = _hg_lower_bound(lbp_ref[...], layer)

    for t in range(TS):
        s_src = s0_ref if t == 0 else s1_ref
        rows = slice(t * NB, (t + 1) * NB)
        ff = f_ref[rows, :]
        qt_s[...] = (q_ref[rows, :] * (HEAD_W ** -0.5)).T
        kt_s[...] = ((1.0 - lb) * jax.nn.sigmoid(-ff)).T
        ft_s[...] = (lb + (1.0 - lb) * jax.nn.sigmoid(ff)).T
        vt_s[...] = i_ref[rows, :].T
        for hl in range(2):
            hs = slice(hl * HEAD_W, (hl + 1) * HEAD_W)
            v_h = vt_s[hs, :]

            def body(k, o, hl=hl, v_h=v_h, s_src=s_src):
                r = pl.ds(hl * HEAD_W + k, 1)
                sk = ft_s[r, :] * s_src[hl, k] + kt_s[r, :] * v_h
                s1_ref[hl, k] = sk
                return o + qt_s[r, :] * sk

            o = lax.fori_loop(0, HEAD_W, body, jnp.zeros((HEAD_W, NB), f32), unroll=4)
            ms = jnp.mean(o * o, axis=0, keepdims=True)
            ot_s[hs, :] = o * lax.rsqrt(ms + EPS)
        out_ref[rows, :] = (ot_s[...].T * nw_ref[...] * _silu(gate_ref[rows, :])).astype(bf16)


def _hgrn_sample(hg, lbp, norm_w, s0t, layer):
    n = hg.shape[0]
    npair = HG_HEADS // 2
    blk = lambda off: pl.BlockSpec((n, LANES), lambda p, off=off: (0, off + p))
    s_spec = pl.BlockSpec((None, 2, HEAD_W, HEAD_W, NB), lambda p: (layer, p, 0, 0, 0))
    return _sample_call(
        functools.partial(_hgrn_sample_body, layer=layer), "hgrn_sample", (npair,),
        [blk(0), blk(npair), blk(2 * npair), blk(3 * npair),
         pl.BlockSpec((DEPTH, LANES), lambda p: (0, p)),
         pl.BlockSpec((None, 1, LANES), lambda p: (layer, 0, p)), s_spec],
        [hg, hg, hg, hg, lbp, norm_w, s0t],
        [pl.BlockSpec((n, LANES), lambda p: (0, p)), s_spec],
        [jax.ShapeDtypeStruct((n, HG_W), bf16), jax.ShapeDtypeStruct(s0t.shape, f32)],
        [pltpu.VMEM((LANES, NB), f32)] * 5, 1)


def _ffn_body(x_ref, mml_ref, mssd_ref, mhg_ref, wout_ref, n2_ref, wug_ref, wuv_ref, cwg_ref, cwv_ref,
              cbg_ref, cbv_ref, wdn_ref, f0g_ref, f0v_ref, fn_ref, y_ref, f1g_ref, f1v_ref,
              x1_s, h2_s, acc_s, act_s, ubuf_s, *, tm, shift, jb, nff, final):
    t = pl.program_id(1)
    f = pl.program_id(2)
    halo = max(HALO, (FFN_CONV - 1) * shift)
    keep = (FFN_CONV - 1) * shift

    def when(cond, fn):
        if nff == 1:
            fn()
        else:
            pl.when(cond)(fn)

    @pl.when(t == 0)
    def _():
        f1g_ref[0] = f0g_ref[0]
        f1v_ref[0] = f0v_ref[0]

    def first():
        x = x_ref[...]
        x1 = (x + _dot(mml_ref[...], wout_ref[0:ML_W, :]) + _dot(mssd_ref[...], wout_ref[ML_W:ML_W + SSD_W, :])
              + _dot(mhg_ref[...], wout_ref[ML_W + SSD_W:D_MODEL, :]))
        x1_s[...] = x1
        h2_s[...] = (x1 * lax.rsqrt(jnp.mean(x1 * x1, axis=-1, keepdims=True) + EPS) * n2_ref[...]).astype(bf16)

    when(f == 0, first)
    h2 = h2_s[...]

    def conv(wu_ref, cw_ref, cb_ref, f1_ref, cols, ubuf):
        u = _dot(h2, wu_ref[:, cols])
        ubuf[halo - keep:halo, :] = f1_ref[0, :, cols]
        ubuf[halo:halo + tm, :] = u
        yv = (cb_ref[:, cols] + cw_ref[2:3, cols] * u
              + cw_ref[1:2, cols] * ubuf[halo - shift:halo - shift + tm, :]
              + cw_ref[0:1, cols] * ubuf[halo - 2 * shift:halo - 2 * shift + tm, :])
        f1_ref[0, :, cols] = ubuf[halo + tm - keep:halo + tm, :]
        return yv

    nbuf = ubuf_s.shape[0]
    for j in range(jb):
        cols = slice(j * FF_BLOCK, (j + 1) * FF_BLOCK)
        gq = conv(wug_ref, cwg_ref, cbg_ref, f1g_ref, cols, ubuf_s.at[(2 * j) % nbuf])
        vq = conv(wuv_ref, cwv_ref, cbv_ref, f1v_ref, cols, ubuf_s.at[(2 * j + 1) % nbuf])
        act_s[:, cols] = (_silu(gq) * vq).astype(bf16)
    contrib = _dot(act_s[...], wdn_ref[...])

    def finish(down):
        x2 = x1_s[...] + down
        if final:
            x2 = x2 * lax.rsqrt(jnp.mean(x2 * x2, axis=-1, keepdims=True) + EPS) * fn_ref[...]
        y_ref[...] = x2

    if nff == 1:
        finish(contrib)
    else:
        @pl.when(f == 0)
        def _():
            acc_s[...] = contrib

        @pl.when(f != 0)
        def _():
            acc_s[...] = acc_s[...] + contrib

        pl.when(f == nff - 1)(lambda: finish(acc_s[...]))


def _ffn(x, mml, mssd, mhg, w_out, norm2, w_up, conv_w, conv_b, w_down, f0, final_w, *, layer, groups, tm, shift, jb,
         final, f0_base=0):
    n = x.shape[0]
    nt = n // (groups * tm)
    fw = jb * FF_BLOCK
    nff = D_FF // fw
    assert nt == 1 or nff == 1
    halo = max(HALO, (FFN_CONV - 1) * shift)
    keep = (FFN_CONV - 1) * shift
    rows = lambda w: pl.BlockSpec((tm, w), lambda g, t, f: (g * nt + t, 0))
    single = nff == 1
    wspec = lambda shape, imap: pl.BlockSpec(shape, imap, pipeline_mode=pl.Buffered(1)) if single else pl.BlockSpec(shape, imap)
    f1_spec = pl.BlockSpec((1, keep, fw), lambda g, t, f: (g, 0, f))
    in_specs = [rows(D_MODEL), rows(ML_W), rows(SSD_W), rows(HG_W),
                _layer_resident((D_MODEL, D_MODEL), layer), _layer_resident((1, D_MODEL), layer),
                wspec((None, D_MODEL, fw), lambda g, t, f: (layer, 0, f)),
                wspec((None, D_MODEL, fw), lambda g, t, f: (layer, 0, nff + f)),
                wspec((None, FFN_CONV, fw), lambda g, t, f: (layer, 0, f)),
                wspec((None, FFN_CONV, fw), lambda g, t, f: (layer, 0, nff + f)),
                wspec((None, 1, fw), lambda g, t, f: (layer, 0, f)),
                wspec((None, 1, fw), lambda g, t, f: (layer, 0, nff + f)),
                wspec((None, fw, D_MODEL), lambda g, t, f: (layer, f, 0)),
                pl.BlockSpec((1, keep, fw), lambda g, t, f: (f0_base + g, 0, f)),
                pl.BlockSpec((1, keep, fw), lambda g, t, f: (f0_base + g, 0, nff + f)),
                _resident((1, D_MODEL))]
    operands = [x, mml, mssd, mhg, w_out, norm2, w_up, w_up, conv_w, conv_w, conv_b, conv_b, w_down, f0, f0, final_w]
    acc_rows = tm if nff > 1 else 8
    return pl.pallas_call(
        functools.partial(_ffn_body, tm=tm, shift=shift, jb=jb, nff=nff, final=final),
        grid=(groups, nt, nff),
        in_specs=in_specs,
        out_specs=[rows(D_MODEL), f1_spec, f1_spec],
        out_shape=[jax.ShapeDtypeStruct((n, D_MODEL), f32),
                   jax.ShapeDtypeStruct((groups, keep, D_FF), f32),
                   jax.ShapeDtypeStruct((groups, keep, D_FF), f32)],
        scratch_shapes=[pltpu.VMEM((tm, D_MODEL), f32), pltpu.VMEM((tm, D_MODEL), bf16),
                        pltpu.VMEM((acc_rows, D_MODEL), f32), pltpu.VMEM((tm, fw), bf16),
                        pltpu.VMEM((min(4, 2 * jb), halo + tm, FF_BLOCK), f32)],
        compiler_params=_cparams("arbitrary", "arbitrary", "arbitrary"),
        name="ffn",
    )(*operands)


def _gate_rows(*parts):
    r = jnp.concatenate([p.astype(f32) for p in parts], axis=1)
    return jnp.pad(r, ((0, 0), (0, GATE_W - r.shape[1])))[:, None, :]


def kernel(x_prompt, x_sample, state_mlstm_C, state_mlstm_n, state_mlstm_m, state_ssd, state_ssd_conv,
           state_hgrn, state_ffn_conv, norm1_w, w_in, ml_ig_b, ml_fg_b, ml_norm_w, ssd_conv_w, ssd_conv_b,
           ssd_dt_bias, ssd_A_log, ssd_D, ssd_norm_w, hg_lower_bounds, hg_norm_w, w_out, norm2_w, w_up,
           ffn_conv_w, ffn_conv_b, w_down, final_norm_w):
    bp, tp, _ = x_prompt.shape
    nb, ts, _ = x_sample.shape
    assert nb == NB and ts == TS and tp % ROW_TILE == 0
    depth = w_in.shape[0]
    assert depth == DEPTH
    xp = x_prompt.reshape(bp * tp, D_MODEL)
    xs = jnp.transpose(x_sample, (1, 0, 2)).reshape(ts * nb, D_MODEL)
    lbp = hg_lower_bounds.astype(f32)
    fin_w = final_norm_w[None, :]
    zero_f = jnp.zeros((bp, FFN_CONV - 1, 2 * D_FF), f32)

    w_cat = _prep_w_in(w_in)
    wo_b, wu_b, wd_b = w_out.astype(bf16), w_up.astype(bf16), w_down.astype(bf16)
    n1, n2 = norm1_w[:, None, :], norm2_w[:, None, :]
    gate_bias = _gate_rows(ml_ig_b, ml_fg_b, ssd_dt_bias)
    alog_row = _gate_rows(jnp.zeros((depth, DT_LANE), f32), ssd_A_log)
    d_row = jnp.repeat(ssd_D.astype(f32), HEAD_W, axis=1)[:, None, :]
    mlw, ssw, hgw = ml_norm_w[:, None, :], ssd_norm_w[:, None, :], hg_norm_w[:, None, :]
    cw, cb = ssd_conv_w, ssd_conv_b[:, None, :]
    fcw, fcb = ffn_conv_w, ffn_conv_b[:, None, :]
    head_params = jnp.broadcast_to(
        jnp.stack([ssd_A_log, ssd_D], axis=1).astype(f32).reshape(depth, 2, SSD_HEADS // 2, 2, 1),
        (depth, 2, SSD_HEADS // 2, 2, LANES))

    c_t = jnp.transpose(state_mlstm_C.astype(f32), (0, 2, 3, 4, 1))
    n_t = jnp.transpose(state_mlstm_n.astype(f32), (0, 2, 3, 1))
    m_t = jnp.transpose(state_mlstm_m.astype(f32), (0, 2, 1))
    h_t = jnp.transpose(state_ssd.astype(f32), (0, 2, 3, 4, 1))
    s_t = jnp.transpose(state_hgrn.astype(f32), (0, 2, 3, 4, 1))
    conv_halo = jnp.transpose(state_ssd_conv.astype(f32), (0, 2, 1, 3)).reshape(depth, (SSD_CONV - 1) * nb, SSD_CONV_DIM)
    f_t = jnp.transpose(state_ffn_conv.astype(f32), (0, 2, 1, 3)).reshape(depth, (FFN_CONV - 1) * nb, 2 * D_FF)

    p_states, s_sconv, s_ffn = [], [], []
    for l in range(depth):
        last = l == depth - 1

        ml, ssd, hg, gates = _inproj(xp, n1, w_cat, l)
        gates3 = gates.reshape(bp, tp, GATE_W)
        ssd3 = ssd.reshape(bp, tp, SSD_COLS)
        hm, p_c, p_n, p_m = _mlstm_prompt(ml.reshape(bp, tp, ML_COLS), gates3, gate_bias, mlw, l)
        hs, p_h = _ssd_prompt(ssd3, gates3, gate_bias, alog_row, cw, cb, d_row, ssw, l)
        ho, p_s = _hgrn_prompt(hg.reshape(bp, tp, HG_COLS), lbp, hgw, l)
        xp, p_fg, p_fv = _ffn(xp, hm.reshape(bp * tp, ML_W), hs.reshape(bp * tp, SSD_W), ho.reshape(bp * tp, HG_W),
                              wo_b, n2, wu_b, fcw, fcb, wd_b, zero_f, fin_w, layer=l,
                              groups=bp, tm=ROW_TILE, shift=1, jb=D_FF // FF_BLOCK, final=last)
        p_states.append((p_c, p_n, p_m[:, 0:ML_HEADS, 0],
                         p_h.reshape(bp, SSD_HEADS, HEAD_W, SSD_DSTATE),
                         ssd3[:, tp - (SSD_CONV - 1):, SSD_W:], p_s,
                         jnp.concatenate([p_fg, p_fv], axis=-1)))

        ml, ssd, hg, gates = _inproj(xs, n1, w_cat, l)
        hm, c_t, n_t, m_t = _mlstm_sample(ml, gates, gate_bias, mlw, c_t, n_t, m_t, l)
        xc = _ssd_conv_sample(ssd, conv_halo, cw, cb, l)
        hs, h_t = _ssd_sample(ssd, xc, gates, gate_bias, head_params, ssw, h_t, l)
        ho, s_t = _hgrn_sample(hg, lbp, hgw, s_t, l)
        xs, s_fg, s_fv = _ffn(xs, hm, hs, ho, wo_b, n2, wu_b, fcw, fcb, wd_b, f_t, fin_w, layer=l,
                              groups=1, tm=ts * nb, shift=nb, jb=1, final=last, f0_base=l)
        s_ffn.append(jnp.concatenate([s_fg, s_fv], axis=-1).reshape(FFN_CONV - 1, nb, 2 * D_FF))
        raw_xbc = ssd[:, SSD_W:].reshape(ts, nb, SSD_CONV_DIM)
        s_sconv.append(jnp.transpose(raw_xbc[ts - (SSD_CONV - 1):], (1, 0, 2)))

    y_prompt = xp.reshape(bp, tp, D_MODEL)
    y_sample = jnp.transpose(xs.reshape(ts, nb, D_MODEL), (1, 0, 2))
    p_out = [jnp.stack([st[i] for st in p_states]) for i in range(7)]
    s_out = [jnp.transpose(c_t, (0, 4, 1, 2, 3)), jnp.transpose(n_t, (0, 3, 1, 2)), jnp.transpose(m_t, (0, 2, 1)),
             jnp.transpose(h_t, (0, 4, 1, 2, 3)), jnp.stack(s_sconv),
             jnp.transpose(s_t, (0, 4, 1, 2, 3)), jnp.transpose(jnp.stack(s_ffn), (0, 2, 1, 3))]
    return (y_prompt, y_sample, *p_out, *s_out)
```

```python
import functools

import jax
import jax.numpy as jnp
from jax import lax
from jax.experimental import pallas as pl
from jax.experimental.pallas import tpu as pltpu

f32 = jnp.float32
bf16 = jnp.bfloat16

D_MODEL = 1024
DEPTH = 2
ML_HEADS = 6
HEAD_W = 64
ML_W = 384
SSD_HEADS = 6
SSD_W = 384
SSD_DSTATE = 128
SSD_CONV = 4
SSD_CONV_DIM = 896
HG_HEADS = 4
HG_W = 256
D_FF = 2816
FFN_CONV = 3
EPS = 1e-6
PROMPT_CHUNK = 128
HG_CHUNK = 64
LANES = 128
FF_BLOCK = 256
ROW_TILE = 512
SEQS_PER_STEP = 4
GATE_W = 128
W_IN_COLS = 4 * ML_W + (SSD_W + SSD_CONV_DIM) + 4 * HG_W + GATE_W
VMEM_LIMIT_BYTES = 56 * 1024 * 1024

NEG_INF = float("-inf")


def _cparams(*sem):
    return pltpu.CompilerParams(dimension_semantics=sem, vmem_limit_bytes=VMEM_LIMIT_BYTES)


def _resident(shape):
    nd = len(shape)
    return pl.BlockSpec(shape, lambda *_: (0,) * nd, pipeline_mode=pl.Buffered(1))


def _layer_resident(shape, layer):
    nd = len(shape)
    return pl.BlockSpec((None,) + tuple(shape), lambda *_: (layer,) + (0,) * nd, pipeline_mode=pl.Buffered(1))


def _dot(a, b):
    return jnp.dot(a, b, preferred_element_type=f32)


def _dot_nt(a, b):
    return lax.dot_general(a, b, (((1,), (1,)), ((), ())), preferred_element_type=f32)


def _split_bf16(x, parts):
    out = []
    r = x
    for _ in range(parts):
        h = r.astype(bf16)
        out.append(h)
        r = r - h.astype(f32)
    return out


def _cumsum_rows(x, tri_b):
    acc = None
    for part in _split_bf16(x, 3):
        d = _dot(tri_b, part)
        acc = d if acc is None else acc + d
    return acc


def _group_mean(x, bd_b):
    hi, lo = _split_bf16(x, 2)
    return (_dot(hi, bd_b) + _dot(lo, bd_b)) * (1.0 / HEAD_W)


def _iota2(shape, axis):
    return lax.broadcasted_iota(jnp.int32, shape, axis)


def _ones_where(mask):
    return jnp.where(mask, 1.0, 0.0).astype(bf16)


def _same_head(a, b):
    return jnp.right_shift(a, 6) == jnp.right_shift(b, 6)


def _silu(x):
    return x * jax.nn.sigmoid(x)


ML_COLS = 4 * ML_W
SSD_COLS = SSD_W + SSD_CONV_DIM
HG_COLS = 4 * HG_W


def _inproj_body(x_ref, nw_ref, w_ref, ml_ref, ssd_ref, hg_ref, g_ref):
    x = x_ref[...]
    h = (x * lax.rsqrt(jnp.mean(x * x, axis=-1, keepdims=True) + EPS) * nw_ref[...]).astype(bf16)
    o0, o1, o2 = ML_COLS, ML_COLS + SSD_COLS, ML_COLS + SSD_COLS + HG_COLS
    ml_ref[...] = _dot(h, w_ref[:, 0:o0])
    ssd_ref[...] = _dot(h, w_ref[:, o0:o1])
    hg_ref[...] = _dot(h, w_ref[:, o1:o2])
    g_ref[...] = _dot(h, w_ref[:, o2:W_IN_COLS])


def _inproj(x, norm_w, w_cat, layer):
    n = x.shape[0]
    tm = ROW_TILE
    row = lambda w: pl.BlockSpec((tm, w), lambda i: (i, 0))
    return pl.pallas_call(
        _inproj_body,
        grid=(n // tm,),
        in_specs=[row(D_MODEL), _layer_resident((1, D_MODEL), layer), _layer_resident((D_MODEL, W_IN_COLS), layer)],
        out_specs=[row(ML_COLS), row(SSD_COLS), row(HG_COLS), row(GATE_W)],
        out_shape=[jax.ShapeDtypeStruct((n, w), f32) for w in (ML_COLS, SSD_COLS, HG_COLS, GATE_W)],
        compiler_params=_cparams("arbitrary"),
        name="inproj",
    )(x, norm_w, w_cat)


W_IN_GATE_IF = ML_COLS
W_IN_SSD = W_IN_GATE_IF + 2 * ML_HEADS
W_IN_DT = W_IN_SSD + SSD_COLS
W_IN_HG = W_IN_DT + SSD_HEADS
W_IN_DIM = W_IN_HG + HG_COLS
W_PREP_ROWS = 256


def _prep_w_in_body(w_ref, out_ref):
    lane = _iota2((W_PREP_ROWS, LANES), 1)
    tile = lambda k: w_ref[:, k * LANES:(k + 1) * LANES]
    out_ref[:, 0:ML_COLS] = w_ref[:, 0:ML_COLS].astype(bf16)

    def shifted(dst0, src0, ntiles):
        t0, sh = src0 // LANES, src0 % LANES
        rolled = [pltpu.roll(tile(t0 + k), LANES - sh, axis=1) for k in range(ntiles + 1)]
        for k in range(ntiles):
            out_ref[:, dst0 + k * LANES:dst0 + (k + 1) * LANES] = jnp.where(
                lane < LANES - sh, rolled[k], rolled[k + 1]).astype(bf16)

    shifted(ML_COLS, W_IN_SSD, SSD_COLS // LANES)
    shifted(ML_COLS + SSD_COLS, W_IN_HG, HG_COLS // LANES)
    assert W_IN_GATE_IF % LANES == 0 and W_IN_DT % LANES == DT_LANE
    gates = jnp.where(lane < DT_LANE, tile(W_IN_GATE_IF // LANES),
                      jnp.where(lane < DT_LANE + SSD_HEADS, tile(W_IN_DT // LANES), 0.0))
    out_ref[:, ML_COLS + SSD_COLS + HG_COLS:W_IN_COLS] = gates.astype(bf16)


def _prep_w_in(w_in):
    depth, d, n = w_in.shape
    assert n == W_IN_DIM and d % W_PREP_ROWS == 0
    return pl.pallas_call(
        _prep_w_in_body,
        grid=(depth, d // W_PREP_ROWS),
        in_specs=[pl.BlockSpec((None, W_PREP_ROWS, W_IN_COLS), lambda l, i: (l, i, 0))],
        out_specs=pl.BlockSpec((None, W_PREP_ROWS, W_IN_COLS), lambda l, i: (l, i, 0)),
        out_shape=jax.ShapeDtypeStruct((depth, d, W_IN_COLS), bf16),
        compiler_params=_cparams("arbitrary", "arbitrary"),
        name="prep_w_in",
    )(w_in)


def _one_seq(ref, bi):
    return ref.at[pl.ds(bi, 1)]


def _interleave(chunks):
    live = list(chunks)
    while live:
        nxt = []
        for g in live:
            try:
                next(g)
                nxt.append(g)
            except StopIteration:
                pass
        live = nxt


def _mlstm_prompt_chunk(ml_ref, g_ref, gb_ref, nw_ref, out_ref, ct_s, nr_s, m_s):
    L = PROMPT_CHUNK
    row = _iota2((L, L), 0)
    col = _iota2((L, L), 1)
    causal = row <= col
    tri_b = _ones_where(causal)
    top = row < HEAD_W
    lo_half = col < HEAD_W
    bd_mask = _same_head(row, col)
    sub8 = _iota2((8, L), 0)
    lo8 = _iota2((8, L), 1) < HEAD_W
    n_mask = jnp.where(sub8 == 0, jnp.where(lo8, 1.0, 0.0), jnp.where(sub8 == 1, jnp.where(lo8, 0.0, 1.0), 0.0))

    g_t = (g_ref[0] + gb_ref[...]).T
    b_t = None
    parts = _split_bf16(jax.nn.log_sigmoid(g_t), 3)
    yield
    for part in parts:
        d = _dot(part, tri_b)
        b_t = d if b_t is None else b_t + d
        yield
    z_nat = (pltpu.roll(b_t, L - ML_HEADS, axis=0) - g_t).T
    m_old = m_s[...]
    m_next = m_old
    yield

    for p in range(ML_HEADS // 2):
        sl = slice(p * LANES, (p + 1) * LANES)
        q2 = ml_ref[0, :, sl]
        k2 = ml_ref[0, :, ML_W + p * LANES:ML_W + (p + 1) * LANES] * (HEAD_W ** -0.5)
        v2 = ml_ref[0, :, 2 * ML_W + p * LANES:2 * ML_W + (p + 1) * LANES]
        o2 = ml_ref[0, :, 3 * ML_W + p * LANES:3 * ML_W + (p + 1) * LANES]
        q_t = q2.T.astype(bf16)
        v_t = v2.T
        kb = k2.astype(bf16)
        k_sel = (jnp.where(lo_half, k2, 0.0).astype(bf16), jnp.where(lo_half, 0.0, k2).astype(bf16))
        sw, den, a_int, e_m, a_st, w_s = [], [], [], [], [], []
        yield
        for j in range(2):
            h = 2 * p + j
            b_row = b_t[ML_HEADS + h:ML_HEADS + h + 1, :]
            i_row = g_t[h:h + 1, :]
            m_prev = m_old[h:h + 1, :]
            dm = jnp.where(causal, b_row - z_nat[:, h:h + 1], NEG_INF)
            inter = b_row + m_prev
            raw = _dot(k_sel[j], q_t)
            yield
            m_t = jnp.maximum(inter, jnp.max(dm, axis=0, keepdims=True))
            yield
            s_w = raw * jnp.exp(dm - m_t)
            den.append(jnp.sum(s_w, axis=0, keepdims=True))
            sw.append(s_w.astype(bf16))
            yield
            a_int.append(jnp.exp(inter - m_t))
            e_m.append(jnp.exp(-m_t))
            m_new = jnp.broadcast_to(m_t[:, L - 1:L], (1, L))
            b_last = jnp.broadcast_to(b_row[:, L - 1:L], (1, L))
            a_st.append(jnp.exp(b_last + m_prev - m_new))
            w_s.append(jnp.exp(b_last - b_row + i_row - m_new))
            m_next = jnp.where(sub8 == h, m_new, m_next)
            yield
        ct = ct_s[p]
        nr = nr_s[p]
        qn = _dot(nr.astype(bf16), q_t)
        num0 = _dot(jnp.where(top, v_t, 0.0).astype(bf16), sw[0])
        yield
        num1 = _dot(jnp.where(top, 0.0, v_t).astype(bf16), sw[1])
        qc = _dot(ct.astype(bf16), q_t)
        yield
        num = num0 + num1 + jnp.where(top, a_int[0], a_int[1]) * qc
        den0 = den[0] + a_int[0] * qn[0:1, :]
        den1 = den[1] + a_int[1] * qn[1:2, :]
        hh = num / jnp.where(top, jnp.maximum(jnp.abs(den0), e_m[0]), jnp.maximum(jnp.abs(den1), e_m[1]))
        yield
        sq = hh * hh
        r0 = lax.rsqrt(jnp.sum(sq[0:HEAD_W, :], axis=0, keepdims=True) * (1.0 / HEAD_W) + EPS)
        r1 = lax.rsqrt(jnp.sum(sq[HEAD_W:LANES, :], axis=0, keepdims=True) * (1.0 / HEAD_W) + EPS)
        y = (hh * jnp.where(top, r0, r1)).T * nw_ref[:, sl] * jax.nn.sigmoid(o2)
        out_ref[0, :, sl] = y.astype(bf16)
        yield
        vw = (v_t * jnp.where(top, w_s[0], w_s[1])).astype(bf16)
        ct_s[p] = jnp.where(top, a_st[0], a_st[1]) * ct + jnp.where(bd_mask, _dot(vw, kb), 0.0)
        w8 = jnp.where(sub8 == 0, w_s[0], jnp.where(sub8 == 1, w_s[1], 0.0)).astype(bf16)
        nr_s[p] = jnp.where(sub8 == 0, a_st[0], a_st[1]) * nr + n_mask * _dot(w8, kb)
        yield
    m_s[...] = m_next


DT_LANE = 2 * ML_HEADS
HALO = 8


def _ssd_prompt_chunk(ssd_ref, g_ref, gb_ref, al_ref, cw_ref, cb_ref, d_ref, nw_ref, out_ref, xbuf_s, h_s):
    L = PROMPT_CHUNK
    xbuf_s[HALO:HALO + L, :] = ssd_ref[0, :, SSD_W:SSD_COLS]
    conv = cb_ref[...]
    for j in range(SSD_CONV):
        o = HALO - (SSD_CONV - 1) + j
        conv = conv + cw_ref[j:j + 1, :] * xbuf_s[o:o + L, :]
    xbuf_s[0:HALO, :] = xbuf_s[L:L + HALO, :]
    yield
    xc = _silu(conv)
    nb = SSD_W
    b_b = [xc[:, nb + gi * LANES:nb + (gi + 1) * LANES].astype(bf16) for gi in range(2)]
    c_b = [xc[:, nb + (2 + gi) * LANES:nb + (3 + gi) * LANES].astype(bf16) for gi in range(2)]

    row = _iota2((L, L), 0)
    col = _iota2((L, L), 1)
    tril = col <= row
    lo_half = col < HEAD_W
    tri_b = _ones_where(tril)
    bd_b = _ones_where(_same_head(row, col))
    rowc = _iota2((L, 1), 0)

    g = g_ref[0] + gb_ref[...]
    dt = jax.nn.softplus(g)
    yield
    cum = _cumsum_rows(dt * (-jnp.exp(al_ref[...])), tri_b)
    yield
    cum_t = cum.T
    dt_t = dt.T
    cb_g = [_dot_nt(c_b[gi], b_b[gi]) for gi in range(2)]
    yield

    for p in range(SSD_HEADS // 2):
        sl = slice(p * LANES, (p + 1) * LANES)
        x2 = xc[:, sl]
        z2 = ssd_ref[0, :, sl]
        hp = h_s[p]
        hb = hp.astype(bf16)
        grp = [(2 * p + j) // (SSD_HEADS // 2) for j in range(2)]
        ys = None
        ecum, w_s, a_last, ch = [], [], [], []
        for j in range(2):
            ln = DT_LANE + 2 * p + j
            cum_col = cum[:, ln:ln + 1]
            cum_row = cum_t[ln:ln + 1, :]
            cum_last = cum_col[L - 1:L, :]
            lm = jnp.exp(jnp.where(tril, cum_col - cum_row, NEG_INF))
            m = (cb_g[grp[j]] * lm * dt_t[ln:ln + 1, :]).astype(bf16)
            yield
            xm = jnp.where(lo_half, x2, 0.0) if j == 0 else jnp.where(lo_half, 0.0, x2)
            d = _dot(m, xm.astype(bf16))
            ys = d if ys is None else ys + d
            ecum.append(jnp.exp(cum_col))
            w_s.append(jnp.exp(cum_last - cum_col) * dt[:, ln:ln + 1])
            a_last.append(jnp.exp(cum_last))
            if j == 0 or grp[1] != grp[0]:
                ch.append(_dot_nt(c_b[grp[j]], hb))
            else:
                ch.append(ch[0])
            yield
        ys = ys + jnp.where(lo_half, ecum[0] * ch[0], ecum[1] * ch[1])
        xw_t = (x2 * jnp.where(lo_half, w_s[0], w_s[1])).T.astype(bf16)
        yield
        if grp[0] == grp[1]:
            upd = _dot(xw_t, b_b[grp[0]])
        else:
            upd = jnp.where(row < HEAD_W, _dot(xw_t, b_b[grp[0]]), _dot(xw_t, b_b[grp[1]]))
        h_s[p] = jnp.where(rowc < HEAD_W, a_last[0], a_last[1]) * hp + upd
        yy = (ys + d_ref[:, sl] * x2) * _silu(z2)
        yield
        ms = _group_mean(yy * yy, bd_b)
        yield
        out_ref[0, :, sl] = (yy * lax.rsqrt(ms + EPS) * nw_ref[:, sl]).astype(bf16)


def _hg_lower_bound(lbp, layer):
    mx = jnp.max(lbp, axis=0, keepdims=True)
    e = jnp.exp(lbp - mx)
    den = jnp.sum(e, axis=0, keepdims=True)
    lb = jnp.zeros_like(den)
    for j in range(1, layer + 1):
        lb = lb + e[j:j + 1, :] / den
    return lb


def _hgrn_prompt_chunk(hg_ref, lbp_ref, nw_ref, out_ref, sbd_s, p_s, r_s, o_s, g_s, k_s, v_s, q_s, kt_s, a_s, *, layer):
    L = HG_CHUNK
    nblk = L // 8
    lb = _hg_lower_bound(lbp_ref[...], layer)
    ff = hg_ref[0, :, HG_W:2 * HG_W]
    sig = jax.nn.sigmoid(ff)
    k_s[...] = (1.0 - lb) * jax.nn.sigmoid(-ff)
    v_s[...] = hg_ref[0, :, 2 * HG_W:3 * HG_W]
    q_s[...] = hg_ref[0, :, 0:HG_W] * (HEAD_W ** -0.5)
    row = _iota2((L, L), 0)
    col = _iota2((L, L), 1)
    tri_b = _ones_where(col <= row)
    yield
    g_s[...] = _cumsum_rows(jnp.log(lb + (1.0 - lb) * sig), tri_b)
    yield

    r2 = _iota2((2 * LANES, 2 * LANES), 0)
    c2 = _iota2((2 * LANES, 2 * LANES), 1)
    bd4_b = _ones_where(_same_head(r2, c2))
    r1 = _iota2((LANES, LANES), 0)
    c1 = _iota2((LANES, LANES), 1)
    bd_mask = _same_head(r1, c1)
    bd_b = _ones_where(bd_mask)

    a_s[0:8, :] = jnp.zeros((8, HG_W), f32)
    for i in range(1, nblk):
        blk = slice(8 * i, 8 * i + 8)
        n = 8 * i
        r_i = g_s[n - 1:n, :]
        kt = k_s[0:n, :] * jnp.exp(r_i - g_s[0:n, :])
        first_head = jnp.bitwise_and(_iota2((n, HG_W), 1), LANES - 1) < HEAD_W
        kt_s[i - 1, 0:n, :] = jnp.where(first_head, kt, 0.0)
        kt_s[i - 1, n:L, :] = jnp.zeros((L - n, HG_W), f32)
        kt_s[i - 1, L:L + n, :] = jnp.where(first_head, 0.0, kt)
        kt_s[i - 1, L + n:2 * L, :] = jnp.zeros((L - n, HG_W), f32)
        qt = q_s[blk, :] * jnp.exp(g_s[blk, :] - r_i)
        qt = jnp.concatenate([qt, jnp.zeros_like(qt)], axis=0).astype(bf16)
        for p in range(HG_HEADS // 2):
            sl = slice(p * LANES, (p + 1) * LANES)
            a_s[blk, sl] = _dot_nt(qt[:, sl], kt_s[i - 1, :, sl].astype(bf16))[0:8, :]
        yield

    sub = _iota2((8, HG_W), 0)
    for j in range(nblk):
        blk = slice(8 * j, 8 * j + 8)
        g_blk, k_blk, q_blk = g_s[blk, :], k_s[blk, :], q_s[blk, :]
        for r in range(8):
            s = 8 * j + r
            e = jnp.exp(jnp.where(sub >= r, g_blk - g_blk[r:r + 1, :], NEG_INF))
            p_s[8 * s:8 * s + 8, :] = q_blk * k_blk[r:r + 1, :] * e
        yield
    r_s[...] = _dot(p_s[...].astype(bf16), bd4_b)
    yield

    gg = g_s[...]
    vv = v_s[...]
    qd = q_s[...] * jnp.exp(gg)
    lo_half = _iota2((L, LANES), 1) < HEAD_W
    for p in range(HG_HEADS // 2):
        sl = slice(p * LANES, (p + 1) * LANES)
        v_cat = jnp.concatenate([jnp.where(lo_half, vv[:, sl], 0.0), jnp.where(lo_half, 0.0, vv[:, sl])], axis=0)
        o_s[:, sl] = (_dot(qd[:, sl].astype(bf16), sbd_s[p].astype(bf16))
                      + _dot(a_s[:, sl].astype(bf16), v_cat.astype(bf16)))
        yield
    for j in range(nblk):
        blk = slice(8 * j, 8 * j + 8)
        acc = o_s[blk, :]
        v_blk = v_s[blk, :]
        for r in range(8):
            s = 8 * j + r
            acc = acc + r_s[8 * s:8 * s + 8, :] * v_blk[r:r + 1, :]
        o_s[blk, :] = acc
        yield

    g_last = gg[L - 1:L, :]
    kd = k_s[...] * jnp.exp(g_last - gg)
    zpad = jnp.zeros((LANES - L, LANES), f32)
    og = o_s[...]
    gate = hg_ref[0, :, 3 * HG_W:4 * HG_W]
    for p in range(HG_HEADS // 2):
        sl = slice(p * LANES, (p + 1) * LANES)
        kd_t = jnp.concatenate([kd[:, sl], zpad], axis=0).T.astype(bf16)
        v_pad = jnp.concatenate([vv[:, sl], zpad], axis=0).astype(bf16)
        g_t = jnp.concatenate([gg[:, sl], zpad], axis=0).T
        dec = jnp.exp(g_t[:, L - 1:L])
        sbd_s[p] = dec * sbd_s[p] + jnp.where(bd_mask, _dot(kd_t, v_pad), 0.0)
        yield
        oo = og[:, sl]
        ms = _group_mean(oo * oo, bd_b)
        out_ref[0, :, sl] = (oo * lax.rsqrt(ms + EPS) * nw_ref[:, sl] * _silu(gate[:, sl])).astype(bf16)


def _mixers_prompt_body(ml_ref, ssd_ref, hg_ref, g_ref, gb_ref, al_ref, cw_ref, cb_ref, d_ref, mlnw_ref, ssnw_ref,
                        lbp_ref, hgnw_ref,
                        hm_ref, c1_ref, n1_ref, m1_ref, hs_ref, h1_ref, ho_ref, s1_ref,
                        ct_s, nr_s, m_s, xbuf_s, h_s, sbd_s, p_s, r_s, o_s, gg_s, k_s, v_s, q_s, kt_s, a_s, *, layer):
    c = pl.program_id(1)
    seqs = range(ml_ref.shape[0])

    @pl.when(c == 0)
    def _():
        ct_s[...] = jnp.zeros(ct_s.shape, f32)
        nr_s[...] = jnp.zeros(nr_s.shape, f32)
        m_s[...] = jnp.zeros(m_s.shape, f32)
        xbuf_s[:, 0:HALO, :] = jnp.zeros((xbuf_s.shape[0], HALO, SSD_CONV_DIM), f32)
        h_s[...] = jnp.zeros(h_s.shape, f32)
        sbd_s[...] = jnp.zeros(sbd_s.shape, f32)

    def hgrn_chunks(bi):
        for half in range(PROMPT_CHUNK // HG_CHUNK):
            rows = pl.ds(half * HG_CHUNK, HG_CHUNK)
            yield from _hgrn_prompt_chunk(hg_ref.at[pl.ds(bi, 1), rows], lbp_ref, hgnw_ref, ho_ref.at[pl.ds(bi, 1), rows],
                                          sbd_s.at[bi], p_s.at[bi], r_s.at[bi], o_s.at[bi], gg_s.at[bi], k_s.at[bi],
                                          v_s.at[bi], q_s.at[bi], kt_s.at[bi], a_s.at[bi], layer=layer)

    chunks = []
    for bi in seqs:
        chunks.append(_ssd_prompt_chunk(_one_seq(ssd_ref, bi), _one_seq(g_ref, bi), gb_ref, al_ref, cw_ref, cb_ref, d_ref,
                                        ssnw_ref, _one_seq(hs_ref, bi), xbuf_s.at[bi], h_s.at[bi]))
        chunks.append(hgrn_chunks(bi))
        chunks.append(_mlstm_prompt_chunk(_one_seq(ml_ref, bi), _one_seq(g_ref, bi), gb_ref, mlnw_ref,
                                          _one_seq(hm_ref, bi), ct_s.at[bi], nr_s.at[bi], m_s.at[bi]))
    _interleave(chunks)

    @pl.when(c == pl.num_programs(1) - 1)
    def _():
        h1_ref[...] = h_s[...]
        for bi in seqs:
            for p in range(ML_HEADS // 2):
                c_pair = ct_s[bi, p].T
                c1_ref[bi, 2 * p] = c_pair[0:HEAD_W, 0:HEAD_W]
                c1_ref[bi, 2 * p + 1] = c_pair[HEAD_W:LANES, HEAD_W:LANES]
                n1_ref[bi, 2 * p:2 * p + 1, :] = nr_s[bi, p, 0:1, 0:HEAD_W]
                n1_ref[bi, 2 * p + 1:2 * p + 2, :] = nr_s[bi, p, 1:2, HEAD_W:LANES]
            m1_ref[bi] = m_s[bi]
            for p in range(HG_HEADS // 2):
                sbd = sbd_s[bi, p]
                s1_ref[bi, 2 * p] = sbd[0:HEAD_W, 0:HEAD_W]
                s1_ref[bi, 2 * p + 1] = sbd[HEAD_W:LANES, HEAD_W:LANES]


def _mixers_prompt(ml, ssd, hg, gates, gate_bias, alog_row, conv_w, conv_b, d_row, ml_nw, ssd_nw, lbp, hg_nw, layer):
    b, t, _ = ml.shape
    L = PROMPT_CHUNK
    ns = SEQS_PER_STEP
    assert b % ns == 0 and t % L == 0
    ml_pairs, ssd_pairs, hg_pairs = ML_HEADS // 2, SSD_HEADS // 2, HG_HEADS // 2
    tok = lambda w: pl.BlockSpec((ns, L, w), lambda i, c: (i, c, 0))
    per_seq = lambda *shape: pl.BlockSpec((ns,) + shape, lambda i, c: (i,) + (0,) * len(shape))
    hg_rows = pltpu.VMEM((ns, HG_CHUNK, HG_W), f32)
    hg_pairs_buf = pltpu.VMEM((ns, 8 * HG_CHUNK, HG_W), f32)
    return pl.pallas_call(
        functools.partial(_mixers_prompt_body, layer=layer),
        grid=(b // ns, t // L),
        in_specs=[tok(ML_COLS), tok(SSD_COLS), tok(HG_COLS), tok(GATE_W),
                  _layer_resident((1, GATE_W), layer), _layer_resident((1, GATE_W), layer),
                  _layer_resident((SSD_CONV, SSD_CONV_DIM), layer), _layer_resident((1, SSD_CONV_DIM), layer),
                  _layer_resident((1, SSD_W), layer), _layer_resident((1, ML_W), layer),
                  _layer_resident((1, SSD_W), layer), _resident((DEPTH, HG_W)), _layer_resident((1, HG_W), layer)],
        out_specs=[tok(ML_W), per_seq(ML_HEADS, HEAD_W, HEAD_W), per_seq(ML_HEADS, HEAD_W), per_seq(8, LANES),
                   tok(SSD_W), per_seq(ssd_pairs, LANES, SSD_DSTATE),
                   tok(HG_W), per_seq(HG_HEADS, HEAD_W, HEAD_W)],
        out_shape=[jax.ShapeDtypeStruct((b, t, ML_W), bf16),
                   jax.ShapeDtypeStruct((b, ML_HEADS, HEAD_W, HEAD_W), f32),
                   jax.ShapeDtypeStruct((b, ML_HEADS, HEAD_W), f32),
                   jax.ShapeDtypeStruct((b, 8, LANES), f32),
                   jax.ShapeDtypeStruct((b, t, SSD_W), bf16),
                   jax.ShapeDtypeStruct((b, ssd_pairs, LANES, SSD_DSTATE), f32),
                   jax.ShapeDtypeStruct((b, t, HG_W), bf16),
                   jax.ShapeDtypeStruct((b, HG_HEADS, HEAD_W, HEAD_W), f32)],
        scratch_shapes=[pltpu.VMEM((ns, ml_pairs, LANES, LANES), f32), pltpu.VMEM((ns, ml_pairs, 8, LANES), f32),
                        pltpu.VMEM((ns, 8, LANES), f32),
                        pltpu.VMEM((ns, HALO + L, SSD_CONV_DIM), f32), pltpu.VMEM((ns, ssd_pairs, LANES, SSD_DSTATE), f32),
                        pltpu.VMEM((ns, hg_pairs, LANES, LANES), f32), hg_pairs_buf, hg_pairs_buf,
                        hg_rows, hg_rows, hg_rows, hg_rows, hg_rows,
                        pltpu.VMEM((ns, HG_CHUNK // 8 - 1, 2 * HG_CHUNK, HG_W), f32), hg_rows],
        compiler_params=_cparams("arbitrary", "arbitrary"),
        name="mixers_prompt",
    )(ml, ssd, hg, gates, gate_bias, alog_row, conv_w, conv_b, d_row, ml_nw, ssd_nw, lbp, hg_nw)


NB = 128
TS = 8


def _sample_call(body, name, grid, in_specs, operands, out_specs, out_shapes, scratch, n_state):
    first = len(operands) - n_state
    return pl.pallas_call(
        body, grid=grid, in_specs=in_specs, out_specs=out_specs, out_shape=out_shapes,
        scratch_shapes=scratch, input_output_aliases={first + i: 1 + i for i in range(n_state)},
        compiler_params=_cparams(*(["arbitrary"] * len(grid))), name=name,
    )(*operands)


def _mlstm_sample_body(q_ref, k_ref, v_ref, o_ref, g_ref, gb_ref, nw_ref, c0_ref, n0_ref, m0_ref,
                       out_ref, c1_ref, n1_ref, m1_ref, qt_s, kt_s, vt_s, gt_s, ht_s):
    p = pl.program_id(0)
    n_cur = [n0_ref[hl] for hl in range(2)]
    m_cur = [m0_ref[pl.ds(2 * p + hl, 1), :] for hl in range(2)]

    for t in range(TS):
        c_src = c0_ref if t == 0 else c1_ref
        rows = slice(t * NB, (t + 1) * NB)
        qt_s[...] = q_ref[rows, :].T
        kt_s[...] = k_ref[rows, :].T * (HEAD_W ** -0.5)
        vt_s[...] = v_ref[rows, :].T
        gt_s[...] = (g_ref[rows, :] + gb_ref[...]).T
        for hl in range(2):
            h = 2 * p + hl
            hs = slice(hl * HEAD_W, (hl + 1) * HEAD_W)
            i_t = gt_s[pl.ds(h, 1), :]
            lf = jax.nn.log_sigmoid(gt_s[pl.ds(ML_HEADS + h, 1), :])
            m_new = jnp.maximum(lf + m_cur[hl], i_t)
            a = jnp.exp(lf + m_cur[hl] - m_new)
            w = jnp.exp(i_t - m_new)
            n_new = a * n_cur[hl] + w * kt_s[hs, :]
            v_h = vt_s[hs, :]

            def body(d, num, hl=hl, a=a, w=w, v_h=v_h, c_src=c_src):
                kd = kt_s[pl.ds(hl * HEAD_W + d, 1), :] * w
                qd = qt_s[pl.ds(hl * HEAD_W + d, 1), :]
                cd = a * c_src[hl, d] + kd * v_h
                c1_ref[hl, d] = cd
                return num + qd * cd

            num = lax.fori_loop(0, HEAD_W, body, jnp.zeros((HEAD_W, NB), f32), unroll=4)
            den = jnp.sum(qt_s[hs, :] * n_new, axis=0, keepdims=True)
            hh = num / jnp.maximum(jnp.abs(den), jnp.exp(-m_new))
            ms = jnp.mean(hh * hh, axis=0, keepdims=True)
            ht_s[hs, :] = hh * lax.rsqrt(ms + EPS)
            n_cur[hl] = n_new
            m_cur[hl] = m_new
        out_ref[rows, :] = (ht_s[...].T * nw_ref[...] * jax.nn.sigmoid(o_ref[rows, :])).astype(bf16)

    for hl in range(2):
        n1_ref[hl] = n_cur[hl]
        m1_ref[pl.ds(2 * p + hl, 1), :] = m_cur[hl]


def _mlstm_sample(ml, gates, gate_bias, norm_w, c0t, n0t, m0t, layer):
    n = ml.shape[0]
    npair = ML_HEADS // 2
    blk = lambda off: pl.BlockSpec((n, LANES), lambda p, off=off: (0, off + p))
    c_spec = pl.BlockSpec((None, 2, HEAD_W, HEAD_W, NB), lambda p: (layer, p, 0, 0, 0))
    n_spec = pl.BlockSpec((None, 2, HEAD_W, NB), lambda p: (layer, p, 0, 0))
    m_spec = pl.BlockSpec((None, ML_HEADS, NB), lambda p: (layer, 0, 0))
    return _sample_call(
        _mlstm_sample_body, "mlstm_sample", (npair,),
        [blk(0), blk(npair), blk(2 * npair), blk(3 * npair), _resident((n, GATE_W)),
         _layer_resident((1, GATE_W), layer), pl.BlockSpec((None, 1, LANES), lambda p: (layer, 0, p)),
         c_spec, n_spec, m_spec],
        [ml, ml, ml, ml, gates, gate_bias, norm_w, c0t, n0t, m0t],
        [pl.BlockSpec((n, LANES), lambda p: (0, p)), c_spec, n_spec, m_spec],
        [jax.ShapeDtypeStruct((n, ML_W), bf16), jax.ShapeDtypeStruct(c0t.shape, f32),
         jax.ShapeDtypeStruct(n0t.shape, f32), jax.ShapeDtypeStruct(m0t.shape, f32)],
        [pltpu.VMEM((LANES, NB), f32)] * 5, 3)


def _ssd_conv_sample_body(ssd_ref, halo_ref, cw_ref, cb_ref, out_ref, xp_s):
    nh = (SSD_CONV - 1) * NB
    n = TS * NB
    xp_s[0:nh, :] = halo_ref[...]
    xp_s[nh:nh + n, :] = ssd_ref[:, SSD_W:SSD_COLS]
    conv = cb_ref[...]
    for j in range(SSD_CONV):
        conv = conv + cw_ref[j:j + 1, :] * xp_s[j * NB:j * NB + n, :]
    out_ref[...] = _silu(conv)


def _ssd_conv_sample(ssd, halo, conv_w, conv_b, layer):
    n = ssd.shape[0]
    nh = (SSD_CONV - 1) * NB
    return pl.pallas_call(
        _ssd_conv_sample_body,
        grid=(1,),
        in_specs=[_resident((n, SSD_COLS)), _layer_resident((nh, SSD_CONV_DIM), layer),
                  _layer_resident((SSD_CONV, SSD_CONV_DIM), layer), _layer_resident((1, SSD_CONV_DIM), layer)],
        out_specs=pl.BlockSpec((n, SSD_CONV_DIM), lambda i: (0, 0)),
        out_shape=jax.ShapeDtypeStruct((n, SSD_CONV_DIM), f32),
        scratch_shapes=[pltpu.VMEM((nh + n, SSD_CONV_DIM), f32)],
        compiler_params=_cparams("arbitrary"),
        name="ssd_conv_sample",
    )(ssd, halo, conv_w, conv_b)


def _ssd_sample_body(z_ref, x_ref, b0_ref, b1_ref, c0_ref, c1_ref, g_ref, gb_ref, hp_ref, nw_ref, h0_ref,
                     out_ref, h1_ref, xt_s, bt_s, ct_s, gt_s, yt_s):
    p = pl.program_id(0)
    b_refs = (b0_ref, b1_ref)
    c_refs = (c0_ref, c1_ref)

    for t in range(TS):
        h_src = h0_ref if t == 0 else h1_ref
        rows = slice(t * NB, (t + 1) * NB)
        xt_s[...] = x_ref[rows, :].T
        gt_s[...] = (g_ref[rows, :] + gb_ref[...]).T
        zt = z_ref[rows, :].T
        for hl in range(2):
            bt_s[hl] = b_refs[hl][rows, :].T
            ct_s[hl] = c_refs[hl][rows, :].T
        for hl in range(2):
            h = 2 * p + hl
            hs = slice(hl * HEAD_W, (hl + 1) * HEAD_W)
            dt = jax.nn.softplus(gt_s[pl.ds(DT_LANE + h, 1), :])
            da = jnp.exp(dt * (-jnp.exp(hp_ref[0, hl:hl + 1, :])))

            def body(pp, carry, hl=hl, dt=dt, da=da, h_src=h_src):
                xrow = xt_s[pl.ds(hl * HEAD_W + pp, 1), :] * dt
                hn = da * h_src[hl, pp] + xrow * bt_s[hl]
                h1_ref[hl, pp] = hn
                yt_s[pl.ds(hl * HEAD_W + pp, 1), :] = jnp.sum(ct_s[hl] * hn, axis=0, keepdims=True)
                return carry

            lax.fori_loop(0, HEAD_W, body, 0, unroll=8)
            yy = (yt_s[hs, :] + hp_ref[1, hl:hl + 1, :] * xt_s[hs, :]) * _silu(zt[hs, :])
            ms = jnp.mean(yy * yy, axis=0, keepdims=True)
            yt_s[hs, :] = yy * lax.rsqrt(ms + EPS)
        out_ref[rows, :] = (yt_s[...].T * nw_ref[...]).astype(bf16)


def _ssd_sample(ssd, xc, gates, gate_bias, head_params, norm_w, h0t, layer):
    n = ssd.shape[0]
    npair = SSD_HEADS // 2
    hpg = SSD_HEADS // 2
    nxb = SSD_W // LANES
    blk = lambda f: pl.BlockSpec((n, LANES), f)
    h_spec = pl.BlockSpec((None, 2, HEAD_W, SSD_DSTATE, NB), lambda p: (layer, p, 0, 0, 0))
    return _sample_call(
        _ssd_sample_body, "ssd_sample", (npair,),
        [blk(lambda p: (0, p)), blk(lambda p: (0, p)),
         blk(lambda p: (0, nxb + (2 * p) // hpg)), blk(lambda p: (0, nxb + (2 * p + 1) // hpg)),
         blk(lambda p: (0, nxb + 2 + (2 * p) // hpg)), blk(lambda p: (0, nxb + 2 + (2 * p + 1) // hpg)),
         _resident((n, GATE_W)), _layer_resident((1, GATE_W), layer),
         pl.BlockSpec((None, 2, None, 2, LANES), lambda p: (layer, 0, p, 0, 0)),
         pl.BlockSpec((None, 1, LANES), lambda p: (layer, 0, p)), h_spec],
        [ssd, xc, xc, xc, xc, xc, gates, gate_bias, head_params, norm_w, h0t],
        [pl.BlockSpec((n, LANES), lambda p: (0, p)), h_spec],
        [jax.ShapeDtypeStruct((n, SSD_W), bf16), jax.ShapeDtypeStruct(h0t.shape, f32)],
        [pltpu.VMEM((LANES, NB), f32), pltpu.VMEM((2, SSD_DSTATE, NB), f32), pltpu.VMEM((2, SSD_DSTATE, NB), f32),
         pltpu.VMEM((LANES, NB), f32), pltpu.VMEM((LANES, NB), f32)], 1)


def _hgrn_sample_body(q_ref, f_ref, i_ref, gate_ref, lbp_ref, nw_ref, s0_ref, out_ref, s1_ref,
                      qt_s, kt_s, ft_s, vt_s, ot_s, *, layer):
    lb = _hg_lower_bound(lbp_ref[...], layer)

    for t in range(TS):
        s_src = s0_ref if t == 0 else s1_ref
        rows = slice(t * NB, (t + 1) * NB)
        ff = f_ref[rows, :]
        qt_s[...] = (q_ref[rows, :] * (HEAD_W ** -0.5)).T
        kt_s[...] = ((1.0 - lb) * jax.nn.sigmoid(-ff)).T
        ft_s[...] = (lb + (1.0 - lb) * jax.nn.sigmoid(ff)).T
        vt_s[...] = i_ref[rows, :].T
        for hl in range(2):
            hs = slice(hl * HEAD_W, (hl + 1) * HEAD_W)
            v_h = vt_s[hs, :]

            def body(k, o, hl=hl, v_h=v_h, s_src=s_src):
                r = pl.ds(hl * HEAD_W + k, 1)
                sk = ft_s[r, :] * s_src[hl, k] + kt_s[r, :] * v_h
                s1_ref[hl, k] = sk
                return o + qt_s[r, :] * sk

            o = lax.fori_loop(0, HEAD_W, body, jnp.zeros((HEAD_W, NB), f32), unroll=4)
            ms = jnp.mean(o * o, axis=0, keepdims=True)
            ot_s[hs, :] = o * lax.rsqrt(ms + EPS)
        out_ref[rows, :] = (ot_s[...].T * nw_ref[...] * _silu(gate_ref[rows, :])).astype(bf16)


def _hgrn_sample(hg, lbp, norm_w, s0t, layer):
    n = hg.shape[0]
    npair = HG_HEADS // 2
    blk = lambda off: pl.BlockSpec((n, LANES), lambda p, off=off: (0, off + p))
    s_spec = pl.BlockSpec((None, 2, HEAD_W, HEAD_W, NB), lambda p: (layer, p, 0, 0, 0))
    return _sample_call(
        functools.partial(_hgrn_sample_body, layer=layer), "hgrn_sample", (npair,),
        [blk(0), blk(npair), blk(2 * npair), blk(3 * npair),
         pl.BlockSpec((DEPTH, LANES), lambda p: (0, p)),
         pl.BlockSpec((None, 1, LANES), lambda p: (layer, 0, p)), s_spec],
        [hg, hg, hg, hg, lbp, norm_w, s0t],
        [pl.BlockSpec((n, LANES), lambda p: (0, p)), s_spec],
        [jax.ShapeDtypeStruct((n, HG_W), bf16), jax.ShapeDtypeStruct(s0t.shape, f32)],
        [pltpu.VMEM((LANES, NB), f32)] * 5, 1)


def _ffn_body(x_ref, mml_ref, mssd_ref, mhg_ref, wout_ref, n2_ref, wug_ref, wuv_ref, cwg_ref, cwv_ref,
              cbg_ref, cbv_ref, wdn_ref, f0g_ref, f0v_ref, fn_ref, y_ref, f1g_ref, f1v_ref,
              x1_s, h2_s, acc_s, act_s, ubuf_s, *, tm, shift, jb, nff, final):
    t = pl.program_id(1)
    f = pl.program_id(2)
    halo = max(HALO, (FFN_CONV - 1) * shift)
    keep = (FFN_CONV - 1) * shift

    def when(cond, fn):
        if nff == 1:
            fn()
        else:
            pl.when(cond)(fn)

    @pl.when(t == 0)
    def _():
        f1g_ref[0] = f0g_ref[0]
        f1v_ref[0] = f0v_ref[0]

    def first():
        x = x_ref[...]
        x1 = (x + _dot(mml_ref[...], wout_ref[0:ML_W, :]) + _dot(mssd_ref[...], wout_ref[ML_W:ML_W + SSD_W, :])
              + _dot(mhg_ref[...], wout_ref[ML_W + SSD_W:D_MODEL, :]))
        x1_s[...] = x1
        h2_s[...] = (x1 * lax.rsqrt(jnp.mean(x1 * x1, axis=-1, keepdims=True) + EPS) * n2_ref[...]).astype(bf16)

    when(f == 0, first)
    h2 = h2_s[...]

    def conv(wu_ref, cw_ref, cb_ref, f1_ref, cols, ubuf):
        u = _dot(h2, wu_ref[:, cols])
        ubuf[halo - keep:halo, :] = f1_ref[0, :, cols]
        ubuf[halo:halo + tm, :] = u
        yv = (cb_ref[:, cols] + cw_ref[2:3, cols] * u
              + cw_ref[1:2, cols] * ubuf[halo - shift:halo - shift + tm, :]
              + cw_ref[0:1, cols] * ubuf[halo - 2 * shift:halo - 2 * shift + tm, :])
        f1_ref[0, :, cols] = ubuf[halo + tm - keep:halo + tm, :]
        return yv

    nbuf = ubuf_s.shape[0]
    for j in range(jb):
        cols = slice(j * FF_BLOCK, (j + 1) * FF_BLOCK)
        gq = conv(wug_ref, cwg_ref, cbg_ref, f1g_ref, cols, ubuf_s.at[(2 * j) % nbuf])
        vq = conv(wuv_ref, cwv_ref, cbv_ref, f1v_ref, cols, ubuf_s.at[(2 * j + 1) % nbuf])
        act_s[:, cols] = (_silu(gq) * vq).astype(bf16)
    contrib = _dot(act_s[...], wdn_ref[...])

    def finish(down):
        x2 = x1_s[...] + down
        if final:
            x2 = x2 * lax.rsqrt(jnp.mean(x2 * x2, axis=-1, keepdims=True) + EPS) * fn_ref[...]
        y_ref[...] = x2

    if nff == 1:
        finish(contrib)
    else:
        @pl.when(f == 0)
        def _():
            acc_s[...] = contrib

        @pl.when(f != 0)
        def _():
            acc_s[...] = acc_s[...] + contrib

        pl.when(f == nff - 1)(lambda: finish(acc_s[...]))


def _ffn(x, mml, mssd, mhg, w_out, norm2, w_up, conv_w, conv_b, w_down, f0, final_w, *, layer, groups, tm, shift, jb,
         final, f0_base=0):
    n = x.shape[0]
    nt = n // (groups * tm)
    fw = jb * FF_BLOCK
    nff = D_FF // fw
    assert nt == 1 or nff == 1
    halo = max(HALO, (FFN_CONV - 1) * shift)
    keep = (FFN_CONV - 1) * shift
    rows = lambda w: pl.BlockSpec((tm, w), lambda g, t, f: (g * nt + t, 0))
    single = nff == 1
    wspec = lambda shape, imap: pl.BlockSpec(shape, imap, pipeline_mode=pl.Buffered(1)) if single else pl.BlockSpec(shape, imap)
    f1_spec = pl.BlockSpec((1, keep, fw), lambda g, t, f: (g, 0, f))
    in_specs = [rows(D_MODEL), rows(ML_W), rows(SSD_W), rows(HG_W),
                _layer_resident((D_MODEL, D_MODEL), layer), _layer_resident((1, D_MODEL), layer),
                wspec((None, D_MODEL, fw), lambda g, t, f: (layer, 0, f)),
                wspec((None, D_MODEL, fw), lambda g, t, f: (layer, 0, nff + f)),
                wspec((None, FFN_CONV, fw), lambda g, t, f: (layer, 0, f)),
                wspec((None, FFN_CONV, fw), lambda g, t, f: (layer, 0, nff + f)),
                wspec((None, 1, fw), lambda g, t, f: (layer, 0, f)),
                wspec((None, 1, fw), lambda g, t, f: (layer, 0, nff + f)),
                wspec((None, fw, D_MODEL), lambda g, t, f: (layer, f, 0)),
                pl.BlockSpec((1, keep, fw), lambda g, t, f: (f0_base + g, 0, f)),
                pl.BlockSpec((1, keep, fw), lambda g, t, f: (f0_base + g, 0, nff + f)),
                _resident((1, D_MODEL))]
    operands = [x, mml, mssd, mhg, w_out, norm2, w_up, w_up, conv_w, conv_w, conv_b, conv_b, w_down, f0, f0, final_w]
    acc_rows = tm if nff > 1 else 8
    return pl.pallas_call(
        functools.partial(_ffn_body, tm=tm, shift=shift, jb=jb, nff=nff, final=final),
        grid=(groups, nt, nff),
        in_specs=in_specs,
        out_specs=[rows(D_MODEL), f1_spec, f1_spec],
        out_shape=[jax.ShapeDtypeStruct((n, D_MODEL), f32),
                   jax.ShapeDtypeStruct((groups, keep, D_FF), f32),
                   jax.ShapeDtypeStruct((groups, keep, D_FF), f32)],
        scratch_shapes=[pltpu.VMEM((tm, D_MODEL), f32), pltpu.VMEM((tm, D_MODEL), bf16),
                        pltpu.VMEM((acc_rows, D_MODEL), f32), pltpu.VMEM((tm, fw), bf16),
                        pltpu.VMEM((min(4, 2 * jb), halo + tm, FF_BLOCK), f32)],
        compiler_params=_cparams("arbitrary", "arbitrary", "arbitrary"),
        name="ffn",
    )(*operands)


def _gate_rows(*parts):
    r = jnp.concatenate([p.astype(f32) for p in parts], axis=1)
    return jnp.pad(r, ((0, 0), (0, GATE_W - r.shape[1])))[:, None, :]


def kernel(x_prompt, x_sample, state_mlstm_C, state_mlstm_n, state_mlstm_m, state_ssd, state_ssd_conv,
           state_hgrn, state_ffn_conv, norm1_w, w_in, ml_ig_b, ml_fg_b, ml_norm_w, ssd_conv_w, ssd_conv_b,
           ssd_dt_bias, ssd_A_log, ssd_D, ssd_norm_w, hg_lower_bounds, hg_norm_w, w_out, norm2_w, w_up,
           ffn_conv_w, ffn_conv_b, w_down, final_norm_w):
    bp, tp, _ = x_prompt.shape
    nb, ts, _ = x_sample.shape
    assert nb == NB and ts == TS and tp % ROW_TILE == 0
    depth = w_in.shape[0]
    assert depth == DEPTH
    xp = x_prompt.reshape(bp * tp, D_MODEL)
    xs = jnp.transpose(x_sample, (1, 0, 2)).reshape(ts * nb, D_MODEL)
    lbp = hg_lower_bounds.astype(f32)
    fin_w = final_norm_w[None, :]
    zero_f = jnp.zeros((bp, FFN_CONV - 1, 2 * D_FF), f32)

    w_cat = _prep_w_in(w_in)
    wo_b, wu_b, wd_b = w_out.astype(bf16), w_up.astype(bf16), w_down.astype(bf16)
    n1, n2 = norm1_w[:, None, :], norm2_w[:, None, :]
    gate_bias = _gate_rows(ml_ig_b, ml_fg_b, ssd_dt_bias)
    alog_row = _gate_rows(jnp.zeros((depth, DT_LANE), f32), ssd_A_log)
    d_row = jnp.repeat(ssd_D.astype(f32), HEAD_W, axis=1)[:, None, :]
    mlw, ssw, hgw = ml_norm_w[:, None, :], ssd_norm_w[:, None, :], hg_norm_w[:, None, :]
    cw, cb = ssd_conv_w, ssd_conv_b[:, None, :]
    fcw, fcb = ffn_conv_w, ffn_conv_b[:, None, :]
    head_params = jnp.broadcast_to(
        jnp.stack([ssd_A_log, ssd_D], axis=1).astype(f32).reshape(depth, 2, SSD_HEADS // 2, 2, 1),
        (depth, 2, SSD_HEADS // 2, 2, LANES))

    c_t = jnp.transpose(state_mlstm_C.astype(f32), (0, 2, 3, 4, 1))
    n_t = jnp.transpose(state_mlstm_n.astype(f32), (0, 2, 3, 1))
    m_t = jnp.transpose(state_mlstm_m.astype(f32), (0, 2, 1))
    h_t = jnp.transpose(state_ssd.astype(f32), (0, 2, 3, 4, 1))
    s_t = jnp.transpose(state_hgrn.astype(f32), (0, 2, 3, 4, 1))
    conv_halo = jnp.transpose(state_ssd_conv.astype(f32), (0, 2, 1, 3)).reshape(depth, (SSD_CONV - 1) * nb, SSD_CONV_DIM)
    f_t = jnp.transpose(state_ffn_conv.astype(f32), (0, 2, 1, 3)).reshape(depth, (FFN_CONV - 1) * nb, 2 * D_FF)

    p_states, s_sconv, s_ffn = [], [], []
    for l in range(depth):
        last = l == depth - 1

        ml, ssd, hg, gates = _inproj(xp, n1, w_cat, l)
        gates3 = gates.reshape(bp, tp, GATE_W)
        ssd3 = ssd.reshape(bp, tp, SSD_COLS)
        hm, p_c, p_n, p_m, hs, p_h, ho, p_s = _mixers_prompt(
            ml.reshape(bp, tp, ML_COLS), ssd3, hg.reshape(bp, tp, HG_COLS), gates3, gate_bias, alog_row, cw, cb, d_row,
            mlw, ssw, lbp, hgw, l)
        xp, p_fg, p_fv = _ffn(xp, hm.reshape(bp * tp, ML_W), hs.reshape(bp * tp, SSD_W), ho.reshape(bp * tp, HG_W),
                              wo_b, n2, wu_b, fcw, fcb, wd_b, zero_f, fin_w, layer=l,
                              groups=bp, tm=ROW_TILE, shift=1, jb=D_FF // FF_BLOCK, final=last)
        p_states.append((p_c, p_n, p_m[:, 0:ML_HEADS, 0],
                         p_h.reshape(bp, SSD_HEADS, HEAD_W, SSD_DSTATE),
                         ssd3[:, tp - (SSD_CONV - 1):, SSD_W:], p_s,
                         jnp.concatenate([p_fg, p_fv], axis=-1)))

        ml, ssd, hg, gates = _inproj(xs, n1, w_cat, l)
        hm, c_t, n_t, m_t = _mlstm_sample(ml, gates, gate_bias, mlw, c_t, n_t, m_t, l)
        xc = _ssd_conv_sample(ssd, conv_halo, cw, cb, l)
        hs, h_t = _ssd_sample(ssd, xc, gates, gate_bias, head_params, ssw, h_t, l)
        ho, s_t = _hgrn_sample(hg, lbp, hgw, s_t, l)
        xs, s_fg, s_fv = _ffn(xs, hm, hs, ho, wo_b, n2, wu_b, fcw, fcb, wd_b, f_t, fin_w, layer=l,
                              groups=1, tm=ts * nb, shift=nb, jb=1, final=last, f0_base=l)
        s_ffn.append(jnp.concatenate([s_fg, s_fv], axis=-1).reshape(FFN_CONV - 1, nb, 2 * D_FF))
        raw_xbc = ssd[:, SSD_W:].reshape(ts, nb, SSD_CONV_DIM)
        s_sconv.append(jnp.transpose(raw_xbc[ts - (SSD_CONV - 1):], (1, 0, 2)))

    y_prompt = xp.reshape(bp, tp, D_MODEL)
    y_sample = jnp.transpose(xs.reshape(ts, nb, D_MODEL), (1, 0, 2))
    p_out = [jnp.stack([st[i] for st in p_states]) for i in range(7)]
    s_out = [jnp.transpose(c_t, (0, 4, 1, 2, 3)), jnp.transpose(n_t, (0, 3, 1, 2)), jnp.transpose(m_t, (0, 2, 1)),
             jnp.transpose(h_t, (0, 4, 1, 2, 3)), jnp.stack(s_sconv),
             jnp.transpose(s_t, (0, 4, 1, 2, 3)), jnp.transpose(jnp.stack(s_ffn), (0, 2, 1, 3))]
    return (y_prompt, y_sample, *p_out, *s_out)
```

```python
import functools

import jax
import jax.numpy as jnp
from jax import lax
from jax.experimental import pallas as pl
from jax.experimental.pallas import tpu as pltpu

f32 = jnp.float32
bf16 = jnp.bfloat16

D_MODEL = 1024
DEPTH = 2
ML_HEADS = 6
HEAD_W = 64
ML_W = 384
SSD_HEADS = 6
SSD_W = 384
SSD_DSTATE = 128
SSD_CONV = 4
SSD_CONV_DIM = 896
HG_HEADS = 4
HG_W = 256
D_FF = 2816
FFN_CONV = 3
EPS = 1e-6
PROMPT_CHUNK = 128
HG_CHUNK = 64
LANES = 128
FF_BLOCK = 256
ROW_TILE = 512
SEQS_PER_STEP = 4
GATE_W = 128
GATE_ROWS = 16
W_IN_COLS = 4 * ML_W + (SSD_W + SSD_CONV_DIM) + 4 * HG_W + GATE_W
VMEM_LIMIT_BYTES = 56 * 1024 * 1024

NEG_INF = float("-inf")


def _cparams(*sem):
    return pltpu.CompilerParams(dimension_semantics=sem, vmem_limit_bytes=VMEM_LIMIT_BYTES)


def _resident(shape):
    nd = len(shape)
    return pl.BlockSpec(shape, lambda *_: (0,) * nd, pipeline_mode=pl.Buffered(1))


def _layer_resident(shape, layer):
    nd = len(shape)
    return pl.BlockSpec((None,) + tuple(shape), lambda *_: (layer,) + (0,) * nd, pipeline_mode=pl.Buffered(1))


def _dot(a, b):
    return jnp.dot(a, b, preferred_element_type=f32)


def _dot_nt(a, b):
    return lax.dot_general(a, b, (((1,), (1,)), ((), ())), preferred_element_type=f32)


def _split_bf16(x, parts):
    out = []
    r = x
    for _ in range(parts):
        h = r.astype(bf16)
        out.append(h)
        r = r - h.astype(f32)
    return out


def _cumsum_rows(x, tri_b):
    acc = None
    for part in _split_bf16(x, 3):
        d = _dot(tri_b, part)
        acc = d if acc is None else acc + d
    return acc


def _group_mean(x, bd_b):
    return _dot(x.astype(bf16), bd_b) * (1.0 / HEAD_W)


def _iota2(shape, axis):
    return lax.broadcasted_iota(jnp.int32, shape, axis)


def _ones_where(mask):
    return jnp.where(mask, 1.0, 0.0).astype(bf16)


def _same_head(a, b):
    return jnp.right_shift(a, 6) == jnp.right_shift(b, 6)


def _silu(x):
    return x * jax.nn.sigmoid(x)


ML_COLS = 4 * ML_W
SSD_COLS = SSD_W + SSD_CONV_DIM
HG_COLS = 4 * HG_W


def _inproj_body(x_ref, nw_ref, w_ref, ml_ref, ssd_ref, hg_ref, g_ref):
    x = x_ref[...]
    h = (x * lax.rsqrt(jnp.mean(x * x, axis=-1, keepdims=True) + EPS) * nw_ref[...]).astype(bf16)
    o0, o1, o2 = ML_COLS, ML_COLS + SSD_COLS, ML_COLS + SSD_COLS + HG_COLS
    ml_ref[...] = _dot(h, w_ref[:, 0:o0])
    ssd_ref[...] = _dot(h, w_ref[:, o0:o1])
    hg_ref[...] = _dot(h, w_ref[:, o1:o2])
    g_ref[...] = _dot(h, w_ref[:, o2:W_IN_COLS])


def _inproj_conv_body(x_ref, nw_ref, w_ref, cw_ref, cb_ref, ml_ref, ssd_ref, hg_ref, g_ref, tail_ref, xbuf_s, *,
                      tiles_per_seq):
    i = pl.program_id(0)
    tm = x_ref.shape[0]

    @pl.when(lax.rem(i, tiles_per_seq) == 0)
    def _():
        xbuf_s[0:HALO, :] = jnp.zeros((HALO, SSD_CONV_DIM), f32)

    x = x_ref[...]
    h = (x * lax.rsqrt(jnp.mean(x * x, axis=-1, keepdims=True) + EPS) * nw_ref[...]).astype(bf16)
    o0, o1, o2 = ML_COLS, ML_COLS + SSD_COLS, ML_COLS + SSD_COLS + HG_COLS
    xbuf_s[HALO:HALO + tm, :] = _dot(h, w_ref[:, o0 + SSD_W:o1])

    conv = cb_ref[...]
    for j in range(SSD_CONV):
        o = HALO - (SSD_CONV - 1) + j
        conv = conv + cw_ref[j:j + 1, :] * xbuf_s[o:o + tm, :]
    ssd_ref[:, SSD_W:SSD_COLS] = _silu(conv)
    tail = xbuf_s[tm:tm + HALO, :]
    tail_ref[0] = tail
    xbuf_s[0:HALO, :] = tail
    ml_ref[...] = _dot(h, w_ref[:, 0:o0])
    ssd_ref[:, 0:SSD_W] = _dot(h, w_ref[:, o0:o0 + SSD_W])
    hg_ref[...] = _dot(h, w_ref[:, o1:o2])
    g_ref[...] = _dot(h, w_ref[:, o2:W_IN_COLS])


def _inproj(x, norm_w, w_cat, layer, conv=None):
    n = x.shape[0]
    tm = ROW_TILE
    row = lambda w: pl.BlockSpec((tm, w), lambda i: (i, 0))
    in_specs = [row(D_MODEL), _layer_resident((1, D_MODEL), layer), _layer_resident((D_MODEL, W_IN_COLS), layer)]
    out_specs = [row(ML_COLS), row(SSD_COLS), row(HG_COLS), row(GATE_W)]
    out_shape = [jax.ShapeDtypeStruct((n, w), f32) for w in (ML_COLS, SSD_COLS, HG_COLS, GATE_W)]
    if conv is None:
        return pl.pallas_call(
            _inproj_body, grid=(n // tm,), in_specs=in_specs, out_specs=out_specs, out_shape=out_shape,
            compiler_params=_cparams("arbitrary"), name="inproj",
        )(x, norm_w, w_cat)
    conv_w, conv_b, rows_per_seq = conv
    tps = rows_per_seq // tm
    assert rows_per_seq % tm == 0
    return pl.pallas_call(
        functools.partial(_inproj_conv_body, tiles_per_seq=tps),
        grid=(n // tm,),
        in_specs=in_specs + [_layer_resident((SSD_CONV, SSD_CONV_DIM), layer), _layer_resident((1, SSD_CONV_DIM), layer)],
        out_specs=out_specs + [pl.BlockSpec((1, HALO, SSD_CONV_DIM), lambda i: (i // tps, 0, 0))],
        out_shape=out_shape + [jax.ShapeDtypeStruct((n // rows_per_seq, HALO, SSD_CONV_DIM), f32)],
        scratch_shapes=[pltpu.VMEM((HALO + tm, SSD_CONV_DIM), f32)],
        compiler_params=_cparams("arbitrary"),
        name="inproj_conv",
    )(x, norm_w, w_cat, conv_w, conv_b)


W_IN_GATE_IF = ML_COLS
W_IN_SSD = W_IN_GATE_IF + 2 * ML_HEADS
W_IN_DT = W_IN_SSD + SSD_COLS
W_IN_HG = W_IN_DT + SSD_HEADS
W_IN_DIM = W_IN_HG + HG_COLS
W_PREP_ROWS = 256


def _prep_w_in_body(w_ref, out_ref):
    lane = _iota2((W_PREP_ROWS, LANES), 1)
    tile = lambda k: w_ref[:, k * LANES:(k + 1) * LANES]
    out_ref[:, 0:ML_COLS] = w_ref[:, 0:ML_COLS].astype(bf16)

    def shifted(dst0, src0, ntiles):
        t0, sh = src0 // LANES, src0 % LANES
        rolled = [pltpu.roll(tile(t0 + k), LANES - sh, axis=1) for k in range(ntiles + 1)]
        for k in range(ntiles):
            out_ref[:, dst0 + k * LANES:dst0 + (k + 1) * LANES] = jnp.where(
                lane < LANES - sh, rolled[k], rolled[k + 1]).astype(bf16)

    shifted(ML_COLS, W_IN_SSD, SSD_COLS // LANES)
    shifted(ML_COLS + SSD_COLS, W_IN_HG, HG_COLS // LANES)
    assert W_IN_GATE_IF % LANES == 0 and W_IN_DT % LANES == DT_LANE
    gates = jnp.where(lane < DT_LANE, tile(W_IN_GATE_IF // LANES),
                      jnp.where(lane < DT_LANE + SSD_HEADS, tile(W_IN_DT // LANES), 0.0))
    out_ref[:, ML_COLS + SSD_COLS + HG_COLS:W_IN_COLS] = gates.astype(bf16)


def _prep_w_in(w_in):
    depth, d, n = w_in.shape
    assert n == W_IN_DIM and d % W_PREP_ROWS == 0
    return pl.pallas_call(
        _prep_w_in_body,
        grid=(depth, d // W_PREP_ROWS),
        in_specs=[pl.BlockSpec((None, W_PREP_ROWS, W_IN_COLS), lambda l, i: (l, i, 0))],
        out_specs=pl.BlockSpec((None, W_PREP_ROWS, W_IN_COLS), lambda l, i: (l, i, 0)),
        out_shape=jax.ShapeDtypeStruct((depth, d, W_IN_COLS), bf16),
        compiler_params=_cparams("arbitrary", "arbitrary"),
        name="prep_w_in",
    )(w_in)


def _one_seq(ref, bi):
    return ref.at[pl.ds(bi, 1)]


def _interleave(chunks):
    live = list(chunks)
    while live:
        nxt = []
        for g in live:
            try:
                next(g)
                nxt.append(g)
            except StopIteration:
                pass
        live = nxt


def _mlstm_prompt_chunk(ml_ref, g_ref, gb_ref, nw_ref, out_ref, ct_s, nr_s, m_s):
    L = PROMPT_CHUNK
    row = _iota2((L, L), 0)
    col = _iota2((L, L), 1)
    causal = row <= col
    tri_b = _ones_where(causal)
    top = row < HEAD_W
    lo_half = col < HEAD_W
    bd_mask = _same_head(row, col)
    sub8 = _iota2((8, L), 0)
    lo8 = _iota2((8, L), 1) < HEAD_W
    n_mask = jnp.where(sub8 == 0, jnp.where(lo8, 1.0, 0.0), jnp.where(sub8 == 1, jnp.where(lo8, 0.0, 1.0), 0.0))

    g_t = (g_ref[0] + gb_ref[...]).T
    b_t = None
    parts = _split_bf16(jax.nn.log_sigmoid(g_t[0:GATE_ROWS, :]), 3)
    yield
    for part in parts:
        d = _dot(part, tri_b)
        b_t = d if b_t is None else b_t + d
        yield
    z_rows = b_t[ML_HEADS:ML_HEADS + 8, :] - g_t[0:8, :]
    z_nat = jnp.concatenate([z_rows, jnp.zeros((L - 8, L), f32)], axis=0).T
    m_old = m_s[...]
    m_next = m_old
    yield

    for p in range(ML_HEADS // 2):
        sl = slice(p * LANES, (p + 1) * LANES)
        q2 = ml_ref[0, :, sl]
        k2 = ml_ref[0, :, ML_W + p * LANES:ML_W + (p + 1) * LANES] * (HEAD_W ** -0.5)
        v2 = ml_ref[0, :, 2 * ML_W + p * LANES:2 * ML_W + (p + 1) * LANES]
        o2 = ml_ref[0, :, 3 * ML_W + p * LANES:3 * ML_W + (p + 1) * LANES]
        q_t = q2.T.astype(bf16)
        v_t = v2.T
        kb = k2.astype(bf16)
        k_sel = (jnp.where(lo_half, k2, 0.0).astype(bf16), jnp.where(lo_half, 0.0, k2).astype(bf16))
        sw, den, a_int, e_m, a_st, w_s = [], [], [], [], [], []
        yield
        for j in range(2):
            h = 2 * p + j
            b_row = b_t[ML_HEADS + h:ML_HEADS + h + 1, :]
            i_row = g_t[h:h + 1, :]
            m_prev = m_old[h:h + 1, :]
            dm = jnp.where(causal, b_row - z_nat[:, h:h + 1], NEG_INF)
            inter = b_row + m_prev
            raw = _dot(k_sel[j], q_t)
            yield
            m_t = jnp.maximum(inter, jnp.max(dm, axis=0, keepdims=True))
            yield
            s_w = raw * jnp.exp(dm - m_t)
            den.append(jnp.sum(s_w, axis=0, keepdims=True))
            sw.append(s_w.astype(bf16))
            yield
            a_int.append(jnp.exp(inter - m_t))
            e_m.append(jnp.exp(-m_t))
            m_new = jnp.broadcast_to(m_t[:, L - 1:L], (1, L))
            b_last = jnp.broadcast_to(b_row[:, L - 1:L], (1, L))
            a_st.append(jnp.exp(b_last + m_prev - m_new))
            w_s.append(jnp.exp(b_last - b_row + i_row - m_new))
            m_next = jnp.where(sub8 == h, m_new, m_next)
            yield
        ct = ct_s[p]
        nr = nr_s[p]
        qn = _dot(nr.astype(bf16), q_t)
        num0 = _dot(jnp.where(top, v_t, 0.0).astype(bf16), sw[0])
        yield
        num1 = _dot(jnp.where(top, 0.0, v_t).astype(bf16), sw[1])
        qc = _dot(ct.astype(bf16), q_t)
        yield
        num = num0 + num1 + jnp.where(top, a_int[0], a_int[1]) * qc
        den0 = den[0] + a_int[0] * qn[0:1, :]
        den1 = den[1] + a_int[1] * qn[1:2, :]
        hh = num / jnp.where(top, jnp.maximum(jnp.abs(den0), e_m[0]), jnp.maximum(jnp.abs(den1), e_m[1]))
        yield
        sq = hh * hh
        r0 = lax.rsqrt(jnp.sum(sq[0:HEAD_W, :], axis=0, keepdims=True) * (1.0 / HEAD_W) + EPS)
        r1 = lax.rsqrt(jnp.sum(sq[HEAD_W:LANES, :], axis=0, keepdims=True) * (1.0 / HEAD_W) + EPS)
        y = (hh * jnp.where(top, r0, r1)).T * nw_ref[:, sl] * jax.nn.sigmoid(o2)
        out_ref[0, :, sl] = y.astype(bf16)
        yield
        vw = (v_t * jnp.where(top, w_s[0], w_s[1])).astype(bf16)
        ct_s[p] = jnp.where(top, a_st[0], a_st[1]) * ct + jnp.where(bd_mask, _dot(vw, kb), 0.0)
        w8 = jnp.where(sub8 == 0, w_s[0], jnp.where(sub8 == 1, w_s[1], 0.0)).astype(bf16)
        nr_s[p] = jnp.where(sub8 == 0, a_st[0], a_st[1]) * nr + n_mask * _dot(w8, kb)
        yield
    m_s[...] = m_next


DT_LANE = 2 * ML_HEADS
SSD_GATE_ROW0 = DT_LANE // 8 * 8
SSD_HEAD_ROW = DT_LANE - SSD_GATE_ROW0
HALO = 8


def _ssd_prompt_chunk(ssd_ref, g_ref, gb_ref, al_ref, d_ref, nw_ref, out_ref, h_s):
    L = PROMPT_CHUNK
    xc = ssd_ref[0, :, SSD_W:SSD_COLS]
    nb = SSD_W
    b_b = [xc[:, nb + gi * LANES:nb + (gi + 1) * LANES].astype(bf16) for gi in range(2)]
    c_b = [xc[:, nb + (2 + gi) * LANES:nb + (3 + gi) * LANES].astype(bf16) for gi in range(2)]

    row = _iota2((L, L), 0)
    col = _iota2((L, L), 1)
    tril = col <= row
    lo_half = col < HEAD_W
    triu_b = _ones_where(row <= col)
    bd_b = _ones_where(_same_head(row, col))
    rowc = _iota2((L, 1), 0)

    g_t = (g_ref[0] + gb_ref[...]).T
    dt_t = jax.nn.softplus(g_t[SSD_GATE_ROW0:SSD_GATE_ROW0 + GATE_ROWS, :])
    yield
    cum_t = None
    for part in _split_bf16(dt_t * (-jnp.exp(al_ref[...])), 3):
        d = _dot(part, triu_b)
        cum_t = d if cum_t is None else cum_t + d
    yield
    nat = jnp.concatenate([cum_t, dt_t, jnp.zeros((L - 2 * GATE_ROWS, L), f32)], axis=0).T
    cb_g = [_dot_nt(c_b[gi], b_b[gi]) for gi in range(2)]
    yield

    for p in range(SSD_HEADS // 2):
        sl = slice(p * LANES, (p + 1) * LANES)
        x2 = xc[:, sl]
        z2 = ssd_ref[0, :, sl]
        hp = h_s[p]
        hb = hp.astype(bf16)
        grp = [(2 * p + j) // (SSD_HEADS // 2) for j in range(2)]
        ys = None
        ecum, w_s, a_last, ch = [], [], [], []
        for j in range(2):
            r = SSD_HEAD_ROW + 2 * p + j
            cum_col = nat[:, r:r + 1]
            cum_row = cum_t[r:r + 1, :]
            cum_last = cum_col[L - 1:L, :]
            lm = jnp.exp(jnp.where(tril, cum_col - cum_row, NEG_INF))
            m = (cb_g[grp[j]] * lm * dt_t[r:r + 1, :]).astype(bf16)
            yield
            xm = jnp.where(lo_half, x2, 0.0) if j == 0 else jnp.where(lo_half, 0.0, x2)
            d = _dot(m, xm.astype(bf16))
            ys = d if ys is None else ys + d
            ecum.append(jnp.exp(cum_col))
            w_s.append(jnp.exp(cum_last - cum_col) * nat[:, GATE_ROWS + r:GATE_ROWS + r + 1])
            a_last.append(jnp.exp(cum_last))
            if j == 0 or grp[1] != grp[0]:
                ch.append(_dot_nt(c_b[grp[j]], hb))
            else:
                ch.append(ch[0])
            yield
        ys = ys + jnp.where(lo_half, ecum[0] * ch[0], ecum[1] * ch[1])
        xw_t = (x2 * jnp.where(lo_half, w_s[0], w_s[1])).T.astype(bf16)
        yield
        if grp[0] == grp[1]:
            upd = _dot(xw_t, b_b[grp[0]])
        else:
            upd = jnp.where(row < HEAD_W, _dot(xw_t, b_b[grp[0]]), _dot(xw_t, b_b[grp[1]]))
        h_s[p] = jnp.where(rowc < HEAD_W, a_last[0], a_last[1]) * hp + upd
        yy = (ys + d_ref[:, sl] * x2) * _silu(z2)
        yield
        ms = _group_mean(yy * yy, bd_b)
        yield
        out_ref[0, :, sl] = (yy * lax.rsqrt(ms + EPS) * nw_ref[:, sl]).astype(bf16)


def _hg_lower_bound(lbp, layer):
    mx = jnp.max(lbp, axis=0, keepdims=True)
    e = jnp.exp(lbp - mx)
    den = jnp.sum(e, axis=0, keepdims=True)
    lb = jnp.zeros_like(den)
    for j in range(1, layer + 1):
        lb = lb + e[j:j + 1, :] / den
    return lb


def _hgrn_prompt_chunk(hg_ref, lbp_ref, nw_ref, out_ref, sbd_s, p_s, r_s, o_s, g_s, k_s, v_s, q_s, kt_s, a_s, *, layer):
    L = HG_CHUNK
    nblk = L // 8
    lb = _hg_lower_bound(lbp_ref[...], layer)
    ff = hg_ref[0, :, HG_W:2 * HG_W]
    sig = jax.nn.sigmoid(ff)
    k_s[...] = (1.0 - lb) * jax.nn.sigmoid(-ff)
    v_s[...] = hg_ref[0, :, 2 * HG_W:3 * HG_W]
    q_s[...] = hg_ref[0, :, 0:HG_W] * (HEAD_W ** -0.5)
    row = _iota2((L, L), 0)
    col = _iota2((L, L), 1)
    tri_b = _ones_where(col <= row)
    yield
    g_s[...] = _cumsum_rows(jnp.log(lb + (1.0 - lb) * sig), tri_b)
    yield

    r2 = _iota2((2 * LANES, 2 * LANES), 0)
    c2 = _iota2((2 * LANES, 2 * LANES), 1)
    bd4_b = _ones_where(_same_head(r2, c2))
    r1 = _iota2((LANES, LANES), 0)
    c1 = _iota2((LANES, LANES), 1)
    bd_mask = _same_head(r1, c1)
    bd_b = _ones_where(bd_mask)

    a_s[0:8, :] = jnp.zeros((8, HG_W), f32)
    for i in range(1, nblk):
        blk = slice(8 * i, 8 * i + 8)
        n = 8 * i
        r_i = g_s[n - 1:n, :]
        kt = k_s[0:n, :] * jnp.exp(r_i - g_s[0:n, :])
        first_head = jnp.bitwise_and(_iota2((n, HG_W), 1), LANES - 1) < HEAD_W
        kt_s[i - 1, 0:n, :] = jnp.where(first_head, kt, 0.0)
        kt_s[i - 1, n:L, :] = jnp.zeros((L - n, HG_W), f32)
        kt_s[i - 1, L:L + n, :] = jnp.where(first_head, 0.0, kt)
        kt_s[i - 1, L + n:2 * L, :] = jnp.zeros((L - n, HG_W), f32)
        qt = q_s[blk, :] * jnp.exp(g_s[blk, :] - r_i)
        qt = jnp.concatenate([qt, jnp.zeros_like(qt)], axis=0).astype(bf16)
        for p in range(HG_HEADS // 2):
            sl = slice(p * LANES, (p + 1) * LANES)
            a_s[blk, sl] = _dot_nt(qt[:, sl], kt_s[i - 1, :, sl].astype(bf16))[0:8, :]
        yield

    sub = _iota2((8, HG_W), 0)
    for j in range(nblk):
        blk = slice(8 * j, 8 * j + 8)
        g_blk, k_blk, q_blk = g_s[blk, :], k_s[blk, :], q_s[blk, :]
        for r in range(8):
            s = 8 * j + r
            e = jnp.exp(jnp.where(sub >= r, g_blk - g_blk[r:r + 1, :], NEG_INF))
            p_s[8 * s:8 * s + 8, :] = q_blk * k_blk[r:r + 1, :] * e
        yield
    r_s[...] = _dot(p_s[...].astype(bf16), bd4_b)
    yield

    gg = g_s[...]
    vv = v_s[...]
    qd = q_s[...] * jnp.exp(gg)
    lo_half = _iota2((L, LANES), 1) < HEAD_W
    for p in range(HG_HEADS // 2):
        sl = slice(p * LANES, (p + 1) * LANES)
        v_cat = jnp.concatenate([jnp.where(lo_half, vv[:, sl], 0.0), jnp.where(lo_half, 0.0, vv[:, sl])], axis=0)
        o_s[:, sl] = (_dot(qd[:, sl].astype(bf16), sbd_s[p].astype(bf16))
                      + _dot(a_s[:, sl].astype(bf16), v_cat.astype(bf16)))
        yield
    for j in range(nblk):
        blk = slice(8 * j, 8 * j + 8)
        acc = o_s[blk, :]
        v_blk = v_s[blk, :]
        for r in range(8):
            s = 8 * j + r
            acc = acc + r_s[8 * s:8 * s + 8, :] * v_blk[r:r + 1, :]
        o_s[blk, :] = acc
        yield

    g_last = gg[L - 1:L, :]
    kd = k_s[...] * jnp.exp(g_last - gg)
    zpad = jnp.zeros((LANES - L, LANES), f32)
    og = o_s[...]
    gate = hg_ref[0, :, 3 * HG_W:4 * HG_W]
    for p in range(HG_HEADS // 2):
        sl = slice(p * LANES, (p + 1) * LANES)
        kd_t = jnp.concatenate([kd[:, sl], zpad], axis=0).T.astype(bf16)
        v_pad = jnp.concatenate([vv[:, sl], zpad], axis=0).astype(bf16)
        g_t = jnp.concatenate([gg[:, sl], zpad], axis=0).T
        dec = jnp.exp(g_t[:, L - 1:L])
        sbd_s[p] = dec * sbd_s[p] + jnp.where(bd_mask, _dot(kd_t, v_pad), 0.0)
        yield
        oo = og[:, sl]
        ms = _group_mean(oo * oo, bd_b)
        out_ref[0, :, sl] = (oo * lax.rsqrt(ms + EPS) * nw_ref[:, sl] * _silu(gate[:, sl])).astype(bf16)


def _mixers_prompt_body(ml_ref, ssd_ref, hg_ref, g_ref, gb_ref, al_ref, d_ref, mlnw_ref, ssnw_ref,
                        lbp_ref, hgnw_ref,
                        hm_ref, c1_ref, n1_ref, m1_ref, hs_ref, h1_ref, ho_ref, s1_ref,
                        ct_s, nr_s, m_s, h_s, sbd_s, p_s, r_s, o_s, gg_s, k_s, v_s, q_s, kt_s, a_s, *, layer):
    c = pl.program_id(1)
    seqs = range(ml_ref.shape[0])

    @pl.when(c == 0)
    def _():
        ct_s[...] = jnp.zeros(ct_s.shape, f32)
        nr_s[...] = jnp.zeros(nr_s.shape, f32)
        m_s[...] = jnp.zeros(m_s.shape, f32)
        h_s[...] = jnp.zeros(h_s.shape, f32)
        sbd_s[...] = jnp.zeros(sbd_s.shape, f32)

    def hgrn_chunks(bi):
        for half in range(PROMPT_CHUNK // HG_CHUNK):
            rows = pl.ds(half * HG_CHUNK, HG_CHUNK)
            yield from _hgrn_prompt_chunk(hg_ref.at[pl.ds(bi, 1), rows], lbp_ref, hgnw_ref, ho_ref.at[pl.ds(bi, 1), rows],
                                          sbd_s.at[bi], p_s.at[bi], r_s.at[bi], o_s.at[bi], gg_s.at[bi], k_s.at[bi],
                                          v_s.at[bi], q_s.at[bi], kt_s.at[bi], a_s.at[bi], layer=layer)

    chunks = []
    for bi in seqs:
        chunks.append(_ssd_prompt_chunk(_one_seq(ssd_ref, bi), _one_seq(g_ref, bi), gb_ref, al_ref, d_ref,
                                        ssnw_ref, _one_seq(hs_ref, bi), h_s.at[bi]))
        chunks.append(hgrn_chunks(bi))
        chunks.append(_mlstm_prompt_chunk(_one_seq(ml_ref, bi), _one_seq(g_ref, bi), gb_ref, mlnw_ref,
                                          _one_seq(hm_ref, bi), ct_s.at[bi], nr_s.at[bi], m_s.at[bi]))
    _interleave(chunks)

    @pl.when(c == pl.num_programs(1) - 1)
    def _():
        h1_ref[...] = h_s[...]
        for bi in seqs:
            for p in range(ML_HEADS // 2):
                c_pair = ct_s[bi, p].T
                c1_ref[bi, 2 * p] = c_pair[0:HEAD_W, 0:HEAD_W]
                c1_ref[bi, 2 * p + 1] = c_pair[HEAD_W:LANES, HEAD_W:LANES]
                n1_ref[bi, 2 * p:2 * p + 1, :] = nr_s[bi, p, 0:1, 0:HEAD_W]
                n1_ref[bi, 2 * p + 1:2 * p + 2, :] = nr_s[bi, p, 1:2, HEAD_W:LANES]
            m1_ref[bi] = m_s[bi]
            for p in range(HG_HEADS // 2):
                sbd = sbd_s[bi, p]
                s1_ref[bi, 2 * p] = sbd[0:HEAD_W, 0:HEAD_W]
                s1_ref[bi, 2 * p + 1] = sbd[HEAD_W:LANES, HEAD_W:LANES]


def _mixers_prompt(ml, ssd, hg, gates, gate_bias, alog_col, d_row, ml_nw, ssd_nw, lbp, hg_nw, layer):
    b, t, _ = ml.shape
    L = PROMPT_CHUNK
    ns = SEQS_PER_STEP
    assert b % ns == 0 and t % L == 0
    ml_pairs, ssd_pairs, hg_pairs = ML_HEADS // 2, SSD_HEADS // 2, HG_HEADS // 2
    tok = lambda w: pl.BlockSpec((ns, L, w), lambda i, c: (i, c, 0))
    per_seq = lambda *shape: pl.BlockSpec((ns,) + shape, lambda i, c: (i,) + (0,) * len(shape))
    hg_rows = pltpu.VMEM((ns, HG_CHUNK, HG_W), f32)
    hg_pairs_buf = pltpu.VMEM((ns, 8 * HG_CHUNK, HG_W), f32)
    return pl.pallas_call(
        functools.partial(_mixers_prompt_body, layer=layer),
        grid=(b // ns, t // L),
        in_specs=[tok(ML_COLS), tok(SSD_COLS), tok(HG_COLS), tok(GATE_W),
                  _layer_resident((1, GATE_W), layer), _layer_resident((GATE_ROWS, 1), layer),
                  _layer_resident((1, SSD_W), layer), _layer_resident((1, ML_W), layer),
                  _layer_resident((1, SSD_W), layer), _resident((DEPTH, HG_W)), _layer_resident((1, HG_W), layer)],
        out_specs=[tok(ML_W), per_seq(ML_HEADS, HEAD_W, HEAD_W), per_seq(ML_HEADS, HEAD_W), per_seq(8, LANES),
                   tok(SSD_W), per_seq(ssd_pairs, LANES, SSD_DSTATE),
                   tok(HG_W), per_seq(HG_HEADS, HEAD_W, HEAD_W)],
        out_shape=[jax.ShapeDtypeStruct((b, t, ML_W), bf16),
                   jax.ShapeDtypeStruct((b, ML_HEADS, HEAD_W, HEAD_W), f32),
                   jax.ShapeDtypeStruct((b, ML_HEADS, HEAD_W), f32),
                   jax.ShapeDtypeStruct((b, 8, LANES), f32),
                   jax.ShapeDtypeStruct((b, t, SSD_W), bf16),
                   jax.ShapeDtypeStruct((b, ssd_pairs, LANES, SSD_DSTATE), f32),
                   jax.ShapeDtypeStruct((b, t, HG_W), bf16),
                   jax.ShapeDtypeStruct((b, HG_HEADS, HEAD_W, HEAD_W), f32)],
        scratch_shapes=[pltpu.VMEM((ns, ml_pairs, LANES, LANES), f32), pltpu.VMEM((ns, ml_pairs, 8, LANES), f32),
                        pltpu.VMEM((ns, 8, LANES), f32),
                        pltpu.VMEM((ns, ssd_pairs, LANES, SSD_DSTATE), f32),
                        pltpu.VMEM((ns, hg_pairs, LANES, LANES), f32), hg_pairs_buf, hg_pairs_buf,
                        hg_rows, hg_rows, hg_rows, hg_rows, hg_rows,
                        pltpu.VMEM((ns, HG_CHUNK // 8 - 1, 2 * HG_CHUNK, HG_W), f32), hg_rows],
        compiler_params=_cparams("arbitrary", "arbitrary"),
        name="mixers_prompt",
    )(ml, ssd, hg, gates, gate_bias, alog_col, d_row, ml_nw, ssd_nw, lbp, hg_nw)


NB = 128
TS = 8


def _sample_call(body, name, grid, in_specs, operands, out_specs, out_shapes, scratch, n_state):
    first = len(operands) - n_state
    return pl.pallas_call(
        body, grid=grid, in_specs=in_specs, out_specs=out_specs, out_shape=out_shapes,
        scratch_shapes=scratch, input_output_aliases={first + i: 1 + i for i in range(n_state)},
        compiler_params=_cparams(*(["arbitrary"] * len(grid))), name=name,
    )(*operands)


def _mlstm_sample_body(q_ref, k_ref, v_ref, o_ref, g_ref, gb_ref, nw_ref, c0_ref, n0_ref, m0_ref,
                       out_ref, c1_ref, n1_ref, m1_ref, qt_s, kt_s, vt_s, gt_s, ht_s):
    p = pl.program_id(0)
    n_cur = [n0_ref[hl] for hl in range(2)]
    m_cur = [m0_ref[pl.ds(2 * p + hl, 1), :] for hl in range(2)]

    for t in range(TS):
        c_src = c0_ref if t == 0 else c1_ref
        rows = slice(t * NB, (t + 1) * NB)
        qt_s[...] = q_ref[rows, :].T
        kt_s[...] = k_ref[rows, :].T * (HEAD_W ** -0.5)
        vt_s[...] = v_ref[rows, :].T
        gt_s[...] = (g_ref[rows, :] + gb_ref[...]).T
        for hl in range(2):
            h = 2 * p + hl
            hs = slice(hl * HEAD_W, (hl + 1) * HEAD_W)
            i_t = gt_s[pl.ds(h, 1), :]
            lf = jax.nn.log_sigmoid(gt_s[pl.ds(ML_HEADS + h, 1), :])
            m_new = jnp.maximum(lf + m_cur[hl], i_t)
            a = jnp.exp(lf + m_cur[hl] - m_new)
            w = jnp.exp(i_t - m_new)
            n_new = a * n_cur[hl] + w * kt_s[hs, :]
            v_h = vt_s[hs, :]

            def body(d, num, hl=hl, a=a, w=w, v_h=v_h, c_src=c_src):
                kd = kt_s[pl.ds(hl * HEAD_W + d, 1), :] * w
                qd = qt_s[pl.ds(hl * HEAD_W + d, 1), :]
                cd = a * c_src[hl, d] + kd * v_h
                c1_ref[hl, d] = cd
                return num + qd * cd

            num = lax.fori_loop(0, HEAD_W, body, jnp.zeros((HEAD_W, NB), f32), unroll=4)
            den = jnp.sum(qt_s[hs, :] * n_new, axis=0, keepdims=True)
            hh = num / jnp.maximum(jnp.abs(den), jnp.exp(-m_new))
            ms = jnp.mean(hh * hh, axis=0, keepdims=True)
            ht_s[hs, :] = hh * lax.rsqrt(ms + EPS)
            n_cur[hl] = n_new
            m_cur[hl] = m_new
        out_ref[rows, :] = (ht_s[...].T * nw_ref[...] * jax.nn.sigmoid(o_ref[rows, :])).astype(bf16)

    for hl in range(2):
        n1_ref[hl] = n_cur[hl]
        m1_ref[pl.ds(2 * p + hl, 1), :] = m_cur[hl]


def _mlstm_sample(ml, gates, gate_bias, norm_w, c0t, n0t, m0t, layer):
    n = ml.shape[0]
    npair = ML_HEADS // 2
    blk = lambda off: pl.BlockSpec((n, LANES), lambda p, off=off: (0, off + p))
    c_spec = pl.BlockSpec((None, 2, HEAD_W, HEAD_W, NB), lambda p: (layer, p, 0, 0, 0))
    n_spec = pl.BlockSpec((None, 2, HEAD_W, NB), lambda p: (layer, p, 0, 0))
    m_spec = pl.BlockSpec((None, ML_HEADS, NB), lambda p: (layer, 0, 0))
    return _sample_call(
        _mlstm_sample_body, "mlstm_sample", (npair,),
        [blk(0), blk(npair), blk(2 * npair), blk(3 * npair), _resident((n, GATE_W)),
         _layer_resident((1, GATE_W), layer), pl.BlockSpec((None, 1, LANES), lambda p: (layer, 0, p)),
         c_spec, n_spec, m_spec],
        [ml, ml, ml, ml, gates, gate_bias, norm_w, c0t, n0t, m0t],
        [pl.BlockSpec((n, LANES), lambda p: (0, p)), c_spec, n_spec, m_spec],
        [jax.ShapeDtypeStruct((n, ML_W), bf16), jax.ShapeDtypeStruct(c0t.shape, f32),
         jax.ShapeDtypeStruct(n0t.shape, f32), jax.ShapeDtypeStruct(m0t.shape, f32)],
        [pltpu.VMEM((LANES, NB), f32)] * 5, 3)


def _ssd_conv_sample_body(ssd_ref, halo_ref, cw_ref, cb_ref, out_ref, xp_s):
    nh = (SSD_CONV - 1) * NB
    n = TS * NB
    xp_s[0:nh, :] = halo_ref[...]
    xp_s[nh:nh + n, :] = ssd_ref[:, SSD_W:SSD_COLS]
    conv = cb_ref[...]
    for j in range(SSD_CONV):
        conv = conv + cw_ref[j:j + 1, :] * xp_s[j * NB:j * NB + n, :]
    out_ref[...] = _silu(conv)


def _ssd_conv_sample(ssd, halo, conv_w, conv_b, layer):
    n = ssd.shape[0]
    nh = (SSD_CONV - 1) * NB
    return pl.pallas_call(
        _ssd_conv_sample_body,
        grid=(1,),
        in_specs=[_resident((n, SSD_COLS)), _layer_resident((nh, SSD_CONV_DIM), layer),
                  _layer_resident((SSD_CONV, SSD_CONV_DIM), layer), _layer_resident((1, SSD_CONV_DIM), layer)],
        out_specs=pl.BlockSpec((n, SSD_CONV_DIM), lambda i: (0, 0)),
        out_shape=jax.ShapeDtypeStruct((n, SSD_CONV_DIM), f32),
        scratch_shapes=[pltpu.VMEM((nh + n, SSD_CONV_DIM), f32)],
        compiler_params=_cparams("arbitrary"),
        name="ssd_conv_sample",
    )(ssd, halo, conv_w, conv_b)


def _ssd_sample_body(z_ref, x_ref, b0_ref, b1_ref, c0_ref, c1_ref, g_ref, gb_ref, hp_ref, nw_ref, h0_ref,
                     out_ref, h1_ref, xt_s, bt_s, ct_s, gt_s, yt_s):
    p = pl.program_id(0)
    b_refs = (b0_ref, b1_ref)
    c_refs = (c0_ref, c1_ref)

    for t in range(TS):
        h_src = h0_ref if t == 0 else h1_ref
        rows = slice(t * NB, (t + 1) * NB)
        xt_s[...] = x_ref[rows, :].T
        gt_s[...] = (g_ref[rows, :] + gb_ref[...]).T
        zt = z_ref[rows, :].T
        for hl in range(2):
            bt_s[hl] = b_refs[hl][rows, :].T
            ct_s[hl] = c_refs[hl][rows, :].T
        for hl in range(2):
            h = 2 * p + hl
            hs = slice(hl * HEAD_W, (hl + 1) * HEAD_W)
            dt = jax.nn.softplus(gt_s[pl.ds(DT_LANE + h, 1), :])
            da = jnp.exp(dt * (-jnp.exp(hp_ref[0, hl:hl + 1, :])))

            def body(pp, carry, hl=hl, dt=dt, da=da, h_src=h_src):
                xrow = xt_s[pl.ds(hl * HEAD_W + pp, 1), :] * dt
                hn = da * h_src[hl, pp] + xrow * bt_s[hl]
                h1_ref[hl, pp] = hn
                yt_s[pl.ds(hl * HEAD_W + pp, 1), :] = jnp.sum(ct_s[hl] * hn, axis=0, keepdims=True)
                return carry

            lax.fori_loop(0, HEAD_W, body, 0, unroll=8)
            yy = (yt_s[hs, :] + hp_ref[1, hl:hl + 1, :] * xt_s[hs, :]) * _silu(zt[hs, :])
            ms = jnp.mean(yy * yy, axis=0, keepdims=True)
            yt_s[hs, :] = yy * lax.rsqrt(ms + EPS)
        out_ref[rows, :] = (yt_s[...].T * nw_ref[...]).astype(bf16)


def _ssd_sample(ssd, xc, gates, gate_bias, head_params, norm_w, h0t, layer):
    n = ssd.shape[0]
    npair = SSD_HEADS // 2
    hpg = SSD_HEADS // 2
    nxb = SSD_W // LANES
    blk = lambda f: pl.BlockSpec((n, LANES), f)
    h_spec = pl.BlockSpec((None, 2, HEAD_W, SSD_DSTATE, NB), lambda p: (layer, p, 0, 0, 0))
    return _sample_call(
        _ssd_sample_body, "ssd_sample", (npair,),
        [blk(lambda p: (0, p)), blk(lambda p: (0, p)),
         blk(lambda p: (0, nxb + (2 * p) // hpg)), blk(lambda p: (0, nxb + (2 * p + 1) // hpg)),
         blk(lambda p: (0, nxb + 2 + (2 * p) // hpg)), blk(lambda p: (0, nxb + 2 + (2 * p + 1) // hpg)),
         _resident((n, GATE_W)), _layer_resident((1, GATE_W), layer),
         pl.BlockSpec((None, 2, None, 2, LANES), lambda p: (layer, 0, p, 0, 0)),
         pl.BlockSpec((None, 1, LANES), lambda p: (layer, 0, p)), h_spec],
        [ssd, xc, xc, xc, xc, xc, gates, gate_bias, head_params, norm_w, h0t],
        [pl.BlockSpec((n, LANES), lambda p: (0, p)), h_spec],
        [jax.ShapeDtypeStruct((n, SSD_W), bf16), jax.ShapeDtypeStruct(h0t.shape, f32)],
        [pltpu.VMEM((LANES, NB), f32), pltpu.VMEM((2, SSD_DSTATE, NB), f32), pltpu.VMEM((2, SSD_DSTATE, NB), f32),
         pltpu.VMEM((LANES, NB), f32), pltpu.VMEM((LANES, NB), f32)], 1)


def _hgrn_sample_body(q_ref, f_ref, i_ref, gate_ref, lbp_ref, nw_ref, s0_ref, out_ref, s1_ref,
                      qt_s, kt_s, ft_s, vt_s, ot_s, *, layer):
    lb = _hg_lower_bound(lbp_ref[...], layer)

    for t in range(TS):
        s_src = s0_ref if t == 0 else s1_ref
        rows = slice(t * NB, (t + 1) * NB)
        ff = f_ref[rows, :]
        qt_s[...] = (q_ref[rows, :] * (HEAD_W ** -0.5)).T
        kt_s[...] = ((1.0 - lb) * jax.nn.sigmoid(-ff)).T
        ft_s[...] = (lb + (1.0 - lb) * jax.nn.sigmoid(ff)).T
        vt_s[...] = i_ref[rows, :].T
        for hl in range(2):
            hs = slice(hl * HEAD_W, (hl + 1) * HEAD_W)
            v_h = vt_s[hs, :]

            def body(k, o, hl=hl, v_h=v_h, s_src=s_src):
                r = pl.ds(hl * HEAD_W + k, 1)
                sk = ft_s[r, :] * s_src[hl, k] + kt_s[r, :] * v_h
                s1_ref[hl, k] = sk
                return o + qt_s[r, :] * sk

            o = lax.fori_loop(0, HEAD_W, body, jnp.zeros((HEAD_W, NB), f32), unroll=4)
            ms = jnp.mean(o * o, axis=0, keepdims=True)
            ot_s[hs, :] = o * lax.rsqrt(ms + EPS)
        out_ref[rows, :] = (ot_s[...].T * nw_ref[...] * _silu(gate_ref[rows, :])).astype(bf16)


def _hgrn_sample(hg, lbp, norm_w, s0t, layer):
    n = hg.shape[0]
    npair = HG_HEADS // 2
    blk = lambda off: pl.BlockSpec((n, LANES), lambda p, off=off: (0, off + p))
    s_spec = pl.BlockSpec((None, 2, HEAD_W, HEAD_W, NB), lambda p: (layer, p, 0, 0, 0))
    return _sample_call(
        functools.partial(_hgrn_sample_body, layer=layer), "hgrn_sample", (npair,),
        [blk(0), blk(npair), blk(2 * npair), blk(3 * npair),
         pl.BlockSpec((DEPTH, LANES), lambda p: (0, p)),
         pl.BlockSpec((None, 1, LANES), lambda p: (layer, 0, p)), s_spec],
        [hg, hg, hg, hg, lbp, norm_w, s0t],
        [pl.BlockSpec((n, LANES), lambda p: (0, p)), s_spec],
        [jax.ShapeDtypeStruct((n, HG_W), bf16), jax.ShapeDtypeStruct(s0t.shape, f32)],
        [pltpu.VMEM((LANES, NB), f32)] * 5, 1)


def _ffn_body(x_ref, mml_ref, mssd_ref, mhg_ref, wout_ref, n2_ref, wug_ref, wuv_ref, cwg_ref, cwv_ref,
              cbg_ref, cbv_ref, wdn_ref, f0g_ref, f0v_ref, fn_ref, y_ref, f1g_ref, f1v_ref,
              x1_s, h2_s, acc_s, act_s, ubuf_s, *, tm, shift, jb, nff, final):
    t = pl.program_id(1)
    f = pl.program_id(2)
    halo = max(HALO, (FFN_CONV - 1) * shift)
    keep = (FFN_CONV - 1) * shift

    def when(cond, fn):
        if nff == 1:
            fn()
        else:
            pl.when(cond)(fn)

    @pl.when(t == 0)
    def _():
        f1g_ref[0] = f0g_ref[0]
        f1v_ref[0] = f0v_ref[0]

    def first():
        x = x_ref[...]
        x1 = (x + _dot(mml_ref[...], wout_ref[0:ML_W, :]) + _dot(mssd_ref[...], wout_ref[ML_W:ML_W + SSD_W, :])
              + _dot(mhg_ref[...], wout_ref[ML_W + SSD_W:D_MODEL, :]))
        x1_s[...] = x1
        h2_s[...] = (x1 * lax.rsqrt(jnp.mean(x1 * x1, axis=-1, keepdims=True) + EPS) * n2_ref[...]).astype(bf16)

    when(f == 0, first)
    h2 = h2_s[...]

    def conv(wu_ref, cw_ref, cb_ref, f1_ref, cols, ubuf):
        u = _dot(h2, wu_ref[:, cols])
        ubuf[halo - keep:halo, :] = f1_ref[0, :, cols]
        ubuf[halo:halo + tm, :] = u
        yv = (cb_ref[:, cols] + cw_ref[2:3, cols] * u
              + cw_ref[1:2, cols] * ubuf[halo - shift:halo - shift + tm, :]
              + cw_ref[0:1, cols] * ubuf[halo - 2 * shift:halo - 2 * shift + tm, :])
        f1_ref[0, :, cols] = ubuf[halo + tm - keep:halo + tm, :]
        return yv

    nbuf = ubuf_s.shape[0]
    for j in range(jb):
        cols = slice(j * FF_BLOCK, (j + 1) * FF_BLOCK)
        gq = conv(wug_ref, cwg_ref, cbg_ref, f1g_ref, cols, ubuf_s.at[(2 * j) % nbuf])
        vq = conv(wuv_ref, cwv_ref, cbv_ref, f1v_ref, cols, ubuf_s.at[(2 * j + 1) % nbuf])
        act_s[:, cols] = (_silu(gq) * vq).astype(bf16)
    contrib = _dot(act_s[...], wdn_ref[...])

    def finish(down):
        x2 = x1_s[...] + down
        if final:
            x2 = x2 * lax.rsqrt(jnp.mean(x2 * x2, axis=-1, keepdims=True) + EPS) * fn_ref[...]
        y_ref[...] = x2

    if nff == 1:
        finish(contrib)
    else:
        @pl.when(f == 0)
        def _():
            acc_s[...] = contrib

        @pl.when(f != 0)
        def _():
            acc_s[...] = acc_s[...] + contrib

        pl.when(f == nff - 1)(lambda: finish(acc_s[...]))


def _ffn(x, mml, mssd, mhg, w_out, norm2, w_up, conv_w, conv_b, w_down, f0, final_w, *, layer, groups, tm, shift, jb,
         final, f0_base=0):
    n = x.shape[0]
    nt = n // (groups * tm)
    fw = jb * FF_BLOCK
    nff = D_FF // fw
    assert nt == 1 or nff == 1
    halo = max(HALO, (FFN_CONV - 1) * shift)
    keep = (FFN_CONV - 1) * shift
    rows = lambda w: pl.BlockSpec((tm, w), lambda g, t, f: (g * nt + t, 0))
    single = nff == 1
    wspec = lambda shape, imap: pl.BlockSpec(shape, imap, pipeline_mode=pl.Buffered(1)) if single else pl.BlockSpec(shape, imap)
    f1_spec = pl.BlockSpec((1, keep, fw), lambda g, t, f: (g, 0, f))
    in_specs = [rows(D_MODEL), rows(ML_W), rows(SSD_W), rows(HG_W),
                _layer_resident((D_MODEL, D_MODEL), layer), _layer_resident((1, D_MODEL), layer),
                wspec((None, D_MODEL, fw), lambda g, t, f: (layer, 0, f)),
                wspec((None, D_MODEL, fw), lambda g, t, f: (layer, 0, nff + f)),
                wspec((None, FFN_CONV, fw), lambda g, t, f: (layer, 0, f)),
                wspec((None, FFN_CONV, fw), lambda g, t, f: (layer, 0, nff + f)),
                wspec((None, 1, fw), lambda g, t, f: (layer, 0, f)),
                wspec((None, 1, fw), lambda g, t, f: (layer, 0, nff + f)),
                wspec((None, fw, D_MODEL), lambda g, t, f: (layer, f, 0)),
                pl.BlockSpec((1, keep, fw), lambda g, t, f: (f0_base + g, 0, f)),
                pl.BlockSpec((1, keep, fw), lambda g, t, f: (f0_base + g, 0, nff + f)),
                _resident((1, D_MODEL))]
    operands = [x, mml, mssd, mhg, w_out, norm2, w_up, w_up, conv_w, conv_w, conv_b, conv_b, w_down, f0, f0, final_w]
    acc_rows = tm if nff > 1 else 8
    return pl.pallas_call(
        functools.partial(_ffn_body, tm=tm, shift=shift, jb=jb, nff=nff, final=final),
        grid=(groups, nt, nff),
        in_specs=in_specs,
        out_specs=[rows(D_MODEL), f1_spec, f1_spec],
        out_shape=[jax.ShapeDtypeStruct((n, D_MODEL), f32),
                   jax.ShapeDtypeStruct((groups, keep, D_FF), f32),
                   jax.ShapeDtypeStruct((groups, keep, D_FF), f32)],
        scratch_shapes=[pltpu.VMEM((tm, D_MODEL), f32), pltpu.VMEM((tm, D_MODEL), bf16),
                        pltpu.VMEM((acc_rows, D_MODEL), f32), pltpu.VMEM((tm, fw), bf16),
                        pltpu.VMEM((min(4, 2 * jb), halo + tm, FF_BLOCK), f32)],
        compiler_params=_cparams("arbitrary", "arbitrary", "arbitrary"),
        name="ffn",
    )(*operands)


def _gate_rows(*parts):
    r = jnp.concatenate([p.astype(f32) for p in parts], axis=1)
    return jnp.pad(r, ((0, 0), (0, GATE_W - r.shape[1])))[:, None, :]


def kernel(x_prompt, x_sample, state_mlstm_C, state_mlstm_n, state_mlstm_m, state_ssd, state_ssd_conv,
           state_hgrn, state_ffn_conv, norm1_w, w_in, ml_ig_b, ml_fg_b, ml_norm_w, ssd_conv_w, ssd_conv_b,
           ssd_dt_bias, ssd_A_log, ssd_D, ssd_norm_w, hg_lower_bounds, hg_norm_w, w_out, norm2_w, w_up,
           ffn_conv_w, ffn_conv_b, w_down, final_norm_w):
    bp, tp, _ = x_prompt.shape
    nb, ts, _ = x_sample.shape
    assert nb == NB and ts == TS and tp % ROW_TILE == 0
    depth = w_in.shape[0]
    assert depth == DEPTH
    xp = x_prompt.reshape(bp * tp, D_MODEL)
    xs = jnp.transpose(x_sample, (1, 0, 2)).reshape(ts * nb, D_MODEL)
    lbp = hg_lower_bounds.astype(f32)
    fin_w = final_norm_w[None, :]
    zero_f = jnp.zeros((bp, FFN_CONV - 1, 2 * D_FF), f32)

    w_cat = _prep_w_in(w_in)
    wo_b, wu_b, wd_b = w_out.astype(bf16), w_up.astype(bf16), w_down.astype(bf16)
    n1, n2 = norm1_w[:, None, :], norm2_w[:, None, :]
    gate_bias = _gate_rows(ml_ig_b, ml_fg_b, ssd_dt_bias)
    alog_col = jnp.pad(ssd_A_log.astype(f32), ((0, 0), (SSD_HEAD_ROW, GATE_ROWS - SSD_HEAD_ROW - SSD_HEADS)))[:, :, None]
    d_row = jnp.repeat(ssd_D.astype(f32), HEAD_W, axis=1)[:, None, :]
    mlw, ssw, hgw = ml_norm_w[:, None, :], ssd_norm_w[:, None, :], hg_norm_w[:, None, :]
    cw, cb = ssd_conv_w, ssd_conv_b[:, None, :]
    fcw, fcb = ffn_conv_w, ffn_conv_b[:, None, :]
    head_params = jnp.broadcast_to(
        jnp.stack([ssd_A_log, ssd_D], axis=1).astype(f32).reshape(depth, 2, SSD_HEADS // 2, 2, 1),
        (depth, 2, SSD_HEADS // 2, 2, LANES))

    c_t = jnp.transpose(state_mlstm_C.astype(f32), (0, 2, 3, 4, 1))
    n_t = jnp.transpose(state_mlstm_n.astype(f32), (0, 2, 3, 1))
    m_t = jnp.transpose(state_mlstm_m.astype(f32), (0, 2, 1))
    h_t = jnp.transpose(state_ssd.astype(f32), (0, 2, 3, 4, 1))
    s_t = jnp.transpose(state_hgrn.astype(f32), (0, 2, 3, 4, 1))
    conv_halo = jnp.transpose(state_ssd_conv.astype(f32), (0, 2, 1, 3)).reshape(depth, (SSD_CONV - 1) * nb, SSD_CONV_DIM)
    f_t = jnp.transpose(state_ffn_conv.astype(f32), (0, 2, 1, 3)).reshape(depth, (FFN_CONV - 1) * nb, 2 * D_FF)

    p_states, s_sconv, s_ffn = [], [], []
    for l in range(depth):
        last = l == depth - 1

        ml, ssd, hg, gates, p_tail = _inproj(xp, n1, w_cat, l, conv=(cw, cb, tp))
        hm, p_c, p_n, p_m, hs, p_h, ho, p_s = _mixers_prompt(
            ml.reshape(bp, tp, ML_COLS), ssd.reshape(bp, tp, SSD_COLS), hg.reshape(bp, tp, HG_COLS),
            gates.reshape(bp, tp, GATE_W), gate_bias, alog_col, d_row, mlw, ssw, lbp, hgw, l)
        xp, p_fg, p_fv = _ffn(xp, hm.reshape(bp * tp, ML_W), hs.reshape(bp * tp, SSD_W), ho.reshape(bp * tp, HG_W),
                              wo_b, n2, wu_b, fcw, fcb, wd_b, zero_f, fin_w, layer=l,
                              groups=bp, tm=ROW_TILE, shift=1, jb=D_FF // FF_BLOCK, final=last)
        p_states.append((p_c, p_n, p_m[:, 0:ML_HEADS, 0],
                         p_h.reshape(bp, SSD_HEADS, HEAD_W, SSD_DSTATE),
                         p_tail[:, HALO - (SSD_CONV - 1):, :], p_s,
                         jnp.concatenate([p_fg, p_fv], axis=-1)))

        ml, ssd, hg, gates = _inproj(xs, n1, w_cat, l)
        hm, c_t, n_t, m_t = _mlstm_sample(ml, gates, gate_bias, mlw, c_t, n_t, m_t, l)
        xc = _ssd_conv_sample(ssd, conv_halo, cw, cb, l)
        hs, h_t = _ssd_sample(ssd, xc, gates, gate_bias, head_params, ssw, h_t, l)
        ho, s_t = _hgrn_sample(hg, lbp, hgw, s_t, l)
        xs, s_fg, s_fv = _ffn(xs, hm, hs, ho, wo_b, n2, wu_b, fcw, fcb, wd_b, f_t, fin_w, layer=l,
                              groups=1, tm=ts * nb, shift=nb, jb=1, final=last, f0_base=l)
        s_ffn.append(jnp.concatenate([s_fg, s_fv], axis=-1).reshape(FFN_CONV - 1, nb, 2 * D_FF))
        raw_xbc = ssd[:, SSD_W:].reshape(ts, nb, SSD_CONV_DIM)
        s_sconv.append(jnp.transpose(raw_xbc[ts - (SSD_CONV - 1):], (1, 0, 2)))

    y_prompt = xp.reshape(bp, tp, D_MODEL)
    y_sample = jnp.transpose(xs.reshape(ts, nb, D_MODEL), (1, 0, 2))
    p_out = [jnp.stack([st[i] for st in p_states]) for i in range(7)]
    s_out = [jnp.transpose(c_t, (0, 4, 1, 2, 3)), jnp.transpose(n_t, (0, 3, 1, 2)), jnp.transpose(m_t, (0, 2, 1)),
             jnp.transpose(h_t, (0, 4, 1, 2, 3)), jnp.stack(s_sconv),
             jnp.transpose(s_t, (0, 4, 1, 2, 3)), jnp.transpose(jnp.stack(s_ffn), (0, 2, 1, 3))]
    return (y_prompt, y_sample, *p_out, *s_out)
```

```python
import functools

import jax
import jax.numpy as jnp
from jax import lax
from jax.experimental import pallas as pl
from jax.experimental.pallas import tpu as pltpu

f32 = jnp.float32
bf16 = jnp.bfloat16

D_MODEL = 1024
DEPTH = 2
ML_HEADS = 6
HEAD_W = 64
ML_W = 384
SSD_HEADS = 6
SSD_W = 384
SSD_DSTATE = 128
SSD_CONV = 4
SSD_CONV_DIM = 896
HG_HEADS = 4
HG_W = 256
D_FF = 2816
FFN_CONV = 3
EPS = 1e-6
PROMPT_CHUNK = 128
HG_CHUNK = 64
LANES = 128
FF_BLOCK = 256
ROW_TILE = 512
SEQS_PER_STEP = 4
GATE_W = 128
GATE_ROWS = 16
W_IN_COLS = 4 * ML_W + (SSD_W + SSD_CONV_DIM) + 4 * HG_W + GATE_W
VMEM_LIMIT_BYTES = 56 * 1024 * 1024

NEG_INF = float("-inf")
LOG2_E = 1.4426950408889634


def _cparams(*sem):
    return pltpu.CompilerParams(dimension_semantics=sem, vmem_limit_bytes=VMEM_LIMIT_BYTES)


def _resident(shape):
    nd = len(shape)
    return pl.BlockSpec(shape, lambda *_: (0,) * nd, pipeline_mode=pl.Buffered(1))


def _layer_resident(shape, layer):
    nd = len(shape)
    return pl.BlockSpec((None,) + tuple(shape), lambda *_: (layer,) + (0,) * nd, pipeline_mode=pl.Buffered(1))


def _dot(a, b):
    return jnp.dot(a, b, preferred_element_type=f32)


def _dot_nt(a, b):
    return lax.dot_general(a, b, (((1,), (1,)), ((), ())), preferred_element_type=f32)


def _split_bf16(x, parts):
    out = []
    r = x
    for _ in range(parts):
        h = r.astype(bf16)
        out.append(h)
        r = r - h.astype(f32)
    return out


def _cumsum_rows(x, tri_b):
    acc = None
    for part in _split_bf16(x, 3):
        d = _dot(tri_b, part)
        acc = d if acc is None else acc + d
    return acc


def _group_mean(x, bd_b):
    return _dot(x.astype(bf16), bd_b) * (1.0 / HEAD_W)


def _iota2(shape, axis):
    return lax.broadcasted_iota(jnp.int32, shape, axis)


def _ones_where(mask):
    return jnp.where(mask, 1.0, 0.0).astype(bf16)


def _same_head(a, b):
    return jnp.right_shift(a, 6) == jnp.right_shift(b, 6)


def _silu(x):
    return x * jax.nn.sigmoid(x)


ML_COLS = 4 * ML_W
SSD_COLS = SSD_W + SSD_CONV_DIM
HG_COLS = 4 * HG_W


def _inproj_body(x_ref, nw_ref, w_ref, ml_ref, ssd_ref, hg_ref, g_ref):
    x = x_ref[...]
    h = (x * lax.rsqrt(jnp.mean(x * x, axis=-1, keepdims=True) + EPS) * nw_ref[...]).astype(bf16)
    o0, o1, o2 = ML_COLS, ML_COLS + SSD_COLS, ML_COLS + SSD_COLS + HG_COLS
    ml_ref[...] = _dot(h, w_ref[:, 0:o0])
    ssd_ref[...] = _dot(h, w_ref[:, o0:o1])
    hg_ref[...] = _dot(h, w_ref[:, o1:o2])
    g_ref[...] = _dot(h, w_ref[:, o2:W_IN_COLS])


def _inproj_conv_body(x_ref, nw_ref, w_ref, cw_ref, cb_ref, ml_ref, ssd_ref, hg_ref, g_ref, tail_ref, xbuf_s, *,
                      tiles_per_seq):
    i = pl.program_id(0)
    tm = x_ref.shape[0]

    @pl.when(lax.rem(i, tiles_per_seq) == 0)
    def _():
        xbuf_s[0:HALO, :] = jnp.zeros((HALO, SSD_CONV_DIM), f32)

    x = x_ref[...]
    h = (x * lax.rsqrt(jnp.mean(x * x, axis=-1, keepdims=True) + EPS) * nw_ref[...]).astype(bf16)
    o0, o1, o2 = ML_COLS, ML_COLS + SSD_COLS, ML_COLS + SSD_COLS + HG_COLS
    zx = _dot(h, w_ref[:, o0:o1])
    ssd_ref[:, 0:SSD_W] = zx[:, 0:SSD_W]
    xbuf_s[HALO:HALO + tm, :] = zx[:, SSD_W:SSD_COLS]
    conv = cb_ref[...]
    for j in range(SSD_CONV):
        o = HALO - (SSD_CONV - 1) + j
        conv = conv + cw_ref[j:j + 1, :] * xbuf_s[o:o + tm, :]
    ssd_ref[:, SSD_W:SSD_COLS] = _silu(conv)
    tail = xbuf_s[tm:tm + HALO, :]
    tail_ref[0] = tail
    xbuf_s[0:HALO, :] = tail
    ml_ref[...] = _dot(h, w_ref[:, 0:o0])
    hg_ref[...] = _dot(h, w_ref[:, o1:o2])
    g_ref[...] = _dot(h, w_ref[:, o2:W_IN_COLS])


def _inproj(x, norm_w, w_cat, layer, conv=None):
    n = x.shape[0]
    tm = ROW_TILE
    row = lambda w: pl.BlockSpec((tm, w), lambda i: (i, 0))
    in_specs = [row(D_MODEL), _layer_resident((1, D_MODEL), layer), _layer_resident((D_MODEL, W_IN_COLS), layer)]
    out_specs = [row(ML_COLS), row(SSD_COLS), row(HG_COLS), row(GATE_W)]
    out_shape = [jax.ShapeDtypeStruct((n, w), f32) for w in (ML_COLS, SSD_COLS, HG_COLS, GATE_W)]
    if conv is None:
        return pl.pallas_call(
            _inproj_body, grid=(n // tm,), in_specs=in_specs, out_specs=out_specs, out_shape=out_shape,
            compiler_params=_cparams("arbitrary"), name="inproj",
        )(x, norm_w, w_cat)
    conv_w, conv_b, rows_per_seq = conv
    tps = rows_per_seq // tm
    assert rows_per_seq % tm == 0
    return pl.pallas_call(
        functools.partial(_inproj_conv_body, tiles_per_seq=tps),
        grid=(n // tm,),
        in_specs=in_specs + [_layer_resident((SSD_CONV, SSD_CONV_DIM), layer), _layer_resident((1, SSD_CONV_DIM), layer)],
        out_specs=out_specs + [pl.BlockSpec((1, HALO, SSD_CONV_DIM), lambda i: (i // tps, 0, 0))],
        out_shape=out_shape + [jax.ShapeDtypeStruct((n // rows_per_seq, HALO, SSD_CONV_DIM), f32)],
        scratch_shapes=[pltpu.VMEM((HALO + tm, SSD_CONV_DIM), f32)],
        compiler_params=_cparams("arbitrary"),
        name="inproj_conv",
    )(x, norm_w, w_cat, conv_w, conv_b)


W_IN_GATE_IF = ML_COLS
W_IN_SSD = W_IN_GATE_IF + 2 * ML_HEADS
W_IN_DT = W_IN_SSD + SSD_COLS
W_IN_HG = W_IN_DT + SSD_HEADS
W_IN_DIM = W_IN_HG + HG_COLS
W_PREP_ROWS = 256


def _prep_w_in_body(w_ref, out_ref):
    lane = _iota2((W_PREP_ROWS, LANES), 1)
    tile = lambda k: w_ref[:, k * LANES:(k + 1) * LANES]
    out_ref[:, 0:ML_COLS] = w_ref[:, 0:ML_COLS].astype(bf16)

    def shifted(dst0, src0, ntiles):
        t0, sh = src0 // LANES, src0 % LANES
        rolled = [pltpu.roll(tile(t0 + k), LANES - sh, axis=1) for k in range(ntiles + 1)]
        for k in range(ntiles):
            out_ref[:, dst0 + k * LANES:dst0 + (k + 1) * LANES] = jnp.where(
                lane < LANES - sh, rolled[k], rolled[k + 1]).astype(bf16)

    shifted(ML_COLS, W_IN_SSD, SSD_COLS // LANES)
    shifted(ML_COLS + SSD_COLS, W_IN_HG, HG_COLS // LANES)
    assert W_IN_GATE_IF % LANES == 0 and W_IN_DT % LANES == DT_LANE
    gates = jnp.where(lane < DT_LANE, tile(W_IN_GATE_IF // LANES),
                      jnp.where(lane < DT_LANE + SSD_HEADS, tile(W_IN_DT // LANES), 0.0))
    out_ref[:, ML_COLS + SSD_COLS + HG_COLS:W_IN_COLS] = gates.astype(bf16)


def _prep_w_in(w_in):
    depth, d, n = w_in.shape
    assert n == W_IN_DIM and d % W_PREP_ROWS == 0
    return pl.pallas_call(
        _prep_w_in_body,
        grid=(depth, d // W_PREP_ROWS),
        in_specs=[pl.BlockSpec((None, W_PREP_ROWS, W_IN_COLS), lambda l, i: (l, i, 0))],
        out_specs=pl.BlockSpec((None, W_PREP_ROWS, W_IN_COLS), lambda l, i: (l, i, 0)),
        out_shape=jax.ShapeDtypeStruct((depth, d, W_IN_COLS), bf16),
        compiler_params=_cparams("arbitrary", "arbitrary"),
        name="prep_w_in",
    )(w_in)


def _one_seq(ref, bi):
    return ref.at[pl.ds(bi, 1)]


def _interleave(chunks):
    live = list(chunks)
    while live:
        nxt = []
        for g in live:
            try:
                next(g)
                nxt.append(g)
            except StopIteration:
                pass
        live = nxt


def _mlstm_prompt_chunk(ml_ref, g_ref, gb_ref, nw_ref, out_ref, ct_s, nr_s, m_s):
    L = PROMPT_CHUNK
    row = _iota2((L, L), 0)
    col = _iota2((L, L), 1)
    causal = row <= col
    tri_b = _ones_where(causal)
    top = row < HEAD_W
    lo_half = col < HEAD_W
    bd_mask = _same_head(row, col)
    sub8 = _iota2((8, L), 0)
    lo8 = _iota2((8, L), 1) < HEAD_W
    n_mask = jnp.where(sub8 == 0, jnp.where(lo8, 1.0, 0.0), jnp.where(sub8 == 1, jnp.where(lo8, 0.0, 1.0), 0.0))

    g_t = (g_ref[0] + gb_ref[...]).T
    b_t = None
    parts = _split_bf16(jax.nn.log_sigmoid(g_t[0:GATE_ROWS, :]), 3)
    yield
    for part in parts:
        d = _dot(part, tri_b)
        b_t = d if b_t is None else b_t + d
        yield
    z_rows = b_t[ML_HEADS:ML_HEADS + 8, :] - g_t[0:8, :]
    z_nat = jnp.concatenate([z_rows, jnp.zeros((L - 8, L), f32)], axis=0).T
    m_old = m_s[...]
    m_next = m_old
    yield

    for p in range(ML_HEADS // 2):
        sl = slice(p * LANES, (p + 1) * LANES)
        q2 = ml_ref[0, :, sl]
        k2 = ml_ref[0, :, ML_W + p * LANES:ML_W + (p + 1) * LANES] * (HEAD_W ** -0.5)
        v2 = ml_ref[0, :, 2 * ML_W + p * LANES:2 * ML_W + (p + 1) * LANES]
        o2 = ml_ref[0, :, 3 * ML_W + p * LANES:3 * ML_W + (p + 1) * LANES]
        q_t = q2.T.astype(bf16)
        v_t = v2.T
        kb = k2.astype(bf16)
        k_sel = (jnp.where(lo_half, k2, 0.0).astype(bf16), jnp.where(lo_half, 0.0, k2).astype(bf16))
        sw, den, a_int, e_m, a_st, w_s = [], [], [], [], [], []
        yield
        for j in range(2):
            h = 2 * p + j
            b_row = b_t[ML_HEADS + h:ML_HEADS + h + 1, :]
            i_row = g_t[h:h + 1, :]
            m_prev = m_old[h:h + 1, :]
            dm = jnp.where(causal, b_row - z_nat[:, h:h + 1], NEG_INF)
            inter = b_row + m_prev
            raw = _dot(k_sel[j], q_t)
            yield
            m_t = jnp.maximum(inter, jnp.max(dm, axis=0, keepdims=True))
            yield
            s_w = raw * jnp.exp(dm - m_t)
            den.append(jnp.sum(s_w, axis=0, keepdims=True))
            sw.append(s_w.astype(bf16))
            yield
            a_int.append(jnp.exp(inter - m_t))
            e_m.append(jnp.exp(-m_t))
            m_new = jnp.broadcast_to(m_t[:, L - 1:L], (1, L))
            b_last = jnp.broadcast_to(b_row[:, L - 1:L], (1, L))
            a_st.append(jnp.exp(b_last + m_prev - m_new))
            w_s.append(jnp.exp(b_last - b_row + i_row - m_new))
            m_next = jnp.where(sub8 == h, m_new, m_next)
            yield
        ct = ct_s[p]
        nr = nr_s[p]
        qn = _dot(nr.astype(bf16), q_t)
        num0 = _dot(jnp.where(top, v_t, 0.0).astype(bf16), sw[0])
        yield
        num1 = _dot(jnp.where(top, 0.0, v_t).astype(bf16), sw[1])
        qc = _dot(ct.astype(bf16), q_t)
        yield
        num = num0 + num1 + jnp.where(top, a_int[0], a_int[1]) * qc
        den0 = den[0] + a_int[0] * qn[0:1, :]
        den1 = den[1] + a_int[1] * qn[1:2, :]
        hh = num / jnp.where(top, jnp.maximum(jnp.abs(den0), e_m[0]), jnp.maximum(jnp.abs(den1), e_m[1]))
        yield
        sq = hh * hh
        r0 = lax.rsqrt(jnp.sum(sq[0:HEAD_W, :], axis=0, keepdims=True) * (1.0 / HEAD_W) + EPS)
        r1 = lax.rsqrt(jnp.sum(sq[HEAD_W:LANES, :], axis=0, keepdims=True) * (1.0 / HEAD_W) + EPS)
        y = (hh * jnp.where(top, r0, r1)).T * nw_ref[:, sl] * jax.nn.sigmoid(o2)
        out_ref[0, :, sl] = y.astype(bf16)
        yield
        vw = (v_t * jnp.where(top, w_s[0], w_s[1])).astype(bf16)
        ct_s[p] = jnp.where(top, a_st[0], a_st[1]) * ct + jnp.where(bd_mask, _dot(vw, kb), 0.0)
        w8 = jnp.where(sub8 == 0, w_s[0], jnp.where(sub8 == 1, w_s[1], 0.0)).astype(bf16)
        nr_s[p] = jnp.where(sub8 == 0, a_st[0], a_st[1]) * nr + n_mask * _dot(w8, kb)
        yield
    m_s[...] = m_next


DT_LANE = 2 * ML_HEADS
SSD_GATE_ROW0 = DT_LANE // 8 * 8
SSD_HEAD_ROW = DT_LANE - SSD_GATE_ROW0
HALO = 8


def _ssd_prompt_chunk(ssd_ref, g_ref, gb_ref, al_ref, d_ref, nw_ref, out_ref, h_s):
    L = PROMPT_CHUNK
    xc = ssd_ref[0, :, SSD_W:SSD_COLS]
    nb = SSD_W
    b_b = [xc[:, nb + gi * LANES:nb + (gi + 1) * LANES].astype(bf16) for gi in range(2)]
    c_b = [xc[:, nb + (2 + gi) * LANES:nb + (3 + gi) * LANES].astype(bf16) for gi in range(2)]

    row = _iota2((L, L), 0)
    col = _iota2((L, L), 1)
    tril = col <= row
    lo_half = col < HEAD_W
    triu_b = _ones_where(row <= col)
    bd_b = _ones_where(_same_head(row, col))
    rowc = _iota2((L, 1), 0)

    g_t = (g_ref[0] + gb_ref[...]).T
    dt_t = jax.nn.softplus(g_t[SSD_GATE_ROW0:SSD_GATE_ROW0 + GATE_ROWS, :])
    yield
    cum_t = None
    for part in _split_bf16(dt_t * (-jnp.exp(al_ref[...])), 3):
        d = _dot(part, triu_b)
        cum_t = d if cum_t is None else cum_t + d
    yield
    nat = jnp.concatenate([cum_t, dt_t, cum_t * LOG2_E, jnp.zeros((L - 3 * GATE_ROWS, L), f32)], axis=0).T
    cb_g = [_dot_nt(c_b[gi], b_b[gi]) for gi in range(2)]
    yield

    for p in range(SSD_HEADS // 2):
        sl = slice(p * LANES, (p + 1) * LANES)
        x2 = xc[:, sl]
        z2 = ssd_ref[0, :, sl]
        hp = h_s[p]
        hb = hp.astype(bf16)
        grp = [(2 * p + j) // (SSD_HEADS // 2) for j in range(2)]
        ys = None
        ecum, w_s, a_last, ch = [], [], [], []
        for j in range(2):
            r = SSD_HEAD_ROW + 2 * p + j
            cum_col = nat[:, r:r + 1]
            cum_last = cum_col[L - 1:L, :]
            key_row = cum_t[r:r + 1, :] * LOG2_E - jnp.log2(dt_t[r:r + 1, :])
            lm_dt = jnp.exp2(jnp.where(tril, nat[:, 2 * GATE_ROWS + r:2 * GATE_ROWS + r + 1] - key_row, NEG_INF))
            m = (cb_g[grp[j]] * lm_dt).astype(bf16)
            yield
            xm = jnp.where(lo_half, x2, 0.0) if j == 0 else jnp.where(lo_half, 0.0, x2)
            d = _dot(m, xm.astype(bf16))
            ys = d if ys is None else ys + d
            ecum.append(jnp.exp(cum_col))
            w_s.append(jnp.exp(cum_last - cum_col) * nat[:, GATE_ROWS + r:GATE_ROWS + r + 1])
            a_last.append(jnp.exp(cum_last))
            if j == 0 or grp[1] != grp[0]:
                ch.append(_dot_nt(c_b[grp[j]], hb))
            else:
                ch.append(ch[0])
            yield
        ys = ys + jnp.where(lo_half, ecum[0] * ch[0], ecum[1] * ch[1])
        xw_t = (x2 * jnp.where(lo_half, w_s[0], w_s[1])).T.astype(bf16)
        yield
        if grp[0] == grp[1]:
            upd = _dot(xw_t, b_b[grp[0]])
        else:
            upd = jnp.where(row < HEAD_W, _dot(xw_t, b_b[grp[0]]), _dot(xw_t, b_b[grp[1]]))
        h_s[p] = jnp.where(rowc < HEAD_W, a_last[0], a_last[1]) * hp + upd
        yy = (ys + d_ref[:, sl] * x2) * _silu(z2)
        yield
        ms = _group_mean(yy * yy, bd_b)
        yield
        out_ref[0, :, sl] = (yy * lax.rsqrt(ms + EPS) * nw_ref[:, sl]).astype(bf16)


def _hg_lower_bound(lbp, layer):
    mx = jnp.max(lbp, axis=0, keepdims=True)
    e = jnp.exp(lbp - mx)
    den = jnp.sum(e, axis=0, keepdims=True)
    lb = jnp.zeros_like(den)
    for j in range(1, layer + 1):
        lb = lb + e[j:j + 1, :] / den
    return lb


def _hgrn_prompt_chunk(hg_ref, lbp_ref, nw_ref, out_ref, sbd_s, p_s, r_s, o_s, g_s, k_s, v_s, q_s, kt_s, a_s, *, layer):
    L = HG_CHUNK
    nblk = L // 8
    lb = _hg_lower_bound(lbp_ref[...], layer)
    ff = hg_ref[0, :, HG_W:2 * HG_W]
    sig = jax.nn.sigmoid(ff)
    log2_k = jnp.log2((1.0 - lb) * jax.nn.sigmoid(-ff))
    v_s[...] = hg_ref[0, :, 2 * HG_W:3 * HG_W]
    q_s[...] = hg_ref[0, :, 0:HG_W] * (HEAD_W ** -0.5)
    row = _iota2((L, L), 0)
    col = _iota2((L, L), 1)
    tri_b = _ones_where(col <= row)
    yield
    g2 = _cumsum_rows(jnp.log(lb + (1.0 - lb) * sig), tri_b) * LOG2_E
    g_s[...] = g2
    k_s[...] = g2 - log2_k
    yield

    r2 = _iota2((2 * LANES, 2 * LANES), 0)
    c2 = _iota2((2 * LANES, 2 * LANES), 1)
    bd4_b = _ones_where(_same_head(r2, c2))
    r1 = _iota2((LANES, LANES), 0)
    c1 = _iota2((LANES, LANES), 1)
    bd_mask = _same_head(r1, c1)
    bd_b = _ones_where(bd_mask)

    a_s[0:8, :] = jnp.zeros((8, HG_W), f32)
    for i in range(1, nblk):
        blk = slice(8 * i, 8 * i + 8)
        n = 8 * i
        r_i = g_s[n - 1:n, :]
        kt = jnp.exp2(r_i - k_s[0:n, :])
        first_head = jnp.bitwise_and(_iota2((n, HG_W), 1), LANES - 1) < HEAD_W
        kt_s[i - 1, 0:n, :] = jnp.where(first_head, kt, 0.0)
        kt_s[i - 1, n:L, :] = jnp.zeros((L - n, HG_W), f32)
        kt_s[i - 1, L:L + n, :] = jnp.where(first_head, 0.0, kt)
        kt_s[i - 1, L + n:2 * L, :] = jnp.zeros((L - n, HG_W), f32)
        qt = q_s[blk, :] * jnp.exp2(g_s[blk, :] - r_i)
        qt = jnp.concatenate([qt, jnp.zeros_like(qt)], axis=0).astype(bf16)
        for p in range(HG_HEADS // 2):
            sl = slice(p * LANES, (p + 1) * LANES)
            a_s[blk, sl] = _dot_nt(qt[:, sl], kt_s[i - 1, :, sl].astype(bf16))[0:8, :]
        yield

    sub = _iota2((8, HG_W), 0)
    for j in range(nblk):
        blk = slice(8 * j, 8 * j + 8)
        g_blk, gk_blk, q_blk = g_s[blk, :], k_s[blk, :], q_s[blk, :]
        for r in range(8):
            s = 8 * j + r
            p_s[8 * s:8 * s + 8, :] = q_blk * jnp.exp2(jnp.where(sub >= r, g_blk - gk_blk[r:r + 1, :], NEG_INF))
        yield
    r_s[...] = _dot(p_s[...].astype(bf16), bd4_b)
    yield

    gg = g_s[...]
    vv = v_s[...]
    qd = q_s[...] * jnp.exp2(gg)
    lo_half = _iota2((L, LANES), 1) < HEAD_W
    for p in range(HG_HEADS // 2):
        sl = slice(p * LANES, (p + 1) * LANES)
        v_cat = jnp.concatenate([jnp.where(lo_half, vv[:, sl], 0.0), jnp.where(lo_half, 0.0, vv[:, sl])], axis=0)
        o_s[:, sl] = (_dot(qd[:, sl].astype(bf16), sbd_s[p].astype(bf16))
                      + _dot(a_s[:, sl].astype(bf16), v_cat.astype(bf16)))
        yield
    for j in range(nblk):
        blk = slice(8 * j, 8 * j + 8)
        acc = o_s[blk, :]
        v_blk = v_s[blk, :]
        for r in range(8):
            s = 8 * j + r
            acc = acc + r_s[8 * s:8 * s + 8, :] * v_blk[r:r + 1, :]
        o_s[blk, :] = acc
        yield

    g_last = gg[L - 1:L, :]
    kd = jnp.exp2(g_last - k_s[...])
    zpad = jnp.zeros((LANES - L, LANES), f32)
    og = o_s[...]
    gate = hg_ref[0, :, 3 * HG_W:4 * HG_W]
    for p in range(HG_HEADS // 2):
        sl = slice(p * LANES, (p + 1) * LANES)
        kd_t = jnp.concatenate([kd[:, sl], zpad], axis=0).T.astype(bf16)
        v_pad = jnp.concatenate([vv[:, sl], zpad], axis=0).astype(bf16)
        g_t = jnp.concatenate([gg[:, sl], zpad], axis=0).T
        dec = jnp.exp2(g_t[:, L - 1:L])
        sbd_s[p] = dec * sbd_s[p] + jnp.where(bd_mask, _dot(kd_t, v_pad), 0.0)
        yield
        oo = og[:, sl]
        ms = _group_mean(oo * oo, bd_b)
        out_ref[0, :, sl] = (oo * lax.rsqrt(ms + EPS) * nw_ref[:, sl] * _silu(gate[:, sl])).astype(bf16)


def _mixers_prompt_body(ml_ref, ssd_ref, hg_ref, g_ref, gb_ref, al_ref, d_ref, mlnw_ref, ssnw_ref,
                        lbp_ref, hgnw_ref,
                        hm_ref, c1_ref, n1_ref, m1_ref, hs_ref, h1_ref, ho_ref, s1_ref,
                        ct_s, nr_s, m_s, h_s, sbd_s, p_s, r_s, o_s, gg_s, k_s, v_s, q_s, kt_s, a_s, *, layer):
    c = pl.program_id(1)
    seqs = range(ml_ref.shape[0])

    @pl.when(c == 0)
    def _():
        ct_s[...] = jnp.zeros(ct_s.shape, f32)
        nr_s[...] = jnp.zeros(nr_s.shape, f32)
        m_s[...] = jnp.zeros(m_s.shape, f32)
        h_s[...] = jnp.zeros(h_s.shape, f32)
        sbd_s[...] = jnp.zeros(sbd_s.shape, f32)

    def hgrn_chunks(bi):
        for half in range(PROMPT_CHUNK // HG_CHUNK):
            rows = pl.ds(half * HG_CHUNK, HG_CHUNK)
            yield from _hgrn_prompt_chunk(hg_ref.at[pl.ds(bi, 1), rows], lbp_ref, hgnw_ref, ho_ref.at[pl.ds(bi, 1), rows],
                                          sbd_s.at[bi], p_s.at[bi], r_s.at[bi], o_s.at[bi], gg_s.at[bi], k_s.at[bi],
                                          v_s.at[bi], q_s.at[bi], kt_s.at[bi], a_s.at[bi], layer=layer)

    chunks = []
    for bi in seqs:
        chunks.append(_ssd_prompt_chunk(_one_seq(ssd_ref, bi), _one_seq(g_ref, bi), gb_ref, al_ref, d_ref,
                                        ssnw_ref, _one_seq(hs_ref, bi), h_s.at[bi]))
        chunks.append(hgrn_chunks(bi))
        chunks.append(_mlstm_prompt_chunk(_one_seq(ml_ref, bi), _one_seq(g_ref, bi), gb_ref, mlnw_ref,
                                          _one_seq(hm_ref, bi), ct_s.at[bi], nr_s.at[bi], m_s.at[bi]))
    _interleave(chunks)

    @pl.when(c == pl.num_programs(1) - 1)
    def _():
        h1_ref[...] = h_s[...]
        for bi in seqs:
            for p in range(ML_HEADS // 2):
                c_pair = ct_s[bi, p].T
                c1_ref[bi, 2 * p] = c_pair[0:HEAD_W, 0:HEAD_W]
                c1_ref[bi, 2 * p + 1] = c_pair[HEAD_W:LANES, HEAD_W:LANES]
                n1_ref[bi, 2 * p:2 * p + 1, :] = nr_s[bi, p, 0:1, 0:HEAD_W]
                n1_ref[bi, 2 * p + 1:2 * p + 2, :] = nr_s[bi, p, 1:2, HEAD_W:LANES]
            m1_ref[bi] = m_s[bi]
            for p in range(HG_HEADS // 2):
                sbd = sbd_s[bi, p]
                s1_ref[bi, 2 * p] = sbd[0:HEAD_W, 0:HEAD_W]
                s1_ref[bi, 2 * p + 1] = sbd[HEAD_W:LANES, HEAD_W:LANES]


def _mixers_prompt(ml, ssd, hg, gates, gate_bias, alog_col, d_row, ml_nw, ssd_nw, lbp, hg_nw, layer):
    b, t, _ = ml.shape
    L = PROMPT_CHUNK
    ns = SEQS_PER_STEP
    assert b % ns == 0 and t % L == 0
    ml_pairs, ssd_pairs, hg_pairs = ML_HEADS // 2, SSD_HEADS // 2, HG_HEADS // 2
    tok = lambda w: pl.BlockSpec((ns, L, w), lambda i, c: (i, c, 0))
    per_seq = lambda *shape: pl.BlockSpec((ns,) + shape, lambda i, c: (i,) + (0,) * len(shape))
    hg_rows = pltpu.VMEM((ns, HG_CHUNK, HG_W), f32)
    hg_pairs_buf = pltpu.VMEM((ns, 8 * HG_CHUNK, HG_W), f32)
    return pl.pallas_call(
        functools.partial(_mixers_prompt_body, layer=layer),
        grid=(b // ns, t // L),
        in_specs=[tok(ML_COLS), tok(SSD_COLS), tok(HG_COLS), tok(GATE_W),
                  _layer_resident((1, GATE_W), layer), _layer_resident((GATE_ROWS, 1), layer),
                  _layer_resident((1, SSD_W), layer), _layer_resident((1, ML_W), layer),
                  _layer_resident((1, SSD_W), layer), _resident((DEPTH, HG_W)), _layer_resident((1, HG_W), layer)],
        out_specs=[tok(ML_W), per_seq(ML_HEADS, HEAD_W, HEAD_W), per_seq(ML_HEADS, HEAD_W), per_seq(8, LANES),
                   tok(SSD_W), per_seq(ssd_pairs, LANES, SSD_DSTATE),
                   tok(HG_W), per_seq(HG_HEADS, HEAD_W, HEAD_W)],
        out_shape=[jax.ShapeDtypeStruct((b, t, ML_W), bf16),
                   jax.ShapeDtypeStruct((b, ML_HEADS, HEAD_W, HEAD_W), f32),
                   jax.ShapeDtypeStruct((b, ML_HEADS, HEAD_W), f32),
                   jax.ShapeDtypeStruct((b, 8, LANES), f32),
                   jax.ShapeDtypeStruct((b, t, SSD_W), bf16),
                   jax.ShapeDtypeStruct((b, ssd_pairs, LANES, SSD_DSTATE), f32),
                   jax.ShapeDtypeStruct((b, t, HG_W), bf16),
                   jax.ShapeDtypeStruct((b, HG_HEADS, HEAD_W, HEAD_W), f32)],
        scratch_shapes=[pltpu.VMEM((ns, ml_pairs, LANES, LANES), f32), pltpu.VMEM((ns, ml_pairs, 8, LANES), f32),
                        pltpu.VMEM((ns, 8, LANES), f32),
                        pltpu.VMEM((ns, ssd_pairs, LANES, SSD_DSTATE), f32),
                        pltpu.VMEM((ns, hg_pairs, LANES, LANES), f32), hg_pairs_buf, hg_pairs_buf,
                        hg_rows, hg_rows, hg_rows, hg_rows, hg_rows,
                        pltpu.VMEM((ns, HG_CHUNK // 8 - 1, 2 * HG_CHUNK, HG_W), f32), hg_rows],
        compiler_params=_cparams("arbitrary", "arbitrary"),
        name="mixers_prompt",
    )(ml, ssd, hg, gates, gate_bias, alog_col, d_row, ml_nw, ssd_nw, lbp, hg_nw)


NB = 128
TS = 8


def _sample_call(body, name, grid, in_specs, operands, out_specs, out_shapes, scratch, n_state):
    first = len(operands) - n_state
    return pl.pallas_call(
        body, grid=grid, in_specs=in_specs, out_specs=out_specs, out_shape=out_shapes,
        scratch_shapes=scratch, input_output_aliases={first + i: 1 + i for i in range(n_state)},
        compiler_params=_cparams(*(["arbitrary"] * len(grid))), name=name,
    )(*operands)


def _mlstm_sample_body(q_ref, k_ref, v_ref, o_ref, g_ref, gb_ref, nw_ref, c0_ref, n0_ref, m0_ref,
                       out_ref, c1_ref, n1_ref, m1_ref, qt_s, kt_s, vt_s, gt_s, ht_s):
    p = pl.program_id(0)
    n_cur = [n0_ref[hl] for hl in range(2)]
    m_cur = [m0_ref[pl.ds(2 * p + hl, 1), :] for hl in range(2)]

    for t in range(TS):
        c_src = c0_ref if t == 0 else c1_ref
        rows = slice(t * NB, (t + 1) * NB)
        qt_s[...] = q_ref[rows, :].T
        kt_s[...] = k_ref[rows, :].T * (HEAD_W ** -0.5)
        vt_s[...] = v_ref[rows, :].T
        gt_s[...] = (g_ref[rows, :] + gb_ref[...]).T
        for hl in range(2):
            h = 2 * p + hl
            hs = slice(hl * HEAD_W, (hl + 1) * HEAD_W)
            i_t = gt_s[pl.ds(h, 1), :]
            lf = jax.nn.log_sigmoid(gt_s[pl.ds(ML_HEADS + h, 1), :])
            m_new = jnp.maximum(lf + m_cur[hl], i_t)
            a = jnp.exp(lf + m_cur[hl] - m_new)
            w = jnp.exp(i_t - m_new)
            n_new = a * n_cur[hl] + w * kt_s[hs, :]
            v_h = vt_s[hs, :]

            def body(d, num, hl=hl, a=a, w=w, v_h=v_h, c_src=c_src):
                kd = kt_s[pl.ds(hl * HEAD_W + d, 1), :] * w
                qd = qt_s[pl.ds(hl * HEAD_W + d, 1), :]
                cd = a * c_src[hl, d] + kd * v_h
                c1_ref[hl, d] = cd
                return num + qd * cd

            num = lax.fori_loop(0, HEAD_W, body, jnp.zeros((HEAD_W, NB), f32), unroll=4)
            den = jnp.sum(qt_s[hs, :] * n_new, axis=0, keepdims=True)
            hh = num / jnp.maximum(jnp.abs(den), jnp.exp(-m_new))
            ms = jnp.mean(hh * hh, axis=0, keepdims=True)
            ht_s[hs, :] = hh * lax.rsqrt(ms + EPS)
            n_cur[hl] = n_new
            m_cur[hl] = m_new
        out_ref[rows, :] = (ht_s[...].T * nw_ref[...] * jax.nn.sigmoid(o_ref[rows, :])).astype(bf16)

    for hl in range(2):
        n1_ref[hl] = n_cur[hl]
        m1_ref[pl.ds(2 * p + hl, 1), :] = m_cur[hl]


def _mlstm_sample(ml, gates, gate_bias, norm_w, c0t, n0t, m0t, layer):
    n = ml.shape[0]
    npair = ML_HEADS // 2
    blk = lambda off: pl.BlockSpec((n, LANES), lambda p, off=off: (0, off + p))
    c_spec = pl.BlockSpec((None, 2, HEAD_W, HEAD_W, NB), lambda p: (layer, p, 0, 0, 0))
    n_spec = pl.BlockSpec((None, 2, HEAD_W, NB), lambda p: (layer, p, 0, 0))
    m_spec = pl.BlockSpec((None, ML_HEADS, NB), lambda p: (layer, 0, 0))
    return _sample_call(
        _mlstm_sample_body, "mlstm_sample", (npair,),
        [blk(0), blk(npair), blk(2 * npair), blk(3 * npair), _resident((n, GATE_W)),
         _layer_resident((1, GATE_W), layer), pl.BlockSpec((None, 1, LANES), lambda p: (layer, 0, p)),
         c_spec, n_spec, m_spec],
        [ml, ml, ml, ml, gates, gate_bias, norm_w, c0t, n0t, m0t],
        [pl.BlockSpec((n, LANES), lambda p: (0, p)), c_spec, n_spec, m_spec],
        [jax.ShapeDtypeStruct((n, ML_W), bf16), jax.ShapeDtypeStruct(c0t.shape, f32),
         jax.ShapeDtypeStruct(n0t.shape, f32), jax.ShapeDtypeStruct(m0t.shape, f32)],
        [pltpu.VMEM((LANES, NB), f32)] * 5, 3)


def _ssd_conv_sample_body(ssd_ref, halo_ref, cw_ref, cb_ref, out_ref, xp_s):
    nh = (SSD_CONV - 1) * NB
    n = TS * NB
    xp_s[0:nh, :] = halo_ref[...]
    xp_s[nh:nh + n, :] = ssd_ref[:, SSD_W:SSD_COLS]
    conv = cb_ref[...]
    for j in range(SSD_CONV):
        conv = conv + cw_ref[j:j + 1, :] * xp_s[j * NB:j * NB + n, :]
    out_ref[...] = _silu(conv)


def _ssd_conv_sample(ssd, halo, conv_w, conv_b, layer):
    n = ssd.shape[0]
    nh = (SSD_CONV - 1) * NB
    return pl.pallas_call(
        _ssd_conv_sample_body,
        grid=(1,),
        in_specs=[_resident((n, SSD_COLS)), _layer_resident((nh, SSD_CONV_DIM), layer),
                  _layer_resident((SSD_CONV, SSD_CONV_DIM), layer), _layer_resident((1, SSD_CONV_DIM), layer)],
        out_specs=pl.BlockSpec((n, SSD_CONV_DIM), lambda i: (0, 0)),
        out_shape=jax.ShapeDtypeStruct((n, SSD_CONV_DIM), f32),
        scratch_shapes=[pltpu.VMEM((nh + n, SSD_CONV_DIM), f32)],
        compiler_params=_cparams("arbitrary"),
        name="ssd_conv_sample",
    )(ssd, halo, conv_w, conv_b)


def _ssd_sample_body(z_ref, x_ref, b0_ref, b1_ref, c0_ref, c1_ref, g_ref, gb_ref, hp_ref, nw_ref, h0_ref,
                     out_ref, h1_ref, xt_s, bt_s, ct_s, gt_s, yt_s):
    p = pl.program_id(0)
    b_refs = (b0_ref, b1_ref)
    c_refs = (c0_ref, c1_ref)

    for t in range(TS):
        h_src = h0_ref if t == 0 else h1_ref
        rows = slice(t * NB, (t + 1) * NB)
        xt_s[...] = x_ref[rows, :].T
        gt_s[...] = (g_ref[rows, :] + gb_ref[...]).T
        zt = z_ref[rows, :].T
        for hl in range(2):
            bt_s[hl] = b_refs[hl][rows, :].T
            ct_s[hl] = c_refs[hl][rows, :].T
        for hl in range(2):
            h = 2 * p + hl
            hs = slice(hl * HEAD_W, (hl + 1) * HEAD_W)
            dt = jax.nn.softplus(gt_s[pl.ds(DT_LANE + h, 1), :])
            da = jnp.exp(dt * (-jnp.exp(hp_ref[0, hl:hl + 1, :])))

            def body(pp, carry, hl=hl, dt=dt, da=da, h_src=h_src):
                xrow = xt_s[pl.ds(hl * HEAD_W + pp, 1), :] * dt
                hn = da * h_src[hl, pp] + xrow * bt_s[hl]
                h1_ref[hl, pp] = hn
                yt_s[pl.ds(hl * HEAD_W + pp, 1), :] = jnp.sum(ct_s[hl] * hn, axis=0, keepdims=True)
                return carry

            lax.fori_loop(0, HEAD_W, body, 0, unroll=8)
            yy = (yt_s[hs, :] + hp_ref[1, hl:hl + 1, :] * xt_s[hs, :]) * _silu(zt[hs, :])
            ms = jnp.mean(yy * yy, axis=0, keepdims=True)
            yt_s[hs, :] = yy * lax.rsqrt(ms + EPS)
        out_ref[rows, :] = (yt_s[...].T * nw_ref[...]).astype(bf16)


def _ssd_sample(ssd, xc, gates, gate_bias, head_params, norm_w, h0t, layer):
    n = ssd.shape[0]
    npair = SSD_HEADS // 2
    hpg = SSD_HEADS // 2
    nxb = SSD_W // LANES
    blk = lambda f: pl.BlockSpec((n, LANES), f)
    h_spec = pl.BlockSpec((None, 2, HEAD_W, SSD_DSTATE, NB), lambda p: (layer, p, 0, 0, 0))
    return _sample_call(
        _ssd_sample_body, "ssd_sample", (npair,),
        [blk(lambda p: (0, p)), blk(lambda p: (0, p)),
         blk(lambda p: (0, nxb + (2 * p) // hpg)), blk(lambda p: (0, nxb + (2 * p + 1) // hpg)),
         blk(lambda p: (0, nxb + 2 + (2 * p) // hpg)), blk(lambda p: (0, nxb + 2 + (2 * p + 1) // hpg)),
         _resident((n, GATE_W)), _layer_resident((1, GATE_W), layer),
         pl.BlockSpec((None, 2, None, 2, LANES), lambda p: (layer, 0, p, 0, 0)),
         pl.BlockSpec((None, 1, LANES), lambda p: (layer, 0, p)), h_spec],
        [ssd, xc, xc, xc, xc, xc, gates, gate_bias, head_params, norm_w, h0t],
        [pl.BlockSpec((n, LANES), lambda p: (0, p)), h_spec],
        [jax.ShapeDtypeStruct((n, SSD_W), bf16), jax.ShapeDtypeStruct(h0t.shape, f32)],
        [pltpu.VMEM((LANES, NB), f32), pltpu.VMEM((2, SSD_DSTATE, NB), f32), pltpu.VMEM((2, SSD_DSTATE, NB), f32),
         pltpu.VMEM((LANES, NB), f32), pltpu.VMEM((LANES, NB), f32)], 1)


def _hgrn_sample_body(q_ref, f_ref, i_ref, gate_ref, lbp_ref, nw_ref, s0_ref, out_ref, s1_ref,
                      qt_s, kt_s, ft_s, vt_s, ot_s, *, layer):
    lb = _hg_lower_bound(lbp_ref[...], layer)

    for t in range(TS):
        s_src = s0_ref if t == 0 else s1_ref
        rows = slice(t * NB, (t + 1) * NB)
        ff = f_ref[rows, :]
        qt_s[...] = (q_ref[rows, :] * (HEAD_W ** -0.5)).T
        kt_s[...] = ((1.0 - lb) * jax.nn.sigmoid(-ff)).T
        ft_s[...] = (lb + (1.0 - lb) * jax.nn.sigmoid(ff)).T
        vt_s[...] = i_ref[rows, :].T
        for hl in range(2):
            hs = slice(hl * HEAD_W, (hl + 1) * HEAD_W)
            v_h = vt_s[hs, :]

            def body(k, o, hl=hl, v_h=v_h, s_src=s_src):
                r = pl.ds(hl * HEAD_W + k, 1)
                sk = ft_s[r, :] * s_src[hl, k] + kt_s[r, :] * v_h
                s1_ref[hl, k] = sk
                return o + qt_s[r, :] * sk

            o = lax.fori_loop(0, HEAD_W, body, jnp.zeros((HEAD_W, NB), f32), unroll=4)
            ms = jnp.mean(o * o, axis=0, keepdims=True)
            ot_s[hs, :] = o * lax.rsqrt(ms + EPS)
        out_ref[rows, :] = (ot_s[...].T * nw_ref[...] * _silu(gate_ref[rows, :])).astype(bf16)


def _hgrn_sample(hg, lbp, norm_w, s0t, layer):
    n = hg.shape[0]
    npair = HG_HEADS // 2
    blk = lambda off: pl.BlockSpec((n, LANES), lambda p, off=off: (0, off + p))
    s_spec = pl.BlockSpec((None, 2, HEAD_W, HEAD_W, NB), lambda p: (layer, p, 0, 0, 0))
    return _sample_call(
        functools.partial(_hgrn_sample_body, layer=layer), "hgrn_sample", (npair,),
        [blk(0), blk(npair), blk(2 * npair), blk(3 * npair),
         pl.BlockSpec((DEPTH, LANES), lambda p: (0, p)),
         pl.BlockSpec((None, 1, LANES), lambda p: (layer, 0, p)), s_spec],
        [hg, hg, hg, hg, lbp, norm_w, s0t],
        [pl.BlockSpec((n, LANES), lambda p: (0, p)), s_spec],
        [jax.ShapeDtypeStruct((n, HG_W), bf16), jax.ShapeDtypeStruct(s0t.shape, f32)],
        [pltpu.VMEM((LANES, NB), f32)] * 5, 1)


def _ffn_body(x_ref, mml_ref, mssd_ref, mhg_ref, wout_ref, n2_ref, wug_ref, wuv_ref, cwg_ref, cwv_ref,
              cbg_ref, cbv_ref, wdn_ref, f0g_ref, f0v_ref, fn_ref, y_ref, f1g_ref, f1v_ref,
              x1_s, h2_s, acc_s, act_s, ubuf_s, *, tm, shift, jb, nff, final):
    t = pl.program_id(1)
    f = pl.program_id(2)
    halo = max(HALO, (FFN_CONV - 1) * shift)
    keep = (FFN_CONV - 1) * shift

    def when(cond, fn):
        if nff == 1:
            fn()
        else:
            pl.when(cond)(fn)

    @pl.when(t == 0)
    def _():
        f1g_ref[0] = f0g_ref[0]
        f1v_ref[0] = f0v_ref[0]

    def first():
        x = x_ref[...]
        mix = jnp.concatenate([mml_ref[...], mssd_ref[...], mhg_ref[...]], axis=1)
        x1 = x + _dot(mix, wout_ref[...])
        x1_s[...] = x1
        h2_s[...] = (x1 * lax.rsqrt(jnp.mean(x1 * x1, axis=-1, keepdims=True) + EPS) * n2_ref[...]).astype(bf16)

    when(f == 0, first)
    h2 = h2_s[...]

    def conv(wu_ref, cw_ref, cb_ref, f1_ref, cols, ubuf):
        u = _dot(h2, wu_ref[:, cols])
        ubuf[halo - keep:halo, :] = f1_ref[0, :, cols]
        ubuf[halo:halo + tm, :] = u
        yv = (cb_ref[:, cols] + cw_ref[2:3, cols] * u
              + cw_ref[1:2, cols] * ubuf[halo - shift:halo - shift + tm, :]
              + cw_ref[0:1, cols] * ubuf[halo - 2 * shift:halo - 2 * shift + tm, :])
        f1_ref[0, :, cols] = ubuf[halo + tm - keep:halo + tm, :]
        return yv

    nbuf = ubuf_s.shape[0]
    for j in range(jb):
        cols = slice(j * FF_BLOCK, (j + 1) * FF_BLOCK)
        gq = conv(wug_ref, cwg_ref, cbg_ref, f1g_ref, cols, ubuf_s.at[(2 * j) % nbuf])
        vq = conv(wuv_ref, cwv_ref, cbv_ref, f1v_ref, cols, ubuf_s.at[(2 * j + 1) % nbuf])
        act_s[:, cols] = (_silu(gq) * vq).astype(bf16)
    contrib = _dot(act_s[...], wdn_ref[...])

    def finish(down):
        x2 = x1_s[...] + down
        if final:
            x2 = x2 * lax.rsqrt(jnp.mean(x2 * x2, axis=-1, keepdims=True) + EPS) * fn_ref[...]
        y_ref[...] = x2

    if nff == 1:
        finish(contrib)
    else:
        @pl.when(f == 0)
        def _():
            acc_s[...] = contrib

        @pl.when(f != 0)
        def _():
            acc_s[...] = acc_s[...] + contrib

        pl.when(f == nff - 1)(lambda: finish(acc_s[...]))


def _ffn(x, mml, mssd, mhg, w_out, norm2, w_up, conv_w, conv_b, w_down, f0, final_w, *, layer, groups, tm, shift, jb,
         final, f0_base=0):
    n = x.shape[0]
    nt = n // (groups * tm)
    fw = jb * FF_BLOCK
    nff = D_FF // fw
    assert nt == 1 or nff == 1
    halo = max(HALO, (FFN_CONV - 1) * shift)
    keep = (FFN_CONV - 1) * shift
    rows = lambda w: pl.BlockSpec((tm, w), lambda g, t, f: (g * nt + t, 0))
    single = nff == 1
    wspec = lambda shape, imap: pl.BlockSpec(shape, imap, pipeline_mode=pl.Buffered(1)) if single else pl.BlockSpec(shape, imap)
    f1_spec = pl.BlockSpec((1, keep, fw), lambda g, t, f: (g, 0, f))
    in_specs = [rows(D_MODEL), rows(ML_W), rows(SSD_W), rows(HG_W),
                _layer_resident((D_MODEL, D_MODEL), layer), _layer_resident((1, D_MODEL), layer),
                wspec((None, D_MODEL, fw), lambda g, t, f: (layer, 0, f)),
                wspec((None, D_MODEL, fw), lambda g, t, f: (layer, 0, nff + f)),
                wspec((None, FFN_CONV, fw), lambda g, t, f: (layer, 0, f)),
                wspec((None, FFN_CONV, fw), lambda g, t, f: (layer, 0, nff + f)),
                wspec((None, 1, fw), lambda g, t, f: (layer, 0, f)),
                wspec((None, 1, fw), lambda g, t, f: (layer, 0, nff + f)),
                wspec((None, fw, D_MODEL), lambda g, t, f: (layer, f, 0)),
                pl.BlockSpec((1, keep, fw), lambda g, t, f: (f0_base + g, 0, f)),
                pl.BlockSpec((1, keep, fw), lambda g, t, f: (f0_base + g, 0, nff + f)),
                _resident((1, D_MODEL))]
    operands = [x, mml, mssd, mhg, w_out, norm2, w_up, w_up, conv_w, conv_w, conv_b, conv_b, w_down, f0, f0, final_w]
    acc_rows = tm if nff > 1 else 8
    return pl.pallas_call(
        functools.partial(_ffn_body, tm=tm, shift=shift, jb=jb, nff=nff, final=final),
        grid=(groups, nt, nff),
        in_specs=in_specs,
        out_specs=[rows(D_MODEL), f1_spec, f1_spec],
        out_shape=[jax.ShapeDtypeStruct((n, D_MODEL), f32),
                   jax.ShapeDtypeStruct((groups, keep, D_FF), f32),
                   jax.ShapeDtypeStruct((groups, keep, D_FF), f32)],
        scratch_shapes=[pltpu.VMEM((tm, D_MODEL), f32), pltpu.VMEM((tm, D_MODEL), bf16),
                        pltpu.VMEM((acc_rows, D_MODEL), f32), pltpu.VMEM((tm, fw), bf16),
                        pltpu.VMEM((min(4, 2 * jb), halo + tm, FF_BLOCK), f32)],
        compiler_params=_cparams("arbitrary", "arbitrary", "arbitrary"),
        name="ffn",
    )(*operands)


def _gate_rows(*parts):
    r = jnp.concatenate([p.astype(f32) for p in parts], axis=1)
    return jnp.pad(r, ((0, 0), (0, GATE_W - r.shape[1])))[:, None, :]


def kernel(x_prompt, x_sample, state_mlstm_C, state_mlstm_n, state_mlstm_m, state_ssd, state_ssd_conv,
           state_hgrn, state_ffn_conv, norm1_w, w_in, ml_ig_b, ml_fg_b, ml_norm_w, ssd_conv_w, ssd_conv_b,
           ssd_dt_bias, ssd_A_log, ssd_D, ssd_norm_w, hg_lower_bounds, hg_norm_w, w_out, norm2_w, w_up,
           ffn_conv_w, ffn_conv_b, w_down, final_norm_w):
    bp, tp, _ = x_prompt.shape
    nb, ts, _ = x_sample.shape
    assert nb == NB and ts == TS and tp % ROW_TILE == 0
    depth = w_in.shape[0]
    assert depth == DEPTH
    xp = x_prompt.reshape(bp * tp, D_MODEL)
    xs = jnp.transpose(x_sample, (1, 0, 2)).reshape(ts * nb, D_MODEL)
    lbp = hg_lower_bounds.astype(f32)
    fin_w = final_norm_w[None, :]
    zero_f = jnp.zeros((bp, FFN_CONV - 1, 2 * D_FF), f32)

    w_cat = _prep_w_in(w_in)
    wo_b, wu_b, wd_b = w_out.astype(bf16), w_up.astype(bf16), w_down.astype(bf16)
    n1, n2 = norm1_w[:, None, :], norm2_w[:, None, :]
    gate_bias = _gate_rows(ml_ig_b, ml_fg_b, ssd_dt_bias)
    alog_col = jnp.pad(ssd_A_log.astype(f32), ((0, 0), (SSD_HEAD_ROW, GATE_ROWS - SSD_HEAD_ROW - SSD_HEADS)))[:, :, None]
    d_row = jnp.repeat(ssd_D.astype(f32), HEAD_W, axis=1)[:, None, :]
    mlw, ssw, hgw = ml_norm_w[:, None, :], ssd_norm_w[:, None, :], hg_norm_w[:, None, :]
    cw, cb = ssd_conv_w, ssd_conv_b[:, None, :]
    fcw, fcb = ffn_conv_w, ffn_conv_b[:, None, :]
    head_params = jnp.broadcast_to(
        jnp.stack([ssd_A_log, ssd_D], axis=1).astype(f32).reshape(depth, 2, SSD_HEADS // 2, 2, 1),
        (depth, 2, SSD_HEADS // 2, 2, LANES))

    c_t = jnp.transpose(state_mlstm_C.astype(f32), (0, 2, 3, 4, 1))
    n_t = jnp.transpose(state_mlstm_n.astype(f32), (0, 2, 3, 1))
    m_t = jnp.transpose(state_mlstm_m.astype(f32), (0, 2, 1))
    h_t = jnp.transpose(state_ssd.astype(f32), (0, 2, 3, 4, 1))
    s_t = jnp.transpose(state_hgrn.astype(f32), (0, 2, 3, 4, 1))
    conv_halo = jnp.transpose(state_ssd_conv.astype(f32), (0, 2, 1, 3)).reshape(depth, (SSD_CONV - 1) * nb, SSD_CONV_DIM)
    f_t = jnp.transpose(state_ffn_conv.astype(f32), (0, 2, 1, 3)).reshape(depth, (FFN_CONV - 1) * nb, 2 * D_FF)

    p_states, s_sconv, s_ffn = [], [], []
    for l in range(depth):
        last = l == depth - 1

        ml, ssd, hg, gates, p_tail = _inproj(xp, n1, w_cat, l, conv=(cw, cb, tp))
        hm, p_c, p_n, p_m, hs, p_h, ho, p_s = _mixers_prompt(
            ml.reshape(bp, tp, ML_COLS), ssd.reshape(bp, tp, SSD_COLS), hg.reshape(bp, tp, HG_COLS),
            gates.reshape(bp, tp, GATE_W), gate_bias, alog_col, d_row, mlw, ssw, lbp, hgw, l)
        xp, p_fg, p_fv = _ffn(xp, hm.reshape(bp * tp, ML_W), hs.reshape(bp * tp, SSD_W), ho.reshape(bp * tp, HG_W),
                              wo_b, n2, wu_b, fcw, fcb, wd_b, zero_f, fin_w, layer=l,
                              groups=bp, tm=ROW_TILE, shift=1, jb=D_FF // FF_BLOCK, final=last)
        p_states.append((p_c, p_n, p_m[:, 0:ML_HEADS, 0],
                         p_h.reshape(bp, SSD_HEADS, HEAD_W, SSD_DSTATE),
                         p_tail[:, HALO - (SSD_CONV - 1):, :], p_s,
                         jnp.concatenate([p_fg, p_fv], axis=-1)))

        ml, ssd, hg, gates = _inproj(xs, n1, w_cat, l)
        hm, c_t, n_t, m_t = _mlstm_sample(ml, gates, gate_bias, mlw, c_t, n_t, m_t, l)
        xc = _ssd_conv_sample(ssd, conv_halo, cw, cb, l)
        hs, h_t = _ssd_sample(ssd, xc, gates, gate_bias, head_params, ssw, h_t, l)
        ho, s_t = _hgrn_sample(hg, lbp, hgw, s_t, l)
        xs, s_fg, s_fv = _ffn(xs, hm, hs, ho, wo_b, n2, wu_b, fcw, fcb, wd_b, f_t, fin_w, layer=l,
                              groups=1, tm=ts * nb, shift=nb, jb=1, final=last, f0_base=l)
        s_ffn.append(jnp.concatenate([s_fg, s_fv], axis=-1).reshape(FFN_CONV - 1, nb, 2 * D_FF))
        raw_xbc = ssd[:, SSD_W:].reshape(ts, nb, SSD_CONV_DIM)
        s_sconv.append(jnp.transpose(raw_xbc[ts - (SSD_CONV - 1):], (1, 0, 2)))

    y_prompt = xp.reshape(bp, tp, D_MODEL)
    y_sample = jnp.transpose(xs.reshape(ts, nb, D_MODEL), (1, 0, 2))
    p_out = [jnp.stack([st[i] for st in p_states]) for i in range(7)]
    s_out = [jnp.transpose(c_t, (0, 4, 1, 2, 3)), jnp.transpose(n_t, (0, 3, 1, 2)), jnp.transpose(m_t, (0, 2, 1)),
             jnp.transpose(h_t, (0, 4, 1, 2, 3)), jnp.stack(s_sconv),
             jnp.transpose(s_t, (0, 4, 1, 2, 3)), jnp.transpose(jnp.stack(s_ffn), (0, 2, 1, 3))]
    return (y_prompt, y_sample, *p_out, *s_out)
```

```python
import functools

import jax
import jax.numpy as jnp
from jax import lax
from jax.experimental import pallas as pl
from jax.experimental.pallas import tpu as pltpu

f32 = jnp.float32
bf16 = jnp.bfloat16

D_MODEL = 1024
DEPTH = 2
ML_HEADS = 6
HEAD_W = 64
ML_W = 384
SSD_HEADS = 6
SSD_W = 384
SSD_DSTATE = 128
SSD_CONV = 4
SSD_CONV_DIM = 896
HG_HEADS = 4
HG_W = 256
D_FF = 2816
FFN_CONV = 3
EPS = 1e-6
PROMPT_CHUNK = 128
HG_CHUNK = 64
LANES = 128
FF_BLOCK = 256
ROW_TILE = 512
SEQS_PER_STEP = 4
GATE_W = 128
GATE_ROWS = 16
W_IN_COLS = 4 * ML_W + (SSD_W + SSD_CONV_DIM) + 4 * HG_W + GATE_W
VMEM_LIMIT_BYTES = 56 * 1024 * 1024

NEG_INF = float("-inf")
LOG2_E = 1.4426950408889634


def _cparams(*sem):
    return pltpu.CompilerParams(dimension_semantics=sem, vmem_limit_bytes=VMEM_LIMIT_BYTES)


def _resident(shape):
    nd = len(shape)
    return pl.BlockSpec(shape, lambda *_: (0,) * nd, pipeline_mode=pl.Buffered(1))


def _layer_resident(shape, layer):
    nd = len(shape)
    return pl.BlockSpec((None,) + tuple(shape), lambda *_: (layer,) + (0,) * nd, pipeline_mode=pl.Buffered(1))


def _dot(a, b):
    return jnp.dot(a, b, preferred_element_type=f32)


def _dot_nt(a, b):
    return lax.dot_general(a, b, (((1,), (1,)), ((), ())), preferred_element_type=f32)


def _split_bf16(x, parts):
    out = []
    r = x
    for _ in range(parts):
        h = r.astype(bf16)
        out.append(h)
        r = r - h.astype(f32)
    return out


def _cumsum_rows(x, tri_b):
    acc = None
    for part in _split_bf16(x, 3):
        d = _dot(tri_b, part)
        acc = d if acc is None else acc + d
    return acc


def _group_mean(x, bd_b):
    return _dot(x.astype(bf16), bd_b) * (1.0 / HEAD_W)


def _iota2(shape, axis):
    return lax.broadcasted_iota(jnp.int32, shape, axis)


def _ones_where(mask):
    return jnp.where(mask, 1.0, 0.0).astype(bf16)


def _same_head(a, b):
    return jnp.right_shift(a, 6) == jnp.right_shift(b, 6)


def _silu(x):
    return x * jax.nn.sigmoid(x)


ML_COLS = 4 * ML_W
SSD_COLS = SSD_W + SSD_CONV_DIM
HG_COLS = 4 * HG_W


def _inproj_body(x_ref, nw_ref, w_ref, ml_ref, ssd_ref, hg_ref, g_ref):
    x = x_ref[...]
    h = (x * lax.rsqrt(jnp.mean(x * x, axis=-1, keepdims=True) + EPS) * nw_ref[...]).astype(bf16)
    o0, o1, o2 = ML_COLS, ML_COLS + SSD_COLS, ML_COLS + SSD_COLS + HG_COLS
    ml_ref[...] = _dot(h, w_ref[:, 0:o0])
    ssd_ref[...] = _dot(h, w_ref[:, o0:o1])
    hg_ref[...] = _dot(h, w_ref[:, o1:o2])
    g_ref[...] = _dot(h, w_ref[:, o2:W_IN_COLS])


def _inproj_conv_body(x_ref, nw_ref, w_ref, cw_ref, cb_ref, ml_ref, ssd_ref, hg_ref, g_ref, tail_ref, xbuf_s, *,
                      tiles_per_seq):
    i = pl.program_id(0)
    tm = x_ref.shape[0]

    @pl.when(lax.rem(i, tiles_per_seq) == 0)
    def _():
        xbuf_s[0:HALO, :] = jnp.zeros((HALO, SSD_CONV_DIM), f32)

    x = x_ref[...]
    h = (x * lax.rsqrt(jnp.mean(x * x, axis=-1, keepdims=True) + EPS) * nw_ref[...]).astype(bf16)
    o0, o1, o2 = ML_COLS, ML_COLS + SSD_COLS, ML_COLS + SSD_COLS + HG_COLS
    zx = _dot(h, w_ref[:, o0:o1])
    ssd_ref[:, 0:SSD_W] = zx[:, 0:SSD_W]
    xbuf_s[HALO:HALO + tm, :] = zx[:, SSD_W:SSD_COLS]
    conv = cb_ref[...]
    for j in range(SSD_CONV):
        o = HALO - (SSD_CONV - 1) + j
        conv = conv + cw_ref[j:j + 1, :] * xbuf_s[o:o + tm, :]
    ssd_ref[:, SSD_W:SSD_COLS] = _silu(conv)
    tail = xbuf_s[tm:tm + HALO, :]
    tail_ref[0] = tail
    xbuf_s[0:HALO, :] = tail
    ml_ref[...] = _dot(h, w_ref[:, 0:o0])
    hg_ref[...] = _dot(h, w_ref[:, o1:o2])
    g_ref[...] = _dot(h, w_ref[:, o2:W_IN_COLS])


def _inproj(x, norm_w, w_cat, layer, conv=None):
    n = x.shape[0]
    tm = ROW_TILE
    row = lambda w: pl.BlockSpec((tm, w), lambda i: (i, 0))
    in_specs = [row(D_MODEL), _layer_resident((1, D_MODEL), layer), _layer_resident((D_MODEL, W_IN_COLS), layer)]
    out_specs = [row(ML_COLS), row(SSD_COLS), row(HG_COLS), row(GATE_W)]
    out_shape = [jax.ShapeDtypeStruct((n, w), f32) for w in (ML_COLS, SSD_COLS, HG_COLS, GATE_W)]
    if conv is None:
        return pl.pallas_call(
            _inproj_body, grid=(n // tm,), in_specs=in_specs, out_specs=out_specs, out_shape=out_shape,
            compiler_params=_cparams("arbitrary"), name="inproj",
        )(x, norm_w, w_cat)
    conv_w, conv_b, rows_per_seq = conv
    tps = rows_per_seq // tm
    assert rows_per_seq % tm == 0
    return pl.pallas_call(
        functools.partial(_inproj_conv_body, tiles_per_seq=tps),
        grid=(n // tm,),
        in_specs=in_specs + [_layer_resident((SSD_CONV, SSD_CONV_DIM), layer), _layer_resident((1, SSD_CONV_DIM), layer)],
        out_specs=out_specs + [pl.BlockSpec((1, HALO, SSD_CONV_DIM), lambda i: (i // tps, 0, 0))],
        out_shape=out_shape + [jax.ShapeDtypeStruct((n // rows_per_seq, HALO, SSD_CONV_DIM), f32)],
        scratch_shapes=[pltpu.VMEM((HALO + tm, SSD_CONV_DIM), f32)],
        compiler_params=_cparams("arbitrary"),
        name="inproj_conv",
    )(x, norm_w, w_cat, conv_w, conv_b)


W_IN_GATE_IF = ML_COLS
W_IN_SSD = W_IN_GATE_IF + 2 * ML_HEADS
W_IN_DT = W_IN_SSD + SSD_COLS
W_IN_HG = W_IN_DT + SSD_HEADS
W_IN_DIM = W_IN_HG + HG_COLS
W_PREP_ROWS = 256


def _prep_w_in_body(w_ref, out_ref):
    lane = _iota2((W_PREP_ROWS, LANES), 1)
    tile = lambda k: w_ref[:, k * LANES:(k + 1) * LANES]
    out_ref[:, 0:ML_COLS] = w_ref[:, 0:ML_COLS].astype(bf16)

    def shifted(dst0, src0, ntiles):
        t0, sh = src0 // LANES, src0 % LANES
        rolled = [pltpu.roll(tile(t0 + k), LANES - sh, axis=1) for k in range(ntiles + 1)]
        for k in range(ntiles):
            out_ref[:, dst0 + k * LANES:dst0 + (k + 1) * LANES] = jnp.where(
                lane < LANES - sh, rolled[k], rolled[k + 1]).astype(bf16)

    shifted(ML_COLS, W_IN_SSD, SSD_COLS // LANES)
    shifted(ML_COLS + SSD_COLS, W_IN_HG, HG_COLS // LANES)
    assert W_IN_GATE_IF % LANES == 0 and W_IN_DT % LANES == DT_LANE
    gates = jnp.where(lane < DT_LANE, tile(W_IN_GATE_IF // LANES),
                      jnp.where(lane < DT_LANE + SSD_HEADS, tile(W_IN_DT // LANES), 0.0))
    out_ref[:, ML_COLS + SSD_COLS + HG_COLS:W_IN_COLS] = gates.astype(bf16)


def _prep_w_in(w_in):
    depth, d, n = w_in.shape
    assert n == W_IN_DIM and d % W_PREP_ROWS == 0
    return pl.pallas_call(
        _prep_w_in_body,
        grid=(depth, d // W_PREP_ROWS),
        in_specs=[pl.BlockSpec((None, W_PREP_ROWS, W_IN_COLS), lambda l, i: (l, i, 0))],
        out_specs=pl.BlockSpec((None, W_PREP_ROWS, W_IN_COLS), lambda l, i: (l, i, 0)),
        out_shape=jax.ShapeDtypeStruct((depth, d, W_IN_COLS), bf16),
        compiler_params=_cparams("arbitrary", "arbitrary"),
        name="prep_w_in",
    )(w_in)


def _one_seq(ref, bi):
    return ref.at[pl.ds(bi, 1)]


def _interleave(chunks):
    live = list(chunks)
    while live:
        nxt = []
        for g in live:
            try:
                next(g)
                nxt.append(g)
            except StopIteration:
                pass
        live = nxt


def _mlstm_prompt_chunk(ml_ref, g_ref, gb_ref, nw_ref, out_ref, ct_s, nr_s, m_s):
    L = PROMPT_CHUNK
    row = _iota2((L, L), 0)
    col = _iota2((L, L), 1)
    causal = row <= col
    tri_b = _ones_where(causal)
    top = row < HEAD_W
    lo_half = col < HEAD_W
    bd_mask = _same_head(row, col)
    sub8 = _iota2((8, L), 0)
    lo8 = _iota2((8, L), 1) < HEAD_W
    n_mask = jnp.where(sub8 == 0, jnp.where(lo8, 1.0, 0.0), jnp.where(sub8 == 1, jnp.where(lo8, 0.0, 1.0), 0.0))

    g_t = (g_ref[0] + gb_ref[...]).T
    b_t = None
    parts = _split_bf16(jax.nn.log_sigmoid(g_t[0:GATE_ROWS, :]), 3)
    yield
    for part in parts:
        d = _dot(part, tri_b)
        b_t = d if b_t is None else b_t + d
        yield
    z_rows = b_t[ML_HEADS:ML_HEADS + 8, :] - g_t[0:8, :]
    z_nat = jnp.concatenate([z_rows, jnp.zeros((L - 8, L), f32)], axis=0).T
    m_old = m_s[...]
    m_next = m_old
    yield

    for p in range(ML_HEADS // 2):
        sl = slice(p * LANES, (p + 1) * LANES)
        q2 = ml_ref[0, :, sl]
        k2 = ml_ref[0, :, ML_W + p * LANES:ML_W + (p + 1) * LANES] * (HEAD_W ** -0.5)
        v2 = ml_ref[0, :, 2 * ML_W + p * LANES:2 * ML_W + (p + 1) * LANES]
        o2 = ml_ref[0, :, 3 * ML_W + p * LANES:3 * ML_W + (p + 1) * LANES]
        q_t = q2.T.astype(bf16)
        v_t = v2.T
        kb = k2.astype(bf16)
        k_sel = (jnp.where(lo_half, k2, 0.0).astype(bf16), jnp.where(lo_half, 0.0, k2).astype(bf16))
        sw, den, a_int, e_m, a_st, w_s = [], [], [], [], [], []
        yield
        for j in range(2):
            h = 2 * p + j
            b_row = b_t[ML_HEADS + h:ML_HEADS + h + 1, :]
            i_row = g_t[h:h + 1, :]
            m_prev = m_old[h:h + 1, :]
            dm = jnp.where(causal, b_row - z_nat[:, h:h + 1], NEG_INF)
            inter = b_row + m_prev
            raw = _dot(k_sel[j], q_t)
            yield
            m_t = jnp.maximum(inter, jnp.max(dm, axis=0, keepdims=True))
            yield
            s_w = raw * jnp.exp(dm - m_t)
            den.append(jnp.sum(s_w, axis=0, keepdims=True))
            sw.append(s_w.astype(bf16))
            yield
            a_int.append(jnp.exp(inter - m_t))
            e_m.append(jnp.exp(-m_t))
            m_new = jnp.broadcast_to(m_t[:, L - 1:L], (1, L))
            b_last = jnp.broadcast_to(b_row[:, L - 1:L], (1, L))
            a_st.append(jnp.exp(b_last + m_prev - m_new))
            w_s.append(jnp.exp(b_last - b_row + i_row - m_new))
            m_next = jnp.where(sub8 == h, m_new, m_next)
            yield
        ct = ct_s[p]
        nr = nr_s[p]
        qn = _dot(nr.astype(bf16), q_t)
        num0 = _dot(jnp.where(top, v_t, 0.0).astype(bf16), sw[0])
        yield
        num1 = _dot(jnp.where(top, 0.0, v_t).astype(bf16), sw[1])
        qc = _dot(ct.astype(bf16), q_t)
        yield
        num = num0 + num1 + jnp.where(top, a_int[0], a_int[1]) * qc
        den0 = den[0] + a_int[0] * qn[0:1, :]
        den1 = den[1] + a_int[1] * qn[1:2, :]
        hh = num / jnp.where(top, jnp.maximum(jnp.abs(den0), e_m[0]), jnp.maximum(jnp.abs(den1), e_m[1]))
        yield
        sq = hh * hh
        r0 = lax.rsqrt(jnp.sum(sq[0:HEAD_W, :], axis=0, keepdims=True) * (1.0 / HEAD_W) + EPS)
        r1 = lax.rsqrt(jnp.sum(sq[HEAD_W:LANES, :], axis=0, keepdims=True) * (1.0 / HEAD_W) + EPS)
        y = (hh * jnp.where(top, r0, r1)).T * nw_ref[:, sl] * jax.nn.sigmoid(o2)
        out_ref[0, :, sl] = y.astype(bf16)
        yield
        vw = (v_t * jnp.where(top, w_s[0], w_s[1])).astype(bf16)
        ct_s[p] = jnp.where(top, a_st[0], a_st[1]) * ct + jnp.where(bd_mask, _dot(vw, kb), 0.0)
        w8 = jnp.where(sub8 == 0, w_s[0], jnp.where(sub8 == 1, w_s[1], 0.0)).astype(bf16)
        nr_s[p] = jnp.where(sub8 == 0, a_st[0], a_st[1]) * nr + n_mask * _dot(w8, kb)
        yield
    m_s[...] = m_next


DT_LANE = 2 * ML_HEADS
SSD_GATE_ROW0 = DT_LANE // 8 * 8
SSD_HEAD_ROW = DT_LANE - SSD_GATE_ROW0
HALO = 8


def _ssd_prompt_chunk(ssd_ref, g_ref, gb_ref, al_ref, d_ref, nw_ref, out_ref, h_s):
    L = PROMPT_CHUNK
    xc = ssd_ref[0, :, SSD_W:SSD_COLS]
    nb = SSD_W
    b_b = [xc[:, nb + gi * LANES:nb + (gi + 1) * LANES].astype(bf16) for gi in range(2)]
    c_b = [xc[:, nb + (2 + gi) * LANES:nb + (3 + gi) * LANES].astype(bf16) for gi in range(2)]

    row = _iota2((L, L), 0)
    col = _iota2((L, L), 1)
    tril = col <= row
    lo_half = col < HEAD_W
    triu_b = _ones_where(row <= col)
    bd_b = _ones_where(_same_head(row, col))
    rowc = _iota2((L, 1), 0)

    g_t = (g_ref[0] + gb_ref[...]).T
    dt_t = jax.nn.softplus(g_t[SSD_GATE_ROW0:SSD_GATE_ROW0 + GATE_ROWS, :])
    yield
    cum_t = None
    for part in _split_bf16(dt_t * (-jnp.exp(al_ref[...])), 3):
        d = _dot(part, triu_b)
        cum_t = d if cum_t is None else cum_t + d
    yield
    nat = jnp.concatenate([cum_t, dt_t, cum_t * LOG2_E, jnp.zeros((L - 3 * GATE_ROWS, L), f32)], axis=0).T
    cb_g = [_dot_nt(c_b[gi], b_b[gi]) for gi in range(2)]
    yield

    for p in range(SSD_HEADS // 2):
        sl = slice(p * LANES, (p + 1) * LANES)
        x2 = xc[:, sl]
        z2 = ssd_ref[0, :, sl]
        hp = h_s[p]
        hb = hp.astype(bf16)
        grp = [(2 * p + j) // (SSD_HEADS // 2) for j in range(2)]
        ys = None
        ecum, w_s, a_last, ch = [], [], [], []
        for j in range(2):
            r = SSD_HEAD_ROW + 2 * p + j
            cum_col = nat[:, r:r + 1]
            cum_last = cum_col[L - 1:L, :]
            key_row = cum_t[r:r + 1, :] * LOG2_E - jnp.log2(dt_t[r:r + 1, :])
            lm_dt = jnp.exp2(jnp.where(tril, nat[:, 2 * GATE_ROWS + r:2 * GATE_ROWS + r + 1] - key_row, NEG_INF))
            m = (cb_g[grp[j]] * lm_dt).astype(bf16)
            yield
            xm = jnp.where(lo_half, x2, 0.0) if j == 0 else jnp.where(lo_half, 0.0, x2)
            d = _dot(m, xm.astype(bf16))
            ys = d if ys is None else ys + d
            ecum.append(jnp.exp(cum_col))
            w_s.append(jnp.exp(cum_last - cum_col) * nat[:, GATE_ROWS + r:GATE_ROWS + r + 1])
            a_last.append(jnp.exp(cum_last))
            if j == 0 or grp[1] != grp[0]:
                ch.append(_dot_nt(c_b[grp[j]], hb))
            else:
                ch.append(ch[0])
            yield
        ys = ys + jnp.where(lo_half, ecum[0] * ch[0], ecum[1] * ch[1])
        xw_t = (x2 * jnp.where(lo_half, w_s[0], w_s[1])).T.astype(bf16)
        yield
        if grp[0] == grp[1]:
            upd = _dot(xw_t, b_b[grp[0]])
        else:
            upd = jnp.where(row < HEAD_W, _dot(xw_t, b_b[grp[0]]), _dot(xw_t, b_b[grp[1]]))
        h_s[p] = jnp.where(rowc < HEAD_W, a_last[0], a_last[1]) * hp + upd
        yy = (ys + d_ref[:, sl] * x2) * _silu(z2)
        yield
        ms = _group_mean(yy * yy, bd_b)
        yield
        out_ref[0, :, sl] = (yy * lax.rsqrt(ms + EPS) * nw_ref[:, sl]).astype(bf16)


def _hg_lower_bound(lbp, layer):
    mx = jnp.max(lbp, axis=0, keepdims=True)
    e = jnp.exp(lbp - mx)
    den = jnp.sum(e, axis=0, keepdims=True)
    lb = jnp.zeros_like(den)
    for j in range(1, layer + 1):
        lb = lb + e[j:j + 1, :] / den
    return lb


def _hgrn_prompt_chunk(hg_ref, lbp_ref, nw_ref, out_ref, sbd_s, p_s, r_s, o_s, g_s, k_s, v_s, q_s, kt_s, a_s, *, layer):
    L = HG_CHUNK
    nblk = L // 8
    lb = _hg_lower_bound(lbp_ref[...], layer)
    ff = hg_ref[0, :, HG_W:2 * HG_W]
    sig = jax.nn.sigmoid(ff)
    log2_k = jnp.log2((1.0 - lb) * jax.nn.sigmoid(-ff))
    v_s[...] = hg_ref[0, :, 2 * HG_W:3 * HG_W]
    q_s[...] = hg_ref[0, :, 0:HG_W] * (HEAD_W ** -0.5)
    row = _iota2((L, L), 0)
    col = _iota2((L, L), 1)
    tri_b = _ones_where(col <= row)
    yield
    g2 = _cumsum_rows(jnp.log(lb + (1.0 - lb) * sig), tri_b) * LOG2_E
    g_s[...] = g2
    k_s[...] = g2 - log2_k
    yield

    r2 = _iota2((2 * LANES, 2 * LANES), 0)
    c2 = _iota2((2 * LANES, 2 * LANES), 1)
    bd4_b = _ones_where(_same_head(r2, c2))
    r1 = _iota2((LANES, LANES), 0)
    c1 = _iota2((LANES, LANES), 1)
    bd_mask = _same_head(r1, c1)
    bd_b = _ones_where(bd_mask)

    a_s[0:8, :] = jnp.zeros((8, HG_W), f32)
    for i in range(1, nblk):
        blk = slice(8 * i, 8 * i + 8)
        n = 8 * i
        r_i = g_s[n - 1:n, :]
        kt = jnp.exp2(r_i - k_s[0:n, :])
        first_head = jnp.bitwise_and(_iota2((n, HG_W), 1), LANES - 1) < HEAD_W
        kt_s[i - 1, 0:n, :] = jnp.where(first_head, kt, 0.0)
        kt_s[i - 1, n:L, :] = jnp.zeros((L - n, HG_W), f32)
        kt_s[i - 1, L:L + n, :] = jnp.where(first_head, 0.0, kt)
        kt_s[i - 1, L + n:2 * L, :] = jnp.zeros((L - n, HG_W), f32)
        qt = q_s[blk, :] * jnp.exp2(g_s[blk, :] - r_i)
        qt = jnp.concatenate([qt, jnp.zeros_like(qt)], axis=0).astype(bf16)
        for p in range(HG_HEADS // 2):
            sl = slice(p * LANES, (p + 1) * LANES)
            a_s[blk, sl] = _dot_nt(qt[:, sl], kt_s[i - 1, :, sl].astype(bf16))[0:8, :]
        yield

    sub = _iota2((8, HG_W), 0)
    for j in range(nblk):
        blk = slice(8 * j, 8 * j + 8)
        g_blk, gk_blk, q_blk = g_s[blk, :], k_s[blk, :], q_s[blk, :]
        for r in range(8):
            s = 8 * j + r
            p_s[8 * s:8 * s + 8, :] = q_blk * jnp.exp2(jnp.where(sub >= r, g_blk - gk_blk[r:r + 1, :], NEG_INF))
        yield
    r_s[...] = _dot(p_s[...].astype(bf16), bd4_b)
    yield

    gg = g_s[...]
    vv = v_s[...]
    qd = q_s[...] * jnp.exp2(gg)
    lo_half = _iota2((L, LANES), 1) < HEAD_W
    for p in range(HG_HEADS // 2):
        sl = slice(p * LANES, (p + 1) * LANES)
        v_cat = jnp.concatenate([jnp.where(lo_half, vv[:, sl], 0.0), jnp.where(lo_half, 0.0, vv[:, sl])], axis=0)
        o_s[:, sl] = (_dot(qd[:, sl].astype(bf16), sbd_s[p].astype(bf16))
                      + _dot(a_s[:, sl].astype(bf16), v_cat.astype(bf16)))
        yield
    for j in range(nblk):
        blk = slice(8 * j, 8 * j + 8)
        acc = o_s[blk, :]
        v_blk = v_s[blk, :]
        for r in range(8):
            s = 8 * j + r
            acc = acc + r_s[8 * s:8 * s + 8, :] * v_blk[r:r + 1, :]
        o_s[blk, :] = acc
        yield

    g_last = gg[L - 1:L, :]
    kd = jnp.exp2(g_last - k_s[...])
    zpad = jnp.zeros((LANES - L, LANES), f32)
    og = o_s[...]
    gate = hg_ref[0, :, 3 * HG_W:4 * HG_W]
    for p in range(HG_HEADS // 2):
        sl = slice(p * LANES, (p + 1) * LANES)
        kd_t = jnp.concatenate([kd[:, sl], zpad], axis=0).T.astype(bf16)
        v_pad = jnp.concatenate([vv[:, sl], zpad], axis=0).astype(bf16)
        g_t = jnp.concatenate([gg[:, sl], zpad], axis=0).T
        dec = jnp.exp2(g_t[:, L - 1:L])
        sbd_s[p] = dec * sbd_s[p] + jnp.where(bd_mask, _dot(kd_t, v_pad), 0.0)
        yield
        oo = og[:, sl]
        ms = _group_mean(oo * oo, bd_b)
        out_ref[0, :, sl] = (oo * lax.rsqrt(ms + EPS) * nw_ref[:, sl] * _silu(gate[:, sl])).astype(bf16)


def _mixers_prompt_body(ml_ref, ssd_ref, hg_ref, g_ref, gb_ref, al_ref, d_ref, mlnw_ref, ssnw_ref,
                        lbp_ref, hgnw_ref,
                        hm_ref, c1_ref, n1_ref, m1_ref, hs_ref, h1_ref, ho_ref, s1_ref,
                        ct_s, nr_s, m_s, h_s, sbd_s, p_s, r_s, o_s, gg_s, k_s, v_s, q_s, kt_s, a_s, *, layer):
    c = pl.program_id(1)
    seqs = range(ml_ref.shape[0])

    @pl.when(c == 0)
    def _():
        ct_s[...] = jnp.zeros(ct_s.shape, f32)
        nr_s[...] = jnp.zeros(nr_s.shape, f32)
        m_s[...] = jnp.zeros(m_s.shape, f32)
        h_s[...] = jnp.zeros(h_s.shape, f32)
        sbd_s[...] = jnp.zeros(sbd_s.shape, f32)

    def hgrn_chunks(bi):
        for half in range(PROMPT_CHUNK // HG_CHUNK):
            rows = pl.ds(half * HG_CHUNK, HG_CHUNK)
            yield from _hgrn_prompt_chunk(hg_ref.at[pl.ds(bi, 1), rows], lbp_ref, hgnw_ref, ho_ref.at[pl.ds(bi, 1), rows],
                                          sbd_s.at[bi], p_s.at[bi], r_s.at[bi], o_s.at[bi], gg_s.at[bi], k_s.at[bi],
                                          v_s.at[bi], q_s.at[bi], kt_s.at[bi], a_s.at[bi], layer=layer)

    chunks = []
    for bi in seqs:
        chunks.append(_ssd_prompt_chunk(_one_seq(ssd_ref, bi), _one_seq(g_ref, bi), gb_ref, al_ref, d_ref,
                                        ssnw_ref, _one_seq(hs_ref, bi), h_s.at[bi]))
        chunks.append(hgrn_chunks(bi))
        chunks.append(_mlstm_prompt_chunk(_one_seq(ml_ref, bi), _one_seq(g_ref, bi), gb_ref, mlnw_ref,
                                          _one_seq(hm_ref, bi), ct_s.at[bi], nr_s.at[bi], m_s.at[bi]))
    _interleave(chunks)

    @pl.when(c == pl.num_programs(1) - 1)
    def _():
        h1_ref[...] = h_s[...]
        for bi in seqs:
            for p in range(ML_HEADS // 2):
                c_pair = ct_s[bi, p].T
                c1_ref[bi, 2 * p] = c_pair[0:HEAD_W, 0:HEAD_W]
                c1_ref[bi, 2 * p + 1] = c_pair[HEAD_W:LANES, HEAD_W:LANES]
                n1_ref[bi, 2 * p:2 * p + 1, :] = nr_s[bi, p, 0:1, 0:HEAD_W]
                n1_ref[bi, 2 * p + 1:2 * p + 2, :] = nr_s[bi, p, 1:2, HEAD_W:LANES]
            m1_ref[bi] = m_s[bi]
            for p in range(HG_HEADS // 2):
                sbd = sbd_s[bi, p]
                s1_ref[bi, 2 * p] = sbd[0:HEAD_W, 0:HEAD_W]
                s1_ref[bi, 2 * p + 1] = sbd[HEAD_W:LANES, HEAD_W:LANES]


def _mixers_prompt(ml, ssd, hg, gates, gate_bias, alog_col, d_row, ml_nw, ssd_nw, lbp, hg_nw, layer):
    b, t, _ = ml.shape
    L = PROMPT_CHUNK
    ns = SEQS_PER_STEP
    assert b % ns == 0 and t % L == 0
    ml_pairs, ssd_pairs, hg_pairs = ML_HEADS // 2, SSD_HEADS // 2, HG_HEADS // 2
    tok = lambda w: pl.BlockSpec((ns, L, w), lambda i, c: (i, c, 0))
    per_seq = lambda *shape: pl.BlockSpec((ns,) + shape, lambda i, c: (i,) + (0,) * len(shape))
    hg_rows = pltpu.VMEM((ns, HG_CHUNK, HG_W), f32)
    hg_pairs_buf = pltpu.VMEM((ns, 8 * HG_CHUNK, HG_W), f32)
    return pl.pallas_call(
        functools.partial(_mixers_prompt_body, layer=layer),
        grid=(b // ns, t // L),
        in_specs=[tok(ML_COLS), tok(SSD_COLS), tok(HG_COLS), tok(GATE_W),
                  _layer_resident((1, GATE_W), layer), _layer_resident((GATE_ROWS, 1), layer),
                  _layer_resident((1, SSD_W), layer), _layer_resident((1, ML_W), layer),
                  _layer_resident((1, SSD_W), layer), _resident((DEPTH, HG_W)), _layer_resident((1, HG_W), layer)],
        out_specs=[tok(ML_W), per_seq(ML_HEADS, HEAD_W, HEAD_W), per_seq(ML_HEADS, HEAD_W), per_seq(8, LANES),
                   tok(SSD_W), per_seq(ssd_pairs, LANES, SSD_DSTATE),
                   tok(HG_W), per_seq(HG_HEADS, HEAD_W, HEAD_W)],
        out_shape=[jax.ShapeDtypeStruct((b, t, ML_W), bf16),
                   jax.ShapeDtypeStruct((b, ML_HEADS, HEAD_W, HEAD_W), f32),
                   jax.ShapeDtypeStruct((b, ML_HEADS, HEAD_W), f32),
                   jax.ShapeDtypeStruct((b, 8, LANES), f32),
                   jax.ShapeDtypeStruct((b, t, SSD_W), bf16),
                   jax.ShapeDtypeStruct((b, ssd_pairs, LANES, SSD_DSTATE), f32),
                   jax.ShapeDtypeStruct((b, t, HG_W), bf16),
                   jax.ShapeDtypeStruct((b, HG_HEADS, HEAD_W, HEAD_W), f32)],
        scratch_shapes=[pltpu.VMEM((ns, ml_pairs, LANES, LANES), f32), pltpu.VMEM((ns, ml_pairs, 8, LANES), f32),
                        pltpu.VMEM((ns, 8, LANES), f32),
                        pltpu.VMEM((ns, ssd_pairs, LANES, SSD_DSTATE), f32),
                        pltpu.VMEM((ns, hg_pairs, LANES, LANES), f32), hg_pairs_buf, hg_pairs_buf,
                        hg_rows, hg_rows, hg_rows, hg_rows, hg_rows,
                        pltpu.VMEM((ns, HG_CHUNK // 8 - 1, 2 * HG_CHUNK, HG_W), f32), hg_rows],
        compiler_params=_cparams("arbitrary", "arbitrary"),
        name="mixers_prompt",
    )(ml, ssd, hg, gates, gate_bias, alog_col, d_row, ml_nw, ssd_nw, lbp, hg_nw)


NB = 128
TS = 8


def _sample_call(body, name, grid, in_specs, operands, out_specs, out_shapes, scratch, n_state):
    first = len(operands) - n_state
    return pl.pallas_call(
        body, grid=grid, in_specs=in_specs, out_specs=out_specs, out_shape=out_shapes,
        scratch_shapes=scratch, input_output_aliases={first + i: 1 + i for i in range(n_state)},
        compiler_params=_cparams(*(["arbitrary"] * len(grid))), name=name,
    )(*operands)


def _mlstm_sample_body(q_ref, k_ref, v_ref, o_ref, g_ref, gb_ref, nw_ref, c0_ref, n0_ref, m0_ref,
                       out_ref, c1_ref, n1_ref, m1_ref, qt_s, kt_s, vt_s, gt_s, ht_s):
    p = pl.program_id(0)
    n_cur = [n0_ref[hl] for hl in range(2)]
    m_cur = [m0_ref[pl.ds(2 * p + hl, 1), :] for hl in range(2)]

    for t in range(TS):
        c_src = c0_ref if t == 0 else c1_ref
        rows = slice(t * NB, (t + 1) * NB)
        qt_s[...] = q_ref[rows, :].T
        kt_s[...] = k_ref[rows, :].T * (HEAD_W ** -0.5)
        vt_s[...] = v_ref[rows, :].T
        gt_s[...] = (g_ref[rows, :] + gb_ref[...]).T
        for hl in range(2):
            h = 2 * p + hl
            hs = slice(hl * HEAD_W, (hl + 1) * HEAD_W)
            i_t = gt_s[pl.ds(h, 1), :]
            lf = jax.nn.log_sigmoid(gt_s[pl.ds(ML_HEADS + h, 1), :])
            m_new = jnp.maximum(lf + m_cur[hl], i_t)
            a = jnp.exp(lf + m_cur[hl] - m_new)
            w = jnp.exp(i_t - m_new)
            n_new = a * n_cur[hl] + w * kt_s[hs, :]
            v_h = vt_s[hs, :]

            def body(d, num, hl=hl, a=a, w=w, v_h=v_h, c_src=c_src):
                kd = kt_s[pl.ds(hl * HEAD_W + d, 1), :] * w
                qd = qt_s[pl.ds(hl * HEAD_W + d, 1), :]
                cd = a * c_src[hl, d] + kd * v_h
                c1_ref[hl, d] = cd
                return num + qd * cd

            num = lax.fori_loop(0, HEAD_W, body, jnp.zeros((HEAD_W, NB), f32), unroll=4)
            den = jnp.sum(qt_s[hs, :] * n_new, axis=0, keepdims=True)
            hh = num / jnp.maximum(jnp.abs(den), jnp.exp(-m_new))
            ms = jnp.mean(hh * hh, axis=0, keepdims=True)
            ht_s[hs, :] = hh * lax.rsqrt(ms + EPS)
            n_cur[hl] = n_new
            m_cur[hl] = m_new
        out_ref[rows, :] = (ht_s[...].T * nw_ref[...] * jax.nn.sigmoid(o_ref[rows, :])).astype(bf16)

    for hl in range(2):
        n1_ref[hl] = n_cur[hl]
        m1_ref[pl.ds(2 * p + hl, 1), :] = m_cur[hl]


def _mlstm_sample(ml, gates, gate_bias, norm_w, c0t, n0t, m0t, layer):
    n = ml.shape[0]
    npair = ML_HEADS // 2
    blk = lambda off: pl.BlockSpec((n, LANES), lambda p, off=off: (0, off + p))
    c_spec = pl.BlockSpec((None, 2, HEAD_W, HEAD_W, NB), lambda p: (layer, p, 0, 0, 0))
    n_spec = pl.BlockSpec((None, 2, HEAD_W, NB), lambda p: (layer, p, 0, 0))
    m_spec = pl.BlockSpec((None, ML_HEADS, NB), lambda p: (layer, 0, 0))
    return _sample_call(
        _mlstm_sample_body, "mlstm_sample", (npair,),
        [blk(0), blk(npair), blk(2 * npair), blk(3 * npair), _resident((n, GATE_W)),
         _layer_resident((1, GATE_W), layer), pl.BlockSpec((None, 1, LANES), lambda p: (layer, 0, p)),
         c_spec, n_spec, m_spec],
        [ml, ml, ml, ml, gates, gate_bias, norm_w, c0t, n0t, m0t],
        [pl.BlockSpec((n, LANES), lambda p: (0, p)), c_spec, n_spec, m_spec],
        [jax.ShapeDtypeStruct((n, ML_W), bf16), jax.ShapeDtypeStruct(c0t.shape, f32),
         jax.ShapeDtypeStruct(n0t.shape, f32), jax.ShapeDtypeStruct(m0t.shape, f32)],
        [pltpu.VMEM((LANES, NB), f32)] * 5, 3)


def _ssd_conv_sample_body(ssd_ref, halo_ref, cw_ref, cb_ref, out_ref, xp_s):
    nh = (SSD_CONV - 1) * NB
    n = TS * NB
    xp_s[0:nh, :] = halo_ref[...]
    xp_s[nh:nh + n, :] = ssd_ref[:, SSD_W:SSD_COLS]
    conv = cb_ref[...]
    for j in range(SSD_CONV):
        conv = conv + cw_ref[j:j + 1, :] * xp_s[j * NB:j * NB + n, :]
    out_ref[...] = _silu(conv)


def _ssd_conv_sample(ssd, halo, conv_w, conv_b, layer):
    n = ssd.shape[0]
    nh = (SSD_CONV - 1) * NB
    return pl.pallas_call(
        _ssd_conv_sample_body,
        grid=(1,),
        in_specs=[_resident((n, SSD_COLS)), _layer_resident((nh, SSD_CONV_DIM), layer),
                  _layer_resident((SSD_CONV, SSD_CONV_DIM), layer), _layer_resident((1, SSD_CONV_DIM), layer)],
        out_specs=pl.BlockSpec((n, SSD_CONV_DIM), lambda i: (0, 0)),
        out_shape=jax.ShapeDtypeStruct((n, SSD_CONV_DIM), f32),
        scratch_shapes=[pltpu.VMEM((nh + n, SSD_CONV_DIM), f32)],
        compiler_params=_cparams("arbitrary"),
        name="ssd_conv_sample",
    )(ssd, halo, conv_w, conv_b)


def _ssd_sample_body(z_ref, x_ref, b0_ref, b1_ref, c0_ref, c1_ref, g_ref, gb_ref, hp_ref, nw_ref, h0_ref,
                     out_ref, h1_ref, xt_s, bt_s, ct_s, gt_s, dd_s, yt_s, *, layer, all_layers):
    p = pl.program_id(0)
    head = pl.program_id(1)
    b_refs = (b0_ref, b1_ref)
    c_refs = (c0_ref, c1_ref)
    h0 = h0_ref.at[layer] if all_layers else h0_ref
    h1 = h1_ref.at[layer] if all_layers else h1_ref
    if all_layers:
        for l in range(DEPTH):
            if l != layer:
                h1_ref[l] = h0_ref[l]

    @pl.when(head == 0)
    def _():
        for t in range(TS):
            rows = slice(t * NB, (t + 1) * NB)
            xt_s[t] = x_ref[rows, :].T
            gt_s[...] = (g_ref[rows, :] + gb_ref[...]).T
            for hl in range(2):
                bt_s[t, hl] = b_refs[hl][rows, :].T
                ct_s[t, hl] = c_refs[hl][rows, :].T
                dt = jax.nn.softplus(gt_s[pl.ds(DT_LANE + 2 * p + hl, 1), :])
                dd_s[t, 2 * hl:2 * hl + 1, :] = dt
                dd_s[t, 2 * hl + 1:2 * hl + 2, :] = jnp.exp(dt * (-jnp.exp(hp_ref[0, hl:hl + 1, :])))

    for hl in range(2):
        @pl.when(head == hl)
        def _(hl=hl):
            def cols(pp):
                return pl.ds(pl.multiple_of(pp * SSD_DSTATE, SSD_DSTATE), SSD_DSTATE)

            def body(pp, h):
                h_next = h0[:, cols(jnp.minimum(pp + 1, HEAD_W - 1))].T
                row = pl.ds(hl * HEAD_W + pp, 1)
                for t in range(TS):
                    xrow = xt_s[t, row, :] * dd_s[t, 2 * hl:2 * hl + 1, :]
                    h = dd_s[t, 2 * hl + 1:2 * hl + 2, :] * h + xrow * bt_s[t, hl]
                    yt_s[t, row, :] = jnp.sum(ct_s[t, hl] * h, axis=0, keepdims=True)
                h1[:, cols(pp)] = h.T
                return h_next

            lax.fori_loop(0, HEAD_W, body, h0[:, 0:SSD_DSTATE].T, unroll=4)

    @pl.when(head == 1)
    def _():
        for t in range(TS):
            rows = slice(t * NB, (t + 1) * NB)
            zt = z_ref[rows, :].T
            for hl in range(2):
                hs = slice(hl * HEAD_W, (hl + 1) * HEAD_W)
                yy = (yt_s[t, hs, :] + hp_ref[1, hl:hl + 1, :] * xt_s[t, hs, :]) * _silu(zt[hs, :])
                ms = jnp.mean(yy * yy, axis=0, keepdims=True)
                yt_s[t, hs, :] = yy * lax.rsqrt(ms + EPS)
            out_ref[rows, :] = (yt_s[t].T * nw_ref[...]).astype(bf16)


def _ssd_sample(ssd, xc, gates, gate_bias, head_params, norm_w, h0, layer):
    n = ssd.shape[0]
    npair = SSD_HEADS // 2
    hpg = SSD_HEADS // 2
    nxb = SSD_W // LANES
    head_cols = HEAD_W * SSD_DSTATE
    all_layers = layer == 0
    blk = lambda f: pl.BlockSpec((n, LANES), f)
    if all_layers:
        h_spec = pl.BlockSpec((DEPTH, NB, head_cols), lambda p, j: (0, 0, 2 * p + j))
    else:
        h_spec = pl.BlockSpec((None, NB, head_cols), lambda p, j: (layer, 0, 2 * p + j))
    return _sample_call(
        functools.partial(_ssd_sample_body, layer=layer, all_layers=all_layers), "ssd_sample", (npair, 2),
        [blk(lambda p, j: (0, p)), blk(lambda p, j: (0, p)),
         blk(lambda p, j: (0, nxb + (2 * p) // hpg)), blk(lambda p, j: (0, nxb + (2 * p + 1) // hpg)),
         blk(lambda p, j: (0, nxb + 2 + (2 * p) // hpg)), blk(lambda p, j: (0, nxb + 2 + (2 * p + 1) // hpg)),
         _resident((n, GATE_W)), _layer_resident((1, GATE_W), layer),
         pl.BlockSpec((None, 2, None, 2, LANES), lambda p, j: (layer, 0, p, 0, 0)),
         pl.BlockSpec((None, 1, LANES), lambda p, j: (layer, 0, p)), h_spec],
        [ssd, xc, xc, xc, xc, xc, gates, gate_bias, head_params, norm_w, h0],
        [pl.BlockSpec((n, LANES), lambda p, j: (0, p)), h_spec],
        [jax.ShapeDtypeStruct((n, SSD_W), bf16), jax.ShapeDtypeStruct(h0.shape, f32)],
        [pltpu.VMEM((TS, LANES, NB), f32), pltpu.VMEM((TS, 2, SSD_DSTATE, NB), f32),
         pltpu.VMEM((TS, 2, SSD_DSTATE, NB), f32), pltpu.VMEM((LANES, NB), f32), pltpu.VMEM((TS, 8, NB), f32),
         pltpu.VMEM((TS, LANES, NB), f32)], 0 if all_layers else 1)


def _hgrn_sample_body(q_ref, f_ref, i_ref, gate_ref, lbp_ref, nw_ref, s0_ref, out_ref, s1_ref,
                      qt_s, kt_s, ft_s, vt_s, ot_s, *, layer):
    lb = _hg_lower_bound(lbp_ref[...], layer)

    for t in range(TS):
        s_src = s0_ref if t == 0 else s1_ref
        rows = slice(t * NB, (t + 1) * NB)
        ff = f_ref[rows, :]
        qt_s[...] = (q_ref[rows, :] * (HEAD_W ** -0.5)).T
        kt_s[...] = ((1.0 - lb) * jax.nn.sigmoid(-ff)).T
        ft_s[...] = (lb + (1.0 - lb) * jax.nn.sigmoid(ff)).T
        vt_s[...] = i_ref[rows, :].T
        for hl in range(2):
            hs = slice(hl * HEAD_W, (hl + 1) * HEAD_W)
            v_h = vt_s[hs, :]

            def body(k, o, hl=hl, v_h=v_h, s_src=s_src):
                r = pl.ds(hl * HEAD_W + k, 1)
                sk = ft_s[r, :] * s_src[hl, k] + kt_s[r, :] * v_h
                s1_ref[hl, k] = sk
                return o + qt_s[r, :] * sk

            o = lax.fori_loop(0, HEAD_W, body, jnp.zeros((HEAD_W, NB), f32), unroll=4)
            ms = jnp.mean(o * o, axis=0, keepdims=True)
            ot_s[hs, :] = o * lax.rsqrt(ms + EPS)
        out_ref[rows, :] = (ot_s[...].T * nw_ref[...] * _silu(gate_ref[rows, :])).astype(bf16)


def _hgrn_sample(hg, lbp, norm_w, s0t, layer):
    n = hg.shape[0]
    npair = HG_HEADS // 2
    blk = lambda off: pl.BlockSpec((n, LANES), lambda p, off=off: (0, off + p))
    s_spec = pl.BlockSpec((None, 2, HEAD_W, HEAD_W, NB), lambda p: (layer, p, 0, 0, 0))
    return _sample_call(
        functools.partial(_hgrn_sample_body, layer=layer), "hgrn_sample", (npair,),
        [blk(0), blk(npair), blk(2 * npair), blk(3 * npair),
         pl.BlockSpec((DEPTH, LANES), lambda p: (0, p)),
         pl.BlockSpec((None, 1, LANES), lambda p: (layer, 0, p)), s_spec],
        [hg, hg, hg, hg, lbp, norm_w, s0t],
        [pl.BlockSpec((n, LANES), lambda p: (0, p)), s_spec],
        [jax.ShapeDtypeStruct((n, HG_W), bf16), jax.ShapeDtypeStruct(s0t.shape, f32)],
        [pltpu.VMEM((LANES, NB), f32)] * 5, 1)


def _ffn_body(x_ref, mml_ref, mssd_ref, mhg_ref, wout_ref, n2_ref, wug_ref, wuv_ref, cwg_ref, cwv_ref,
              cbg_ref, cbv_ref, wdn_ref, f0g_ref, f0v_ref, fn_ref, y_ref, f1g_ref, f1v_ref,
              x1_s, h2_s, acc_s, act_s, ubuf_s, *, tm, shift, jb, nff, final):
    t = pl.program_id(1)
    f = pl.program_id(2)
    halo = max(HALO, (FFN_CONV - 1) * shift)
    keep = (FFN_CONV - 1) * shift

    def when(cond, fn):
        if nff == 1:
            fn()
        else:
            pl.when(cond)(fn)

    @pl.when(t == 0)
    def _():
        f1g_ref[0] = f0g_ref[0]
        f1v_ref[0] = f0v_ref[0]

    def first():
        x = x_ref[...]
        mix = jnp.concatenate([mml_ref[...], mssd_ref[...], mhg_ref[...]], axis=1)
        x1 = x + _dot(mix, wout_ref[...])
        x1_s[...] = x1
        h2_s[...] = (x1 * lax.rsqrt(jnp.mean(x1 * x1, axis=-1, keepdims=True) + EPS) * n2_ref[...]).astype(bf16)

    when(f == 0, first)
    h2 = h2_s[...]

    def conv(wu_ref, cw_ref, cb_ref, f1_ref, cols, ubuf):
        u = _dot(h2, wu_ref[:, cols])
        ubuf[halo - keep:halo, :] = f1_ref[0, :, cols]
        ubuf[halo:halo + tm, :] = u
        yv = (cb_ref[:, cols] + cw_ref[2:3, cols] * u
              + cw_ref[1:2, cols] * ubuf[halo - shift:halo - shift + tm, :]
              + cw_ref[0:1, cols] * ubuf[halo - 2 * shift:halo - 2 * shift + tm, :])
        f1_ref[0, :, cols] = ubuf[halo + tm - keep:halo + tm, :]
        return yv

    nbuf = ubuf_s.shape[0]
    for j in range(jb):
        cols = slice(j * FF_BLOCK, (j + 1) * FF_BLOCK)
        gq = conv(wug_ref, cwg_ref, cbg_ref, f1g_ref, cols, ubuf_s.at[(2 * j) % nbuf])
        vq = conv(wuv_ref, cwv_ref, cbv_ref, f1v_ref, cols, ubuf_s.at[(2 * j + 1) % nbuf])
        act_s[:, cols] = (_silu(gq) * vq).astype(bf16)
    contrib = _dot(act_s[...], wdn_ref[...])

    def finish(down):
        x2 = x1_s[...] + down
        if final:
            x2 = x2 * lax.rsqrt(jnp.mean(x2 * x2, axis=-1, keepdims=True) + EPS) * fn_ref[...]
        y_ref[...] = x2

    if nff == 1:
        finish(contrib)
    else:
        @pl.when(f == 0)
        def _():
            acc_s[...] = contrib

        @pl.when(f != 0)
        def _():
            acc_s[...] = acc_s[...] + contrib

        pl.when(f == nff - 1)(lambda: finish(acc_s[...]))


def _ffn(x, mml, mssd, mhg, w_out, norm2, w_up, conv_w, conv_b, w_down, f0, final_w, *, layer, groups, tm, shift, jb,
         final, f0_base=0):
    n = x.shape[0]
    nt = n // (groups * tm)
    fw = jb * FF_BLOCK
    nff = D_FF // fw
    assert nt == 1 or nff == 1
    halo = max(HALO, (FFN_CONV - 1) * shift)
    keep = (FFN_CONV - 1) * shift
    rows = lambda w: pl.BlockSpec((tm, w), lambda g, t, f: (g * nt + t, 0))
    single = nff == 1
    wspec = lambda shape, imap: pl.BlockSpec(shape, imap, pipeline_mode=pl.Buffered(1)) if single else pl.BlockSpec(shape, imap)
    f1_spec = pl.BlockSpec((1, keep, fw), lambda g, t, f: (g, 0, f))
    in_specs = [rows(D_MODEL), rows(ML_W), rows(SSD_W), rows(HG_W),
                _layer_resident((D_MODEL, D_MODEL), layer), _layer_resident((1, D_MODEL), layer),
                wspec((None, D_MODEL, fw), lambda g, t, f: (layer, 0, f)),
                wspec((None, D_MODEL, fw), lambda g, t, f: (layer, 0, nff + f)),
                wspec((None, FFN_CONV, fw), lambda g, t, f: (layer, 0, f)),
                wspec((None, FFN_CONV, fw), lambda g, t, f: (layer, 0, nff + f)),
                wspec((None, 1, fw), lambda g, t, f: (layer, 0, f)),
                wspec((None, 1, fw), lambda g, t, f: (layer, 0, nff + f)),
                wspec((None, fw, D_MODEL), lambda g, t, f: (layer, f, 0)),
                pl.BlockSpec((1, keep, fw), lambda g, t, f: (f0_base + g, 0, f)),
                pl.BlockSpec((1, keep, fw), lambda g, t, f: (f0_base + g, 0, nff + f)),
                _resident((1, D_MODEL))]
    operands = [x, mml, mssd, mhg, w_out, norm2, w_up, w_up, conv_w, conv_w, conv_b, conv_b, w_down, f0, f0, final_w]
    acc_rows = tm if nff > 1 else 8
    return pl.pallas_call(
        functools.partial(_ffn_body, tm=tm, shift=shift, jb=jb, nff=nff, final=final),
        grid=(groups, nt, nff),
        in_specs=in_specs,
        out_specs=[rows(D_MODEL), f1_spec, f1_spec],
        out_shape=[jax.ShapeDtypeStruct((n, D_MODEL), f32),
                   jax.ShapeDtypeStruct((groups, keep, D_FF), f32),
                   jax.ShapeDtypeStruct((groups, keep, D_FF), f32)],
        scratch_shapes=[pltpu.VMEM((tm, D_MODEL), f32), pltpu.VMEM((tm, D_MODEL), bf16),
                        pltpu.VMEM((acc_rows, D_MODEL), f32), pltpu.VMEM((tm, fw), bf16),
                        pltpu.VMEM((min(4, 2 * jb), halo + tm, FF_BLOCK), f32)],
        compiler_params=_cparams("arbitrary", "arbitrary", "arbitrary"),
        name="ffn",
    )(*operands)


def _gate_rows(*parts):
    r = jnp.concatenate([p.astype(f32) for p in parts], axis=1)
    return jnp.pad(r, ((0, 0), (0, GATE_W - r.shape[1])))[:, None, :]


def kernel(x_prompt, x_sample, state_mlstm_C, state_mlstm_n, state_mlstm_m, state_ssd, state_ssd_conv,
           state_hgrn, state_ffn_conv, norm1_w, w_in, ml_ig_b, ml_fg_b, ml_norm_w, ssd_conv_w, ssd_conv_b,
           ssd_dt_bias, ssd_A_log, ssd_D, ssd_norm_w, hg_lower_bounds, hg_norm_w, w_out, norm2_w, w_up,
           ffn_conv_w, ffn_conv_b, w_down, final_norm_w):
    bp, tp, _ = x_prompt.shape
    nb, ts, _ = x_sample.shape
    assert nb == NB and ts == TS and tp % ROW_TILE == 0
    depth = w_in.shape[0]
    assert depth == DEPTH
    xp = x_prompt.reshape(bp * tp, D_MODEL)
    xs = jnp.transpose(x_sample, (1, 0, 2)).reshape(ts * nb, D_MODEL)
    lbp = hg_lower_bounds.astype(f32)
    fin_w = final_norm_w[None, :]
    zero_f = jnp.zeros((bp, FFN_CONV - 1, 2 * D_FF), f32)

    w_cat = _prep_w_in(w_in)
    wo_b, wu_b, wd_b = w_out.astype(bf16), w_up.astype(bf16), w_down.astype(bf16)
    n1, n2 = norm1_w[:, None, :], norm2_w[:, None, :]
    gate_bias = _gate_rows(ml_ig_b, ml_fg_b, ssd_dt_bias)
    alog_col = jnp.pad(ssd_A_log.astype(f32), ((0, 0), (SSD_HEAD_ROW, GATE_ROWS - SSD_HEAD_ROW - SSD_HEADS)))[:, :, None]
    d_row = jnp.repeat(ssd_D.astype(f32), HEAD_W, axis=1)[:, None, :]
    mlw, ssw, hgw = ml_norm_w[:, None, :], ssd_norm_w[:, None, :], hg_norm_w[:, None, :]
    cw, cb = ssd_conv_w, ssd_conv_b[:, None, :]
    fcw, fcb = ffn_conv_w, ffn_conv_b[:, None, :]
    head_params = jnp.broadcast_to(
        jnp.stack([ssd_A_log, ssd_D], axis=1).astype(f32).reshape(depth, 2, SSD_HEADS // 2, 2, 1),
        (depth, 2, SSD_HEADS // 2, 2, LANES))

    c_t = jnp.transpose(state_mlstm_C.astype(f32), (0, 2, 3, 4, 1))
    n_t = jnp.transpose(state_mlstm_n.astype(f32), (0, 2, 3, 1))
    m_t = jnp.transpose(state_mlstm_m.astype(f32), (0, 2, 1))
    h_n = state_ssd.astype(f32).reshape(depth, nb, SSD_HEADS * HEAD_W * SSD_DSTATE)
    s_t = jnp.transpose(state_hgrn.astype(f32), (0, 2, 3, 4, 1))
    conv_halo = jnp.transpose(state_ssd_conv.astype(f32), (0, 2, 1, 3)).reshape(depth, (SSD_CONV - 1) * nb, SSD_CONV_DIM)
    f_t = jnp.transpose(state_ffn_conv.astype(f32), (0, 2, 1, 3)).reshape(depth, (FFN_CONV - 1) * nb, 2 * D_FF)

    p_states, s_sconv, s_ffn = [], [], []
    for l in range(depth):
        last = l == depth - 1

        ml, ssd, hg, gates, p_tail = _inproj(xp, n1, w_cat, l, conv=(cw, cb, tp))
        hm, p_c, p_n, p_m, hs, p_h, ho, p_s = _mixers_prompt(
            ml.reshape(bp, tp, ML_COLS), ssd.reshape(bp, tp, SSD_COLS), hg.reshape(bp, tp, HG_COLS),
            gates.reshape(bp, tp, GATE_W), gate_bias, alog_col, d_row, mlw, ssw, lbp, hgw, l)
        xp, p_fg, p_fv = _ffn(xp, hm.reshape(bp * tp, ML_W), hs.reshape(bp * tp, SSD_W), ho.reshape(bp * tp, HG_W),
                              wo_b, n2, wu_b, fcw, fcb, wd_b, zero_f, fin_w, layer=l,
                              groups=bp, tm=ROW_TILE, shift=1, jb=D_FF // FF_BLOCK, final=last)
        p_states.append((p_c, p_n, p_m[:, 0:ML_HEADS, 0],
                         p_h.reshape(bp, SSD_HEADS, HEAD_W, SSD_DSTATE),
                         p_tail[:, HALO - (SSD_CONV - 1):, :], p_s,
                         jnp.concatenate([p_fg, p_fv], axis=-1)))

        ml, ssd, hg, gates = _inproj(xs, n1, w_cat, l)
        hm, c_t, n_t, m_t = _mlstm_sample(ml, gates, gate_bias, mlw, c_t, n_t, m_t, l)
        xc = _ssd_conv_sample(ssd, conv_halo, cw, cb, l)
        hs, h_n = _ssd_sample(ssd, xc, gates, gate_bias, head_params, ssw, h_n, l)
        ho, s_t = _hgrn_sample(hg, lbp, hgw, s_t, l)
        xs, s_fg, s_fv = _ffn(xs, hm, hs, ho, wo_b, n2, wu_b, fcw, fcb, wd_b, f_t, fin_w, layer=l,
                              groups=1, tm=ts * nb, shift=nb, jb=1, final=last, f0_base=l)
        s_ffn.append(jnp.concatenate([s_fg, s_fv], axis=-1).reshape(FFN_CONV - 1, nb, 2 * D_FF))
        raw_xbc = ssd[:, SSD_W:].reshape(ts, nb, SSD_CONV_DIM)
        s_sconv.append(jnp.transpose(raw_xbc[ts - (SSD_CONV - 1):], (1, 0, 2)))

    y_prompt = xp.reshape(bp, tp, D_MODEL)
    y_sample = jnp.transpose(xs.reshape(ts, nb, D_MODEL), (1, 0, 2))
    p_out = [jnp.stack([st[i] for st in p_states]) for i in range(7)]
    s_out = [jnp.transpose(c_t, (0, 4, 1, 2, 3)), jnp.transpose(n_t, (0, 3, 1, 2)), jnp.transpose(m_t, (0, 2, 1)),
             h_n.reshape(state_ssd.shape), jnp.stack(s_sconv),
             jnp.transpose(s_t, (0, 4, 1, 2, 3)), jnp.transpose(jnp.stack(s_ffn), (0, 2, 1, 3))]
    return (y_prompt, y_sample, *p_out, *s_out)
```

```python
import functools

import jax
import jax.numpy as jnp
from jax import lax
from jax.experimental import pallas as pl
from jax.experimental.pallas import tpu as pltpu

f32 = jnp.float32
bf16 = jnp.bfloat16

D_MODEL = 1024
DEPTH = 2
ML_HEADS = 6
HEAD_W = 64
ML_W = 384
SSD_HEADS = 6
SSD_W = 384
SSD_DSTATE = 128
SSD_CONV = 4
SSD_CONV_DIM = 896
HG_HEADS = 4
HG_W = 256
D_FF = 2816
FFN_CONV = 3
EPS = 1e-6
PROMPT_CHUNK = 128
HG_CHUNK = 64
LANES = 128
FF_BLOCK = 256
ROW_TILE = 512
SEQS_PER_STEP = 4
GATE_W = 128
GATE_ROWS = 16
W_IN_COLS = 4 * ML_W + (SSD_W + SSD_CONV_DIM) + 4 * HG_W + GATE_W
VMEM_LIMIT_BYTES = 56 * 1024 * 1024

NEG_INF = float("-inf")
LOG2_E = 1.4426950408889634


def _cparams(*sem):
    return pltpu.CompilerParams(dimension_semantics=sem, vmem_limit_bytes=VMEM_LIMIT_BYTES)


def _resident(shape):
    nd = len(shape)
    return pl.BlockSpec(shape, lambda *_: (0,) * nd, pipeline_mode=pl.Buffered(1))


def _layer_resident(shape, layer):
    nd = len(shape)
    return pl.BlockSpec((None,) + tuple(shape), lambda *_: (layer,) + (0,) * nd, pipeline_mode=pl.Buffered(1))


def _dot(a, b):
    return jnp.dot(a, b, preferred_element_type=f32)


def _dot_nt(a, b):
    return lax.dot_general(a, b, (((1,), (1,)), ((), ())), preferred_element_type=f32)


def _split_bf16(x, parts):
    out = []
    r = x
    for _ in range(parts):
        h = r.astype(bf16)
        out.append(h)
        r = r - h.astype(f32)
    return out


def _cumsum_rows(x, tri_b):
    acc = None
    for part in _split_bf16(x, 3):
        d = _dot(tri_b, part)
        acc = d if acc is None else acc + d
    return acc


def _group_mean(x, bd_b):
    return _dot(x.astype(bf16), bd_b) * (1.0 / HEAD_W)


def _iota2(shape, axis):
    return lax.broadcasted_iota(jnp.int32, shape, axis)


def _ones_where(mask):
    return jnp.where(mask, 1.0, 0.0).astype(bf16)


def _same_head(a, b):
    return jnp.right_shift(a, 6) == jnp.right_shift(b, 6)


def _silu(x):
    return x * jax.nn.sigmoid(x)


ML_COLS = 4 * ML_W
SSD_COLS = SSD_W + SSD_CONV_DIM
HG_COLS = 4 * HG_W


def _inproj_body(x_ref, nw_ref, w_ref, ml_ref, ssd_ref, hg_ref, g_ref):
    x = x_ref[...]
    h = (x * lax.rsqrt(jnp.mean(x * x, axis=-1, keepdims=True) + EPS) * nw_ref[...]).astype(bf16)
    o0, o1, o2 = ML_COLS, ML_COLS + SSD_COLS, ML_COLS + SSD_COLS + HG_COLS
    ml_ref[...] = _dot(h, w_ref[:, 0:o0])
    ssd_ref[...] = _dot(h, w_ref[:, o0:o1])
    hg_ref[...] = _dot(h, w_ref[:, o1:o2])
    g_ref[...] = _dot(h, w_ref[:, o2:W_IN_COLS])


def _inproj_conv_body(x_ref, nw_ref, w_ref, cw_ref, cb_ref, ml_ref, ssd_ref, hg_ref, g_ref, tail_ref, xbuf_s, *,
                      tiles_per_seq):
    i = pl.program_id(0)
    tm = x_ref.shape[0]

    @pl.when(lax.rem(i, tiles_per_seq) == 0)
    def _():
        xbuf_s[0:HALO, :] = jnp.zeros((HALO, SSD_CONV_DIM), f32)

    x = x_ref[...]
    h = (x * lax.rsqrt(jnp.mean(x * x, axis=-1, keepdims=True) + EPS) * nw_ref[...]).astype(bf16)
    o0, o1, o2 = ML_COLS, ML_COLS + SSD_COLS, ML_COLS + SSD_COLS + HG_COLS
    zx = _dot(h, w_ref[:, o0:o1])
    ssd_ref[:, 0:SSD_W] = zx[:, 0:SSD_W]
    xbuf_s[HALO:HALO + tm, :] = zx[:, SSD_W:SSD_COLS]
    conv = cb_ref[...]
    for j in range(SSD_CONV):
        o = HALO - (SSD_CONV - 1) + j
        conv = conv + cw_ref[j:j + 1, :] * xbuf_s[o:o + tm, :]
    ssd_ref[:, SSD_W:SSD_COLS] = _silu(conv)
    tail = xbuf_s[tm:tm + HALO, :]
    tail_ref[0] = tail
    xbuf_s[0:HALO, :] = tail
    ml_ref[...] = _dot(h, w_ref[:, 0:o0])
    hg_ref[...] = _dot(h, w_ref[:, o1:o2])
    g_ref[...] = _dot(h, w_ref[:, o2:W_IN_COLS])


def _inproj(x, norm_w, w_cat, layer, conv=None):
    n = x.shape[0]
    tm = ROW_TILE
    row = lambda w: pl.BlockSpec((tm, w), lambda i: (i, 0))
    in_specs = [row(D_MODEL), _layer_resident((1, D_MODEL), layer), _layer_resident((D_MODEL, W_IN_COLS), layer)]
    out_specs = [row(ML_COLS), row(SSD_COLS), row(HG_COLS), row(GATE_W)]
    out_shape = [jax.ShapeDtypeStruct((n, w), f32) for w in (ML_COLS, SSD_COLS, HG_COLS, GATE_W)]
    if conv is None:
        return pl.pallas_call(
            _inproj_body, grid=(n // tm,), in_specs=in_specs, out_specs=out_specs, out_shape=out_shape,
            compiler_params=_cparams("arbitrary"), name="inproj",
        )(x, norm_w, w_cat)
    conv_w, conv_b, rows_per_seq = conv
    tps = rows_per_seq // tm
    assert rows_per_seq % tm == 0
    return pl.pallas_call(
        functools.partial(_inproj_conv_body, tiles_per_seq=tps),
        grid=(n // tm,),
        in_specs=in_specs + [_layer_resident((SSD_CONV, SSD_CONV_DIM), layer), _layer_resident((1, SSD_CONV_DIM), layer)],
        out_specs=out_specs + [pl.BlockSpec((1, HALO, SSD_CONV_DIM), lambda i: (i // tps, 0, 0))],
        out_shape=out_shape + [jax.ShapeDtypeStruct((n // rows_per_seq, HALO, SSD_CONV_DIM), f32)],
        scratch_shapes=[pltpu.VMEM((HALO + tm, SSD_CONV_DIM), f32)],
        compiler_params=_cparams("arbitrary"),
        name="inproj_conv",
    )(x, norm_w, w_cat, conv_w, conv_b)


W_IN_GATE_IF = ML_COLS
W_IN_SSD = W_IN_GATE_IF + 2 * ML_HEADS
W_IN_DT = W_IN_SSD + SSD_COLS
W_IN_HG = W_IN_DT + SSD_HEADS
W_IN_DIM = W_IN_HG + HG_COLS
W_PREP_ROWS = 256


def _prep_w_in_body(w_ref, out_ref):
    lane = _iota2((W_PREP_ROWS, LANES), 1)
    tile = lambda k: w_ref[:, k * LANES:(k + 1) * LANES]
    out_ref[:, 0:ML_COLS] = w_ref[:, 0:ML_COLS].astype(bf16)

    def shifted(dst0, src0, ntiles):
        t0, sh = src0 // LANES, src0 % LANES
        rolled = [pltpu.roll(tile(t0 + k), LANES - sh, axis=1) for k in range(ntiles + 1)]
        for k in range(ntiles):
            out_ref[:, dst0 + k * LANES:dst0 + (k + 1) * LANES] = jnp.where(
                lane < LANES - sh, rolled[k], rolled[k + 1]).astype(bf16)

    shifted(ML_COLS, W_IN_SSD, SSD_COLS // LANES)
    shifted(ML_COLS + SSD_COLS, W_IN_HG, HG_COLS // LANES)
    assert W_IN_GATE_IF % LANES == 0 and W_IN_DT % LANES == DT_LANE
    gates = jnp.where(lane < DT_LANE, tile(W_IN_GATE_IF // LANES),
                      jnp.where(lane < DT_LANE + SSD_HEADS, tile(W_IN_DT // LANES), 0.0))
    out_ref[:, ML_COLS + SSD_COLS + HG_COLS:W_IN_COLS] = gates.astype(bf16)


def _prep_w_in(w_in):
    depth, d, n = w_in.shape
    assert n == W_IN_DIM and d % W_PREP_ROWS == 0
    return pl.pallas_call(
        _prep_w_in_body,
        grid=(depth, d // W_PREP_ROWS),
        in_specs=[pl.BlockSpec((None, W_PREP_ROWS, W_IN_COLS), lambda l, i: (l, i, 0))],
        out_specs=pl.BlockSpec((None, W_PREP_ROWS, W_IN_COLS), lambda l, i: (l, i, 0)),
        out_shape=jax.ShapeDtypeStruct((depth, d, W_IN_COLS), bf16),
        compiler_params=_cparams("arbitrary", "arbitrary"),
        name="prep_w_in",
    )(w_in)


def _one_seq(ref, bi):
    return ref.at[pl.ds(bi, 1)]


def _interleave(chunks):
    live = list(chunks)
    while live:
        nxt = []
        for g in live:
            try:
                next(g)
                nxt.append(g)
            except StopIteration:
                pass
        live = nxt


def _mlstm_prompt_chunk(ml_ref, g_ref, gb_ref, nw_ref, out_ref, ct_s, nr_s, m_s):
    L = PROMPT_CHUNK
    row = _iota2((L, L), 0)
    col = _iota2((L, L), 1)
    causal = row <= col
    tri_b = _ones_where(causal)
    top = row < HEAD_W
    lo_half = col < HEAD_W
    bd_mask = _same_head(row, col)
    sub8 = _iota2((8, L), 0)
    lo8 = _iota2((8, L), 1) < HEAD_W
    n_mask = jnp.where(sub8 == 0, jnp.where(lo8, 1.0, 0.0), jnp.where(sub8 == 1, jnp.where(lo8, 0.0, 1.0), 0.0))

    g_t = (g_ref[0] + gb_ref[...]).T
    b_t = None
    parts = _split_bf16(jax.nn.log_sigmoid(g_t[0:GATE_ROWS, :]), 3)
    yield
    for part in parts:
        d = _dot(part, tri_b)
        b_t = d if b_t is None else b_t + d
        yield
    z_rows = b_t[ML_HEADS:ML_HEADS + 8, :] - g_t[0:8, :]
    z_nat = jnp.concatenate([z_rows, jnp.zeros((L - 8, L), f32)], axis=0).T
    m_old = m_s[...]
    m_next = m_old
    yield

    for p in range(ML_HEADS // 2):
        sl = slice(p * LANES, (p + 1) * LANES)
        q2 = ml_ref[0, :, sl]
        k2 = ml_ref[0, :, ML_W + p * LANES:ML_W + (p + 1) * LANES] * (HEAD_W ** -0.5)
        v2 = ml_ref[0, :, 2 * ML_W + p * LANES:2 * ML_W + (p + 1) * LANES]
        o2 = ml_ref[0, :, 3 * ML_W + p * LANES:3 * ML_W + (p + 1) * LANES]
        q_t = q2.T.astype(bf16)
        v_t = v2.T
        kb = k2.astype(bf16)
        k_sel = (jnp.where(lo_half, k2, 0.0).astype(bf16), jnp.where(lo_half, 0.0, k2).astype(bf16))
        sw, den, a_int, e_m, a_st, w_s = [], [], [], [], [], []
        yield
        for j in range(2):
            h = 2 * p + j
            b_row = b_t[ML_HEADS + h:ML_HEADS + h + 1, :]
            i_row = g_t[h:h + 1, :]
            m_prev = m_old[h:h + 1, :]
            dm = jnp.where(causal, b_row - z_nat[:, h:h + 1], NEG_INF)
            inter = b_row + m_prev
            raw = _dot(k_sel[j], q_t)
            yield
            m_t = jnp.maximum(inter, jnp.max(dm, axis=0, keepdims=True))
            yield
            s_w = raw * jnp.exp(dm - m_t)
            den.append(jnp.sum(s_w, axis=0, keepdims=True))
            sw.append(s_w.astype(bf16))
            yield
            a_int.append(jnp.exp(inter - m_t))
            e_m.append(jnp.exp(-m_t))
            m_new = jnp.broadcast_to(m_t[:, L - 1:L], (1, L))
            b_last = jnp.broadcast_to(b_row[:, L - 1:L], (1, L))
            a_st.append(jnp.exp(b_last + m_prev - m_new))
            w_s.append(jnp.exp(b_last - b_row + i_row - m_new))
            m_next = jnp.where(sub8 == h, m_new, m_next)
            yield
        ct = ct_s[p]
        nr = nr_s[p]
        qn = _dot(nr.astype(bf16), q_t)
        num0 = _dot(jnp.where(top, v_t, 0.0).astype(bf16), sw[0])
        yield
        num1 = _dot(jnp.where(top, 0.0, v_t).astype(bf16), sw[1])
        qc = _dot(ct.astype(bf16), q_t)
        yield
        num = num0 + num1 + jnp.where(top, a_int[0], a_int[1]) * qc
        den0 = den[0] + a_int[0] * qn[0:1, :]
        den1 = den[1] + a_int[1] * qn[1:2, :]
        hh = num / jnp.where(top, jnp.maximum(jnp.abs(den0), e_m[0]), jnp.maximum(jnp.abs(den1), e_m[1]))
        yield
        sq = hh * hh
        r0 = lax.rsqrt(jnp.sum(sq[0:HEAD_W, :], axis=0, keepdims=True) * (1.0 / HEAD_W) + EPS)
        r1 = lax.rsqrt(jnp.sum(sq[HEAD_W:LANES, :], axis=0, keepdims=True) * (1.0 / HEAD_W) + EPS)
        y = (hh * jnp.where(top, r0, r1)).T * nw_ref[:, sl] * jax.nn.sigmoid(o2)
        out_ref[0, :, sl] = y.astype(bf16)
        yield
        vw = (v_t * jnp.where(top, w_s[0], w_s[1])).astype(bf16)
        ct_s[p] = jnp.where(top, a_st[0], a_st[1]) * ct + jnp.where(bd_mask, _dot(vw, kb), 0.0)
        w8 = jnp.where(sub8 == 0, w_s[0], jnp.where(sub8 == 1, w_s[1], 0.0)).astype(bf16)
        nr_s[p] = jnp.where(sub8 == 0, a_st[0], a_st[1]) * nr + n_mask * _dot(w8, kb)
        yield
    m_s[...] = m_next


DT_LANE = 2 * ML_HEADS
SSD_GATE_ROW0 = DT_LANE // 8 * 8
SSD_HEAD_ROW = DT_LANE - SSD_GATE_ROW0
HALO = 8


def _ssd_prompt_chunk(ssd_ref, g_ref, gb_ref, al_ref, d_ref, nw_ref, out_ref, h_s):
    L = PROMPT_CHUNK
    xc = ssd_ref[0, :, SSD_W:SSD_COLS]
    nb = SSD_W
    b_b = [xc[:, nb + gi * LANES:nb + (gi + 1) * LANES].astype(bf16) for gi in range(2)]
    c_b = [xc[:, nb + (2 + gi) * LANES:nb + (3 + gi) * LANES].astype(bf16) for gi in range(2)]

    row = _iota2((L, L), 0)
    col = _iota2((L, L), 1)
    tril = col <= row
    lo_half = col < HEAD_W
    triu_b = _ones_where(row <= col)
    bd_b = _ones_where(_same_head(row, col))
    rowc = _iota2((L, 1), 0)

    g_t = (g_ref[0] + gb_ref[...]).T
    dt_t = jax.nn.softplus(g_t[SSD_GATE_ROW0:SSD_GATE_ROW0 + GATE_ROWS, :])
    yield
    cum_t = None
    for part in _split_bf16(dt_t * (-jnp.exp(al_ref[...])), 3):
        d = _dot(part, triu_b)
        cum_t = d if cum_t is None else cum_t + d
    yield
    nat = jnp.concatenate([cum_t, dt_t, cum_t * LOG2_E, jnp.zeros((L - 3 * GATE_ROWS, L), f32)], axis=0).T
    cb_g = [_dot_nt(c_b[gi], b_b[gi]) for gi in range(2)]
    yield

    for p in range(SSD_HEADS // 2):
        sl = slice(p * LANES, (p + 1) * LANES)
        x2 = xc[:, sl]
        z2 = ssd_ref[0, :, sl]
        hp = h_s[p]
        hb = hp.astype(bf16)
        grp = [(2 * p + j) // (SSD_HEADS // 2) for j in range(2)]
        ys = None
        ecum, w_s, a_last, ch = [], [], [], []
        for j in range(2):
            r = SSD_HEAD_ROW + 2 * p + j
            cum_col = nat[:, r:r + 1]
            cum_last = cum_col[L - 1:L, :]
            key_row = cum_t[r:r + 1, :] * LOG2_E - jnp.log2(dt_t[r:r + 1, :])
            lm_dt = jnp.exp2(jnp.where(tril, nat[:, 2 * GATE_ROWS + r:2 * GATE_ROWS + r + 1] - key_row, NEG_INF))
            m = (cb_g[grp[j]] * lm_dt).astype(bf16)
            yield
            xm = jnp.where(lo_half, x2, 0.0) if j == 0 else jnp.where(lo_half, 0.0, x2)
            d = _dot(m, xm.astype(bf16))
            ys = d if ys is None else ys + d
            ecum.append(jnp.exp(cum_col))
            w_s.append(jnp.exp(cum_last - cum_col) * nat[:, GATE_ROWS + r:GATE_ROWS + r + 1])
            a_last.append(jnp.exp(cum_last))
            if j == 0 or grp[1] != grp[0]:
                ch.append(_dot_nt(c_b[grp[j]], hb))
            else:
                ch.append(ch[0])
            yield
        ys = ys + jnp.where(lo_half, ecum[0] * ch[0], ecum[1] * ch[1])
        xw_t = (x2 * jnp.where(lo_half, w_s[0], w_s[1])).T.astype(bf16)
        yield
        if grp[0] == grp[1]:
            upd = _dot(xw_t, b_b[grp[0]])
        else:
            upd = jnp.where(row < HEAD_W, _dot(xw_t, b_b[grp[0]]), _dot(xw_t, b_b[grp[1]]))
        h_s[p] = jnp.where(rowc < HEAD_W, a_last[0], a_last[1]) * hp + upd
        yy = (ys + d_ref[:, sl] * x2) * _silu(z2)
        yield
        ms = _group_mean(yy * yy, bd_b)
        yield
        out_ref[0, :, sl] = (yy * lax.rsqrt(ms + EPS) * nw_ref[:, sl]).astype(bf16)


def _hg_lower_bound(lbp, layer):
    mx = jnp.max(lbp, axis=0, keepdims=True)
    e = jnp.exp(lbp - mx)
    den = jnp.sum(e, axis=0, keepdims=True)
    lb = jnp.zeros_like(den)
    for j in range(1, layer + 1):
        lb = lb + e[j:j + 1, :] / den
    return lb


def _hgrn_prompt_chunk(hg_ref, lbp_ref, nw_ref, out_ref, sbd_s, p_s, r_s, o_s, g_s, k_s, v_s, q_s, kt_s, a_s, *, layer):
    L = HG_CHUNK
    nblk = L // 8
    lb = _hg_lower_bound(lbp_ref[...], layer)
    ff = hg_ref[0, :, HG_W:2 * HG_W]
    sig = jax.nn.sigmoid(ff)
    log2_k = jnp.log2((1.0 - lb) * jax.nn.sigmoid(-ff))
    v_s[...] = hg_ref[0, :, 2 * HG_W:3 * HG_W]
    q_s[...] = hg_ref[0, :, 0:HG_W] * (HEAD_W ** -0.5)
    row = _iota2((L, L), 0)
    col = _iota2((L, L), 1)
    tri_b = _ones_where(col <= row)
    yield
    g2 = _cumsum_rows(jnp.log(lb + (1.0 - lb) * sig), tri_b) * LOG2_E
    g_s[...] = g2
    k_s[...] = g2 - log2_k
    yield

    r2 = _iota2((2 * LANES, 2 * LANES), 0)
    c2 = _iota2((2 * LANES, 2 * LANES), 1)
    bd4_b = _ones_where(_same_head(r2, c2))
    r1 = _iota2((LANES, LANES), 0)
    c1 = _iota2((LANES, LANES), 1)
    bd_mask = _same_head(r1, c1)
    bd_b = _ones_where(bd_mask)

    a_s[0:8, :] = jnp.zeros((8, HG_W), f32)
    for i in range(1, nblk):
        blk = slice(8 * i, 8 * i + 8)
        n = 8 * i
        r_i = g_s[n - 1:n, :]
        kt = jnp.exp2(r_i - k_s[0:n, :])
        first_head = jnp.bitwise_and(_iota2((n, HG_W), 1), LANES - 1) < HEAD_W
        kt_s[i - 1, 0:n, :] = jnp.where(first_head, kt, 0.0)
        kt_s[i - 1, n:L, :] = jnp.zeros((L - n, HG_W), f32)
        kt_s[i - 1, L:L + n, :] = jnp.where(first_head, 0.0, kt)
        kt_s[i - 1, L + n:2 * L, :] = jnp.zeros((L - n, HG_W), f32)
        qt = q_s[blk, :] * jnp.exp2(g_s[blk, :] - r_i)
        qt = jnp.concatenate([qt, jnp.zeros_like(qt)], axis=0).astype(bf16)
        for p in range(HG_HEADS // 2):
            sl = slice(p * LANES, (p + 1) * LANES)
            a_s[blk, sl] = _dot_nt(qt[:, sl], kt_s[i - 1, :, sl].astype(bf16))[0:8, :]
        yield

    sub = _iota2((8, HG_W), 0)
    for j in range(nblk):
        blk = slice(8 * j, 8 * j + 8)
        g_blk, gk_blk, q_blk = g_s[blk, :], k_s[blk, :], q_s[blk, :]
        for r in range(8):
            s = 8 * j + r
            p_s[8 * s:8 * s + 8, :] = q_blk * jnp.exp2(jnp.where(sub >= r, g_blk - gk_blk[r:r + 1, :], NEG_INF))
        yield
    r_s[...] = _dot(p_s[...].astype(bf16), bd4_b)
    yield

    gg = g_s[...]
    vv = v_s[...]
    qd = q_s[...] * jnp.exp2(gg)
    lo_half = _iota2((L, LANES), 1) < HEAD_W
    for p in range(HG_HEADS // 2):
        sl = slice(p * LANES, (p + 1) * LANES)
        v_cat = jnp.concatenate([jnp.where(lo_half, vv[:, sl], 0.0), jnp.where(lo_half, 0.0, vv[:, sl])], axis=0)
        o_s[:, sl] = (_dot(qd[:, sl].astype(bf16), sbd_s[p].astype(bf16))
                      + _dot(a_s[:, sl].astype(bf16), v_cat.astype(bf16)))
        yield
    for j in range(nblk):
        blk = slice(8 * j, 8 * j + 8)
        acc = o_s[blk, :]
        v_blk = v_s[blk, :]
        for r in range(8):
            s = 8 * j + r
            acc = acc + r_s[8 * s:8 * s + 8, :] * v_blk[r:r + 1, :]
        o_s[blk, :] = acc
        yield

    g_last = gg[L - 1:L, :]
    kd = jnp.exp2(g_last - k_s[...])
    zpad = jnp.zeros((LANES - L, LANES), f32)
    og = o_s[...]
    gate = hg_ref[0, :, 3 * HG_W:4 * HG_W]
    for p in range(HG_HEADS // 2):
        sl = slice(p * LANES, (p + 1) * LANES)
        kd_t = jnp.concatenate([kd[:, sl], zpad], axis=0).T.astype(bf16)
        v_pad = jnp.concatenate([vv[:, sl], zpad], axis=0).astype(bf16)
        g_t = jnp.concatenate([gg[:, sl], zpad], axis=0).T
        dec = jnp.exp2(g_t[:, L - 1:L])
        sbd_s[p] = dec * sbd_s[p] + jnp.where(bd_mask, _dot(kd_t, v_pad), 0.0)
        yield
        oo = og[:, sl]
        ms = _group_mean(oo * oo, bd_b)
        out_ref[0, :, sl] = (oo * lax.rsqrt(ms + EPS) * nw_ref[:, sl] * _silu(gate[:, sl])).astype(bf16)


def _mixers_prompt_body(ml_ref, ssd_ref, hg_ref, g_ref, gb_ref, al_ref, d_ref, mlnw_ref, ssnw_ref,
                        lbp_ref, hgnw_ref,
                        hm_ref, c1_ref, n1_ref, m1_ref, hs_ref, h1_ref, ho_ref, s1_ref,
                        ct_s, nr_s, m_s, h_s, sbd_s, p_s, r_s, o_s, gg_s, k_s, v_s, q_s, kt_s, a_s, *, layer):
    c = pl.program_id(1)
    seqs = range(ml_ref.shape[0])

    @pl.when(c == 0)
    def _():
        ct_s[...] = jnp.zeros(ct_s.shape, f32)
        nr_s[...] = jnp.zeros(nr_s.shape, f32)
        m_s[...] = jnp.zeros(m_s.shape, f32)
        h_s[...] = jnp.zeros(h_s.shape, f32)
        sbd_s[...] = jnp.zeros(sbd_s.shape, f32)

    def hgrn_chunks(bi):
        for half in range(PROMPT_CHUNK // HG_CHUNK):
            rows = pl.ds(half * HG_CHUNK, HG_CHUNK)
            yield from _hgrn_prompt_chunk(hg_ref.at[pl.ds(bi, 1), rows], lbp_ref, hgnw_ref, ho_ref.at[pl.ds(bi, 1), rows],
                                          sbd_s.at[bi], p_s.at[bi], r_s.at[bi], o_s.at[bi], gg_s.at[bi], k_s.at[bi],
                                          v_s.at[bi], q_s.at[bi], kt_s.at[bi], a_s.at[bi], layer=layer)

    chunks = []
    for bi in seqs:
        chunks.append(_ssd_prompt_chunk(_one_seq(ssd_ref, bi), _one_seq(g_ref, bi), gb_ref, al_ref, d_ref,
                                        ssnw_ref, _one_seq(hs_ref, bi), h_s.at[bi]))
        chunks.append(hgrn_chunks(bi))
        chunks.append(_mlstm_prompt_chunk(_one_seq(ml_ref, bi), _one_seq(g_ref, bi), gb_ref, mlnw_ref,
                                          _one_seq(hm_ref, bi), ct_s.at[bi], nr_s.at[bi], m_s.at[bi]))
    _interleave(chunks)

    @pl.when(c == pl.num_programs(1) - 1)
    def _():
        h1_ref[...] = h_s[...]
        for bi in seqs:
            for p in range(ML_HEADS // 2):
                c_pair = ct_s[bi, p].T
                c1_ref[bi, 2 * p] = c_pair[0:HEAD_W, 0:HEAD_W]
                c1_ref[bi, 2 * p + 1] = c_pair[HEAD_W:LANES, HEAD_W:LANES]
                n1_ref[bi, 2 * p:2 * p + 1, :] = nr_s[bi, p, 0:1, 0:HEAD_W]
                n1_ref[bi, 2 * p + 1:2 * p + 2, :] = nr_s[bi, p, 1:2, HEAD_W:LANES]
            m1_ref[bi] = m_s[bi]
            for p in range(HG_HEADS // 2):
                sbd = sbd_s[bi, p]
                s1_ref[bi, 2 * p] = sbd[0:HEAD_W, 0:HEAD_W]
                s1_ref[bi, 2 * p + 1] = sbd[HEAD_W:LANES, HEAD_W:LANES]


def _mixers_prompt(ml, ssd, hg, gates, gate_bias, alog_col, d_row, ml_nw, ssd_nw, lbp, hg_nw, layer):
    b, t, _ = ml.shape
    L = PROMPT_CHUNK
    ns = SEQS_PER_STEP
    assert b % ns == 0 and t % L == 0
    ml_pairs, ssd_pairs, hg_pairs = ML_HEADS // 2, SSD_HEADS // 2, HG_HEADS // 2
    tok = lambda w: pl.BlockSpec((ns, L, w), lambda i, c: (i, c, 0))
    per_seq = lambda *shape: pl.BlockSpec((ns,) + shape, lambda i, c: (i,) + (0,) * len(shape))
    hg_rows = pltpu.VMEM((ns, HG_CHUNK, HG_W), f32)
    hg_pairs_buf = pltpu.VMEM((ns, 8 * HG_CHUNK, HG_W), f32)
    return pl.pallas_call(
        functools.partial(_mixers_prompt_body, layer=layer),
        grid=(b // ns, t // L),
        in_specs=[tok(ML_COLS), tok(SSD_COLS), tok(HG_COLS), tok(GATE_W),
                  _layer_resident((1, GATE_W), layer), _layer_resident((GATE_ROWS, 1), layer),
                  _layer_resident((1, SSD_W), layer), _layer_resident((1, ML_W), layer),
                  _layer_resident((1, SSD_W), layer), _resident((DEPTH, HG_W)), _layer_resident((1, HG_W), layer)],
        out_specs=[tok(ML_W), per_seq(ML_HEADS, HEAD_W, HEAD_W), per_seq(ML_HEADS, HEAD_W), per_seq(8, LANES),
                   tok(SSD_W), per_seq(ssd_pairs, LANES, SSD_DSTATE),
                   tok(HG_W), per_seq(HG_HEADS, HEAD_W, HEAD_W)],
        out_shape=[jax.ShapeDtypeStruct((b, t, ML_W), bf16),
                   jax.ShapeDtypeStruct((b, ML_HEADS, HEAD_W, HEAD_W), f32),
                   jax.ShapeDtypeStruct((b, ML_HEADS, HEAD_W), f32),
                   jax.ShapeDtypeStruct((b, 8, LANES), f32),
                   jax.ShapeDtypeStruct((b, t, SSD_W), bf16),
                   jax.ShapeDtypeStruct((b, ssd_pairs, LANES, SSD_DSTATE), f32),
                   jax.ShapeDtypeStruct((b, t, HG_W), bf16),
                   jax.ShapeDtypeStruct((b, HG_HEADS, HEAD_W, HEAD_W), f32)],
        scratch_shapes=[pltpu.VMEM((ns, ml_pairs, LANES, LANES), f32), pltpu.VMEM((ns, ml_pairs, 8, LANES), f32),
                        pltpu.VMEM((ns, 8, LANES), f32),
                        pltpu.VMEM((ns, ssd_pairs, LANES, SSD_DSTATE), f32),
                        pltpu.VMEM((ns, hg_pairs, LANES, LANES), f32), hg_pairs_buf, hg_pairs_buf,
                        hg_rows, hg_rows, hg_rows, hg_rows, hg_rows,
                        pltpu.VMEM((ns, HG_CHUNK // 8 - 1, 2 * HG_CHUNK, HG_W), f32), hg_rows],
        compiler_params=_cparams("arbitrary", "arbitrary"),
        name="mixers_prompt",
    )(ml, ssd, hg, gates, gate_bias, alog_col, d_row, ml_nw, ssd_nw, lbp, hg_nw)


NB = 128
TS = 8


def _sample_call(body, name, grid, in_specs, operands, out_specs, out_shapes, scratch, n_state):
    first = len(operands) - n_state
    return pl.pallas_call(
        body, grid=grid, in_specs=in_specs, out_specs=out_specs, out_shape=out_shapes,
        scratch_shapes=scratch, input_output_aliases={first + i: 1 + i for i in range(n_state)},
        compiler_params=_cparams(*(["arbitrary"] * len(grid))), name=name,
    )(*operands)


def _mlstm_sample_body(q_ref, k_ref, v_ref, o_ref, g_ref, gb_ref, nw_ref, c0_ref, n0_ref, m0_ref,
                       out_ref, c1_ref, n1_ref, m1_ref, qt_s, kt_s, vt_s, gt_s, ht_s):
    p = pl.program_id(0)
    n_cur = [n0_ref[hl] for hl in range(2)]
    m_cur = [m0_ref[pl.ds(2 * p + hl, 1), :] for hl in range(2)]

    for t in range(TS):
        c_src = c0_ref if t == 0 else c1_ref
        rows = slice(t * NB, (t + 1) * NB)
        qt_s[...] = q_ref[rows, :].T
        kt_s[...] = k_ref[rows, :].T * (HEAD_W ** -0.5)
        vt_s[...] = v_ref[rows, :].T
        gt_s[...] = (g_ref[rows, :] + gb_ref[...]).T
        for hl in range(2):
            h = 2 * p + hl
            hs = slice(hl * HEAD_W, (hl + 1) * HEAD_W)
            i_t = gt_s[pl.ds(h, 1), :]
            lf = jax.nn.log_sigmoid(gt_s[pl.ds(ML_HEADS + h, 1), :])
            m_new = jnp.maximum(lf + m_cur[hl], i_t)
            a = jnp.exp(lf + m_cur[hl] - m_new)
            w = jnp.exp(i_t - m_new)
            n_new = a * n_cur[hl] + w * kt_s[hs, :]
            v_h = vt_s[hs, :]

            def body(d, num, hl=hl, a=a, w=w, v_h=v_h, c_src=c_src):
                kd = kt_s[pl.ds(hl * HEAD_W + d, 1), :] * w
                qd = qt_s[pl.ds(hl * HEAD_W + d, 1), :]
                cd = a * c_src[hl, d] + kd * v_h
                c1_ref[hl, d] = cd
                return num + qd * cd

            num = lax.fori_loop(0, HEAD_W, body, jnp.zeros((HEAD_W, NB), f32), unroll=4)
            den = jnp.sum(qt_s[hs, :] * n_new, axis=0, keepdims=True)
            hh = num / jnp.maximum(jnp.abs(den), jnp.exp(-m_new))
            ms = jnp.mean(hh * hh, axis=0, keepdims=True)
            ht_s[hs, :] = hh * lax.rsqrt(ms + EPS)
            n_cur[hl] = n_new
            m_cur[hl] = m_new
        out_ref[rows, :] = (ht_s[...].T * nw_ref[...] * jax.nn.sigmoid(o_ref[rows, :])).astype(bf16)

    for hl in range(2):
        n1_ref[hl] = n_cur[hl]
        m1_ref[pl.ds(2 * p + hl, 1), :] = m_cur[hl]


def _mlstm_sample(ml, gates, gate_bias, norm_w, c0t, n0t, m0t, layer):
    n = ml.shape[0]
    npair = ML_HEADS // 2
    blk = lambda off: pl.BlockSpec((n, LANES), lambda p, off=off: (0, off + p))
    c_spec = pl.BlockSpec((None, 2, HEAD_W, HEAD_W, NB), lambda p: (layer, p, 0, 0, 0))
    n_spec = pl.BlockSpec((None, 2, HEAD_W, NB), lambda p: (layer, p, 0, 0))
    m_spec = pl.BlockSpec((None, ML_HEADS, NB), lambda p: (layer, 0, 0))
    return _sample_call(
        _mlstm_sample_body, "mlstm_sample", (npair,),
        [blk(0), blk(npair), blk(2 * npair), blk(3 * npair), _resident((n, GATE_W)),
         _layer_resident((1, GATE_W), layer), pl.BlockSpec((None, 1, LANES), lambda p: (layer, 0, p)),
         c_spec, n_spec, m_spec],
        [ml, ml, ml, ml, gates, gate_bias, norm_w, c0t, n0t, m0t],
        [pl.BlockSpec((n, LANES), lambda p: (0, p)), c_spec, n_spec, m_spec],
        [jax.ShapeDtypeStruct((n, ML_W), bf16), jax.ShapeDtypeStruct(c0t.shape, f32),
         jax.ShapeDtypeStruct(n0t.shape, f32), jax.ShapeDtypeStruct(m0t.shape, f32)],
        [pltpu.VMEM((LANES, NB), f32)] * 5, 3)


def _ssd_conv_sample_body(ssd_ref, halo_ref, cw_ref, cb_ref, out_ref, xp_s):
    nh = (SSD_CONV - 1) * NB
    n = TS * NB
    xp_s[0:nh, :] = halo_ref[...]
    xp_s[nh:nh + n, :] = ssd_ref[:, SSD_W:SSD_COLS]
    conv = cb_ref[...]
    for j in range(SSD_CONV):
        conv = conv + cw_ref[j:j + 1, :] * xp_s[j * NB:j * NB + n, :]
    out_ref[...] = _silu(conv)


def _ssd_conv_sample(ssd, halo, conv_w, conv_b, layer):
    n = ssd.shape[0]
    nh = (SSD_CONV - 1) * NB
    return pl.pallas_call(
        _ssd_conv_sample_body,
        grid=(1,),
        in_specs=[_resident((n, SSD_COLS)), _layer_resident((nh, SSD_CONV_DIM), layer),
                  _layer_resident((SSD_CONV, SSD_CONV_DIM), layer), _layer_resident((1, SSD_CONV_DIM), layer)],
        out_specs=pl.BlockSpec((n, SSD_CONV_DIM), lambda i: (0, 0)),
        out_shape=jax.ShapeDtypeStruct((n, SSD_CONV_DIM), f32),
        scratch_shapes=[pltpu.VMEM((nh + n, SSD_CONV_DIM), f32)],
        compiler_params=_cparams("arbitrary"),
        name="ssd_conv_sample",
    )(ssd, halo, conv_w, conv_b)


def _ssd_sample_body(z_ref, x_ref, b0_ref, b1_ref, c0_ref, c1_ref, g_ref, gb_ref, hp_ref, nw_ref, h0_ref,
                     out_ref, h1_ref, xt_s, bt_s, ct_s, gt_s, dd_s, yt_s, *, layer, all_layers):
    p = pl.program_id(0)
    head = pl.program_id(1)
    b_refs = (b0_ref, b1_ref)
    c_refs = (c0_ref, c1_ref)
    h0 = h0_ref.at[layer] if all_layers else h0_ref
    h1 = h1_ref.at[layer] if all_layers else h1_ref
    if all_layers:
        for l in range(DEPTH):
            if l != layer:
                h1_ref[l] = h0_ref[l]

    @pl.when(head == 0)
    def _():
        for t in range(TS):
            rows = slice(t * NB, (t + 1) * NB)
            xt_s[t] = x_ref[rows, :].T
            gt_s[...] = (g_ref[rows, :] + gb_ref[...]).T
            for hl in range(2):
                bt_s[t, hl] = b_refs[hl][rows, :].T
                ct_s[t, hl] = c_refs[hl][rows, :].T
                dt = jax.nn.softplus(gt_s[pl.ds(DT_LANE + 2 * p + hl, 1), :])
                dd_s[t, 2 * hl:2 * hl + 1, :] = dt
                dd_s[t, 2 * hl + 1:2 * hl + 2, :] = jnp.exp(dt * (-jnp.exp(hp_ref[0, hl:hl + 1, :])))

    for hl in range(2):
        @pl.when(head == hl)
        def _(hl=hl):
            def body(pp, h):
                h_next = h0[:, jnp.minimum(pp + 1, HEAD_W - 1), :].T
                row = pl.ds(hl * HEAD_W + pp, 1)
                for t in range(TS):
                    xrow = xt_s[t, row, :] * dd_s[t, 2 * hl:2 * hl + 1, :]
                    h = dd_s[t, 2 * hl + 1:2 * hl + 2, :] * h + xrow * bt_s[t, hl]
                    yt_s[t, row, :] = jnp.sum(ct_s[t, hl] * h, axis=0, keepdims=True)
                h1[:, pp, :] = h.T
                return h_next

            lax.fori_loop(0, HEAD_W, body, h0[:, 0, :].T, unroll=4)

    @pl.when(head == 1)
    def _():
        for t in range(TS):
            rows = slice(t * NB, (t + 1) * NB)
            zt = z_ref[rows, :].T
            for hl in range(2):
                hs = slice(hl * HEAD_W, (hl + 1) * HEAD_W)
                yy = (yt_s[t, hs, :] + hp_ref[1, hl:hl + 1, :] * xt_s[t, hs, :]) * _silu(zt[hs, :])
                ms = jnp.mean(yy * yy, axis=0, keepdims=True)
                yt_s[t, hs, :] = yy * lax.rsqrt(ms + EPS)
            out_ref[rows, :] = (yt_s[t].T * nw_ref[...]).astype(bf16)


def _ssd_sample(ssd, xc, gates, gate_bias, head_params, norm_w, h0, layer):
    n = ssd.shape[0]
    npair = SSD_HEADS // 2
    hpg = SSD_HEADS // 2
    nxb = SSD_W // LANES
    all_layers = layer == 0
    blk = lambda f: pl.BlockSpec((n, LANES), f)
    if all_layers:
        h_spec = pl.BlockSpec((DEPTH, NB, HEAD_W, SSD_DSTATE), lambda p, j: (0, 0, 2 * p + j, 0))
    else:
        h_spec = pl.BlockSpec((None, NB, HEAD_W, SSD_DSTATE), lambda p, j: (layer, 0, 2 * p + j, 0))
    return _sample_call(
        functools.partial(_ssd_sample_body, layer=layer, all_layers=all_layers), "ssd_sample", (npair, 2),
        [blk(lambda p, j: (0, p)), blk(lambda p, j: (0, p)),
         blk(lambda p, j: (0, nxb + (2 * p) // hpg)), blk(lambda p, j: (0, nxb + (2 * p + 1) // hpg)),
         blk(lambda p, j: (0, nxb + 2 + (2 * p) // hpg)), blk(lambda p, j: (0, nxb + 2 + (2 * p + 1) // hpg)),
         _resident((n, GATE_W)), _layer_resident((1, GATE_W), layer),
         pl.BlockSpec((None, 2, None, 2, LANES), lambda p, j: (layer, 0, p, 0, 0)),
         pl.BlockSpec((None, 1, LANES), lambda p, j: (layer, 0, p)), h_spec],
        [ssd, xc, xc, xc, xc, xc, gates, gate_bias, head_params, norm_w, h0],
        [pl.BlockSpec((n, LANES), lambda p, j: (0, p)), h_spec],
        [jax.ShapeDtypeStruct((n, SSD_W), bf16), jax.ShapeDtypeStruct(h0.shape, f32)],
        [pltpu.VMEM((TS, LANES, NB), f32), pltpu.VMEM((TS, 2, SSD_DSTATE, NB), f32),
         pltpu.VMEM((TS, 2, SSD_DSTATE, NB), f32), pltpu.VMEM((LANES, NB), f32), pltpu.VMEM((TS, 8, NB), f32),
         pltpu.VMEM((TS, LANES, NB), f32)], 0 if all_layers else 1)


def _hgrn_sample_body(q_ref, f_ref, i_ref, gate_ref, lbp_ref, nw_ref, s0_ref, out_ref, s1_ref,
                      qt_s, kt_s, ft_s, vt_s, ot_s, *, layer):
    lb = _hg_lower_bound(lbp_ref[...], layer)

    for t in range(TS):
        s_src = s0_ref if t == 0 else s1_ref
        rows = slice(t * NB, (t + 1) * NB)
        ff = f_ref[rows, :]
        qt_s[...] = (q_ref[rows, :] * (HEAD_W ** -0.5)).T
        kt_s[...] = ((1.0 - lb) * jax.nn.sigmoid(-ff)).T
        ft_s[...] = (lb + (1.0 - lb) * jax.nn.sigmoid(ff)).T
        vt_s[...] = i_ref[rows, :].T
        for hl in range(2):
            hs = slice(hl * HEAD_W, (hl + 1) * HEAD_W)
            v_h = vt_s[hs, :]

            def body(k, o, hl=hl, v_h=v_h, s_src=s_src):
                r = pl.ds(hl * HEAD_W + k, 1)
                sk = ft_s[r, :] * s_src[hl, k] + kt_s[r, :] * v_h
                s1_ref[hl, k] = sk
                return o + qt_s[r, :] * sk

            o = lax.fori_loop(0, HEAD_W, body, jnp.zeros((HEAD_W, NB), f32), unroll=4)
            ms = jnp.mean(o * o, axis=0, keepdims=True)
            ot_s[hs, :] = o * lax.rsqrt(ms + EPS)
        out_ref[rows, :] = (ot_s[...].T * nw_ref[...] * _silu(gate_ref[rows, :])).astype(bf16)


def _hgrn_sample(hg, lbp, norm_w, s0t, layer):
    n = hg.shape[0]
    npair = HG_HEADS // 2
    blk = lambda off: pl.BlockSpec((n, LANES), lambda p, off=off: (0, off + p))
    s_spec = pl.BlockSpec((None, 2, HEAD_W, HEAD_W, NB), lambda p: (layer, p, 0, 0, 0))
    return _sample_call(
        functools.partial(_hgrn_sample_body, layer=layer), "hgrn_sample", (npair,),
        [blk(0), blk(npair), blk(2 * npair), blk(3 * npair),
         pl.BlockSpec((DEPTH, LANES), lambda p: (0, p)),
         pl.BlockSpec((None, 1, LANES), lambda p: (layer, 0, p)), s_spec],
        [hg, hg, hg, hg, lbp, norm_w, s0t],
        [pl.BlockSpec((n, LANES), lambda p: (0, p)), s_spec],
        [jax.ShapeDtypeStruct((n, HG_W), bf16), jax.ShapeDtypeStruct(s0t.shape, f32)],
        [pltpu.VMEM((LANES, NB), f32)] * 5, 1)


def _ffn_body(x_ref, mml_ref, mssd_ref, mhg_ref, wout_ref, n2_ref, wug_ref, wuv_ref, cwg_ref, cwv_ref,
              cbg_ref, cbv_ref, wdn_ref, f0g_ref, f0v_ref, fn_ref, y_ref, f1g_ref, f1v_ref,
              x1_s, h2_s, acc_s, act_s, ubuf_s, *, tm, shift, jb, nff, final):
    t = pl.program_id(1)
    f = pl.program_id(2)
    halo = max(HALO, (FFN_CONV - 1) * shift)
    keep = (FFN_CONV - 1) * shift

    def when(cond, fn):
        if nff == 1:
            fn()
        else:
            pl.when(cond)(fn)

    @pl.when(t == 0)
    def _():
        f1g_ref[0] = f0g_ref[0]
        f1v_ref[0] = f0v_ref[0]

    def first():
        x = x_ref[...]
        mix = jnp.concatenate([mml_ref[...], mssd_ref[...], mhg_ref[...]], axis=1)
        x1 = x + _dot(mix, wout_ref[...])
        x1_s[...] = x1
        h2_s[...] = (x1 * lax.rsqrt(jnp.mean(x1 * x1, axis=-1, keepdims=True) + EPS) * n2_ref[...]).astype(bf16)

    when(f == 0, first)
    h2 = h2_s[...]

    def conv(wu_ref, cw_ref, cb_ref, f1_ref, cols, ubuf):
        u = _dot(h2, wu_ref[:, cols])
        ubuf[halo - keep:halo, :] = f1_ref[0, :, cols]
        ubuf[halo:halo + tm, :] = u
        yv = (cb_ref[:, cols] + cw_ref[2:3, cols] * u
              + cw_ref[1:2, cols] * ubuf[halo - shift:halo - shift + tm, :]
              + cw_ref[0:1, cols] * ubuf[halo - 2 * shift:halo - 2 * shift + tm, :])
        f1_ref[0, :, cols] = ubuf[halo + tm - keep:halo + tm, :]
        return yv

    nbuf = ubuf_s.shape[0]
    for j in range(jb):
        cols = slice(j * FF_BLOCK, (j + 1) * FF_BLOCK)
        gq = conv(wug_ref, cwg_ref, cbg_ref, f1g_ref, cols, ubuf_s.at[(2 * j) % nbuf])
        vq = conv(wuv_ref, cwv_ref, cbv_ref, f1v_ref, cols, ubuf_s.at[(2 * j + 1) % nbuf])
        act_s[:, cols] = (_silu(gq) * vq).astype(bf16)
    contrib = _dot(act_s[...], wdn_ref[...])

    def finish(down):
        x2 = x1_s[...] + down
        if final:
            x2 = x2 * lax.rsqrt(jnp.mean(x2 * x2, axis=-1, keepdims=True) + EPS) * fn_ref[...]
        y_ref[...] = x2

    if nff == 1:
        finish(contrib)
    else:
        @pl.when(f == 0)
        def _():
            acc_s[...] = contrib

        @pl.when(f != 0)
        def _():
            acc_s[...] = acc_s[...] + contrib

        pl.when(f == nff - 1)(lambda: finish(acc_s[...]))


def _ffn(x, mml, mssd, mhg, w_out, norm2, w_up, conv_w, conv_b, w_down, f0, final_w, *, layer, groups, tm, shift, jb,
         final, f0_base=0):
    n = x.shape[0]
    nt = n // (groups * tm)
    fw = jb * FF_BLOCK
    nff = D_FF // fw
    assert nt == 1 or nff == 1
    halo = max(HALO, (FFN_CONV - 1) * shift)
    keep = (FFN_CONV - 1) * shift
    rows = lambda w: pl.BlockSpec((tm, w), lambda g, t, f: (g * nt + t, 0))
    single = nff == 1
    wspec = lambda shape, imap: pl.BlockSpec(shape, imap, pipeline_mode=pl.Buffered(1)) if single else pl.BlockSpec(shape, imap)
    f1_spec = pl.BlockSpec((1, keep, fw), lambda g, t, f: (g, 0, f))
    in_specs = [rows(D_MODEL), rows(ML_W), rows(SSD_W), rows(HG_W),
                _layer_resident((D_MODEL, D_MODEL), layer), _layer_resident((1, D_MODEL), layer),
                wspec((None, D_MODEL, fw), lambda g, t, f: (layer, 0, f)),
                wspec((None, D_MODEL, fw), lambda g, t, f: (layer, 0, nff + f)),
                wspec((None, FFN_CONV, fw), lambda g, t, f: (layer, 0, f)),
                wspec((None, FFN_CONV, fw), lambda g, t, f: (layer, 0, nff + f)),
                wspec((None, 1, fw), lambda g, t, f: (layer, 0, f)),
                wspec((None, 1, fw), lambda g, t, f: (layer, 0, nff + f)),
                wspec((None, fw, D_MODEL), lambda g, t, f: (layer, f, 0)),
                pl.BlockSpec((1, keep, fw), lambda g, t, f: (f0_base + g, 0, f)),
                pl.BlockSpec((1, keep, fw), lambda g, t, f: (f0_base + g, 0, nff + f)),
                _resident((1, D_MODEL))]
    operands = [x, mml, mssd, mhg, w_out, norm2, w_up, w_up, conv_w, conv_w, conv_b, conv_b, w_down, f0, f0, final_w]
    acc_rows = tm if nff > 1 else 8
    return pl.pallas_call(
        functools.partial(_ffn_body, tm=tm, shift=shift, jb=jb, nff=nff, final=final),
        grid=(groups, nt, nff),
        in_specs=in_specs,
        out_specs=[rows(D_MODEL), f1_spec, f1_spec],
        out_shape=[jax.ShapeDtypeStruct((n, D_MODEL), f32),
                   jax.ShapeDtypeStruct((groups, keep, D_FF), f32),
                   jax.ShapeDtypeStruct((groups, keep, D_FF), f32)],
        scratch_shapes=[pltpu.VMEM((tm, D_MODEL), f32), pltpu.VMEM((tm, D_MODEL), bf16),
                        pltpu.VMEM((acc_rows, D_MODEL), f32), pltpu.VMEM((tm, fw), bf16),
                        pltpu.VMEM((min(4, 2 * jb), halo + tm, FF_BLOCK), f32)],
        compiler_params=_cparams("arbitrary", "arbitrary", "arbitrary"),
        name="ffn",
    )(*operands)


def _gate_rows(*parts):
    r = jnp.concatenate([p.astype(f32) for p in parts], axis=1)
    return jnp.pad(r, ((0, 0), (0, GATE_W - r.shape[1])))[:, None, :]


def kernel(x_prompt, x_sample, state_mlstm_C, state_mlstm_n, state_mlstm_m, state_ssd, state_ssd_conv,
           state_hgrn, state_ffn_conv, norm1_w, w_in, ml_ig_b, ml_fg_b, ml_norm_w, ssd_conv_w, ssd_conv_b,
           ssd_dt_bias, ssd_A_log, ssd_D, ssd_norm_w, hg_lower_bounds, hg_norm_w, w_out, norm2_w, w_up,
           ffn_conv_w, ffn_conv_b, w_down, final_norm_w):
    bp, tp, _ = x_prompt.shape
    nb, ts, _ = x_sample.shape
    assert nb == NB and ts == TS and tp % ROW_TILE == 0
    depth = w_in.shape[0]
    assert depth == DEPTH
    xp = x_prompt.reshape(bp * tp, D_MODEL)
    xs = jnp.transpose(x_sample, (1, 0, 2)).reshape(ts * nb, D_MODEL)
    lbp = hg_lower_bounds.astype(f32)
    fin_w = final_norm_w[None, :]
    zero_f = jnp.zeros((bp, FFN_CONV - 1, 2 * D_FF), f32)

    w_cat = _prep_w_in(w_in)
    wo_b, wu_b, wd_b = w_out.astype(bf16), w_up.astype(bf16), w_down.astype(bf16)
    n1, n2 = norm1_w[:, None, :], norm2_w[:, None, :]
    gate_bias = _gate_rows(ml_ig_b, ml_fg_b, ssd_dt_bias)
    alog_col = jnp.pad(ssd_A_log.astype(f32), ((0, 0), (SSD_HEAD_ROW, GATE_ROWS - SSD_HEAD_ROW - SSD_HEADS)))[:, :, None]
    d_row = jnp.repeat(ssd_D.astype(f32), HEAD_W, axis=1)[:, None, :]
    mlw, ssw, hgw = ml_norm_w[:, None, :], ssd_norm_w[:, None, :], hg_norm_w[:, None, :]
    cw, cb = ssd_conv_w, ssd_conv_b[:, None, :]
    fcw, fcb = ffn_conv_w, ffn_conv_b[:, None, :]
    head_params = jnp.broadcast_to(
        jnp.stack([ssd_A_log, ssd_D], axis=1).astype(f32).reshape(depth, 2, SSD_HEADS // 2, 2, 1),
        (depth, 2, SSD_HEADS // 2, 2, LANES))

    c_t = jnp.transpose(state_mlstm_C.astype(f32), (0, 2, 3, 4, 1))
    n_t = jnp.transpose(state_mlstm_n.astype(f32), (0, 2, 3, 1))
    m_t = jnp.transpose(state_mlstm_m.astype(f32), (0, 2, 1))
    h_n = state_ssd.astype(f32).reshape(depth, nb, SSD_HEADS * HEAD_W, SSD_DSTATE)
    s_t = jnp.transpose(state_hgrn.astype(f32), (0, 2, 3, 4, 1))
    conv_halo = jnp.transpose(state_ssd_conv.astype(f32), (0, 2, 1, 3)).reshape(depth, (SSD_CONV - 1) * nb, SSD_CONV_DIM)
    f_t = jnp.transpose(state_ffn_conv.astype(f32), (0, 2, 1, 3)).reshape(depth, (FFN_CONV - 1) * nb, 2 * D_FF)

    p_states, s_sconv, s_ffn = [], [], []
    for l in range(depth):
        last = l == depth - 1

        ml, ssd, hg, gates, p_tail = _inproj(xp, n1, w_cat, l, conv=(cw, cb, tp))
        hm, p_c, p_n, p_m, hs, p_h, ho, p_s = _mixers_prompt(
            ml.reshape(bp, tp, ML_COLS), ssd.reshape(bp, tp, SSD_COLS), hg.reshape(bp, tp, HG_COLS),
            gates.reshape(bp, tp, GATE_W), gate_bias, alog_col, d_row, mlw, ssw, lbp, hgw, l)
        xp, p_fg, p_fv = _ffn(xp, hm.reshape(bp * tp, ML_W), hs.reshape(bp * tp, SSD_W), ho.reshape(bp * tp, HG_W),
                              wo_b, n2, wu_b, fcw, fcb, wd_b, zero_f, fin_w, layer=l,
                              groups=bp, tm=ROW_TILE, shift=1, jb=D_FF // FF_BLOCK, final=last)
        p_states.append((p_c, p_n, p_m[:, 0:ML_HEADS, 0],
                         p_h.reshape(bp, SSD_HEADS, HEAD_W, SSD_DSTATE),
                         p_tail[:, HALO - (SSD_CONV - 1):, :], p_s,
                         jnp.concatenate([p_fg, p_fv], axis=-1)))

        ml, ssd, hg, gates = _inproj(xs, n1, w_cat, l)
        hm, c_t, n_t, m_t = _mlstm_sample(ml, gates, gate_bias, mlw, c_t, n_t, m_t, l)
        xc = _ssd_conv_sample(ssd, conv_halo, cw, cb, l)
        hs, h_n = _ssd_sample(ssd, xc, gates, gate_bias, head_params, ssw, h_n, l)
        ho, s_t = _hgrn_sample(hg, lbp, hgw, s_t, l)
        xs, s_fg, s_fv = _ffn(xs, hm, hs, ho, wo_b, n2, wu_b, fcw, fcb, wd_b, f_t, fin_w, layer=l,
                              groups=1, tm=ts * nb, shift=nb, jb=1, final=last, f0_base=l)
        s_ffn.append(jnp.concatenate([s_fg, s_fv], axis=-1).reshape(FFN_CONV - 1, nb, 2 * D_FF))
        raw_xbc = ssd[:, SSD_W:].reshape(ts, nb, SSD_CONV_DIM)
        s_sconv.append(jnp.transpose(raw_xbc[ts - (SSD_CONV - 1):], (1, 0, 2)))

    y_prompt = xp.reshape(bp, tp, D_MODEL)
    y_sample = jnp.transpose(xs.reshape(ts, nb, D_MODEL), (1, 0, 2))
    p_out = [jnp.stack([st[i] for st in p_states]) for i in range(7)]
    s_out = [jnp.transpose(c_t, (0, 4, 1, 2, 3)), jnp.transpose(n_t, (0, 3, 1, 2)), jnp.transpose(m_t, (0, 2, 1)),
             h_n.reshape(state_ssd.shape), jnp.stack(s_sconv),
             jnp.transpose(s_t, (0, 4, 1, 2, 3)), jnp.transpose(jnp.stack(s_ffn), (0, 2, 1, 3))]
    return (y_prompt, y_sample, *p_out, *s_out)
```

```python
import functools

import jax
import jax.numpy as jnp
from jax import lax
from jax.experimental import pallas as pl
from jax.experimental.pallas import tpu as pltpu

f32 = jnp.float32
bf16 = jnp.bfloat16

D_MODEL = 1024
DEPTH = 2
ML_HEADS = 6
HEAD_W = 64
ML_W = 384
SSD_HEADS = 6
SSD_W = 384
SSD_DSTATE = 128
SSD_CONV = 4
SSD_CONV_DIM = 896
HG_HEADS = 4
HG_W = 256
D_FF = 2816
FFN_CONV = 3
EPS = 1e-6
PROMPT_CHUNK = 128
HG_CHUNK = 64
LANES = 128
FF_BLOCK = 256
ROW_TILE = 512
SEQS_PER_STEP = 4
GATE_W = 128
GATE_ROWS = 16
W_IN_COLS = 4 * ML_W + (SSD_W + SSD_CONV_DIM) + 4 * HG_W + GATE_W
VMEM_LIMIT_BYTES = 56 * 1024 * 1024

NEG_INF = float("-inf")
LOG2_E = 1.4426950408889634


def _cparams(*sem):
    return pltpu.CompilerParams(dimension_semantics=sem, vmem_limit_bytes=VMEM_LIMIT_BYTES)


def _resident(shape):
    nd = len(shape)
    return pl.BlockSpec(shape, lambda *_: (0,) * nd, pipeline_mode=pl.Buffered(1))


def _layer_resident(shape, layer):
    nd = len(shape)
    return pl.BlockSpec((None,) + tuple(shape), lambda *_: (layer,) + (0,) * nd, pipeline_mode=pl.Buffered(1))


def _dot(a, b):
    return jnp.dot(a, b, preferred_element_type=f32)


def _dot_nt(a, b):
    return lax.dot_general(a, b, (((1,), (1,)), ((), ())), preferred_element_type=f32)


def _split_bf16(x, parts):
    out = []
    r = x
    for _ in range(parts):
        h = r.astype(bf16)
        out.append(h)
        r = r - h.astype(f32)
    return out


def _cumsum_rows(x, tri_b):
    acc = None
    for part in _split_bf16(x, 3):
        d = _dot(tri_b, part)
        acc = d if acc is None else acc + d
    return acc


def _group_mean(x, bd_b):
    return _dot(x.astype(bf16), bd_b) * (1.0 / HEAD_W)


def _iota2(shape, axis):
    return lax.broadcasted_iota(jnp.int32, shape, axis)


def _ones_where(mask):
    return jnp.where(mask, 1.0, 0.0).astype(bf16)


def _same_head(a, b):
    return jnp.right_shift(a, 6) == jnp.right_shift(b, 6)


def _silu(x):
    return x * jax.nn.sigmoid(x)


ML_COLS = 4 * ML_W
SSD_COLS = SSD_W + SSD_CONV_DIM
HG_COLS = 4 * HG_W


def _inproj_body(x_ref, nw_ref, w_ref, ml_ref, ssd_ref, hg_ref, g_ref):
    x = x_ref[...]
    h = (x * lax.rsqrt(jnp.mean(x * x, axis=-1, keepdims=True) + EPS) * nw_ref[...]).astype(bf16)
    o0, o1, o2 = ML_COLS, ML_COLS + SSD_COLS, ML_COLS + SSD_COLS + HG_COLS
    ml_ref[...] = _dot(h, w_ref[:, 0:o0])
    ssd_ref[...] = _dot(h, w_ref[:, o0:o1])
    hg_ref[...] = _dot(h, w_ref[:, o1:o2])
    g_ref[...] = _dot(h, w_ref[:, o2:W_IN_COLS])


def _inproj_conv_body(x_ref, nw_ref, w_ref, cw_ref, cb_ref, ml_ref, ssd_ref, hg_ref, g_ref, tail_ref, xbuf_s, *,
                      tiles_per_seq):
    i = pl.program_id(0)
    tm = x_ref.shape[0]

    @pl.when(lax.rem(i, tiles_per_seq) == 0)
    def _():
        xbuf_s[0:HALO, :] = jnp.zeros((HALO, SSD_CONV_DIM), f32)

    x = x_ref[...]
    h = (x * lax.rsqrt(jnp.mean(x * x, axis=-1, keepdims=True) + EPS) * nw_ref[...]).astype(bf16)
    o0, o1, o2 = ML_COLS, ML_COLS + SSD_COLS, ML_COLS + SSD_COLS + HG_COLS
    zx = _dot(h, w_ref[:, o0:o1])
    ssd_ref[:, 0:SSD_W] = zx[:, 0:SSD_W]
    xbuf_s[HALO:HALO + tm, :] = zx[:, SSD_W:SSD_COLS]
    conv = cb_ref[...]
    for j in range(SSD_CONV):
        o = HALO - (SSD_CONV - 1) + j
        conv = conv + cw_ref[j:j + 1, :] * xbuf_s[o:o + tm, :]
    ssd_ref[:, SSD_W:SSD_COLS] = _silu(conv)
    tail = xbuf_s[tm:tm + HALO, :]
    tail_ref[0] = tail
    xbuf_s[0:HALO, :] = tail
    ml_ref[...] = _dot(h, w_ref[:, 0:o0])
    hg_ref[...] = _dot(h, w_ref[:, o1:o2])
    g_ref[...] = _dot(h, w_ref[:, o2:W_IN_COLS])


def _inproj(x, norm_w, w_cat, layer, conv=None):
    n = x.shape[0]
    tm = ROW_TILE
    row = lambda w: pl.BlockSpec((tm, w), lambda i: (i, 0))
    in_specs = [row(D_MODEL), _layer_resident((1, D_MODEL), layer), _layer_resident((D_MODEL, W_IN_COLS), layer)]
    out_specs = [row(ML_COLS), row(SSD_COLS), row(HG_COLS), row(GATE_W)]
    out_shape = [jax.ShapeDtypeStruct((n, w), f32) for w in (ML_COLS, SSD_COLS, HG_COLS, GATE_W)]
    if conv is None:
        return pl.pallas_call(
            _inproj_body, grid=(n // tm,), in_specs=in_specs, out_specs=out_specs, out_shape=out_shape,
            compiler_params=_cparams("arbitrary"), name="inproj",
        )(x, norm_w, w_cat)
    conv_w, conv_b, rows_per_seq = conv
    tps = rows_per_seq // tm
    assert rows_per_seq % tm == 0
    return pl.pallas_call(
        functools.partial(_inproj_conv_body, tiles_per_seq=tps),
        grid=(n // tm,),
        in_specs=in_specs + [_layer_resident((SSD_CONV, SSD_CONV_DIM), layer), _layer_resident((1, SSD_CONV_DIM), layer)],
        out_specs=out_specs + [pl.BlockSpec((1, HALO, SSD_CONV_DIM), lambda i: (i // tps, 0, 0))],
        out_shape=out_shape + [jax.ShapeDtypeStruct((n // rows_per_seq, HALO, SSD_CONV_DIM), f32)],
        scratch_shapes=[pltpu.VMEM((HALO + tm, SSD_CONV_DIM), f32)],
        compiler_params=_cparams("arbitrary"),
        name="inproj_conv",
    )(x, norm_w, w_cat, conv_w, conv_b)


W_IN_GATE_IF = ML_COLS
W_IN_SSD = W_IN_GATE_IF + 2 * ML_HEADS
W_IN_DT = W_IN_SSD + SSD_COLS
W_IN_HG = W_IN_DT + SSD_HEADS
W_IN_DIM = W_IN_HG + HG_COLS
W_PREP_ROWS = 256


def _prep_w_in_body(w_ref, out_ref):
    lane = _iota2((W_PREP_ROWS, LANES), 1)
    tile = lambda k: w_ref[:, k * LANES:(k + 1) * LANES]
    out_ref[:, 0:ML_COLS] = w_ref[:, 0:ML_COLS].astype(bf16)

    def shifted(dst0, src0, ntiles):
        t0, sh = src0 // LANES, src0 % LANES
        rolled = [pltpu.roll(tile(t0 + k), LANES - sh, axis=1) for k in range(ntiles + 1)]
        for k in range(ntiles):
            out_ref[:, dst0 + k * LANES:dst0 + (k + 1) * LANES] = jnp.where(
                lane < LANES - sh, rolled[k], rolled[k + 1]).astype(bf16)

    shifted(ML_COLS, W_IN_SSD, SSD_COLS // LANES)
    shifted(ML_COLS + SSD_COLS, W_IN_HG, HG_COLS // LANES)
    assert W_IN_GATE_IF % LANES == 0 and W_IN_DT % LANES == DT_LANE
    gates = jnp.where(lane < DT_LANE, tile(W_IN_GATE_IF // LANES),
                      jnp.where(lane < DT_LANE + SSD_HEADS, tile(W_IN_DT // LANES), 0.0))
    out_ref[:, ML_COLS + SSD_COLS + HG_COLS:W_IN_COLS] = gates.astype(bf16)


def _prep_w_in(w_in):
    depth, d, n = w_in.shape
    assert n == W_IN_DIM and d % W_PREP_ROWS == 0
    return pl.pallas_call(
        _prep_w_in_body,
        grid=(depth, d // W_PREP_ROWS),
        in_specs=[pl.BlockSpec((None, W_PREP_ROWS, W_IN_COLS), lambda l, i: (l, i, 0))],
        out_specs=pl.BlockSpec((None, W_PREP_ROWS, W_IN_COLS), lambda l, i: (l, i, 0)),
        out_shape=jax.ShapeDtypeStruct((depth, d, W_IN_COLS), bf16),
        compiler_params=_cparams("arbitrary", "arbitrary"),
        name="prep_w_in",
    )(w_in)


def _one_seq(ref, bi):
    return ref.at[pl.ds(bi, 1)]


def _interleave(chunks):
    live = list(chunks)
    while live:
        nxt = []
        for g in live:
            try:
                next(g)
                nxt.append(g)
            except StopIteration:
                pass
        live = nxt


def _mlstm_prompt_chunk(ml_ref, g_ref, gb_ref, nw_ref, out_ref, ct_s, nr_s, m_s):
    L = PROMPT_CHUNK
    row = _iota2((L, L), 0)
    col = _iota2((L, L), 1)
    causal = row <= col
    tri_b = _ones_where(causal)
    top = row < HEAD_W
    lo_half = col < HEAD_W
    bd_mask = _same_head(row, col)
    sub8 = _iota2((8, L), 0)
    lo8 = _iota2((8, L), 1) < HEAD_W
    n_mask = jnp.where(sub8 == 0, jnp.where(lo8, 1.0, 0.0), jnp.where(sub8 == 1, jnp.where(lo8, 0.0, 1.0), 0.0))

    g_t = (g_ref[0] + gb_ref[...]).T
    b_t = None
    parts = _split_bf16(jax.nn.log_sigmoid(g_t[0:GATE_ROWS, :]), 3)
    yield
    for part in parts:
        d = _dot(part, tri_b)
        b_t = d if b_t is None else b_t + d
        yield
    z_rows = b_t[ML_HEADS:ML_HEADS + 8, :] - g_t[0:8, :]
    z_nat = jnp.concatenate([z_rows, jnp.zeros((L - 8, L), f32)], axis=0).T
    m_old = m_s[...]
    m_next = m_old
    yield

    for p in range(ML_HEADS // 2):
        sl = slice(p * LANES, (p + 1) * LANES)
        q2 = ml_ref[0, :, sl]
        k2 = ml_ref[0, :, ML_W + p * LANES:ML_W + (p + 1) * LANES] * (HEAD_W ** -0.5)
        v2 = ml_ref[0, :, 2 * ML_W + p * LANES:2 * ML_W + (p + 1) * LANES]
        o2 = ml_ref[0, :, 3 * ML_W + p * LANES:3 * ML_W + (p + 1) * LANES]
        q_t = q2.T.astype(bf16)
        v_t = v2.T
        kb = k2.astype(bf16)
        k_sel = (jnp.where(lo_half, k2, 0.0).astype(bf16), jnp.where(lo_half, 0.0, k2).astype(bf16))
        sw, den, a_int, e_m, a_st, w_s = [], [], [], [], [], []
        yield
        for j in range(2):
            h = 2 * p + j
            b_row = b_t[ML_HEADS + h:ML_HEADS + h + 1, :]
            i_row = g_t[h:h + 1, :]
            m_prev = m_old[h:h + 1, :]
            dm = jnp.where(causal, b_row - z_nat[:, h:h + 1], NEG_INF)
            inter = b_row + m_prev
            raw = _dot(k_sel[j], q_t)
            yield
            m_t = jnp.maximum(inter, jnp.max(dm, axis=0, keepdims=True))
            yield
            s_w = raw * jnp.exp(dm - m_t)
            den.append(jnp.sum(s_w, axis=0, keepdims=True))
            sw.append(s_w.astype(bf16))
            yield
            a_int.append(jnp.exp(inter - m_t))
            e_m.append(jnp.exp(-m_t))
            m_new = jnp.broadcast_to(m_t[:, L - 1:L], (1, L))
            b_last = jnp.broadcast_to(b_row[:, L - 1:L], (1, L))
            a_st.append(jnp.exp(b_last + m_prev - m_new))
            w_s.append(jnp.exp(b_last - b_row + i_row - m_new))
            m_next = jnp.where(sub8 == h, m_new, m_next)
            yield
        ct = ct_s[p]
        nr = nr_s[p]
        qn = _dot(nr.astype(bf16), q_t)
        num0 = _dot(jnp.where(top, v_t, 0.0).astype(bf16), sw[0])
        yield
        num1 = _dot(jnp.where(top, 0.0, v_t).astype(bf16), sw[1])
        qc = _dot(ct.astype(bf16), q_t)
        yield
        num = num0 + num1 + jnp.where(top, a_int[0], a_int[1]) * qc
        den0 = den[0] + a_int[0] * qn[0:1, :]
        den1 = den[1] + a_int[1] * qn[1:2, :]
        hh = num / jnp.where(top, jnp.maximum(jnp.abs(den0), e_m[0]), jnp.maximum(jnp.abs(den1), e_m[1]))
        yield
        sq = hh * hh
        r0 = lax.rsqrt(jnp.sum(sq[0:HEAD_W, :], axis=0, keepdims=True) * (1.0 / HEAD_W) + EPS)
        r1 = lax.rsqrt(jnp.sum(sq[HEAD_W:LANES, :], axis=0, keepdims=True) * (1.0 / HEAD_W) + EPS)
        y = (hh * jnp.where(top, r0, r1)).T * nw_ref[:, sl] * jax.nn.sigmoid(o2)
        out_ref[0, :, sl] = y.astype(bf16)
        yield
        vw = (v_t * jnp.where(top, w_s[0], w_s[1])).astype(bf16)
        ct_s[p] = jnp.where(top, a_st[0], a_st[1]) * ct + jnp.where(bd_mask, _dot(vw, kb), 0.0)
        w8 = jnp.where(sub8 == 0, w_s[0], jnp.where(sub8 == 1, w_s[1], 0.0)).astype(bf16)
        nr_s[p] = jnp.where(sub8 == 0, a_st[0], a_st[1]) * nr + n_mask * _dot(w8, kb)
        yield
    m_s[...] = m_next


DT_LANE = 2 * ML_HEADS
SSD_GATE_ROW0 = DT_LANE // 8 * 8
SSD_HEAD_ROW = DT_LANE - SSD_GATE_ROW0
HALO = 8


def _ssd_prompt_chunk(ssd_ref, g_ref, gb_ref, al_ref, d_ref, nw_ref, out_ref, h_s):
    L = PROMPT_CHUNK
    xc = ssd_ref[0, :, SSD_W:SSD_COLS]
    nb = SSD_W
    b_b = [xc[:, nb + gi * LANES:nb + (gi + 1) * LANES].astype(bf16) for gi in range(2)]
    c_b = [xc[:, nb + (2 + gi) * LANES:nb + (3 + gi) * LANES].astype(bf16) for gi in range(2)]

    row = _iota2((L, L), 0)
    col = _iota2((L, L), 1)
    tril = col <= row
    lo_half = col < HEAD_W
    triu_b = _ones_where(row <= col)
    bd_b = _ones_where(_same_head(row, col))
    rowc = _iota2((L, 1), 0)

    g_t = (g_ref[0] + gb_ref[...]).T
    dt_t = jax.nn.softplus(g_t[SSD_GATE_ROW0:SSD_GATE_ROW0 + GATE_ROWS, :])
    yield
    cum_t = None
    for part in _split_bf16(dt_t * (-jnp.exp(al_ref[...])), 3):
        d = _dot(part, triu_b)
        cum_t = d if cum_t is None else cum_t + d
    yield
    nat = jnp.concatenate([cum_t, dt_t, cum_t * LOG2_E, jnp.zeros((L - 3 * GATE_ROWS, L), f32)], axis=0).T
    cb_g = [_dot_nt(c_b[gi], b_b[gi]) for gi in range(2)]
    yield

    for p in range(SSD_HEADS // 2):
        sl = slice(p * LANES, (p + 1) * LANES)
        x2 = xc[:, sl]
        z2 = ssd_ref[0, :, sl]
        hp = h_s[p]
        hb = hp.astype(bf16)
        grp = [(2 * p + j) // (SSD_HEADS // 2) for j in range(2)]
        ys = None
        ecum, w_s, a_last, ch = [], [], [], []
        for j in range(2):
            r = SSD_HEAD_ROW + 2 * p + j
            cum_col = nat[:, r:r + 1]
            cum_last = cum_col[L - 1:L, :]
            key_row = cum_t[r:r + 1, :] * LOG2_E - jnp.log2(dt_t[r:r + 1, :])
            lm_dt = jnp.exp2(jnp.where(tril, nat[:, 2 * GATE_ROWS + r:2 * GATE_ROWS + r + 1] - key_row, NEG_INF))
            m = (cb_g[grp[j]] * lm_dt).astype(bf16)
            yield
            xm = jnp.where(lo_half, x2, 0.0) if j == 0 else jnp.where(lo_half, 0.0, x2)
            d = _dot(m, xm.astype(bf16))
            ys = d if ys is None else ys + d
            ecum.append(jnp.exp(cum_col))
            w_s.append(jnp.exp(cum_last - cum_col) * nat[:, GATE_ROWS + r:GATE_ROWS + r + 1])
            a_last.append(jnp.exp(cum_last))
            if j == 0 or grp[1] != grp[0]:
                ch.append(_dot_nt(c_b[grp[j]], hb))
            else:
                ch.append(ch[0])
            yield
        ys = ys + jnp.where(lo_half, ecum[0] * ch[0], ecum[1] * ch[1])
        xw_t = (x2 * jnp.where(lo_half, w_s[0], w_s[1])).T.astype(bf16)
        yield
        if grp[0] == grp[1]:
            upd = _dot(xw_t, b_b[grp[0]])
        else:
            upd = jnp.where(row < HEAD_W, _dot(xw_t, b_b[grp[0]]), _dot(xw_t, b_b[grp[1]]))
        h_s[p] = jnp.where(rowc < HEAD_W, a_last[0], a_last[1]) * hp + upd
        yy = (ys + d_ref[:, sl] * x2) * _silu(z2)
        yield
        ms = _group_mean(yy * yy, bd_b)
        yield
        out_ref[0, :, sl] = (yy * lax.rsqrt(ms + EPS) * nw_ref[:, sl]).astype(bf16)


def _hg_lower_bound(lbp, layer):
    mx = jnp.max(lbp, axis=0, keepdims=True)
    e = jnp.exp(lbp - mx)
    den = jnp.sum(e, axis=0, keepdims=True)
    lb = jnp.zeros_like(den)
    for j in range(1, layer + 1):
        lb = lb + e[j:j + 1, :] / den
    return lb


def _hgrn_prompt_chunk(hg_ref, lbp_ref, nw_ref, out_ref, sbd_s, p_s, r_s, o_s, g_s, k_s, v_s, q_s, kt_s, a_s, *, layer):
    L = HG_CHUNK
    nblk = L // 8
    lb = _hg_lower_bound(lbp_ref[...], layer)
    ff = hg_ref[0, :, HG_W:2 * HG_W]
    sig = jax.nn.sigmoid(ff)
    log2_k = jnp.log2((1.0 - lb) * jax.nn.sigmoid(-ff))
    v_s[...] = hg_ref[0, :, 2 * HG_W:3 * HG_W]
    q_s[...] = hg_ref[0, :, 0:HG_W] * (HEAD_W ** -0.5)
    row = _iota2((L, L), 0)
    col = _iota2((L, L), 1)
    tri_b = _ones_where(col <= row)
    yield
    g2 = _cumsum_rows(jnp.log(lb + (1.0 - lb) * sig), tri_b) * LOG2_E
    g_s[...] = g2
    k_s[...] = g2 - log2_k
    yield

    r2 = _iota2((2 * LANES, 2 * LANES), 0)
    c2 = _iota2((2 * LANES, 2 * LANES), 1)
    bd4_b = _ones_where(_same_head(r2, c2))
    r1 = _iota2((LANES, LANES), 0)
    c1 = _iota2((LANES, LANES), 1)
    bd_mask = _same_head(r1, c1)
    bd_b = _ones_where(bd_mask)

    a_s[0:8, :] = jnp.zeros((8, HG_W), f32)
    for i in range(1, nblk):
        blk = slice(8 * i, 8 * i + 8)
        n = 8 * i
        r_i = g_s[n - 1:n, :]
        kt = jnp.exp2(r_i - k_s[0:n, :])
        first_head = jnp.bitwise_and(_iota2((n, HG_W), 1), LANES - 1) < HEAD_W
        kt_s[i - 1, 0:n, :] = jnp.where(first_head, kt, 0.0)
        kt_s[i - 1, n:L, :] = jnp.zeros((L - n, HG_W), f32)
        kt_s[i - 1, L:L + n, :] = jnp.where(first_head, 0.0, kt)
        kt_s[i - 1, L + n:2 * L, :] = jnp.zeros((L - n, HG_W), f32)
        qt = q_s[blk, :] * jnp.exp2(g_s[blk, :] - r_i)
        qt = jnp.concatenate([qt, jnp.zeros_like(qt)], axis=0).astype(bf16)
        for p in range(HG_HEADS // 2):
            sl = slice(p * LANES, (p + 1) * LANES)
            a_s[blk, sl] = _dot_nt(qt[:, sl], kt_s[i - 1, :, sl].astype(bf16))[0:8, :]
        yield

    sub = _iota2((8, HG_W), 0)
    for j in range(nblk):
        blk = slice(8 * j, 8 * j + 8)
        g_blk, gk_blk, q_blk = g_s[blk, :], k_s[blk, :], q_s[blk, :]
        for r in range(8):
            s = 8 * j + r
            p_s[8 * s:8 * s + 8, :] = q_blk * jnp.exp2(jnp.where(sub >= r, g_blk - gk_blk[r:r + 1, :], NEG_INF))
        yield
    r_s[...] = _dot(p_s[...].astype(bf16), bd4_b)
    yield

    gg = g_s[...]
    vv = v_s[...]
    qd = q_s[...] * jnp.exp2(gg)
    lo_half = _iota2((L, LANES), 1) < HEAD_W
    for p in range(HG_HEADS // 2):
        sl = slice(p * LANES, (p + 1) * LANES)
        v_cat = jnp.concatenate([jnp.where(lo_half, vv[:, sl], 0.0), jnp.where(lo_half, 0.0, vv[:, sl])], axis=0)
        o_s[:, sl] = (_dot(qd[:, sl].astype(bf16), sbd_s[p].astype(bf16))
                      + _dot(a_s[:, sl].astype(bf16), v_cat.astype(bf16)))
        yield
    for j in range(nblk):
        blk = slice(8 * j, 8 * j + 8)
        acc = o_s[blk, :]
        v_blk = v_s[blk, :]
        for r in range(8):
            s = 8 * j + r
            acc = acc + r_s[8 * s:8 * s + 8, :] * v_blk[r:r + 1, :]
        o_s[blk, :] = acc
        yield

    g_last = gg[L - 1:L, :]
    kd = jnp.exp2(g_last - k_s[...])
    zpad = jnp.zeros((LANES - L, LANES), f32)
    og = o_s[...]
    gate = hg_ref[0, :, 3 * HG_W:4 * HG_W]
    for p in range(HG_HEADS // 2):
        sl = slice(p * LANES, (p + 1) * LANES)
        kd_t = jnp.concatenate([kd[:, sl], zpad], axis=0).T.astype(bf16)
        v_pad = jnp.concatenate([vv[:, sl], zpad], axis=0).astype(bf16)
        g_t = jnp.concatenate([gg[:, sl], zpad], axis=0).T
        dec = jnp.exp2(g_t[:, L - 1:L])
        sbd_s[p] = dec * sbd_s[p] + jnp.where(bd_mask, _dot(kd_t, v_pad), 0.0)
        yield
        oo = og[:, sl]
        ms = _group_mean(oo * oo, bd_b)
        out_ref[0, :, sl] = (oo * lax.rsqrt(ms + EPS) * nw_ref[:, sl] * _silu(gate[:, sl])).astype(bf16)


def _mixers_prompt_body(ml_ref, ssd_ref, hg_ref, g_ref, gb_ref, al_ref, d_ref, mlnw_ref, ssnw_ref,
                        lbp_ref, hgnw_ref,
                        hm_ref, c1_ref, n1_ref, m1_ref, hs_ref, h1_ref, ho_ref, s1_ref,
                        ct_s, nr_s, m_s, h_s, sbd_s, p_s, r_s, o_s, gg_s, k_s, v_s, q_s, kt_s, a_s, *, layer):
    c = pl.program_id(1)
    seqs = range(ml_ref.shape[0])

    @pl.when(c == 0)
    def _():
        ct_s[...] = jnp.zeros(ct_s.shape, f32)
        nr_s[...] = jnp.zeros(nr_s.shape, f32)
        m_s[...] = jnp.zeros(m_s.shape, f32)
        h_s[...] = jnp.zeros(h_s.shape, f32)
        sbd_s[...] = jnp.zeros(sbd_s.shape, f32)

    def hgrn_chunks(bi):
        for half in range(PROMPT_CHUNK // HG_CHUNK):
            rows = pl.ds(half * HG_CHUNK, HG_CHUNK)
            yield from _hgrn_prompt_chunk(hg_ref.at[pl.ds(bi, 1), rows], lbp_ref, hgnw_ref, ho_ref.at[pl.ds(bi, 1), rows],
                                          sbd_s.at[bi], p_s.at[bi], r_s.at[bi], o_s.at[bi], gg_s.at[bi], k_s.at[bi],
                                          v_s.at[bi], q_s.at[bi], kt_s.at[bi], a_s.at[bi], layer=layer)

    chunks = []
    for bi in seqs:
        chunks.append(_ssd_prompt_chunk(_one_seq(ssd_ref, bi), _one_seq(g_ref, bi), gb_ref, al_ref, d_ref,
                                        ssnw_ref, _one_seq(hs_ref, bi), h_s.at[bi]))
        chunks.append(hgrn_chunks(bi))
        chunks.append(_mlstm_prompt_chunk(_one_seq(ml_ref, bi), _one_seq(g_ref, bi), gb_ref, mlnw_ref,
                                          _one_seq(hm_ref, bi), ct_s.at[bi], nr_s.at[bi], m_s.at[bi]))
    _interleave(chunks)

    @pl.when(c == pl.num_programs(1) - 1)
    def _():
        h1_ref[...] = h_s[...]
        for bi in seqs:
            for p in range(ML_HEADS // 2):
                c_pair = ct_s[bi, p].T
                c1_ref[bi, 2 * p] = c_pair[0:HEAD_W, 0:HEAD_W]
                c1_ref[bi, 2 * p + 1] = c_pair[HEAD_W:LANES, HEAD_W:LANES]
                n1_ref[bi, 2 * p:2 * p + 1, :] = nr_s[bi, p, 0:1, 0:HEAD_W]
                n1_ref[bi, 2 * p + 1:2 * p + 2, :] = nr_s[bi, p, 1:2, HEAD_W:LANES]
            m1_ref[bi] = m_s[bi]
            for p in range(HG_HEADS // 2):
                sbd = sbd_s[bi, p]
                s1_ref[bi, 2 * p] = sbd[0:HEAD_W, 0:HEAD_W]
                s1_ref[bi, 2 * p + 1] = sbd[HEAD_W:LANES, HEAD_W:LANES]


def _mixers_prompt(ml, ssd, hg, gates, gate_bias, alog_col, d_row, ml_nw, ssd_nw, lbp, hg_nw, layer):
    b, t, _ = ml.shape
    L = PROMPT_CHUNK
    ns = SEQS_PER_STEP
    assert b % ns == 0 and t % L == 0
    ml_pairs, ssd_pairs, hg_pairs = ML_HEADS // 2, SSD_HEADS // 2, HG_HEADS // 2
    tok = lambda w: pl.BlockSpec((ns, L, w), lambda i, c: (i, c, 0))
    per_seq = lambda *shape: pl.BlockSpec((ns,) + shape, lambda i, c: (i,) + (0,) * len(shape))
    hg_rows = pltpu.VMEM((ns, HG_CHUNK, HG_W), f32)
    hg_pairs_buf = pltpu.VMEM((ns, 8 * HG_CHUNK, HG_W), f32)
    return pl.pallas_call(
        functools.partial(_mixers_prompt_body, layer=layer),
        grid=(b // ns, t // L),
        in_specs=[tok(ML_COLS), tok(SSD_COLS), tok(HG_COLS), tok(GATE_W),
                  _layer_resident((1, GATE_W), layer), _layer_resident((GATE_ROWS, 1), layer),
                  _layer_resident((1, SSD_W), layer), _layer_resident((1, ML_W), layer),
                  _layer_resident((1, SSD_W), layer), _resident((DEPTH, HG_W)), _layer_resident((1, HG_W), layer)],
        out_specs=[tok(ML_W), per_seq(ML_HEADS, HEAD_W, HEAD_W), per_seq(ML_HEADS, HEAD_W), per_seq(8, LANES),
                   tok(SSD_W), per_seq(ssd_pairs, LANES, SSD_DSTATE),
                   tok(HG_W), per_seq(HG_HEADS, HEAD_W, HEAD_W)],
        out_shape=[jax.ShapeDtypeStruct((b, t, ML_W), bf16),
                   jax.ShapeDtypeStruct((b, ML_HEADS, HEAD_W, HEAD_W), f32),
                   jax.ShapeDtypeStruct((b, ML_HEADS, HEAD_W), f32),
                   jax.ShapeDtypeStruct((b, 8, LANES), f32),
                   jax.ShapeDtypeStruct((b, t, SSD_W), bf16),
                   jax.ShapeDtypeStruct((b, ssd_pairs, LANES, SSD_DSTATE), f32),
                   jax.ShapeDtypeStruct((b, t, HG_W), bf16),
                   jax.ShapeDtypeStruct((b, HG_HEADS, HEAD_W, HEAD_W), f32)],
        scratch_shapes=[pltpu.VMEM((ns, ml_pairs, LANES, LANES), f32), pltpu.VMEM((ns, ml_pairs, 8, LANES), f32),
                        pltpu.VMEM((ns, 8, LANES), f32),
                        pltpu.VMEM((ns, ssd_pairs, LANES, SSD_DSTATE), f32),
                        pltpu.VMEM((ns, hg_pairs, LANES, LANES), f32), hg_pairs_buf, hg_pairs_buf,
                        hg_rows, hg_rows, hg_rows, hg_rows, hg_rows,
                        pltpu.VMEM((ns, HG_CHUNK // 8 - 1, 2 * HG_CHUNK, HG_W), f32), hg_rows],
        compiler_params=_cparams("arbitrary", "arbitrary"),
        name="mixers_prompt",
    )(ml, ssd, hg, gates, gate_bias, alog_col, d_row, ml_nw, ssd_nw, lbp, hg_nw)


NB = 128
TS = 8


def _sample_call(body, name, grid, in_specs, operands, out_specs, out_shapes, scratch, n_state):
    first = len(operands) - n_state
    return pl.pallas_call(
        body, grid=grid, in_specs=in_specs, out_specs=out_specs, out_shape=out_shapes,
        scratch_shapes=scratch, input_output_aliases={first + i: 1 + i for i in range(n_state)},
        compiler_params=_cparams(*(["arbitrary"] * len(grid))), name=name,
    )(*operands)


def _mlstm_sample_body(q_ref, k_ref, v_ref, o_ref, g_ref, gb_ref, nw_ref, c0_ref, n0_ref, m0_ref,
                       out_ref, c1_ref, n1_ref, m1_ref, qt_s, kt_s, vt_s, gt_s, ht_s):
    p = pl.program_id(0)
    n_cur = [n0_ref[hl] for hl in range(2)]
    m_cur = [m0_ref[pl.ds(2 * p + hl, 1), :] for hl in range(2)]

    for t in range(TS):
        c_src = c0_ref if t == 0 else c1_ref
        rows = slice(t * NB, (t + 1) * NB)
        qt_s[...] = q_ref[rows, :].T
        kt_s[...] = k_ref[rows, :].T * (HEAD_W ** -0.5)
        vt_s[...] = v_ref[rows, :].T
        gt_s[...] = (g_ref[rows, :] + gb_ref[...]).T
        for hl in range(2):
            h = 2 * p + hl
            hs = slice(hl * HEAD_W, (hl + 1) * HEAD_W)
            i_t = gt_s[pl.ds(h, 1), :]
            lf = jax.nn.log_sigmoid(gt_s[pl.ds(ML_HEADS + h, 1), :])
            m_new = jnp.maximum(lf + m_cur[hl], i_t)
            a = jnp.exp(lf + m_cur[hl] - m_new)
            w = jnp.exp(i_t - m_new)
            n_new = a * n_cur[hl] + w * kt_s[hs, :]
            v_h = vt_s[hs, :]

            def body(d, num, hl=hl, a=a, w=w, v_h=v_h, c_src=c_src):
                kd = kt_s[pl.ds(hl * HEAD_W + d, 1), :] * w
                qd = qt_s[pl.ds(hl * HEAD_W + d, 1), :]
                cd = a * c_src[hl, d] + kd * v_h
                c1_ref[hl, d] = cd
                return num + qd * cd

            num = lax.fori_loop(0, HEAD_W, body, jnp.zeros((HEAD_W, NB), f32), unroll=4)
            den = jnp.sum(qt_s[hs, :] * n_new, axis=0, keepdims=True)
            hh = num / jnp.maximum(jnp.abs(den), jnp.exp(-m_new))
            ms = jnp.mean(hh * hh, axis=0, keepdims=True)
            ht_s[hs, :] = hh * lax.rsqrt(ms + EPS)
            n_cur[hl] = n_new
            m_cur[hl] = m_new
        out_ref[rows, :] = (ht_s[...].T * nw_ref[...] * jax.nn.sigmoid(o_ref[rows, :])).astype(bf16)

    for hl in range(2):
        n1_ref[hl] = n_cur[hl]
        m1_ref[pl.ds(2 * p + hl, 1), :] = m_cur[hl]


def _mlstm_sample(ml, gates, gate_bias, norm_w, c0t, n0t, m0t, layer):
    n = ml.shape[0]
    npair = ML_HEADS // 2
    blk = lambda off: pl.BlockSpec((n, LANES), lambda p, off=off: (0, off + p))
    c_spec = pl.BlockSpec((None, 2, HEAD_W, HEAD_W, NB), lambda p: (layer, p, 0, 0, 0))
    n_spec = pl.BlockSpec((None, 2, HEAD_W, NB), lambda p: (layer, p, 0, 0))
    m_spec = pl.BlockSpec((None, ML_HEADS, NB), lambda p: (layer, 0, 0))
    return _sample_call(
        _mlstm_sample_body, "mlstm_sample", (npair,),
        [blk(0), blk(npair), blk(2 * npair), blk(3 * npair), _resident((n, GATE_W)),
         _layer_resident((1, GATE_W), layer), pl.BlockSpec((None, 1, LANES), lambda p: (layer, 0, p)),
         c_spec, n_spec, m_spec],
        [ml, ml, ml, ml, gates, gate_bias, norm_w, c0t, n0t, m0t],
        [pl.BlockSpec((n, LANES), lambda p: (0, p)), c_spec, n_spec, m_spec],
        [jax.ShapeDtypeStruct((n, ML_W), bf16), jax.ShapeDtypeStruct(c0t.shape, f32),
         jax.ShapeDtypeStruct(n0t.shape, f32), jax.ShapeDtypeStruct(m0t.shape, f32)],
        [pltpu.VMEM((LANES, NB), f32)] * 5, 3)


def _ssd_conv_sample_body(ssd_ref, halo_ref, cw_ref, cb_ref, out_ref, xp_s):
    nh = (SSD_CONV - 1) * NB
    n = TS * NB
    xp_s[0:nh, :] = halo_ref[...]
    xp_s[nh:nh + n, :] = ssd_ref[:, SSD_W:SSD_COLS]
    conv = cb_ref[...]
    for j in range(SSD_CONV):
        conv = conv + cw_ref[j:j + 1, :] * xp_s[j * NB:j * NB + n, :]
    out_ref[...] = _silu(conv)


def _ssd_conv_sample(ssd, halo, conv_w, conv_b, layer):
    n = ssd.shape[0]
    nh = (SSD_CONV - 1) * NB
    return pl.pallas_call(
        _ssd_conv_sample_body,
        grid=(1,),
        in_specs=[_resident((n, SSD_COLS)), _layer_resident((nh, SSD_CONV_DIM), layer),
                  _layer_resident((SSD_CONV, SSD_CONV_DIM), layer), _layer_resident((1, SSD_CONV_DIM), layer)],
        out_specs=pl.BlockSpec((n, SSD_CONV_DIM), lambda i: (0, 0)),
        out_shape=jax.ShapeDtypeStruct((n, SSD_CONV_DIM), f32),
        scratch_shapes=[pltpu.VMEM((nh + n, SSD_CONV_DIM), f32)],
        compiler_params=_cparams("arbitrary"),
        name="ssd_conv_sample",
    )(ssd, halo, conv_w, conv_b)


def _ssd_sample_body(z_ref, x_ref, b0_ref, b1_ref, c0_ref, c1_ref, g_ref, gb_ref, hp_ref, nw_ref, h0_ref,
                     *rest, layer, last):
    prev_refs = rest[:len(rest) - 8]
    out_ref, h1_ref, xt_s, bt_s, ct_s, gt_s, dd_s, yt_s = rest[len(rest) - 8:]
    p = pl.program_id(0)
    head = pl.program_id(1)
    b_refs = (b0_ref, b1_ref)
    c_refs = (c0_ref, c1_ref)
    h1 = h1_ref.at[layer] if last else h1_ref
    for l, prev in enumerate(prev_refs):
        h1_ref[l] = prev[...]

    @pl.when(head == 0)
    def _():
        for t in range(TS):
            rows = slice(t * NB, (t + 1) * NB)
            xt_s[t] = x_ref[rows, :].T
            gt_s[...] = (g_ref[rows, :] + gb_ref[...]).T
            for hl in range(2):
                bt_s[t, hl] = b_refs[hl][rows, :].T
                ct_s[t, hl] = c_refs[hl][rows, :].T
                dt = jax.nn.softplus(gt_s[pl.ds(DT_LANE + 2 * p + hl, 1), :])
                dd_s[t, 2 * hl:2 * hl + 1, :] = dt
                dd_s[t, 2 * hl + 1:2 * hl + 2, :] = jnp.exp(dt * (-jnp.exp(hp_ref[0, hl:hl + 1, :])))

    for hl in range(2):
        @pl.when(head == hl)
        def _(hl=hl):
            def body(pp, carry):
                row = pl.ds(hl * HEAD_W + pp, 1)
                h = h0_ref[pp]
                for t in range(TS):
                    xrow = xt_s[t, row, :] * dd_s[t, 2 * hl:2 * hl + 1, :]
                    h = dd_s[t, 2 * hl + 1:2 * hl + 2, :] * h + xrow * bt_s[t, hl]
                    yt_s[t, row, :] = jnp.sum(ct_s[t, hl] * h, axis=0, keepdims=True)
                h1[:, pp, :] = h.T
                return carry

            lax.fori_loop(0, HEAD_W, body, 0, unroll=4)

    @pl.when(head == 1)
    def _():
        for t in range(TS):
            rows = slice(t * NB, (t + 1) * NB)
            zt = z_ref[rows, :].T
            for hl in range(2):
                hs = slice(hl * HEAD_W, (hl + 1) * HEAD_W)
                yy = (yt_s[t, hs, :] + hp_ref[1, hl:hl + 1, :] * xt_s[t, hs, :]) * _silu(zt[hs, :])
                ms = jnp.mean(yy * yy, axis=0, keepdims=True)
                yt_s[t, hs, :] = yy * lax.rsqrt(ms + EPS)
            out_ref[rows, :] = (yt_s[t].T * nw_ref[...]).astype(bf16)


def _ssd_sample(ssd, xc, gates, gate_bias, head_params, norm_w, h0t, h_new, layer):
    n = ssd.shape[0]
    npair = SSD_HEADS // 2
    hpg = SSD_HEADS // 2
    nxb = SSD_W // LANES
    last = layer == DEPTH - 1
    prev = list(h_new) if last else []
    assert len(prev) == (layer if last else 0)
    blk = lambda f: pl.BlockSpec((n, LANES), f)
    head_blk = (NB, HEAD_W, SSD_DSTATE)
    new_shape = (NB, SSD_HEADS * HEAD_W, SSD_DSTATE)
    if last:
        new_spec = pl.BlockSpec((DEPTH,) + head_blk, lambda p, j: (0, 0, 2 * p + j, 0))
        new_shape = (DEPTH,) + new_shape
    else:
        new_spec = pl.BlockSpec(head_blk, lambda p, j: (0, 2 * p + j, 0))
    return _sample_call(
        functools.partial(_ssd_sample_body, layer=layer, last=last), "ssd_sample", (npair, 2),
        [blk(lambda p, j: (0, p)), blk(lambda p, j: (0, p)),
         blk(lambda p, j: (0, nxb + (2 * p) // hpg)), blk(lambda p, j: (0, nxb + (2 * p + 1) // hpg)),
         blk(lambda p, j: (0, nxb + 2 + (2 * p) // hpg)), blk(lambda p, j: (0, nxb + 2 + (2 * p + 1) // hpg)),
         _resident((n, GATE_W)), _layer_resident((1, GATE_W), layer),
         pl.BlockSpec((None, 2, None, 2, LANES), lambda p, j: (layer, 0, p, 0, 0)),
         pl.BlockSpec((None, 1, LANES), lambda p, j: (layer, 0, p)),
         pl.BlockSpec((None, None, HEAD_W, SSD_DSTATE, NB), lambda p, j: (layer, 2 * p + j, 0, 0, 0))]
        + [pl.BlockSpec(head_blk, lambda p, j: (0, 2 * p + j, 0)) for _ in prev],
        [ssd, xc, xc, xc, xc, xc, gates, gate_bias, head_params, norm_w, h0t] + prev,
        [pl.BlockSpec((n, LANES), lambda p, j: (0, p)), new_spec],
        [jax.ShapeDtypeStruct((n, SSD_W), bf16), jax.ShapeDtypeStruct(new_shape, f32)],
        [pltpu.VMEM((TS, LANES, NB), f32), pltpu.VMEM((TS, 2, SSD_DSTATE, NB), f32),
         pltpu.VMEM((TS, 2, SSD_DSTATE, NB), f32), pltpu.VMEM((LANES, NB), f32), pltpu.VMEM((TS, 8, NB), f32),
         pltpu.VMEM((TS, LANES, NB), f32)], 0)


def _hgrn_sample_body(q_ref, f_ref, i_ref, gate_ref, lbp_ref, nw_ref, s0_ref, out_ref, s1_ref,
                      qt_s, kt_s, ft_s, vt_s, ot_s, *, layer):
    lb = _hg_lower_bound(lbp_ref[...], layer)

    for t in range(TS):
        s_src = s0_ref if t == 0 else s1_ref
        rows = slice(t * NB, (t + 1) * NB)
        ff = f_ref[rows, :]
        qt_s[...] = (q_ref[rows, :] * (HEAD_W ** -0.5)).T
        kt_s[...] = ((1.0 - lb) * jax.nn.sigmoid(-ff)).T
        ft_s[...] = (lb + (1.0 - lb) * jax.nn.sigmoid(ff)).T
        vt_s[...] = i_ref[rows, :].T
        for hl in range(2):
            hs = slice(hl * HEAD_W, (hl + 1) * HEAD_W)
            v_h = vt_s[hs, :]

            def body(k, o, hl=hl, v_h=v_h, s_src=s_src):
                r = pl.ds(hl * HEAD_W + k, 1)
                sk = ft_s[r, :] * s_src[hl, k] + kt_s[r, :] * v_h
                s1_ref[hl, k] = sk
                return o + qt_s[r, :] * sk

            o = lax.fori_loop(0, HEAD_W, body, jnp.zeros((HEAD_W, NB), f32), unroll=4)
            ms = jnp.mean(o * o, axis=0, keepdims=True)
            ot_s[hs, :] = o * lax.rsqrt(ms + EPS)
        out_ref[rows, :] = (ot_s[...].T * nw_ref[...] * _silu(gate_ref[rows, :])).astype(bf16)


def _hgrn_sample(hg, lbp, norm_w, s0t, layer):
    n = hg.shape[0]
    npair = HG_HEADS // 2
    blk = lambda off: pl.BlockSpec((n, LANES), lambda p, off=off: (0, off + p))
    s_spec = pl.BlockSpec((None, 2, HEAD_W, HEAD_W, NB), lambda p: (layer, p, 0, 0, 0))
    return _sample_call(
        functools.partial(_hgrn_sample_body, layer=layer), "hgrn_sample", (npair,),
        [blk(0), blk(npair), blk(2 * npair), blk(3 * npair),
         pl.BlockSpec((DEPTH, LANES), lambda p: (0, p)),
         pl.BlockSpec((None, 1, LANES), lambda p: (layer, 0, p)), s_spec],
        [hg, hg, hg, hg, lbp, norm_w, s0t],
        [pl.BlockSpec((n, LANES), lambda p: (0, p)), s_spec],
        [jax.ShapeDtypeStruct((n, HG_W), bf16), jax.ShapeDtypeStruct(s0t.shape, f32)],
        [pltpu.VMEM((LANES, NB), f32)] * 5, 1)


def _ffn_body(x_ref, mml_ref, mssd_ref, mhg_ref, wout_ref, n2_ref, wug_ref, wuv_ref, cwg_ref, cwv_ref,
              cbg_ref, cbv_ref, wdn_ref, f0g_ref, f0v_ref, fn_ref, y_ref, f1g_ref, f1v_ref,
              x1_s, h2_s, acc_s, act_s, ubuf_s, *, tm, shift, jb, nff, final):
    t = pl.program_id(1)
    f = pl.program_id(2)
    halo = max(HALO, (FFN_CONV - 1) * shift)
    keep = (FFN_CONV - 1) * shift

    def when(cond, fn):
        if nff == 1:
            fn()
        else:
            pl.when(cond)(fn)

    @pl.when(t == 0)
    def _():
        f1g_ref[0] = f0g_ref[0]
        f1v_ref[0] = f0v_ref[0]

    def first():
        x = x_ref[...]
        mix = jnp.concatenate([mml_ref[...], mssd_ref[...], mhg_ref[...]], axis=1)
        x1 = x + _dot(mix, wout_ref[...])
        x1_s[...] = x1
        h2_s[...] = (x1 * lax.rsqrt(jnp.mean(x1 * x1, axis=-1, keepdims=True) + EPS) * n2_ref[...]).astype(bf16)

    when(f == 0, first)
    h2 = h2_s[...]

    def conv(wu_ref, cw_ref, cb_ref, f1_ref, cols, ubuf):
        u = _dot(h2, wu_ref[:, cols])
        ubuf[halo - keep:halo, :] = f1_ref[0, :, cols]
        ubuf[halo:halo + tm, :] = u
        yv = (cb_ref[:, cols] + cw_ref[2:3, cols] * u
              + cw_ref[1:2, cols] * ubuf[halo - shift:halo - shift + tm, :]
              + cw_ref[0:1, cols] * ubuf[halo - 2 * shift:halo - 2 * shift + tm, :])
        f1_ref[0, :, cols] = ubuf[halo + tm - keep:halo + tm, :]
        return yv

    nbuf = ubuf_s.shape[0]
    for j in range(jb):
        cols = slice(j * FF_BLOCK, (j + 1) * FF_BLOCK)
        gq = conv(wug_ref, cwg_ref, cbg_ref, f1g_ref, cols, ubuf_s.at[(2 * j) % nbuf])
        vq = conv(wuv_ref, cwv_ref, cbv_ref, f1v_ref, cols, ubuf_s.at[(2 * j + 1) % nbuf])
        act_s[:, cols] = (_silu(gq) * vq).astype(bf16)
    contrib = _dot(act_s[...], wdn_ref[...])

    def finish(down):
        x2 = x1_s[...] + down
        if final:
            x2 = x2 * lax.rsqrt(jnp.mean(x2 * x2, axis=-1, keepdims=True) + EPS) * fn_ref[...]
        y_ref[...] = x2

    if nff == 1:
        finish(contrib)
    else:
        @pl.when(f == 0)
        def _():
            acc_s[...] = contrib

        @pl.when(f != 0)
        def _():
            acc_s[...] = acc_s[...] + contrib

        pl.when(f == nff - 1)(lambda: finish(acc_s[...]))


def _ffn(x, mml, mssd, mhg, w_out, norm2, w_up, conv_w, conv_b, w_down, f0, final_w, *, layer, groups, tm, shift, jb,
         final, f0_base=0):
    n = x.shape[0]
    nt = n // (groups * tm)
    fw = jb * FF_BLOCK
    nff = D_FF // fw
    assert nt == 1 or nff == 1
    halo = max(HALO, (FFN_CONV - 1) * shift)
    keep = (FFN_CONV - 1) * shift
    rows = lambda w: pl.BlockSpec((tm, w), lambda g, t, f: (g * nt + t, 0))
    single = nff == 1
    wspec = lambda shape, imap: pl.BlockSpec(shape, imap, pipeline_mode=pl.Buffered(1)) if single else pl.BlockSpec(shape, imap)
    f1_spec = pl.BlockSpec((1, keep, fw), lambda g, t, f: (g, 0, f))
    in_specs = [rows(D_MODEL), rows(ML_W), rows(SSD_W), rows(HG_W),
                _layer_resident((D_MODEL, D_MODEL), layer), _layer_resident((1, D_MODEL), layer),
                wspec((None, D_MODEL, fw), lambda g, t, f: (layer, 0, f)),
                wspec((None, D_MODEL, fw), lambda g, t, f: (layer, 0, nff + f)),
                wspec((None, FFN_CONV, fw), lambda g, t, f: (layer, 0, f)),
                wspec((None, FFN_CONV, fw), lambda g, t, f: (layer, 0, nff + f)),
                wspec((None, 1, fw), lambda g, t, f: (layer, 0, f)),
                wspec((None, 1, fw), lambda g, t, f: (layer, 0, nff + f)),
                wspec((None, fw, D_MODEL), lambda g, t, f: (layer, f, 0)),
                pl.BlockSpec((1, keep, fw), lambda g, t, f: (f0_base + g, 0, f)),
                pl.BlockSpec((1, keep, fw), lambda g, t, f: (f0_base + g, 0, nff + f)),
                _resident((1, D_MODEL))]
    operands = [x, mml, mssd, mhg, w_out, norm2, w_up, w_up, conv_w, conv_w, conv_b, conv_b, w_down, f0, f0, final_w]
    acc_rows = tm if nff > 1 else 8
    return pl.pallas_call(
        functools.partial(_ffn_body, tm=tm, shift=shift, jb=jb, nff=nff, final=final),
        grid=(groups, nt, nff),
        in_specs=in_specs,
        out_specs=[rows(D_MODEL), f1_spec, f1_spec],
        out_shape=[jax.ShapeDtypeStruct((n, D_MODEL), f32),
                   jax.ShapeDtypeStruct((groups, keep, D_FF), f32),
                   jax.ShapeDtypeStruct((groups, keep, D_FF), f32)],
        scratch_shapes=[pltpu.VMEM((tm, D_MODEL), f32), pltpu.VMEM((tm, D_MODEL), bf16),
                        pltpu.VMEM((acc_rows, D_MODEL), f32), pltpu.VMEM((tm, fw), bf16),
                        pltpu.VMEM((min(4, 2 * jb), halo + tm, FF_BLOCK), f32)],
        compiler_params=_cparams("arbitrary", "arbitrary", "arbitrary"),
        name="ffn",
    )(*operands)


def _gate_rows(*parts):
    r = jnp.concatenate([p.astype(f32) for p in parts], axis=1)
    return jnp.pad(r, ((0, 0), (0, GATE_W - r.shape[1])))[:, None, :]


def kernel(x_prompt, x_sample, state_mlstm_C, state_mlstm_n, state_mlstm_m, state_ssd, state_ssd_conv,
           state_hgrn, state_ffn_conv, norm1_w, w_in, ml_ig_b, ml_fg_b, ml_norm_w, ssd_conv_w, ssd_conv_b,
           ssd_dt_bias, ssd_A_log, ssd_D, ssd_norm_w, hg_lower_bounds, hg_norm_w, w_out, norm2_w, w_up,
           ffn_conv_w, ffn_conv_b, w_down, final_norm_w):
    bp, tp, _ = x_prompt.shape
    nb, ts, _ = x_sample.shape
    assert nb == NB and ts == TS and tp % ROW_TILE == 0
    depth = w_in.shape[0]
    assert depth == DEPTH
    xp = x_prompt.reshape(bp * tp, D_MODEL)
    xs = jnp.transpose(x_sample, (1, 0, 2)).reshape(ts * nb, D_MODEL)
    lbp = hg_lower_bounds.astype(f32)
    fin_w = final_norm_w[None, :]
    zero_f = jnp.zeros((bp, FFN_CONV - 1, 2 * D_FF), f32)

    w_cat = _prep_w_in(w_in)
    wo_b, wu_b, wd_b = w_out.astype(bf16), w_up.astype(bf16), w_down.astype(bf16)
    n1, n2 = norm1_w[:, None, :], norm2_w[:, None, :]
    gate_bias = _gate_rows(ml_ig_b, ml_fg_b, ssd_dt_bias)
    alog_col = jnp.pad(ssd_A_log.astype(f32), ((0, 0), (SSD_HEAD_ROW, GATE_ROWS - SSD_HEAD_ROW - SSD_HEADS)))[:, :, None]
    d_row = jnp.repeat(ssd_D.astype(f32), HEAD_W, axis=1)[:, None, :]
    mlw, ssw, hgw = ml_norm_w[:, None, :], ssd_norm_w[:, None, :], hg_norm_w[:, None, :]
    cw, cb = ssd_conv_w, ssd_conv_b[:, None, :]
    fcw, fcb = ffn_conv_w, ffn_conv_b[:, None, :]
    head_params = jnp.broadcast_to(
        jnp.stack([ssd_A_log, ssd_D], axis=1).astype(f32).reshape(depth, 2, SSD_HEADS // 2, 2, 1),
        (depth, 2, SSD_HEADS // 2, 2, LANES))

    c_t = jnp.transpose(state_mlstm_C.astype(f32), (0, 2, 3, 4, 1))
    n_t = jnp.transpose(state_mlstm_n.astype(f32), (0, 2, 3, 1))
    m_t = jnp.transpose(state_mlstm_m.astype(f32), (0, 2, 1))
    h_t = jnp.transpose(state_ssd.astype(f32), (0, 2, 3, 4, 1))
    h_new = []
    s_t = jnp.transpose(state_hgrn.astype(f32), (0, 2, 3, 4, 1))
    conv_halo = jnp.transpose(state_ssd_conv.astype(f32), (0, 2, 1, 3)).reshape(depth, (SSD_CONV - 1) * nb, SSD_CONV_DIM)
    f_t = jnp.transpose(state_ffn_conv.astype(f32), (0, 2, 1, 3)).reshape(depth, (FFN_CONV - 1) * nb, 2 * D_FF)

    p_states, s_sconv, s_ffn = [], [], []
    for l in range(depth):
        last = l == depth - 1

        ml, ssd, hg, gates, p_tail = _inproj(xp, n1, w_cat, l, conv=(cw, cb, tp))
        hm, p_c, p_n, p_m, hs, p_h, ho, p_s = _mixers_prompt(
            ml.reshape(bp, tp, ML_COLS), ssd.reshape(bp, tp, SSD_COLS), hg.reshape(bp, tp, HG_COLS),
            gates.reshape(bp, tp, GATE_W), gate_bias, alog_col, d_row, mlw, ssw, lbp, hgw, l)
        xp, p_fg, p_fv = _ffn(xp, hm.reshape(bp * tp, ML_W), hs.reshape(bp * tp, SSD_W), ho.reshape(bp * tp, HG_W),
                              wo_b, n2, wu_b, fcw, fcb, wd_b, zero_f, fin_w, layer=l,
                              groups=bp, tm=ROW_TILE, shift=1, jb=D_FF // FF_BLOCK, final=last)
        p_states.append((p_c, p_n, p_m[:, 0:ML_HEADS, 0],
                         p_h.reshape(bp, SSD_HEADS, HEAD_W, SSD_DSTATE),
                         p_tail[:, HALO - (SSD_CONV - 1):, :], p_s,
                         jnp.concatenate([p_fg, p_fv], axis=-1)))

        ml, ssd, hg, gates = _inproj(xs, n1, w_cat, l)
        hm, c_t, n_t, m_t = _mlstm_sample(ml, gates, gate_bias, mlw, c_t, n_t, m_t, l)
        xc = _ssd_conv_sample(ssd, conv_halo, cw, cb, l)
        hs, h_l = _ssd_sample(ssd, xc, gates, gate_bias, head_params, ssw, h_t, h_new, l)
        h_new.append(h_l)
        ho, s_t = _hgrn_sample(hg, lbp, hgw, s_t, l)
        xs, s_fg, s_fv = _ffn(xs, hm, hs, ho, wo_b, n2, wu_b, fcw, fcb, wd_b, f_t, fin_w, layer=l,
                              groups=1, tm=ts * nb, shift=nb, jb=1, final=last, f0_base=l)
        s_ffn.append(jnp.concatenate([s_fg, s_fv], axis=-1).reshape(FFN_CONV - 1, nb, 2 * D_FF))
        raw_xbc = ssd[:, SSD_W:].reshape(ts, nb, SSD_CONV_DIM)
        s_sconv.append(jnp.transpose(raw_xbc[ts - (SSD_CONV - 1):], (1, 0, 2)))

    y_prompt = xp.reshape(bp, tp, D_MODEL)
    y_sample = jnp.transpose(xs.reshape(ts, nb, D_MODEL), (1, 0, 2))
    p_out = [jnp.stack([st[i] for st in p_states]) for i in range(7)]
    s_out = [jnp.transpose(c_t, (0, 4, 1, 2, 3)), jnp.transpose(n_t, (0, 3, 1, 2)), jnp.transpose(m_t, (0, 2, 1)),
             h_new[-1].reshape(state_ssd.shape), jnp.stack(s_sconv),
             jnp.transpose(s_t, (0, 4, 1, 2, 3)), jnp.transpose(jnp.stack(s_ffn), (0, 2, 1, 3))]
    return (y_prompt, y_sample, *p_out, *s_out)
```

```python
import functools

import jax
import jax.numpy as jnp
from jax import lax
from jax.experimental import pallas as pl
from jax.experimental.pallas import tpu as pltpu

f32 = jnp.float32
bf16 = jnp.bfloat16

D_MODEL = 1024
DEPTH = 2
ML_HEADS = 6
HEAD_W = 64
ML_W = 384
SSD_HEADS = 6
SSD_W = 384
SSD_DSTATE = 128
SSD_CONV = 4
SSD_CONV_DIM = 896
HG_HEADS = 4
HG_W = 256
D_FF = 2816
FFN_CONV = 3
EPS = 1e-6
PROMPT_CHUNK = 128
HG_CHUNK = 64
LANES = 128
FF_BLOCK = 256
ROW_TILE = 512
SEQS_PER_STEP = 4
GATE_W = 128
GATE_ROWS = 16
W_IN_COLS = 4 * ML_W + (SSD_W + SSD_CONV_DIM) + 4 * HG_W + GATE_W
VMEM_LIMIT_BYTES = 56 * 1024 * 1024

NEG_INF = float("-inf")
LOG2_E = 1.4426950408889634


def _cparams(*sem):
    return pltpu.CompilerParams(dimension_semantics=sem, vmem_limit_bytes=VMEM_LIMIT_BYTES)


def _resident(shape):
    nd = len(shape)
    return pl.BlockSpec(shape, lambda *_: (0,) * nd, pipeline_mode=pl.Buffered(1))


def _layer_resident(shape, layer):
    nd = len(shape)
    return pl.BlockSpec((None,) + tuple(shape), lambda *_: (layer,) + (0,) * nd, pipeline_mode=pl.Buffered(1))


def _dot(a, b):
    return jnp.dot(a, b, preferred_element_type=f32)


def _dot_nt(a, b):
    return lax.dot_general(a, b, (((1,), (1,)), ((), ())), preferred_element_type=f32)


def _split_bf16(x, parts):
    out = []
    r = x
    for _ in range(parts):
        h = r.astype(bf16)
        out.append(h)
        r = r - h.astype(f32)
    return out


def _cumsum_rows(x, tri_b):
    acc = None
    for part in _split_bf16(x, 3):
        d = _dot(tri_b, part)
        acc = d if acc is None else acc + d
    return acc


def _group_mean(x, bd_b):
    return _dot(x.astype(bf16), bd_b) * (1.0 / HEAD_W)


def _iota2(shape, axis):
    return lax.broadcasted_iota(jnp.int32, shape, axis)


def _ones_where(mask):
    return jnp.where(mask, 1.0, 0.0).astype(bf16)


def _same_head(a, b):
    return jnp.right_shift(a, 6) == jnp.right_shift(b, 6)


def _silu(x):
    return x * jax.nn.sigmoid(x)


ML_COLS = 4 * ML_W
SSD_COLS = SSD_W + SSD_CONV_DIM
HG_COLS = 4 * HG_W


def _inproj_body(x_ref, nw_ref, w_ref, ml_ref, ssd_ref, hg_ref, g_ref):
    x = x_ref[...]
    h = (x * lax.rsqrt(jnp.mean(x * x, axis=-1, keepdims=True) + EPS) * nw_ref[...]).astype(bf16)
    o0, o1, o2 = ML_COLS, ML_COLS + SSD_COLS, ML_COLS + SSD_COLS + HG_COLS
    ml_ref[...] = _dot(h, w_ref[:, 0:o0])
    ssd_ref[...] = _dot(h, w_ref[:, o0:o1])
    hg_ref[...] = _dot(h, w_ref[:, o1:o2])
    g_ref[...] = _dot(h, w_ref[:, o2:W_IN_COLS])


def _inproj_conv_body(x_ref, nw_ref, w_ref, cw_ref, cb_ref, ml_ref, ssd_ref, hg_ref, g_ref, tail_ref, xbuf_s, *,
                      tiles_per_seq):
    i = pl.program_id(0)
    tm = x_ref.shape[0]

    @pl.when(lax.rem(i, tiles_per_seq) == 0)
    def _():
        xbuf_s[0:HALO, :] = jnp.zeros((HALO, SSD_CONV_DIM), f32)

    x = x_ref[...]
    h = (x * lax.rsqrt(jnp.mean(x * x, axis=-1, keepdims=True) + EPS) * nw_ref[...]).astype(bf16)
    o0, o1, o2 = ML_COLS, ML_COLS + SSD_COLS, ML_COLS + SSD_COLS + HG_COLS
    zx = _dot(h, w_ref[:, o0:o1])
    ssd_ref[:, 0:SSD_W] = zx[:, 0:SSD_W]
    xbuf_s[HALO:HALO + tm, :] = zx[:, SSD_W:SSD_COLS]
    conv = cb_ref[...]
    for j in range(SSD_CONV):
        o = HALO - (SSD_CONV - 1) + j
        conv = conv + cw_ref[j:j + 1, :] * xbuf_s[o:o + tm, :]
    ssd_ref[:, SSD_W:SSD_COLS] = _silu(conv)
    tail = xbuf_s[tm:tm + HALO, :]
    tail_ref[0] = tail
    xbuf_s[0:HALO, :] = tail
    ml_ref[...] = _dot(h, w_ref[:, 0:o0])
    hg_ref[...] = _dot(h, w_ref[:, o1:o2])
    g_ref[...] = _dot(h, w_ref[:, o2:W_IN_COLS])


def _inproj(x, norm_w, w_cat, layer, conv=None):
    n = x.shape[0]
    tm = ROW_TILE
    row = lambda w: pl.BlockSpec((tm, w), lambda i: (i, 0))
    in_specs = [row(D_MODEL), _layer_resident((1, D_MODEL), layer), _layer_resident((D_MODEL, W_IN_COLS), layer)]
    out_specs = [row(ML_COLS), row(SSD_COLS), row(HG_COLS), row(GATE_W)]
    out_shape = [jax.ShapeDtypeStruct((n, w), f32) for w in (ML_COLS, SSD_COLS, HG_COLS, GATE_W)]
    if conv is None:
        return pl.pallas_call(
            _inproj_body, grid=(n // tm,), in_specs=in_specs, out_specs=out_specs, out_shape=out_shape,
            compiler_params=_cparams("arbitrary"), name="inproj",
        )(x, norm_w, w_cat)
    conv_w, conv_b, rows_per_seq = conv
    tps = rows_per_seq // tm
    assert rows_per_seq % tm == 0
    return pl.pallas_call(
        functools.partial(_inproj_conv_body, tiles_per_seq=tps),
        grid=(n // tm,),
        in_specs=in_specs + [_layer_resident((SSD_CONV, SSD_CONV_DIM), layer), _layer_resident((1, SSD_CONV_DIM), layer)],
        out_specs=out_specs + [pl.BlockSpec((1, HALO, SSD_CONV_DIM), lambda i: (i // tps, 0, 0))],
        out_shape=out_shape + [jax.ShapeDtypeStruct((n // rows_per_seq, HALO, SSD_CONV_DIM), f32)],
        scratch_shapes=[pltpu.VMEM((HALO + tm, SSD_CONV_DIM), f32)],
        compiler_params=_cparams("arbitrary"),
        name="inproj_conv",
    )(x, norm_w, w_cat, conv_w, conv_b)


W_IN_GATE_IF = ML_COLS
W_IN_SSD = W_IN_GATE_IF + 2 * ML_HEADS
W_IN_DT = W_IN_SSD + SSD_COLS
W_IN_HG = W_IN_DT + SSD_HEADS
W_IN_DIM = W_IN_HG + HG_COLS
W_PREP_ROWS = 256


def _prep_w_in_body(w_ref, out_ref):
    lane = _iota2((W_PREP_ROWS, LANES), 1)
    tile = lambda k: w_ref[:, k * LANES:(k + 1) * LANES]
    out_ref[:, 0:ML_COLS] = w_ref[:, 0:ML_COLS].astype(bf16)

    def shifted(dst0, src0, ntiles):
        t0, sh = src0 // LANES, src0 % LANES
        rolled = [pltpu.roll(tile(t0 + k), LANES - sh, axis=1) for k in range(ntiles + 1)]
        for k in range(ntiles):
            out_ref[:, dst0 + k * LANES:dst0 + (k + 1) * LANES] = jnp.where(
                lane < LANES - sh, rolled[k], rolled[k + 1]).astype(bf16)

    shifted(ML_COLS, W_IN_SSD, SSD_COLS // LANES)
    shifted(ML_COLS + SSD_COLS, W_IN_HG, HG_COLS // LANES)
    assert W_IN_GATE_IF % LANES == 0 and W_IN_DT % LANES == DT_LANE
    gates = jnp.where(lane < DT_LANE, tile(W_IN_GATE_IF // LANES),
                      jnp.where(lane < DT_LANE + SSD_HEADS, tile(W_IN_DT // LANES), 0.0))
    out_ref[:, ML_COLS + SSD_COLS + HG_COLS:W_IN_COLS] = gates.astype(bf16)


def _prep_w_in(w_in):
    depth, d, n = w_in.shape
    assert n == W_IN_DIM and d % W_PREP_ROWS == 0
    return pl.pallas_call(
        _prep_w_in_body,
        grid=(depth, d // W_PREP_ROWS),
        in_specs=[pl.BlockSpec((None, W_PREP_ROWS, W_IN_COLS), lambda l, i: (l, i, 0))],
        out_specs=pl.BlockSpec((None, W_PREP_ROWS, W_IN_COLS), lambda l, i: (l, i, 0)),
        out_shape=jax.ShapeDtypeStruct((depth, d, W_IN_COLS), bf16),
        compiler_params=_cparams("arbitrary", "arbitrary"),
        name="prep_w_in",
    )(w_in)


def _one_seq(ref, bi):
    return ref.at[pl.ds(bi, 1)]


def _interleave(chunks):
    live = list(chunks)
    while live:
        nxt = []
        for g in live:
            try:
                next(g)
                nxt.append(g)
            except StopIteration:
                pass
        live = nxt


def _mlstm_prompt_chunk(ml_ref, g_ref, gb_ref, nw_ref, out_ref, ct_s, nr_s, m_s):
    L = PROMPT_CHUNK
    row = _iota2((L, L), 0)
    col = _iota2((L, L), 1)
    causal = row <= col
    tri_b = _ones_where(causal)
    top = row < HEAD_W
    lo_half = col < HEAD_W
    bd_mask = _same_head(row, col)
    sub8 = _iota2((8, L), 0)
    lo8 = _iota2((8, L), 1) < HEAD_W
    n_mask = jnp.where(sub8 == 0, jnp.where(lo8, 1.0, 0.0), jnp.where(sub8 == 1, jnp.where(lo8, 0.0, 1.0), 0.0))

    g_t = (g_ref[0] + gb_ref[...]).T
    b_t = None
    parts = _split_bf16(jax.nn.log_sigmoid(g_t[0:GATE_ROWS, :]), 3)
    yield
    for part in parts:
        d = _dot(part, tri_b)
        b_t = d if b_t is None else b_t + d
        yield
    z_rows = b_t[ML_HEADS:ML_HEADS + 8, :] - g_t[0:8, :]
    z_nat = jnp.concatenate([z_rows, jnp.zeros((L - 8, L), f32)], axis=0).T
    m_old = m_s[...]
    m_next = m_old
    yield

    for p in range(ML_HEADS // 2):
        sl = slice(p * LANES, (p + 1) * LANES)
        q2 = ml_ref[0, :, sl]
        k2 = ml_ref[0, :, ML_W + p * LANES:ML_W + (p + 1) * LANES] * (HEAD_W ** -0.5)
        v2 = ml_ref[0, :, 2 * ML_W + p * LANES:2 * ML_W + (p + 1) * LANES]
        o2 = ml_ref[0, :, 3 * ML_W + p * LANES:3 * ML_W + (p + 1) * LANES]
        q_t = q2.T.astype(bf16)
        v_t = v2.T
        kb = k2.astype(bf16)
        k_sel = (jnp.where(lo_half, k2, 0.0).astype(bf16), jnp.where(lo_half, 0.0, k2).astype(bf16))
        sw, den, a_int, e_m, a_st, w_s = [], [], [], [], [], []
        yield
        for j in range(2):
            h = 2 * p + j
            b_row = b_t[ML_HEADS + h:ML_HEADS + h + 1, :]
            i_row = g_t[h:h + 1, :]
            m_prev = m_old[h:h + 1, :]
            dm = jnp.where(causal, b_row - z_nat[:, h:h + 1], NEG_INF)
            inter = b_row + m_prev
            raw = _dot(k_sel[j], q_t)
            yield
            m_t = jnp.maximum(inter, jnp.max(dm, axis=0, keepdims=True))
            yield
            s_w = raw * jnp.exp(dm - m_t)
            den.append(jnp.sum(s_w, axis=0, keepdims=True))
            sw.append(s_w.astype(bf16))
            yield
            a_int.append(jnp.exp(inter - m_t))
            e_m.append(jnp.exp(-m_t))
            m_new = jnp.broadcast_to(m_t[:, L - 1:L], (1, L))
            b_last = jnp.broadcast_to(b_row[:, L - 1:L], (1, L))
            a_st.append(jnp.exp(b_last + m_prev - m_new))
            w_s.append(jnp.exp(b_last - b_row + i_row - m_new))
            m_next = jnp.where(sub8 == h, m_new, m_next)
            yield
        ct = ct_s[p]
        nr = nr_s[p]
        qn = _dot(nr.astype(bf16), q_t)
        num0 = _dot(jnp.where(top, v_t, 0.0).astype(bf16), sw[0])
        yield
        num1 = _dot(jnp.where(top, 0.0, v_t).astype(bf16), sw[1])
        qc = _dot(ct.astype(bf16), q_t)
        yield
        num = num0 + num1 + jnp.where(top, a_int[0], a_int[1]) * qc
        den0 = den[0] + a_int[0] * qn[0:1, :]
        den1 = den[1] + a_int[1] * qn[1:2, :]
        hh = num / jnp.where(top, jnp.maximum(jnp.abs(den0), e_m[0]), jnp.maximum(jnp.abs(den1), e_m[1]))
        yield
        sq = hh * hh
        r0 = lax.rsqrt(jnp.sum(sq[0:HEAD_W, :], axis=0, keepdims=True) * (1.0 / HEAD_W) + EPS)
        r1 = lax.rsqrt(jnp.sum(sq[HEAD_W:LANES, :], axis=0, keepdims=True) * (1.0 / HEAD_W) + EPS)
        y = (hh * jnp.where(top, r0, r1)).T * nw_ref[:, sl] * jax.nn.sigmoid(o2)
        out_ref[0, :, sl] = y.astype(bf16)
        yield
        vw = (v_t * jnp.where(top, w_s[0], w_s[1])).astype(bf16)
        ct_s[p] = jnp.where(top, a_st[0], a_st[1]) * ct + jnp.where(bd_mask, _dot(vw, kb), 0.0)
        w8 = jnp.where(sub8 == 0, w_s[0], jnp.where(sub8 == 1, w_s[1], 0.0)).astype(bf16)
        nr_s[p] = jnp.where(sub8 == 0, a_st[0], a_st[1]) * nr + n_mask * _dot(w8, kb)
        yield
    m_s[...] = m_next


DT_LANE = 2 * ML_HEADS
SSD_GATE_ROW0 = DT_LANE // 8 * 8
SSD_HEAD_ROW = DT_LANE - SSD_GATE_ROW0
HALO = 8


def _ssd_prompt_chunk(ssd_ref, g_ref, gb_ref, al_ref, d_ref, nw_ref, out_ref, h_s):
    L = PROMPT_CHUNK
    xc = ssd_ref[0, :, SSD_W:SSD_COLS]
    nb = SSD_W
    b_b = [xc[:, nb + gi * LANES:nb + (gi + 1) * LANES].astype(bf16) for gi in range(2)]
    c_b = [xc[:, nb + (2 + gi) * LANES:nb + (3 + gi) * LANES].astype(bf16) for gi in range(2)]

    row = _iota2((L, L), 0)
    col = _iota2((L, L), 1)
    tril = col <= row
    lo_half = col < HEAD_W
    triu_b = _ones_where(row <= col)
    bd_b = _ones_where(_same_head(row, col))
    rowc = _iota2((L, 1), 0)

    g_t = (g_ref[0] + gb_ref[...]).T
    dt_t = jax.nn.softplus(g_t[SSD_GATE_ROW0:SSD_GATE_ROW0 + GATE_ROWS, :])
    yield
    cum_t = None
    for part in _split_bf16(dt_t * (-jnp.exp(al_ref[...])), 3):
        d = _dot(part, triu_b)
        cum_t = d if cum_t is None else cum_t + d
    yield
    nat = jnp.concatenate([cum_t, dt_t, cum_t * LOG2_E, jnp.zeros((L - 3 * GATE_ROWS, L), f32)], axis=0).T
    cb_g = [_dot_nt(c_b[gi], b_b[gi]) for gi in range(2)]
    yield

    for p in range(SSD_HEADS // 2):
        sl = slice(p * LANES, (p + 1) * LANES)
        x2 = xc[:, sl]
        z2 = ssd_ref[0, :, sl]
        hp = h_s[p]
        hb = hp.astype(bf16)
        grp = [(2 * p + j) // (SSD_HEADS // 2) for j in range(2)]
        ys = None
        ecum, w_s, a_last, ch = [], [], [], []
        for j in range(2):
            r = SSD_HEAD_ROW + 2 * p + j
            cum_col = nat[:, r:r + 1]
            cum_last = cum_col[L - 1:L, :]
            key_row = cum_t[r:r + 1, :] * LOG2_E - jnp.log2(dt_t[r:r + 1, :])
            lm_dt = jnp.exp2(jnp.where(tril, nat[:, 2 * GATE_ROWS + r:2 * GATE_ROWS + r + 1] - key_row, NEG_INF))
            m = (cb_g[grp[j]] * lm_dt).astype(bf16)
            yield
            xm = jnp.where(lo_half, x2, 0.0) if j == 0 else jnp.where(lo_half, 0.0, x2)
            d = _dot(m, xm.astype(bf16))
            ys = d if ys is None else ys + d
            ecum.append(jnp.exp(cum_col))
            w_s.append(jnp.exp(cum_last - cum_col) * nat[:, GATE_ROWS + r:GATE_ROWS + r + 1])
            a_last.append(jnp.exp(cum_last))
            if j == 0 or grp[1] != grp[0]:
                ch.append(_dot_nt(c_b[grp[j]], hb))
            else:
                ch.append(ch[0])
            yield
        ys = ys + jnp.where(lo_half, ecum[0] * ch[0], ecum[1] * ch[1])
        xw_t = (x2 * jnp.where(lo_half, w_s[0], w_s[1])).T.astype(bf16)
        yield
        if grp[0] == grp[1]:
            upd = _dot(xw_t, b_b[grp[0]])
        else:
            upd = jnp.where(row < HEAD_W, _dot(xw_t, b_b[grp[0]]), _dot(xw_t, b_b[grp[1]]))
        h_s[p] = jnp.where(rowc < HEAD_W, a_last[0], a_last[1]) * hp + upd
        yy = (ys + d_ref[:, sl] * x2) * _silu(z2)
        yield
        ms = _group_mean(yy * yy, bd_b)
        yield
        out_ref[0, :, sl] = (yy * lax.rsqrt(ms + EPS) * nw_ref[:, sl]).astype(bf16)


def _hg_lower_bound(lbp, layer):
    mx = jnp.max(lbp, axis=0, keepdims=True)
    e = jnp.exp(lbp - mx)
    den = jnp.sum(e, axis=0, keepdims=True)
    lb = jnp.zeros_like(den)
    for j in range(1, layer + 1):
        lb = lb + e[j:j + 1, :] / den
    return lb


def _hgrn_prompt_chunk(hg_ref, lbp_ref, nw_ref, out_ref, sbd_s, p_s, r_s, o_s, g_s, k_s, v_s, q_s, kt_s, a_s, *, layer):
    L = HG_CHUNK
    nblk = L // 8
    lb = _hg_lower_bound(lbp_ref[...], layer)
    ff = hg_ref[0, :, HG_W:2 * HG_W]
    sig = jax.nn.sigmoid(ff)
    log2_k = jnp.log2((1.0 - lb) * jax.nn.sigmoid(-ff))
    v_s[...] = hg_ref[0, :, 2 * HG_W:3 * HG_W]
    q_s[...] = hg_ref[0, :, 0:HG_W] * (HEAD_W ** -0.5)
    row = _iota2((L, L), 0)
    col = _iota2((L, L), 1)
    tri_b = _ones_where(col <= row)
    yield
    g2 = _cumsum_rows(jnp.log(lb + (1.0 - lb) * sig), tri_b) * LOG2_E
    g_s[...] = g2
    k_s[...] = g2 - log2_k
    yield

    r2 = _iota2((2 * LANES, 2 * LANES), 0)
    c2 = _iota2((2 * LANES, 2 * LANES), 1)
    bd4_b = _ones_where(_same_head(r2, c2))
    r1 = _iota2((LANES, LANES), 0)
    c1 = _iota2((LANES, LANES), 1)
    bd_mask = _same_head(r1, c1)
    bd_b = _ones_where(bd_mask)

    a_s[0:8, :] = jnp.zeros((8, HG_W), f32)
    for i in range(1, nblk):
        blk = slice(8 * i, 8 * i + 8)
        n = 8 * i
        r_i = g_s[n - 1:n, :]
        kt = jnp.exp2(r_i - k_s[0:n, :])
        first_head = jnp.bitwise_and(_iota2((n, HG_W), 1), LANES - 1) < HEAD_W
        kt_s[i - 1, 0:n, :] = jnp.where(first_head, kt, 0.0)
        kt_s[i - 1, n:L, :] = jnp.zeros((L - n, HG_W), f32)
        kt_s[i - 1, L:L + n, :] = jnp.where(first_head, 0.0, kt)
        kt_s[i - 1, L + n:2 * L, :] = jnp.zeros((L - n, HG_W), f32)
        qt = q_s[blk, :] * jnp.exp2(g_s[blk, :] - r_i)
        qt = jnp.concatenate([qt, jnp.zeros_like(qt)], axis=0).astype(bf16)
        for p in range(HG_HEADS // 2):
            sl = slice(p * LANES, (p + 1) * LANES)
            a_s[blk, sl] = _dot_nt(qt[:, sl], kt_s[i - 1, :, sl].astype(bf16))[0:8, :]
        yield

    sub = _iota2((8, HG_W), 0)
    for j in range(nblk):
        blk = slice(8 * j, 8 * j + 8)
        g_blk, gk_blk, q_blk = g_s[blk, :], k_s[blk, :], q_s[blk, :]
        for r in range(8):
            s = 8 * j + r
            p_s[8 * s:8 * s + 8, :] = q_blk * jnp.exp2(jnp.where(sub >= r, g_blk - gk_blk[r:r + 1, :], NEG_INF))
        yield
    r_s[...] = _dot(p_s[...].astype(bf16), bd4_b)
    yield

    gg = g_s[...]
    vv = v_s[...]
    qd = q_s[...] * jnp.exp2(gg)
    lo_half = _iota2((L, LANES), 1) < HEAD_W
    for p in range(HG_HEADS // 2):
        sl = slice(p * LANES, (p + 1) * LANES)
        v_cat = jnp.concatenate([jnp.where(lo_half, vv[:, sl], 0.0), jnp.where(lo_half, 0.0, vv[:, sl])], axis=0)
        o_s[:, sl] = (_dot(qd[:, sl].astype(bf16), sbd_s[p].astype(bf16))
                      + _dot(a_s[:, sl].astype(bf16), v_cat.astype(bf16)))
        yield
    for j in range(nblk):
        blk = slice(8 * j, 8 * j + 8)
        acc = o_s[blk, :]
        v_blk = v_s[blk, :]
        for r in range(8):
            s = 8 * j + r
            acc = acc + r_s[8 * s:8 * s + 8, :] * v_blk[r:r + 1, :]
        o_s[blk, :] = acc
        yield

    g_last = gg[L - 1:L, :]
    kd = jnp.exp2(g_last - k_s[...])
    zpad = jnp.zeros((LANES - L, LANES), f32)
    og = o_s[...]
    gate = hg_ref[0, :, 3 * HG_W:4 * HG_W]
    for p in range(HG_HEADS // 2):
        sl = slice(p * LANES, (p + 1) * LANES)
        kd_t = jnp.concatenate([kd[:, sl], zpad], axis=0).T.astype(bf16)
        v_pad = jnp.concatenate([vv[:, sl], zpad], axis=0).astype(bf16)
        g_t = jnp.concatenate([gg[:, sl], zpad], axis=0).T
        dec = jnp.exp2(g_t[:, L - 1:L])
        sbd_s[p] = dec * sbd_s[p] + jnp.where(bd_mask, _dot(kd_t, v_pad), 0.0)
        yield
        oo = og[:, sl]
        ms = _group_mean(oo * oo, bd_b)
        out_ref[0, :, sl] = (oo * lax.rsqrt(ms + EPS) * nw_ref[:, sl] * _silu(gate[:, sl])).astype(bf16)


def _mixers_prompt_body(ml_ref, ssd_ref, hg_ref, g_ref, gb_ref, al_ref, d_ref, mlnw_ref, ssnw_ref,
                        lbp_ref, hgnw_ref,
                        hm_ref, c1_ref, n1_ref, m1_ref, hs_ref, h1_ref, ho_ref, s1_ref,
                        ct_s, nr_s, m_s, h_s, sbd_s, p_s, r_s, o_s, gg_s, k_s, v_s, q_s, kt_s, a_s, *, layer):
    c = pl.program_id(1)
    seqs = range(ml_ref.shape[0])

    @pl.when(c == 0)
    def _():
        ct_s[...] = jnp.zeros(ct_s.shape, f32)
        nr_s[...] = jnp.zeros(nr_s.shape, f32)
        m_s[...] = jnp.zeros(m_s.shape, f32)
        h_s[...] = jnp.zeros(h_s.shape, f32)
        sbd_s[...] = jnp.zeros(sbd_s.shape, f32)

    def hgrn_chunks(bi):
        for half in range(PROMPT_CHUNK // HG_CHUNK):
            rows = pl.ds(half * HG_CHUNK, HG_CHUNK)
            yield from _hgrn_prompt_chunk(hg_ref.at[pl.ds(bi, 1), rows], lbp_ref, hgnw_ref, ho_ref.at[pl.ds(bi, 1), rows],
                                          sbd_s.at[bi], p_s.at[bi], r_s.at[bi], o_s.at[bi], gg_s.at[bi], k_s.at[bi],
                                          v_s.at[bi], q_s.at[bi], kt_s.at[bi], a_s.at[bi], layer=layer)

    chunks = []
    for bi in seqs:
        chunks.append(_ssd_prompt_chunk(_one_seq(ssd_ref, bi), _one_seq(g_ref, bi), gb_ref, al_ref, d_ref,
                                        ssnw_ref, _one_seq(hs_ref, bi), h_s.at[bi]))
        chunks.append(hgrn_chunks(bi))
        chunks.append(_mlstm_prompt_chunk(_one_seq(ml_ref, bi), _one_seq(g_ref, bi), gb_ref, mlnw_ref,
                                          _one_seq(hm_ref, bi), ct_s.at[bi], nr_s.at[bi], m_s.at[bi]))
    _interleave(chunks)

    @pl.when(c == pl.num_programs(1) - 1)
    def _():
        h1_ref[...] = h_s[...]
        for bi in seqs:
            for p in range(ML_HEADS // 2):
                c_pair = ct_s[bi, p].T
                c1_ref[bi, 2 * p] = c_pair[0:HEAD_W, 0:HEAD_W]
                c1_ref[bi, 2 * p + 1] = c_pair[HEAD_W:LANES, HEAD_W:LANES]
                n1_ref[bi, 2 * p:2 * p + 1, :] = nr_s[bi, p, 0:1, 0:HEAD_W]
                n1_ref[bi, 2 * p + 1:2 * p + 2, :] = nr_s[bi, p, 1:2, HEAD_W:LANES]
            m1_ref[bi] = m_s[bi]
            for p in range(HG_HEADS // 2):
                sbd = sbd_s[bi, p]
                s1_ref[bi, 2 * p] = sbd[0:HEAD_W, 0:HEAD_W]
                s1_ref[bi, 2 * p + 1] = sbd[HEAD_W:LANES, HEAD_W:LANES]


def _mixers_prompt(ml, ssd, hg, gates, gate_bias, alog_col, d_row, ml_nw, ssd_nw, lbp, hg_nw, layer):
    b, t, _ = ml.shape
    L = PROMPT_CHUNK
    ns = SEQS_PER_STEP
    assert b % ns == 0 and t % L == 0
    ml_pairs, ssd_pairs, hg_pairs = ML_HEADS // 2, SSD_HEADS // 2, HG_HEADS // 2
    tok = lambda w: pl.BlockSpec((ns, L, w), lambda i, c: (i, c, 0))
    per_seq = lambda *shape: pl.BlockSpec((ns,) + shape, lambda i, c: (i,) + (0,) * len(shape))
    hg_rows = pltpu.VMEM((ns, HG_CHUNK, HG_W), f32)
    hg_pairs_buf = pltpu.VMEM((ns, 8 * HG_CHUNK, HG_W), f32)
    return pl.pallas_call(
        functools.partial(_mixers_prompt_body, layer=layer),
        grid=(b // ns, t // L),
        in_specs=[tok(ML_COLS), tok(SSD_COLS), tok(HG_COLS), tok(GATE_W),
                  _layer_resident((1, GATE_W), layer), _layer_resident((GATE_ROWS, 1), layer),
                  _layer_resident((1, SSD_W), layer), _layer_resident((1, ML_W), layer),
                  _layer_resident((1, SSD_W), layer), _resident((DEPTH, HG_W)), _layer_resident((1, HG_W), layer)],
        out_specs=[tok(ML_W), per_seq(ML_HEADS, HEAD_W, HEAD_W), per_seq(ML_HEADS, HEAD_W), per_seq(8, LANES),
                   tok(SSD_W), per_seq(ssd_pairs, LANES, SSD_DSTATE),
                   tok(HG_W), per_seq(HG_HEADS, HEAD_W, HEAD_W)],
        out_shape=[jax.ShapeDtypeStruct((b, t, ML_W), bf16),
                   jax.ShapeDtypeStruct((b, ML_HEADS, HEAD_W, HEAD_W), f32),
                   jax.ShapeDtypeStruct((b, ML_HEADS, HEAD_W), f32),
                   jax.ShapeDtypeStruct((b, 8, LANES), f32),
                   jax.ShapeDtypeStruct((b, t, SSD_W), bf16),
                   jax.ShapeDtypeStruct((b, ssd_pairs, LANES, SSD_DSTATE), f32),
                   jax.ShapeDtypeStruct((b, t, HG_W), bf16),
                   jax.ShapeDtypeStruct((b, HG_HEADS, HEAD_W, HEAD_W), f32)],
        scratch_shapes=[pltpu.VMEM((ns, ml_pairs, LANES, LANES), f32), pltpu.VMEM((ns, ml_pairs, 8, LANES), f32),
                        pltpu.VMEM((ns, 8, LANES), f32),
                        pltpu.VMEM((ns, ssd_pairs, LANES, SSD_DSTATE), f32),
                        pltpu.VMEM((ns, hg_pairs, LANES, LANES), f32), hg_pairs_buf, hg_pairs_buf,
                        hg_rows, hg_rows, hg_rows, hg_rows, hg_rows,
                        pltpu.VMEM((ns, HG_CHUNK // 8 - 1, 2 * HG_CHUNK, HG_W), f32), hg_rows],
        compiler_params=_cparams("arbitrary", "arbitrary"),
        name="mixers_prompt",
    )(ml, ssd, hg, gates, gate_bias, alog_col, d_row, ml_nw, ssd_nw, lbp, hg_nw)


NB = 128
TS = 8


def _sample_call(body, name, grid, in_specs, operands, out_specs, out_shapes, scratch, n_state, n_fresh=0):
    first = len(operands) - n_state
    return pl.pallas_call(
        body, grid=grid, in_specs=in_specs, out_specs=out_specs, out_shape=out_shapes,
        scratch_shapes=scratch, input_output_aliases={first + i: 1 + i for i in range(n_fresh, n_state)},
        compiler_params=_cparams(*(["arbitrary"] * len(grid))), name=name,
    )(*operands)


def _mlstm_sample_body(q_ref, k_ref, v_ref, o_ref, g_ref, gb_ref, nw_ref, c0_ref, n0_ref, m0_ref,
                       out_ref, c1_ref, n1_ref, m1_ref, qt_s, kt_s, vt_s, gt_s, ht_s, *, layer, all_layers):
    p = pl.program_id(0)
    if all_layers:
        for l in range(DEPTH):
            if l != layer:
                c1_ref[l] = c0_ref[l]
        c0_ref, c1_ref = c0_ref.at[layer], c1_ref.at[layer]
    n_cur = [n0_ref[hl] for hl in range(2)]
    m_cur = [m0_ref[pl.ds(2 * p + hl, 1), :] for hl in range(2)]

    for t in range(TS):
        c_src = c0_ref if t == 0 else c1_ref
        rows = slice(t * NB, (t + 1) * NB)
        qt_s[...] = q_ref[rows, :].T
        kt_s[...] = k_ref[rows, :].T * (HEAD_W ** -0.5)
        vt_s[...] = v_ref[rows, :].T
        gt_s[...] = (g_ref[rows, :] + gb_ref[...]).T
        for hl in range(2):
            h = 2 * p + hl
            hs = slice(hl * HEAD_W, (hl + 1) * HEAD_W)
            i_t = gt_s[pl.ds(h, 1), :]
            lf = jax.nn.log_sigmoid(gt_s[pl.ds(ML_HEADS + h, 1), :])
            m_new = jnp.maximum(lf + m_cur[hl], i_t)
            a = jnp.exp(lf + m_cur[hl] - m_new)
            w = jnp.exp(i_t - m_new)
            n_new = a * n_cur[hl] + w * kt_s[hs, :]
            v_h = vt_s[hs, :]

            def body(d, num, hl=hl, a=a, w=w, v_h=v_h, c_src=c_src):
                kd = kt_s[pl.ds(hl * HEAD_W + d, 1), :] * w
                qd = qt_s[pl.ds(hl * HEAD_W + d, 1), :]
                cd = a * c_src[hl, d] + kd * v_h
                c1_ref[hl, d] = cd
                return num + qd * cd

            num = lax.fori_loop(0, HEAD_W, body, jnp.zeros((HEAD_W, NB), f32), unroll=4)
            den = jnp.sum(qt_s[hs, :] * n_new, axis=0, keepdims=True)
            hh = num / jnp.maximum(jnp.abs(den), jnp.exp(-m_new))
            ms = jnp.mean(hh * hh, axis=0, keepdims=True)
            ht_s[hs, :] = hh * lax.rsqrt(ms + EPS)
            n_cur[hl] = n_new
            m_cur[hl] = m_new
        out_ref[rows, :] = (ht_s[...].T * nw_ref[...] * jax.nn.sigmoid(o_ref[rows, :])).astype(bf16)

    for hl in range(2):
        n1_ref[hl] = n_cur[hl]
        m1_ref[pl.ds(2 * p + hl, 1), :] = m_cur[hl]


def _mlstm_sample(ml, gates, gate_bias, norm_w, c0t, n0t, m0t, layer):
    n = ml.shape[0]
    npair = ML_HEADS // 2
    blk = lambda off: pl.BlockSpec((n, LANES), lambda p, off=off: (0, off + p))
    all_layers = layer == 0
    if all_layers:
        c_spec = pl.BlockSpec((DEPTH, 2, HEAD_W, HEAD_W, NB), lambda p: (0, p, 0, 0, 0))
    else:
        c_spec = pl.BlockSpec((None, 2, HEAD_W, HEAD_W, NB), lambda p: (layer, p, 0, 0, 0))
    n_spec = pl.BlockSpec((None, 2, HEAD_W, NB), lambda p: (layer, p, 0, 0))
    m_spec = pl.BlockSpec((None, ML_HEADS, NB), lambda p: (layer, 0, 0))
    return _sample_call(
        functools.partial(_mlstm_sample_body, layer=layer, all_layers=all_layers), "mlstm_sample", (npair,),
        [blk(0), blk(npair), blk(2 * npair), blk(3 * npair), _resident((n, GATE_W)),
         _layer_resident((1, GATE_W), layer), pl.BlockSpec((None, 1, LANES), lambda p: (layer, 0, p)),
         c_spec, n_spec, m_spec],
        [ml, ml, ml, ml, gates, gate_bias, norm_w, c0t, n0t, m0t],
        [pl.BlockSpec((n, LANES), lambda p: (0, p)), c_spec, n_spec, m_spec],
        [jax.ShapeDtypeStruct((n, ML_W), bf16), jax.ShapeDtypeStruct(c0t.shape, f32),
         jax.ShapeDtypeStruct(n0t.shape, f32), jax.ShapeDtypeStruct(m0t.shape, f32)],
        [pltpu.VMEM((LANES, NB), f32)] * 5, 3, n_fresh=1 if all_layers else 0)


def _ssd_conv_sample_body(ssd_ref, halo_ref, cw_ref, cb_ref, out_ref, xp_s):
    nh = (SSD_CONV - 1) * NB
    n = TS * NB
    xp_s[0:nh, :] = halo_ref[...]
    xp_s[nh:nh + n, :] = ssd_ref[:, SSD_W:SSD_COLS]
    conv = cb_ref[...]
    for j in range(SSD_CONV):
        conv = conv + cw_ref[j:j + 1, :] * xp_s[j * NB:j * NB + n, :]
    out_ref[...] = _silu(conv)


def _ssd_conv_sample(ssd, halo, conv_w, conv_b, layer):
    n = ssd.shape[0]
    nh = (SSD_CONV - 1) * NB
    return pl.pallas_call(
        _ssd_conv_sample_body,
        grid=(1,),
        in_specs=[_resident((n, SSD_COLS)), _layer_resident((nh, SSD_CONV_DIM), layer),
                  _layer_resident((SSD_CONV, SSD_CONV_DIM), layer), _layer_resident((1, SSD_CONV_DIM), layer)],
        out_specs=pl.BlockSpec((n, SSD_CONV_DIM), lambda i: (0, 0)),
        out_shape=jax.ShapeDtypeStruct((n, SSD_CONV_DIM), f32),
        scratch_shapes=[pltpu.VMEM((nh + n, SSD_CONV_DIM), f32)],
        compiler_params=_cparams("arbitrary"),
        name="ssd_conv_sample",
    )(ssd, halo, conv_w, conv_b)


def _ssd_sample_body(z_ref, x_ref, b0_ref, b1_ref, c0_ref, c1_ref, g_ref, gb_ref, hp_ref, nw_ref, h0_ref,
                     out_ref, h1_ref, xt_s, bt_s, ct_s, gt_s, dd_s, yt_s):
    p = pl.program_id(0)
    b_refs = (b0_ref, b1_ref)
    c_refs = (c0_ref, c1_ref)

    for t in range(TS):
        rows = slice(t * NB, (t + 1) * NB)
        xt_s[t] = x_ref[rows, :].T
        gt_s[...] = (g_ref[rows, :] + gb_ref[...]).T
        for hl in range(2):
            bt_s[t, hl] = b_refs[hl][rows, :].T
            ct_s[t, hl] = c_refs[hl][rows, :].T
            dt = jax.nn.softplus(gt_s[pl.ds(DT_LANE + 2 * p + hl, 1), :])
            dd_s[t, 2 * hl:2 * hl + 1, :] = dt
            dd_s[t, 2 * hl + 1:2 * hl + 2, :] = jnp.exp(dt * (-jnp.exp(hp_ref[0, hl:hl + 1, :])))

    for hl in range(2):
        def body(pp, carry, hl=hl):
            row = pl.ds(hl * HEAD_W + pp, 1)
            h = h0_ref[hl, pp]
            for t in range(TS):
                xrow = xt_s[t, row, :] * dd_s[t, 2 * hl:2 * hl + 1, :]
                h = dd_s[t, 2 * hl + 1:2 * hl + 2, :] * h + xrow * bt_s[t, hl]
                yt_s[t, row, :] = jnp.sum(ct_s[t, hl] * h, axis=0, keepdims=True)
            h1_ref[hl, pp] = h
            return carry

        lax.fori_loop(0, HEAD_W, body, 0, unroll=2)

    for t in range(TS):
        rows = slice(t * NB, (t + 1) * NB)
        zt = z_ref[rows, :].T
        for hl in range(2):
            hs = slice(hl * HEAD_W, (hl + 1) * HEAD_W)
            yy = (yt_s[t, hs, :] + hp_ref[1, hl:hl + 1, :] * xt_s[t, hs, :]) * _silu(zt[hs, :])
            ms = jnp.mean(yy * yy, axis=0, keepdims=True)
            yt_s[t, hs, :] = yy * lax.rsqrt(ms + EPS)
        out_ref[rows, :] = (yt_s[t].T * nw_ref[...]).astype(bf16)


def _ssd_sample(ssd, xc, gates, gate_bias, head_params, norm_w, h0t, layer):
    n = ssd.shape[0]
    npair = SSD_HEADS // 2
    hpg = SSD_HEADS // 2
    nxb = SSD_W // LANES
    blk = lambda f: pl.BlockSpec((n, LANES), f)
    h_spec = pl.BlockSpec((None, 2, HEAD_W, SSD_DSTATE, NB), lambda p: (layer, p, 0, 0, 0))
    return _sample_call(
        _ssd_sample_body, "ssd_sample", (npair,),
        [blk(lambda p: (0, p)), blk(lambda p: (0, p)),
         blk(lambda p: (0, nxb + (2 * p) // hpg)), blk(lambda p: (0, nxb + (2 * p + 1) // hpg)),
         blk(lambda p: (0, nxb + 2 + (2 * p) // hpg)), blk(lambda p: (0, nxb + 2 + (2 * p + 1) // hpg)),
         _resident((n, GATE_W)), _layer_resident((1, GATE_W), layer),
         pl.BlockSpec((None, 2, None, 2, LANES), lambda p: (layer, 0, p, 0, 0)),
         pl.BlockSpec((None, 1, LANES), lambda p: (layer, 0, p)), h_spec],
        [ssd, xc, xc, xc, xc, xc, gates, gate_bias, head_params, norm_w, h0t],
        [pl.BlockSpec((n, LANES), lambda p: (0, p)), h_spec],
        [jax.ShapeDtypeStruct((n, SSD_W), bf16), jax.ShapeDtypeStruct(h0t.shape, f32)],
        [pltpu.VMEM((TS, LANES, NB), f32), pltpu.VMEM((TS, 2, SSD_DSTATE, NB), f32),
         pltpu.VMEM((TS, 2, SSD_DSTATE, NB), f32), pltpu.VMEM((LANES, NB), f32), pltpu.VMEM((TS, 8, NB), f32),
         pltpu.VMEM((TS, LANES, NB), f32)], 1)


def _hgrn_sample_body(q_ref, f_ref, i_ref, gate_ref, lbp_ref, nw_ref, s0_ref, out_ref, s1_ref,
                      qt_s, kt_s, ft_s, vt_s, ot_s, *, layer, all_layers):
    lb = _hg_lower_bound(lbp_ref[...], layer)
    if all_layers:
        for l in range(DEPTH):
            if l != layer:
                s1_ref[l] = s0_ref[l]
        s0_ref, s1_ref = s0_ref.at[layer], s1_ref.at[layer]

    for t in range(TS):
        s_src = s0_ref if t == 0 else s1_ref
        rows = slice(t * NB, (t + 1) * NB)
        ff = f_ref[rows, :]
        qt_s[...] = (q_ref[rows, :] * (HEAD_W ** -0.5)).T
        kt_s[...] = ((1.0 - lb) * jax.nn.sigmoid(-ff)).T
        ft_s[...] = (lb + (1.0 - lb) * jax.nn.sigmoid(ff)).T
        vt_s[...] = i_ref[rows, :].T
        for hl in range(2):
            hs = slice(hl * HEAD_W, (hl + 1) * HEAD_W)
            v_h = vt_s[hs, :]

            def body(k, o, hl=hl, v_h=v_h, s_src=s_src):
                r = pl.ds(hl * HEAD_W + k, 1)
                sk = ft_s[r, :] * s_src[hl, k] + kt_s[r, :] * v_h
                s1_ref[hl, k] = sk
                return o + qt_s[r, :] * sk

            o = lax.fori_loop(0, HEAD_W, body, jnp.zeros((HEAD_W, NB), f32), unroll=4)
            ms = jnp.mean(o * o, axis=0, keepdims=True)
            ot_s[hs, :] = o * lax.rsqrt(ms + EPS)
        out_ref[rows, :] = (ot_s[...].T * nw_ref[...] * _silu(gate_ref[rows, :])).astype(bf16)


def _hgrn_sample(hg, lbp, norm_w, s0t, layer):
    n = hg.shape[0]
    npair = HG_HEADS // 2
    blk = lambda off: pl.BlockSpec((n, LANES), lambda p, off=off: (0, off + p))
    all_layers = layer == 0
    if all_layers:
        s_spec = pl.BlockSpec((DEPTH, 2, HEAD_W, HEAD_W, NB), lambda p: (0, p, 0, 0, 0))
    else:
        s_spec = pl.BlockSpec((None, 2, HEAD_W, HEAD_W, NB), lambda p: (layer, p, 0, 0, 0))
    return _sample_call(
        functools.partial(_hgrn_sample_body, layer=layer, all_layers=all_layers), "hgrn_sample", (npair,),
        [blk(0), blk(npair), blk(2 * npair), blk(3 * npair),
         pl.BlockSpec((DEPTH, LANES), lambda p: (0, p)),
         pl.BlockSpec((None, 1, LANES), lambda p: (layer, 0, p)), s_spec],
        [hg, hg, hg, hg, lbp, norm_w, s0t],
        [pl.BlockSpec((n, LANES), lambda p: (0, p)), s_spec],
        [jax.ShapeDtypeStruct((n, HG_W), bf16), jax.ShapeDtypeStruct(s0t.shape, f32)],
        [pltpu.VMEM((LANES, NB), f32)] * 5, 1, n_fresh=1 if all_layers else 0)


def _ffn_body(x_ref, mml_ref, mssd_ref, mhg_ref, wout_ref, n2_ref, wug_ref, wuv_ref, cwg_ref, cwv_ref,
              cbg_ref, cbv_ref, wdn_ref, f0g_ref, f0v_ref, fn_ref, y_ref, f1g_ref, f1v_ref,
              x1_s, h2_s, acc_s, act_s, ubuf_s, *, tm, shift, jb, nff, final):
    t = pl.program_id(1)
    f = pl.program_id(2)
    halo = max(HALO, (FFN_CONV - 1) * shift)
    keep = (FFN_CONV - 1) * shift

    def when(cond, fn):
        if nff == 1:
            fn()
        else:
            pl.when(cond)(fn)

    @pl.when(t == 0)
    def _():
        f1g_ref[0] = f0g_ref[0]
        f1v_ref[0] = f0v_ref[0]

    def first():
        x = x_ref[...]
        mix = jnp.concatenate([mml_ref[...], mssd_ref[...], mhg_ref[...]], axis=1)
        x1 = x + _dot(mix, wout_ref[...])
        x1_s[...] = x1
        h2_s[...] = (x1 * lax.rsqrt(jnp.mean(x1 * x1, axis=-1, keepdims=True) + EPS) * n2_ref[...]).astype(bf16)

    when(f == 0, first)
    h2 = h2_s[...]

    def conv(wu_ref, cw_ref, cb_ref, f1_ref, cols, ubuf):
        u = _dot(h2, wu_ref[:, cols])
        ubuf[halo - keep:halo, :] = f1_ref[0, :, cols]
        ubuf[halo:halo + tm, :] = u
        yv = (cb_ref[:, cols] + cw_ref[2:3, cols] * u
              + cw_ref[1:2, cols] * ubuf[halo - shift:halo - shift + tm, :]
              + cw_ref[0:1, cols] * ubuf[halo - 2 * shift:halo - 2 * shift + tm, :])
        f1_ref[0, :, cols] = ubuf[halo + tm - keep:halo + tm, :]
        return yv

    nbuf = ubuf_s.shape[0]
    for j in range(jb):
        cols = slice(j * FF_BLOCK, (j + 1) * FF_BLOCK)
        gq = conv(wug_ref, cwg_ref, cbg_ref, f1g_ref, cols, ubuf_s.at[(2 * j) % nbuf])
        vq = conv(wuv_ref, cwv_ref, cbv_ref, f1v_ref, cols, ubuf_s.at[(2 * j + 1) % nbuf])
        act_s[:, cols] = (_silu(gq) * vq).astype(bf16)
    contrib = _dot(act_s[...], wdn_ref[...])

    def finish(down):
        x2 = x1_s[...] + down
        if final:
            x2 = x2 * lax.rsqrt(jnp.mean(x2 * x2, axis=-1, keepdims=True) + EPS) * fn_ref[...]
        y_ref[...] = x2

    if nff == 1:
        finish(contrib)
    else:
        @pl.when(f == 0)
        def _():
            acc_s[...] = contrib

        @pl.when(f != 0)
        def _():
            acc_s[...] = acc_s[...] + contrib

        pl.when(f == nff - 1)(lambda: finish(acc_s[...]))


def _ffn(x, mml, mssd, mhg, w_out, norm2, w_up, conv_w, conv_b, w_down, f0, final_w, *, layer, groups, tm, shift, jb,
         final, f0_base=0):
    n = x.shape[0]
    nt = n // (groups * tm)
    fw = jb * FF_BLOCK
    nff = D_FF // fw
    assert nt == 1 or nff == 1
    halo = max(HALO, (FFN_CONV - 1) * shift)
    keep = (FFN_CONV - 1) * shift
    rows = lambda w: pl.BlockSpec((tm, w), lambda g, t, f: (g * nt + t, 0))
    single = nff == 1
    wspec = lambda shape, imap: pl.BlockSpec(shape, imap, pipeline_mode=pl.Buffered(1)) if single else pl.BlockSpec(shape, imap)
    f1_spec = pl.BlockSpec((1, keep, fw), lambda g, t, f: (g, 0, f))
    in_specs = [rows(D_MODEL), rows(ML_W), rows(SSD_W), rows(HG_W),
                _layer_resident((D_MODEL, D_MODEL), layer), _layer_resident((1, D_MODEL), layer),
                wspec((None, D_MODEL, fw), lambda g, t, f: (layer, 0, f)),
                wspec((None, D_MODEL, fw), lambda g, t, f: (layer, 0, nff + f)),
                wspec((None, FFN_CONV, fw), lambda g, t, f: (layer, 0, f)),
                wspec((None, FFN_CONV, fw), lambda g, t, f: (layer, 0, nff + f)),
                wspec((None, 1, fw), lambda g, t, f: (layer, 0, f)),
                wspec((None, 1, fw), lambda g, t, f: (layer, 0, nff + f)),
                wspec((None, fw, D_MODEL), lambda g, t, f: (layer, f, 0)),
                pl.BlockSpec((1, keep, fw), lambda g, t, f: (f0_base + g, 0, f)),
                pl.BlockSpec((1, keep, fw), lambda g, t, f: (f0_base + g, 0, nff + f)),
                _resident((1, D_MODEL))]
    operands = [x, mml, mssd, mhg, w_out, norm2, w_up, w_up, conv_w, conv_w, conv_b, conv_b, w_down, f0, f0, final_w]
    acc_rows = tm if nff > 1 else 8
    return pl.pallas_call(
        functools.partial(_ffn_body, tm=tm, shift=shift, jb=jb, nff=nff, final=final),
        grid=(groups, nt, nff),
        in_specs=in_specs,
        out_specs=[rows(D_MODEL), f1_spec, f1_spec],
        out_shape=[jax.ShapeDtypeStruct((n, D_MODEL), f32),
                   jax.ShapeDtypeStruct((groups, keep, D_FF), f32),
                   jax.ShapeDtypeStruct((groups, keep, D_FF), f32)],
        scratch_shapes=[pltpu.VMEM((tm, D_MODEL), f32), pltpu.VMEM((tm, D_MODEL), bf16),
                        pltpu.VMEM((acc_rows, D_MODEL), f32), pltpu.VMEM((tm, fw), bf16),
                        pltpu.VMEM((min(4, 2 * jb), halo + tm, FF_BLOCK), f32)],
        compiler_params=_cparams("arbitrary", "arbitrary", "arbitrary"),
        name="ffn",
    )(*operands)


def _gate_rows(*parts):
    r = jnp.concatenate([p.astype(f32) for p in parts], axis=1)
    return jnp.pad(r, ((0, 0), (0, GATE_W - r.shape[1])))[:, None, :]


def kernel(x_prompt, x_sample, state_mlstm_C, state_mlstm_n, state_mlstm_m, state_ssd, state_ssd_conv,
           state_hgrn, state_ffn_conv, norm1_w, w_in, ml_ig_b, ml_fg_b, ml_norm_w, ssd_conv_w, ssd_conv_b,
           ssd_dt_bias, ssd_A_log, ssd_D, ssd_norm_w, hg_lower_bounds, hg_norm_w, w_out, norm2_w, w_up,
           ffn_conv_w, ffn_conv_b, w_down, final_norm_w):
    bp, tp, _ = x_prompt.shape
    nb, ts, _ = x_sample.shape
    assert nb == NB and ts == TS and tp % ROW_TILE == 0
    depth = w_in.shape[0]
    assert depth == DEPTH
    xp = x_prompt.reshape(bp * tp, D_MODEL)
    xs = jnp.transpose(x_sample, (1, 0, 2)).reshape(ts * nb, D_MODEL)
    lbp = hg_lower_bounds.astype(f32)
    fin_w = final_norm_w[None, :]
    zero_f = jnp.zeros((bp, FFN_CONV - 1, 2 * D_FF), f32)

    w_cat = _prep_w_in(w_in)
    wo_b, wu_b, wd_b = w_out.astype(bf16), w_up.astype(bf16), w_down.astype(bf16)
    n1, n2 = norm1_w[:, None, :], norm2_w[:, None, :]
    gate_bias = _gate_rows(ml_ig_b, ml_fg_b, ssd_dt_bias)
    alog_col = jnp.pad(ssd_A_log.astype(f32), ((0, 0), (SSD_HEAD_ROW, GATE_ROWS - SSD_HEAD_ROW - SSD_HEADS)))[:, :, None]
    d_row = jnp.repeat(ssd_D.astype(f32), HEAD_W, axis=1)[:, None, :]
    mlw, ssw, hgw = ml_norm_w[:, None, :], ssd_norm_w[:, None, :], hg_norm_w[:, None, :]
    cw, cb = ssd_conv_w, ssd_conv_b[:, None, :]
    fcw, fcb = ffn_conv_w, ffn_conv_b[:, None, :]
    head_params = jnp.broadcast_to(
        jnp.stack([ssd_A_log, ssd_D], axis=1).astype(f32).reshape(depth, 2, SSD_HEADS // 2, 2, 1),
        (depth, 2, SSD_HEADS // 2, 2, LANES))

    c_t = jnp.transpose(state_mlstm_C.astype(f32), (0, 2, 3, 4, 1))
    n_t = jnp.transpose(state_mlstm_n.astype(f32), (0, 2, 3, 1))
    m_t = jnp.transpose(state_mlstm_m.astype(f32), (0, 2, 1))
    h_t = jnp.transpose(state_ssd.astype(f32), (0, 2, 3, 4, 1))
    s_t = jnp.transpose(state_hgrn.astype(f32), (0, 2, 3, 4, 1))
    conv_halo = jnp.transpose(state_ssd_conv.astype(f32), (0, 2, 1, 3)).reshape(depth, (SSD_CONV - 1) * nb, SSD_CONV_DIM)
    f_t = jnp.transpose(state_ffn_conv.astype(f32), (0, 2, 1, 3)).reshape(depth, (FFN_CONV - 1) * nb, 2 * D_FF)

    p_states, s_sconv, s_ffn = [], [], []
    for l in range(depth):
        last = l == depth - 1

        ml, ssd, hg, gates, p_tail = _inproj(xp, n1, w_cat, l, conv=(cw, cb, tp))
        hm, p_c, p_n, p_m, hs, p_h, ho, p_s = _mixers_prompt(
            ml.reshape(bp, tp, ML_COLS), ssd.reshape(bp, tp, SSD_COLS), hg.reshape(bp, tp, HG_COLS),
            gates.reshape(bp, tp, GATE_W), gate_bias, alog_col, d_row, mlw, ssw, lbp, hgw, l)
        xp, p_fg, p_fv = _ffn(xp, hm.reshape(bp * tp, ML_W), hs.reshape(bp * tp, SSD_W), ho.reshape(bp * tp, HG_W),
                              wo_b, n2, wu_b, fcw, fcb, wd_b, zero_f, fin_w, layer=l,
                              groups=bp, tm=ROW_TILE, shift=1, jb=D_FF // FF_BLOCK, final=last)
        p_states.append((p_c, p_n, p_m[:, 0:ML_HEADS, 0],
                         p_h.reshape(bp, SSD_HEADS, HEAD_W, SSD_DSTATE),
                         p_tail[:, HALO - (SSD_CONV - 1):, :], p_s,
                         jnp.concatenate([p_fg, p_fv], axis=-1)))

        ml, ssd, hg, gates = _inproj(xs, n1, w_cat, l)
        hm, c_t, n_t, m_t = _mlstm_sample(ml, gates, gate_bias, mlw, c_t, n_t, m_t, l)
        xc = _ssd_conv_sample(ssd, conv_halo, cw, cb, l)
        hs, h_t = _ssd_sample(ssd, xc, gates, gate_bias, head_params, ssw, h_t, l)
        ho, s_t = _hgrn_sample(hg, lbp, hgw, s_t, l)
        xs, s_fg, s_fv = _ffn(xs, hm, hs, ho, wo_b, n2, wu_b, fcw, fcb, wd_b, f_t, fin_w, layer=l,
                              groups=1, tm=ts * nb, shift=nb, jb=1, final=last, f0_base=l)
        s_ffn.append(jnp.concatenate([s_fg, s_fv], axis=-1).reshape(FFN_CONV - 1, nb, 2 * D_FF))
        raw_xbc = ssd[:, SSD_W:].reshape(ts, nb, SSD_CONV_DIM)
        s_sconv.append(jnp.transpose(raw_xbc[ts - (SSD_CONV - 1):], (1, 0, 2)))

    y_prompt = xp.reshape(bp, tp, D_MODEL)
    y_sample = jnp.transpose(xs.reshape(ts, nb, D_MODEL), (1, 0, 2))
    p_out = [jnp.stack([st[i] for st in p_states]) for i in range(7)]
    s_out = [jnp.transpose(c_t, (0, 4, 1, 2, 3)), jnp.transpose(n_t, (0, 3, 1, 2)), jnp.transpose(m_t, (0, 2, 1)),
             jnp.transpose(h_t, (0, 4, 1, 2, 3)), jnp.stack(s_sconv),
             jnp.transpose(s_t, (0, 4, 1, 2, 3)), jnp.transpose(jnp.stack(s_ffn), (0, 2, 1, 3))]
    return (y_prompt, y_sample, *p_out, *s_out)
```

```python
import functools

import jax
import jax.numpy as jnp
from jax import lax
from jax.experimental import pallas as pl
from jax.experimental.pallas import tpu as pltpu

f32 = jnp.float32
bf16 = jnp.bfloat16

D_MODEL = 1024
DEPTH = 2
ML_HEADS = 6
HEAD_W = 64
ML_W = 384
SSD_HEADS = 6
SSD_W = 384
SSD_DSTATE = 128
SSD_CONV = 4
SSD_CONV_DIM = 896
HG_HEADS = 4
HG_W = 256
D_FF = 2816
FFN_CONV = 3
EPS = 1e-6
PROMPT_CHUNK = 128
HG_CHUNK = 64
LANES = 128
FF_BLOCK = 256
ROW_TILE = 512
SEQS_PER_STEP = 4
GATE_W = 128
GATE_ROWS = 16
W_IN_COLS = 4 * ML_W + (SSD_W + SSD_CONV_DIM) + 4 * HG_W + GATE_W
VMEM_LIMIT_BYTES = 56 * 1024 * 1024

NEG_INF = float("-inf")
LOG2_E = 1.4426950408889634


def _cparams(*sem):
    return pltpu.CompilerParams(dimension_semantics=sem, vmem_limit_bytes=VMEM_LIMIT_BYTES)


def _resident(shape):
    nd = len(shape)
    return pl.BlockSpec(shape, lambda *_: (0,) * nd, pipeline_mode=pl.Buffered(1))


def _layer_resident(shape, layer):
    nd = len(shape)
    return pl.BlockSpec((None,) + tuple(shape), lambda *_: (layer,) + (0,) * nd, pipeline_mode=pl.Buffered(1))


def _dot(a, b):
    return jnp.dot(a, b, preferred_element_type=f32)


def _dot_nt(a, b):
    return lax.dot_general(a, b, (((1,), (1,)), ((), ())), preferred_element_type=f32)


def _split_bf16(x, parts):
    out = []
    r = x
    for _ in range(parts):
        h = r.astype(bf16)
        out.append(h)
        r = r - h.astype(f32)
    return out


def _cumsum_rows(x, tri_b):
    acc = None
    for part in _split_bf16(x, 3):
        d = _dot(tri_b, part)
        acc = d if acc is None else acc + d
    return acc


def _group_mean(x, bd_b):
    return _dot(x.astype(bf16), bd_b) * (1.0 / HEAD_W)


def _iota2(shape, axis):
    return lax.broadcasted_iota(jnp.int32, shape, axis)


def _ones_where(mask):
    return jnp.where(mask, 1.0, 0.0).astype(bf16)


def _same_head(a, b):
    shift = HEAD_W.bit_length() - 1
    assert 1 << shift == HEAD_W
    return jnp.right_shift(a, shift) == jnp.right_shift(b, shift)


def _silu(x):
    return x * jax.nn.sigmoid(x)


ML_COLS = 4 * ML_W
SSD_COLS = SSD_W + SSD_CONV_DIM
HG_COLS = 4 * HG_W


def _inproj_body(x_ref, nw_ref, w_ref, ml_ref, ssd_ref, hg_ref, g_ref):
    x = x_ref[...]
    h = (x * lax.rsqrt(jnp.mean(x * x, axis=-1, keepdims=True) + EPS) * nw_ref[...]).astype(bf16)
    o0, o1, o2 = ML_COLS, ML_COLS + SSD_COLS, ML_COLS + SSD_COLS + HG_COLS
    ml_ref[...] = _dot(h, w_ref[:, 0:o0])
    ssd_ref[...] = _dot(h, w_ref[:, o0:o1])
    hg_ref[...] = _dot(h, w_ref[:, o1:o2])
    g_ref[...] = _dot(h, w_ref[:, o2:W_IN_COLS])


def _inproj_conv_body(x_ref, nw_ref, w_ref, cw_ref, cb_ref, ml_ref, ssd_ref, hg_ref, g_ref, tail_ref, xbuf_s, *,
                      tiles_per_seq):
    i = pl.program_id(0)
    tm = x_ref.shape[0]

    @pl.when(lax.rem(i, tiles_per_seq) == 0)
    def _():
        xbuf_s[0:HALO, :] = jnp.zeros((HALO, SSD_CONV_DIM), f32)

    x = x_ref[...]
    h = (x * lax.rsqrt(jnp.mean(x * x, axis=-1, keepdims=True) + EPS) * nw_ref[...]).astype(bf16)
    o0, o1, o2 = ML_COLS, ML_COLS + SSD_COLS, ML_COLS + SSD_COLS + HG_COLS
    zx = _dot(h, w_ref[:, o0:o1])
    ssd_ref[:, 0:SSD_W] = zx[:, 0:SSD_W]
    xbuf_s[HALO:HALO + tm, :] = zx[:, SSD_W:SSD_COLS]
    conv = cb_ref[...]
    for j in range(SSD_CONV):
        o = HALO - (SSD_CONV - 1) + j
        conv = conv + cw_ref[j:j + 1, :] * xbuf_s[o:o + tm, :]
    ssd_ref[:, SSD_W:SSD_COLS] = _silu(conv)
    tail = xbuf_s[tm:tm + HALO, :]
    tail_ref[0] = tail
    xbuf_s[0:HALO, :] = tail
    ml_ref[...] = _dot(h, w_ref[:, 0:o0])
    hg_ref[...] = _dot(h, w_ref[:, o1:o2])
    g_ref[...] = _dot(h, w_ref[:, o2:W_IN_COLS])


def _inproj(x, norm_w, w_cat, layer, conv=None):
    n = x.shape[0]
    tm = ROW_TILE
    row = lambda w: pl.BlockSpec((tm, w), lambda i: (i, 0))
    in_specs = [row(D_MODEL), _layer_resident((1, D_MODEL), layer), _layer_resident((D_MODEL, W_IN_COLS), layer)]
    out_specs = [row(ML_COLS), row(SSD_COLS), row(HG_COLS), row(GATE_W)]
    out_shape = [jax.ShapeDtypeStruct((n, w), f32) for w in (ML_COLS, SSD_COLS, HG_COLS, GATE_W)]
    if conv is None:
        return pl.pallas_call(
            _inproj_body, grid=(n // tm,), in_specs=in_specs, out_specs=out_specs, out_shape=out_shape,
            compiler_params=_cparams("arbitrary"), name="inproj",
        )(x, norm_w, w_cat)
    conv_w, conv_b, rows_per_seq = conv
    tps = rows_per_seq // tm
    assert rows_per_seq % tm == 0
    return pl.pallas_call(
        functools.partial(_inproj_conv_body, tiles_per_seq=tps),
        grid=(n // tm,),
        in_specs=in_specs + [_layer_resident((SSD_CONV, SSD_CONV_DIM), layer), _layer_resident((1, SSD_CONV_DIM), layer)],
        out_specs=out_specs + [pl.BlockSpec((1, HALO, SSD_CONV_DIM), lambda i: (i // tps, 0, 0))],
        out_shape=out_shape + [jax.ShapeDtypeStruct((n // rows_per_seq, HALO, SSD_CONV_DIM), f32)],
        scratch_shapes=[pltpu.VMEM((HALO + tm, SSD_CONV_DIM), f32)],
        compiler_params=_cparams("arbitrary"),
        name="inproj_conv",
    )(x, norm_w, w_cat, conv_w, conv_b)


W_IN_GATE_IF = ML_COLS
W_IN_SSD = W_IN_GATE_IF + 2 * ML_HEADS
W_IN_DT = W_IN_SSD + SSD_COLS
W_IN_HG = W_IN_DT + SSD_HEADS
W_IN_DIM = W_IN_HG + HG_COLS
W_PREP_ROWS = 256


def _prep_w_in_body(w_ref, out_ref):
    lane = _iota2((W_PREP_ROWS, LANES), 1)
    tile = lambda k: w_ref[:, k * LANES:(k + 1) * LANES]
    out_ref[:, 0:ML_COLS] = w_ref[:, 0:ML_COLS].astype(bf16)

    def shifted(dst0, src0, ntiles):
        t0, sh = src0 // LANES, src0 % LANES
        rolled = [pltpu.roll(tile(t0 + k), LANES - sh, axis=1) for k in range(ntiles + 1)]
        for k in range(ntiles):
            out_ref[:, dst0 + k * LANES:dst0 + (k + 1) * LANES] = jnp.where(
                lane < LANES - sh, rolled[k], rolled[k + 1]).astype(bf16)

    shifted(ML_COLS, W_IN_SSD, SSD_COLS // LANES)
    shifted(ML_COLS + SSD_COLS, W_IN_HG, HG_COLS // LANES)
    assert W_IN_GATE_IF % LANES == 0 and W_IN_DT % LANES == DT_LANE
    gates = jnp.where(lane < DT_LANE, tile(W_IN_GATE_IF // LANES),
                      jnp.where(lane < DT_LANE + SSD_HEADS, tile(W_IN_DT // LANES), 0.0))
    out_ref[:, ML_COLS + SSD_COLS + HG_COLS:W_IN_COLS] = gates.astype(bf16)


def _prep_w_in(w_in):
    depth, d, n = w_in.shape
    assert n == W_IN_DIM and d % W_PREP_ROWS == 0
    return pl.pallas_call(
        _prep_w_in_body,
        grid=(depth, d // W_PREP_ROWS),
        in_specs=[pl.BlockSpec((None, W_PREP_ROWS, W_IN_COLS), lambda l, i: (l, i, 0))],
        out_specs=pl.BlockSpec((None, W_PREP_ROWS, W_IN_COLS), lambda l, i: (l, i, 0)),
        out_shape=jax.ShapeDtypeStruct((depth, d, W_IN_COLS), bf16),
        compiler_params=_cparams("arbitrary", "arbitrary"),
        name="prep_w_in",
    )(w_in)


def _one_seq(ref, bi):
    return ref.at[pl.ds(bi, 1)]


def _interleave(chunks):
    live = list(chunks)
    while live:
        nxt = []
        for g in live:
            try:
                next(g)
                nxt.append(g)
            except StopIteration:
                pass
        live = nxt


def _mlstm_prompt_chunk(ml_ref, g_ref, gb_ref, nw_ref, out_ref, ct_s, nr_s, m_s):
    L = PROMPT_CHUNK
    row = _iota2((L, L), 0)
    col = _iota2((L, L), 1)
    causal = row <= col
    tri_b = _ones_where(causal)
    top = row < HEAD_W
    lo_half = col < HEAD_W
    bd_mask = _same_head(row, col)
    sub8 = _iota2((8, L), 0)
    lo8 = _iota2((8, L), 1) < HEAD_W
    n_mask = jnp.where(sub8 == 0, jnp.where(lo8, 1.0, 0.0), jnp.where(sub8 == 1, jnp.where(lo8, 0.0, 1.0), 0.0))

    g_t = (g_ref[0] + gb_ref[...]).T
    b_t = None
    parts = _split_bf16(jax.nn.log_sigmoid(g_t[0:GATE_ROWS, :]), 3)
    yield
    for part in parts:
        d = _dot(part, tri_b)
        b_t = d if b_t is None else b_t + d
        yield
    z_rows = b_t[ML_HEADS:ML_HEADS + 8, :] - g_t[0:8, :]
    z_nat = jnp.concatenate([z_rows, jnp.zeros((L - 8, L), f32)], axis=0).T
    m_old = m_s[...]
    m_next = m_old
    yield

    for p in range(ML_HEADS // 2):
        sl = slice(p * LANES, (p + 1) * LANES)
        q2 = ml_ref[0, :, sl]
        k2 = ml_ref[0, :, ML_W + p * LANES:ML_W + (p + 1) * LANES] * (HEAD_W ** -0.5)
        v2 = ml_ref[0, :, 2 * ML_W + p * LANES:2 * ML_W + (p + 1) * LANES]
        o2 = ml_ref[0, :, 3 * ML_W + p * LANES:3 * ML_W + (p + 1) * LANES]
        q_t = q2.T.astype(bf16)
        v_t = v2.T
        kb = k2.astype(bf16)
        k_sel = (jnp.where(lo_half, k2, 0.0).astype(bf16), jnp.where(lo_half, 0.0, k2).astype(bf16))
        sw, den, a_int, e_m, a_st, w_s = [], [], [], [], [], []
        yield
        for j in range(2):
            h = 2 * p + j
            b_row = b_t[ML_HEADS + h:ML_HEADS + h + 1, :]
            i_row = g_t[h:h + 1, :]
            m_prev = m_old[h:h + 1, :]
            dm = jnp.where(causal, b_row - z_nat[:, h:h + 1], NEG_INF)
            inter = b_row + m_prev
            raw = _dot(k_sel[j], q_t)
            yield
            m_t = jnp.maximum(inter, jnp.max(dm, axis=0, keepdims=True))
            yield
            s_w = raw * jnp.exp(dm - m_t)
            den.append(jnp.sum(s_w, axis=0, keepdims=True))
            sw.append(s_w.astype(bf16))
            yield
            a_int.append(jnp.exp(inter - m_t))
            e_m.append(jnp.exp(-m_t))
            m_new = jnp.broadcast_to(m_t[:, L - 1:L], (1, L))
            b_last = jnp.broadcast_to(b_row[:, L - 1:L], (1, L))
            a_st.append(jnp.exp(b_last + m_prev - m_new))
            w_s.append(jnp.exp(b_last - b_row + i_row - m_new))
            m_next = jnp.where(sub8 == h, m_new, m_next)
            yield
        ct = ct_s[p]
        nr = nr_s[p]
        qn = _dot(nr.astype(bf16), q_t)
        num0 = _dot(jnp.where(top, v_t, 0.0).astype(bf16), sw[0])
        yield
        num1 = _dot(jnp.where(top, 0.0, v_t).astype(bf16), sw[1])
        qc = _dot(ct.astype(bf16), q_t)
        yield
        num = num0 + num1 + jnp.where(top, a_int[0], a_int[1]) * qc
        den0 = den[0] + a_int[0] * qn[0:1, :]
        den1 = den[1] + a_int[1] * qn[1:2, :]
        hh = num / jnp.where(top, jnp.maximum(jnp.abs(den0), e_m[0]), jnp.maximum(jnp.abs(den1), e_m[1]))
        yield
        sq = hh * hh
        r0 = lax.rsqrt(jnp.sum(sq[0:HEAD_W, :], axis=0, keepdims=True) * (1.0 / HEAD_W) + EPS)
        r1 = lax.rsqrt(jnp.sum(sq[HEAD_W:LANES, :], axis=0, keepdims=True) * (1.0 / HEAD_W) + EPS)
        y = (hh * jnp.where(top, r0, r1)).T * nw_ref[:, sl] * jax.nn.sigmoid(o2)
        out_ref[0, :, sl] = y.astype(bf16)
        yield
        vw = (v_t * jnp.where(top, w_s[0], w_s[1])).astype(bf16)
        ct_s[p] = jnp.where(top, a_st[0], a_st[1]) * ct + jnp.where(bd_mask, _dot(vw, kb), 0.0)
        w8 = jnp.where(sub8 == 0, w_s[0], jnp.where(sub8 == 1, w_s[1], 0.0)).astype(bf16)
        nr_s[p] = jnp.where(sub8 == 0, a_st[0], a_st[1]) * nr + n_mask * _dot(w8, kb)
        yield
    m_s[...] = m_next


DT_LANE = 2 * ML_HEADS
SSD_GATE_ROW0 = DT_LANE // 8 * 8
SSD_HEAD_ROW = DT_LANE - SSD_GATE_ROW0
HALO = 8


def _ssd_prompt_chunk(ssd_ref, g_ref, gb_ref, al_ref, d_ref, nw_ref, out_ref, h_s):
    L = PROMPT_CHUNK
    xc = ssd_ref[0, :, SSD_W:SSD_COLS]
    nb = SSD_W
    b_b = [xc[:, nb + gi * LANES:nb + (gi + 1) * LANES].astype(bf16) for gi in range(2)]
    c_b = [xc[:, nb + (2 + gi) * LANES:nb + (3 + gi) * LANES].astype(bf16) for gi in range(2)]

    row = _iota2((L, L), 0)
    col = _iota2((L, L), 1)
    tril = col <= row
    lo_half = col < HEAD_W
    triu_b = _ones_where(row <= col)
    bd_b = _ones_where(_same_head(row, col))
    rowc = _iota2((L, 1), 0)

    g_t = (g_ref[0] + gb_ref[...]).T
    dt_t = jax.nn.softplus(g_t[SSD_GATE_ROW0:SSD_GATE_ROW0 + GATE_ROWS, :])
    yield
    cum_t = None
    for part in _split_bf16(dt_t * (-jnp.exp(al_ref[...])), 3):
        d = _dot(part, triu_b)
        cum_t = d if cum_t is None else cum_t + d
    yield
    nat = jnp.concatenate([cum_t, dt_t, cum_t * LOG2_E, jnp.zeros((L - 3 * GATE_ROWS, L), f32)], axis=0).T
    cb_g = [_dot_nt(c_b[gi], b_b[gi]) for gi in range(2)]
    yield

    for p in range(SSD_HEADS // 2):
        sl = slice(p * LANES, (p + 1) * LANES)
        x2 = xc[:, sl]
        z2 = ssd_ref[0, :, sl]
        hp = h_s[p]
        hb = hp.astype(bf16)
        grp = [(2 * p + j) // (SSD_HEADS // 2) for j in range(2)]
        ys = None
        ecum, w_s, a_last, ch = [], [], [], []
        for j in range(2):
            r = SSD_HEAD_ROW + 2 * p + j
            cum_col = nat[:, r:r + 1]
            cum_last = cum_col[L - 1:L, :]
            key_row = cum_t[r:r + 1, :] * LOG2_E - jnp.log2(dt_t[r:r + 1, :])
            lm_dt = jnp.exp2(jnp.where(tril, nat[:, 2 * GATE_ROWS + r:2 * GATE_ROWS + r + 1] - key_row, NEG_INF))
            m = (cb_g[grp[j]] * lm_dt).astype(bf16)
            yield
            xm = jnp.where(lo_half, x2, 0.0) if j == 0 else jnp.where(lo_half, 0.0, x2)
            d = _dot(m, xm.astype(bf16))
            ys = d if ys is None else ys + d
            ecum.append(jnp.exp(cum_col))
            w_s.append(jnp.exp(cum_last - cum_col) * nat[:, GATE_ROWS + r:GATE_ROWS + r + 1])
            a_last.append(jnp.exp(cum_last))
            if j == 0 or grp[1] != grp[0]:
                ch.append(_dot_nt(c_b[grp[j]], hb))
            else:
                ch.append(ch[0])
            yield
        ys = ys + jnp.where(lo_half, ecum[0] * ch[0], ecum[1] * ch[1])
        xw_t = (x2 * jnp.where(lo_half, w_s[0], w_s[1])).T.astype(bf16)
        yield
        if grp[0] == grp[1]:
            upd = _dot(xw_t, b_b[grp[0]])
        else:
            upd = jnp.where(row < HEAD_W, _dot(xw_t, b_b[grp[0]]), _dot(xw_t, b_b[grp[1]]))
        h_s[p] = jnp.where(rowc < HEAD_W, a_last[0], a_last[1]) * hp + upd
        yy = (ys + d_ref[:, sl] * x2) * _silu(z2)
        yield
        ms = _group_mean(yy * yy, bd_b)
        yield
        out_ref[0, :, sl] = (yy * lax.rsqrt(ms + EPS) * nw_ref[:, sl]).astype(bf16)


def _hg_lower_bound(lbp, layer):
    mx = jnp.max(lbp, axis=0, keepdims=True)
    e = jnp.exp(lbp - mx)
    den = jnp.sum(e, axis=0, keepdims=True)
    lb = jnp.zeros_like(den)
    for j in range(1, layer + 1):
        lb = lb + e[j:j + 1, :] / den
    return lb


def _hgrn_prompt_chunk(hg_ref, lbp_ref, nw_ref, out_ref, sbd_s, p_s, r_s, o_s, g_s, k_s, v_s, q_s, kt_s, a_s, *, layer):
    L = HG_CHUNK
    nblk = L // 8
    lb = _hg_lower_bound(lbp_ref[...], layer)
    ff = hg_ref[0, :, HG_W:2 * HG_W]
    sig = jax.nn.sigmoid(ff)
    log2_k = jnp.log2((1.0 - lb) * jax.nn.sigmoid(-ff))
    v_s[...] = hg_ref[0, :, 2 * HG_W:3 * HG_W]
    q_s[...] = hg_ref[0, :, 0:HG_W] * (HEAD_W ** -0.5)
    row = _iota2((L, L), 0)
    col = _iota2((L, L), 1)
    tri_b = _ones_where(col <= row)
    yield
    g2 = _cumsum_rows(jnp.log(lb + (1.0 - lb) * sig), tri_b) * LOG2_E
    g_s[...] = g2
    k_s[...] = g2 - log2_k
    yield

    r2 = _iota2((2 * LANES, 2 * LANES), 0)
    c2 = _iota2((2 * LANES, 2 * LANES), 1)
    bd4_b = _ones_where(_same_head(r2, c2))
    r1 = _iota2((LANES, LANES), 0)
    c1 = _iota2((LANES, LANES), 1)
    bd_mask = _same_head(r1, c1)
    bd_b = _ones_where(bd_mask)

    a_s[0:8, :] = jnp.zeros((8, HG_W), f32)
    for i in range(1, nblk):
        blk = slice(8 * i, 8 * i + 8)
        n = 8 * i
        r_i = g_s[n - 1:n, :]
        kt = jnp.exp2(r_i - k_s[0:n, :])
        first_head = jnp.bitwise_and(_iota2((n, HG_W), 1), LANES - 1) < HEAD_W
        kt_s[i - 1, 0:n, :] = jnp.where(first_head, kt, 0.0)
        kt_s[i - 1, n:L, :] = jnp.zeros((L - n, HG_W), f32)
        kt_s[i - 1, L:L + n, :] = jnp.where(first_head, 0.0, kt)
        kt_s[i - 1, L + n:2 * L, :] = jnp.zeros((L - n, HG_W), f32)
        qt = q_s[blk, :] * jnp.exp2(g_s[blk, :] - r_i)
        qt = jnp.concatenate([qt, jnp.zeros_like(qt)], axis=0).astype(bf16)
        for p in range(HG_HEADS // 2):
            sl = slice(p * LANES, (p + 1) * LANES)
            a_s[blk, sl] = _dot_nt(qt[:, sl], kt_s[i - 1, :, sl].astype(bf16))[0:8, :]
        yield

    sub = _iota2((8, HG_W), 0)
    for j in range(nblk):
        blk = slice(8 * j, 8 * j + 8)
        g_blk, gk_blk, q_blk = g_s[blk, :], k_s[blk, :], q_s[blk, :]
        for r in range(8):
            s = 8 * j + r
            p_s[8 * s:8 * s + 8, :] = q_blk * jnp.exp2(jnp.where(sub >= r, g_blk - gk_blk[r:r + 1, :], NEG_INF))
        yield
    r_s[...] = _dot(p_s[...].astype(bf16), bd4_b)
    yield

    gg = g_s[...]
    vv = v_s[...]
    qd = q_s[...] * jnp.exp2(gg)
    lo_half = _iota2((L, LANES), 1) < HEAD_W
    for p in range(HG_HEADS // 2):
        sl = slice(p * LANES, (p + 1) * LANES)
        v_cat = jnp.concatenate([jnp.where(lo_half, vv[:, sl], 0.0), jnp.where(lo_half, 0.0, vv[:, sl])], axis=0)
        o_s[:, sl] = (_dot(qd[:, sl].astype(bf16), sbd_s[p].astype(bf16))
                      + _dot(a_s[:, sl].astype(bf16), v_cat.astype(bf16)))
        yield
    for j in range(nblk):
        blk = slice(8 * j, 8 * j + 8)
        acc = o_s[blk, :]
        v_blk = v_s[blk, :]
        for r in range(8):
            s = 8 * j + r
            acc = acc + r_s[8 * s:8 * s + 8, :] * v_blk[r:r + 1, :]
        o_s[blk, :] = acc
        yield

    g_last = gg[L - 1:L, :]
    kd = jnp.exp2(g_last - k_s[...])
    zpad = jnp.zeros((LANES - L, LANES), f32)
    og = o_s[...]
    gate = hg_ref[0, :, 3 * HG_W:4 * HG_W]
    for p in range(HG_HEADS // 2):
        sl = slice(p * LANES, (p + 1) * LANES)
        kd_t = jnp.concatenate([kd[:, sl], zpad], axis=0).T.astype(bf16)
        v_pad = jnp.concatenate([vv[:, sl], zpad], axis=0).astype(bf16)
        g_t = jnp.concatenate([gg[:, sl], zpad], axis=0).T
        dec = jnp.exp2(g_t[:, L - 1:L])
        sbd_s[p] = dec * sbd_s[p] + jnp.where(bd_mask, _dot(kd_t, v_pad), 0.0)
        yield
        oo = og[:, sl]
        ms = _group_mean(oo * oo, bd_b)
        out_ref[0, :, sl] = (oo * lax.rsqrt(ms + EPS) * nw_ref[:, sl] * _silu(gate[:, sl])).astype(bf16)


def _mixers_prompt_body(ml_ref, ssd_ref, hg_ref, g_ref, gb_ref, al_ref, d_ref, mlnw_ref, ssnw_ref,
                        lbp_ref, hgnw_ref,
                        hm_ref, c1_ref, n1_ref, m1_ref, hs_ref, h1_ref, ho_ref, s1_ref,
                        ct_s, nr_s, m_s, h_s, sbd_s, p_s, r_s, o_s, gg_s, k_s, v_s, q_s, kt_s, a_s, *, layer):
    c = pl.program_id(1)
    seqs = range(ml_ref.shape[0])

    @pl.when(c == 0)
    def _():
        ct_s[...] = jnp.zeros(ct_s.shape, f32)
        nr_s[...] = jnp.zeros(nr_s.shape, f32)
        m_s[...] = jnp.zeros(m_s.shape, f32)
        h_s[...] = jnp.zeros(h_s.shape, f32)
        sbd_s[...] = jnp.zeros(sbd_s.shape, f32)

    def hgrn_chunks(bi):
        for half in range(PROMPT_CHUNK // HG_CHUNK):
            rows = pl.ds(half * HG_CHUNK, HG_CHUNK)
            yield from _hgrn_prompt_chunk(hg_ref.at[pl.ds(bi, 1), rows], lbp_ref, hgnw_ref, ho_ref.at[pl.ds(bi, 1), rows],
                                          sbd_s.at[bi], p_s.at[bi], r_s.at[bi], o_s.at[bi], gg_s.at[bi], k_s.at[bi],
                                          v_s.at[bi], q_s.at[bi], kt_s.at[bi], a_s.at[bi], layer=layer)

    chunks = []
    for bi in seqs:
        chunks.append(_ssd_prompt_chunk(_one_seq(ssd_ref, bi), _one_seq(g_ref, bi), gb_ref, al_ref, d_ref,
                                        ssnw_ref, _one_seq(hs_ref, bi), h_s.at[bi]))
        chunks.append(hgrn_chunks(bi))
        chunks.append(_mlstm_prompt_chunk(_one_seq(ml_ref, bi), _one_seq(g_ref, bi), gb_ref, mlnw_ref,
                                          _one_seq(hm_ref, bi), ct_s.at[bi], nr_s.at[bi], m_s.at[bi]))
    _interleave(chunks)

    @pl.when(c == pl.num_programs(1) - 1)
    def _():
        h1_ref[...] = h_s[...]
        for bi in seqs:
            for p in range(ML_HEADS // 2):
                c_pair = ct_s[bi, p].T
                c1_ref[bi, 2 * p] = c_pair[0:HEAD_W, 0:HEAD_W]
                c1_ref[bi, 2 * p + 1] = c_pair[HEAD_W:LANES, HEAD_W:LANES]
                n1_ref[bi, 2 * p:2 * p + 1, :] = nr_s[bi, p, 0:1, 0:HEAD_W]
                n1_ref[bi, 2 * p + 1:2 * p + 2, :] = nr_s[bi, p, 1:2, HEAD_W:LANES]
            m1_ref[bi] = m_s[bi]
            for p in range(HG_HEADS // 2):
                sbd = sbd_s[bi, p]
                s1_ref[bi, 2 * p] = sbd[0:HEAD_W, 0:HEAD_W]
                s1_ref[bi, 2 * p + 1] = sbd[HEAD_W:LANES, HEAD_W:LANES]


def _mixers_prompt(ml, ssd, hg, gates, gate_bias, alog_col, d_row, ml_nw, ssd_nw, lbp, hg_nw, layer):
    b, t, _ = ml.shape
    L = PROMPT_CHUNK
    ns = SEQS_PER_STEP
    assert b % ns == 0 and t % L == 0
    ml_pairs, ssd_pairs, hg_pairs = ML_HEADS // 2, SSD_HEADS // 2, HG_HEADS // 2
    tok = lambda w: pl.BlockSpec((ns, L, w), lambda i, c: (i, c, 0))
    per_seq = lambda *shape: pl.BlockSpec((ns,) + shape, lambda i, c: (i,) + (0,) * len(shape))
    hg_rows = pltpu.VMEM((ns, HG_CHUNK, HG_W), f32)
    hg_pairs_buf = pltpu.VMEM((ns, 8 * HG_CHUNK, HG_W), f32)
    return pl.pallas_call(
        functools.partial(_mixers_prompt_body, layer=layer),
        grid=(b // ns, t // L),
        in_specs=[tok(ML_COLS), tok(SSD_COLS), tok(HG_COLS), tok(GATE_W),
                  _layer_resident((1, GATE_W), layer), _layer_resident((GATE_ROWS, 1), layer),
                  _layer_resident((1, SSD_W), layer), _layer_resident((1, ML_W), layer),
                  _layer_resident((1, SSD_W), layer), _resident((DEPTH, HG_W)), _layer_resident((1, HG_W), layer)],
        out_specs=[tok(ML_W), per_seq(ML_HEADS, HEAD_W, HEAD_W), per_seq(ML_HEADS, HEAD_W), per_seq(8, LANES),
                   tok(SSD_W), per_seq(ssd_pairs, LANES, SSD_DSTATE),
                   tok(HG_W), per_seq(HG_HEADS, HEAD_W, HEAD_W)],
        out_shape=[jax.ShapeDtypeStruct((b, t, ML_W), bf16),
                   jax.ShapeDtypeStruct((b, ML_HEADS, HEAD_W, HEAD_W), f32),
                   jax.ShapeDtypeStruct((b, ML_HEADS, HEAD_W), f32),
                   jax.ShapeDtypeStruct((b, 8, LANES), f32),
                   jax.ShapeDtypeStruct((b, t, SSD_W), bf16),
                   jax.ShapeDtypeStruct((b, ssd_pairs, LANES, SSD_DSTATE), f32),
                   jax.ShapeDtypeStruct((b, t, HG_W), bf16),
                   jax.ShapeDtypeStruct((b, HG_HEADS, HEAD_W, HEAD_W), f32)],
        scratch_shapes=[pltpu.VMEM((ns, ml_pairs, LANES, LANES), f32), pltpu.VMEM((ns, ml_pairs, 8, LANES), f32),
                        pltpu.VMEM((ns, 8, LANES), f32),
                        pltpu.VMEM((ns, ssd_pairs, LANES, SSD_DSTATE), f32),
                        pltpu.VMEM((ns, hg_pairs, LANES, LANES), f32), hg_pairs_buf, hg_pairs_buf,
                        hg_rows, hg_rows, hg_rows, hg_rows, hg_rows,
                        pltpu.VMEM((ns, HG_CHUNK // 8 - 1, 2 * HG_CHUNK, HG_W), f32), hg_rows],
        compiler_params=_cparams("arbitrary", "arbitrary"),
        name="mixers_prompt",
    )(ml, ssd, hg, gates, gate_bias, alog_col, d_row, ml_nw, ssd_nw, lbp, hg_nw)


NB = 128
TS = 8


def _sample_call(body, name, grid, in_specs, operands, out_specs, out_shapes, scratch, n_state, n_fresh=0):
    first = len(operands) - n_state
    return pl.pallas_call(
        body, grid=grid, in_specs=in_specs, out_specs=out_specs, out_shape=out_shapes,
        scratch_shapes=scratch, input_output_aliases={first + i: 1 + i for i in range(n_fresh, n_state)},
        compiler_params=_cparams(*(["arbitrary"] * len(grid))), name=name,
    )(*operands)


def _mlstm_sample_body(q_ref, k_ref, v_ref, o_ref, g_ref, gb_ref, nw_ref, c0_ref, n0_ref, m0_ref,
                       out_ref, c1_ref, n1_ref, m1_ref, qt_s, kt_s, vt_s, gt_s, ht_s, *, layer, all_layers):
    p = pl.program_id(0)
    if all_layers:
        for l in range(DEPTH):
            if l != layer:
                c1_ref[l] = c0_ref[l]
        c0_ref, c1_ref = c0_ref.at[layer], c1_ref.at[layer]
    n_cur = [n0_ref[hl] for hl in range(2)]
    m_cur = [m0_ref[pl.ds(2 * p + hl, 1), :] for hl in range(2)]

    for t in range(TS):
        c_src = c0_ref if t == 0 else c1_ref
        rows = slice(t * NB, (t + 1) * NB)
        qt_s[...] = q_ref[rows, :].T
        kt_s[...] = k_ref[rows, :].T * (HEAD_W ** -0.5)
        vt_s[...] = v_ref[rows, :].T
        gt_s[...] = (g_ref[rows, :] + gb_ref[...]).T
        for hl in range(2):
            h = 2 * p + hl
            hs = slice(hl * HEAD_W, (hl + 1) * HEAD_W)
            i_t = gt_s[pl.ds(h, 1), :]
            lf = jax.nn.log_sigmoid(gt_s[pl.ds(ML_HEADS + h, 1), :])
            m_new = jnp.maximum(lf + m_cur[hl], i_t)
            a = jnp.exp(lf + m_cur[hl] - m_new)
            w = jnp.exp(i_t - m_new)
            n_new = a * n_cur[hl] + w * kt_s[hs, :]
            v_h = vt_s[hs, :]

            def body(d, num, hl=hl, a=a, w=w, v_h=v_h, c_src=c_src):
                kd = kt_s[pl.ds(hl * HEAD_W + d, 1), :] * w
                qd = qt_s[pl.ds(hl * HEAD_W + d, 1), :]
                cd = a * c_src[hl, d] + kd * v_h
                c1_ref[hl, d] = cd
                return num + qd * cd

            num = lax.fori_loop(0, HEAD_W, body, jnp.zeros((HEAD_W, NB), f32), unroll=4)
            den = jnp.sum(qt_s[hs, :] * n_new, axis=0, keepdims=True)
            hh = num / jnp.maximum(jnp.abs(den), jnp.exp(-m_new))
            ms = jnp.mean(hh * hh, axis=0, keepdims=True)
            ht_s[hs, :] = hh * lax.rsqrt(ms + EPS)
            n_cur[hl] = n_new
            m_cur[hl] = m_new
        out_ref[rows, :] = (ht_s[...].T * nw_ref[...] * jax.nn.sigmoid(o_ref[rows, :])).astype(bf16)

    for hl in range(2):
        n1_ref[hl] = n_cur[hl]
        m1_ref[pl.ds(2 * p + hl, 1), :] = m_cur[hl]


def _mlstm_sample(ml, gates, gate_bias, norm_w, c0t, n0t, m0t, layer):
    n = ml.shape[0]
    npair = ML_HEADS // 2
    blk = lambda off: pl.BlockSpec((n, LANES), lambda p, off=off: (0, off + p))
    all_layers = layer == 0
    if all_layers:
        c_spec = pl.BlockSpec((DEPTH, 2, HEAD_W, HEAD_W, NB), lambda p: (0, p, 0, 0, 0))
    else:
        c_spec = pl.BlockSpec((None, 2, HEAD_W, HEAD_W, NB), lambda p: (layer, p, 0, 0, 0))
    n_spec = pl.BlockSpec((None, 2, HEAD_W, NB), lambda p: (layer, p, 0, 0))
    m_spec = pl.BlockSpec((None, ML_HEADS, NB), lambda p: (layer, 0, 0))
    return _sample_call(
        functools.partial(_mlstm_sample_body, layer=layer, all_layers=all_layers), "mlstm_sample", (npair,),
        [blk(0), blk(npair), blk(2 * npair), blk(3 * npair), _resident((n, GATE_W)),
         _layer_resident((1, GATE_W), layer), pl.BlockSpec((None, 1, LANES), lambda p: (layer, 0, p)),
         c_spec, n_spec, m_spec],
        [ml, ml, ml, ml, gates, gate_bias, norm_w, c0t, n0t, m0t],
        [pl.BlockSpec((n, LANES), lambda p: (0, p)), c_spec, n_spec, m_spec],
        [jax.ShapeDtypeStruct((n, ML_W), bf16), jax.ShapeDtypeStruct(c0t.shape, f32),
         jax.ShapeDtypeStruct(n0t.shape, f32), jax.ShapeDtypeStruct(m0t.shape, f32)],
        [pltpu.VMEM((LANES, NB), f32)] * 5, 3, n_fresh=1 if all_layers else 0)


def _ssd_conv_sample_body(ssd_ref, halo_ref, cw_ref, cb_ref, out_ref, xp_s):
    nh = (SSD_CONV - 1) * NB
    n = TS * NB
    xp_s[0:nh, :] = halo_ref[...]
    xp_s[nh:nh + n, :] = ssd_ref[:, SSD_W:SSD_COLS]
    conv = cb_ref[...]
    for j in range(SSD_CONV):
        conv = conv + cw_ref[j:j + 1, :] * xp_s[j * NB:j * NB + n, :]
    out_ref[...] = _silu(conv)


def _ssd_conv_sample(ssd, halo, conv_w, conv_b, layer):
    n = ssd.shape[0]
    nh = (SSD_CONV - 1) * NB
    return pl.pallas_call(
        _ssd_conv_sample_body,
        grid=(1,),
        in_specs=[_resident((n, SSD_COLS)), _layer_resident((nh, SSD_CONV_DIM), layer),
                  _layer_resident((SSD_CONV, SSD_CONV_DIM), layer), _layer_resident((1, SSD_CONV_DIM), layer)],
        out_specs=pl.BlockSpec((n, SSD_CONV_DIM), lambda i: (0, 0)),
        out_shape=jax.ShapeDtypeStruct((n, SSD_CONV_DIM), f32),
        scratch_shapes=[pltpu.VMEM((nh + n, SSD_CONV_DIM), f32)],
        compiler_params=_cparams("arbitrary"),
        name="ssd_conv_sample",
    )(ssd, halo, conv_w, conv_b)


def _ssd_sample_body(z_ref, x_ref, b0_ref, b1_ref, c0_ref, c1_ref, g_ref, gb_ref, hp_ref, nw_ref, h0_ref,
                     out_ref, h1_ref, xt_s, bt_s, ct_s, gt_s, dd_s, yt_s):
    p = pl.program_id(0)
    b_refs = (b0_ref, b1_ref)
    c_refs = (c0_ref, c1_ref)

    for t in range(TS):
        rows = slice(t * NB, (t + 1) * NB)
        xt_s[t] = x_ref[rows, :].T
        gt_s[...] = (g_ref[rows, :] + gb_ref[...]).T
        for hl in range(2):
            bt_s[t, hl] = b_refs[hl][rows, :].T
            ct_s[t, hl] = c_refs[hl][rows, :].T
            dt = jax.nn.softplus(gt_s[pl.ds(DT_LANE + 2 * p + hl, 1), :])
            dd_s[t, 2 * hl:2 * hl + 1, :] = dt
            dd_s[t, 2 * hl + 1:2 * hl + 2, :] = jnp.exp(dt * (-jnp.exp(hp_ref[0, hl:hl + 1, :])))

    for hl in range(2):
        def body(pp, carry, hl=hl):
            row = pl.ds(hl * HEAD_W + pp, 1)
            h = h0_ref[hl, pp]
            for t in range(TS):
                xrow = xt_s[t, row, :] * dd_s[t, 2 * hl:2 * hl + 1, :]
                h = dd_s[t, 2 * hl + 1:2 * hl + 2, :] * h + xrow * bt_s[t, hl]
                yt_s[t, row, :] = jnp.sum(ct_s[t, hl] * h, axis=0, keepdims=True)
            h1_ref[hl, pp] = h
            return carry

        lax.fori_loop(0, HEAD_W, body, 0, unroll=4)

    for t in range(TS):
        rows = slice(t * NB, (t + 1) * NB)
        zt = z_ref[rows, :].T
        for hl in range(2):
            hs = slice(hl * HEAD_W, (hl + 1) * HEAD_W)
            yy = (yt_s[t, hs, :] + hp_ref[1, hl:hl + 1, :] * xt_s[t, hs, :]) * _silu(zt[hs, :])
            ms = jnp.mean(yy * yy, axis=0, keepdims=True)
            yt_s[t, hs, :] = yy * lax.rsqrt(ms + EPS)
        out_ref[rows, :] = (yt_s[t].T * nw_ref[...]).astype(bf16)


def _ssd_sample(ssd, xc, gates, gate_bias, head_params, norm_w, h0t, layer):
    n = ssd.shape[0]
    npair = SSD_HEADS // 2
    hpg = SSD_HEADS // 2
    nxb = SSD_W // LANES
    blk = lambda f: pl.BlockSpec((n, LANES), f)
    h_spec = pl.BlockSpec((None, 2, HEAD_W, SSD_DSTATE, NB), lambda p: (layer, p, 0, 0, 0))
    return _sample_call(
        _ssd_sample_body, "ssd_sample", (npair,),
        [blk(lambda p: (0, p)), blk(lambda p: (0, p)),
         blk(lambda p: (0, nxb + (2 * p) // hpg)), blk(lambda p: (0, nxb + (2 * p + 1) // hpg)),
         blk(lambda p: (0, nxb + 2 + (2 * p) // hpg)), blk(lambda p: (0, nxb + 2 + (2 * p + 1) // hpg)),
         _resident((n, GATE_W)), _layer_resident((1, GATE_W), layer),
         pl.BlockSpec((None, 2, None, 2, LANES), lambda p: (layer, 0, p, 0, 0)),
         pl.BlockSpec((None, 1, LANES), lambda p: (layer, 0, p)), h_spec],
        [ssd, xc, xc, xc, xc, xc, gates, gate_bias, head_params, norm_w, h0t],
        [pl.BlockSpec((n, LANES), lambda p: (0, p)), h_spec],
        [jax.ShapeDtypeStruct((n, SSD_W), bf16), jax.ShapeDtypeStruct(h0t.shape, f32)],
        [pltpu.VMEM((TS, LANES, NB), f32), pltpu.VMEM((TS, 2, SSD_DSTATE, NB), f32),
         pltpu.VMEM((TS, 2, SSD_DSTATE, NB), f32), pltpu.VMEM((LANES, NB), f32), pltpu.VMEM((TS, 8, NB), f32),
         pltpu.VMEM((TS, LANES, NB), f32)], 1)


def _hgrn_sample_body(q_ref, f_ref, i_ref, gate_ref, lbp_ref, nw_ref, s0_ref, out_ref, s1_ref,
                      qt_s, kt_s, ft_s, vt_s, ot_s, *, layer, all_layers):
    lb = _hg_lower_bound(lbp_ref[...], layer)
    if all_layers:
        for l in range(DEPTH):
            if l != layer:
                s1_ref[l] = s0_ref[l]
        s0_ref, s1_ref = s0_ref.at[layer], s1_ref.at[layer]

    for t in range(TS):
        s_src = s0_ref if t == 0 else s1_ref
        rows = slice(t * NB, (t + 1) * NB)
        ff = f_ref[rows, :]
        qt_s[...] = (q_ref[rows, :] * (HEAD_W ** -0.5)).T
        kt_s[...] = ((1.0 - lb) * jax.nn.sigmoid(-ff)).T
        ft_s[...] = (lb + (1.0 - lb) * jax.nn.sigmoid(ff)).T
        vt_s[...] = i_ref[rows, :].T
        for hl in range(2):
            hs = slice(hl * HEAD_W, (hl + 1) * HEAD_W)
            v_h = vt_s[hs, :]

            def body(k, o, hl=hl, v_h=v_h, s_src=s_src):
                r = pl.ds(hl * HEAD_W + k, 1)
                sk = ft_s[r, :] * s_src[hl, k] + kt_s[r, :] * v_h
                s1_ref[hl, k] = sk
                return o + qt_s[r, :] * sk

            o = lax.fori_loop(0, HEAD_W, body, jnp.zeros((HEAD_W, NB), f32), unroll=4)
            ms = jnp.mean(o * o, axis=0, keepdims=True)
            ot_s[hs, :] = o * lax.rsqrt(ms + EPS)
        out_ref[rows, :] = (ot_s[...].T * nw_ref[...] * _silu(gate_ref[rows, :])).astype(bf16)


def _hgrn_sample(hg, lbp, norm_w, s0t, layer):
    n = hg.shape[0]
    npair = HG_HEADS // 2
    blk = lambda off: pl.BlockSpec((n, LANES), lambda p, off=off: (0, off + p))
    all_layers = layer == 0
    if all_layers:
        s_spec = pl.BlockSpec((DEPTH, 2, HEAD_W, HEAD_W, NB), lambda p: (0, p, 0, 0, 0))
    else:
        s_spec = pl.BlockSpec((None, 2, HEAD_W, HEAD_W, NB), lambda p: (layer, p, 0, 0, 0))
    return _sample_call(
        functools.partial(_hgrn_sample_body, layer=layer, all_layers=all_layers), "hgrn_sample", (npair,),
        [blk(0), blk(npair), blk(2 * npair), blk(3 * npair),
         pl.BlockSpec((DEPTH, LANES), lambda p: (0, p)),
         pl.BlockSpec((None, 1, LANES), lambda p: (layer, 0, p)), s_spec],
        [hg, hg, hg, hg, lbp, norm_w, s0t],
        [pl.BlockSpec((n, LANES), lambda p: (0, p)), s_spec],
        [jax.ShapeDtypeStruct((n, HG_W), bf16), jax.ShapeDtypeStruct(s0t.shape, f32)],
        [pltpu.VMEM((LANES, NB), f32)] * 5, 1, n_fresh=1 if all_layers else 0)


def _ffn_body(x_ref, mml_ref, mssd_ref, mhg_ref, wout_ref, n2_ref, wug_ref, wuv_ref, cwg_ref, cwv_ref,
              cbg_ref, cbv_ref, wdn_ref, f0g_ref, f0v_ref, fn_ref, y_ref, f1g_ref, f1v_ref,
              x1_s, h2_s, acc_s, act_s, ubuf_s, *, tm, shift, jb, nff, final):
    t = pl.program_id(1)
    f = pl.program_id(2)
    halo = max(HALO, (FFN_CONV - 1) * shift)
    keep = (FFN_CONV - 1) * shift

    def when(cond, fn):
        if nff == 1:
            fn()
        else:
            pl.when(cond)(fn)

    @pl.when(t == 0)
    def _():
        f1g_ref[0] = f0g_ref[0]
        f1v_ref[0] = f0v_ref[0]

    def first():
        x = x_ref[...]
        mix = jnp.concatenate([mml_ref[...], mssd_ref[...], mhg_ref[...]], axis=1)
        x1 = x + _dot(mix, wout_ref[...])
        x1_s[...] = x1
        h2_s[...] = (x1 * lax.rsqrt(jnp.mean(x1 * x1, axis=-1, keepdims=True) + EPS) * n2_ref[...]).astype(bf16)

    when(f == 0, first)
    h2 = h2_s[...]

    def conv(wu_ref, cw_ref, cb_ref, f1_ref, cols, ubuf):
        u = _dot(h2, wu_ref[:, cols])
        ubuf[halo - keep:halo, :] = f1_ref[0, :, cols]
        ubuf[halo:halo + tm, :] = u
        yv = (cb_ref[:, cols] + cw_ref[2:3, cols] * u
              + cw_ref[1:2, cols] * ubuf[halo - shift:halo - shift + tm, :]
              + cw_ref[0:1, cols] * ubuf[halo - 2 * shift:halo - 2 * shift + tm, :])
        f1_ref[0, :, cols] = ubuf[halo + tm - keep:halo + tm, :]
        return yv

    nbuf = ubuf_s.shape[0]
    for j in range(jb):
        cols = slice(j * FF_BLOCK, (j + 1) * FF_BLOCK)
        gq = conv(wug_ref, cwg_ref, cbg_ref, f1g_ref, cols, ubuf_s.at[(2 * j) % nbuf])
        vq = conv(wuv_ref, cwv_ref, cbv_ref, f1v_ref, cols, ubuf_s.at[(2 * j + 1) % nbuf])
        act_s[:, cols] = (_silu(gq) * vq).astype(bf16)
    contrib = _dot(act_s[...], wdn_ref[...])

    def finish(down):
        x2 = x1_s[...] + down
        if final:
            x2 = x2 * lax.rsqrt(jnp.mean(x2 * x2, axis=-1, keepdims=True) + EPS) * fn_ref[...]
        y_ref[...] = x2

    if nff == 1:
        finish(contrib)
    else:
        @pl.when(f == 0)
        def _():
            acc_s[...] = contrib

        @pl.when(f != 0)
        def _():
            acc_s[...] = acc_s[...] + contrib

        pl.when(f == nff - 1)(lambda: finish(acc_s[...]))


def _ffn(x, mml, mssd, mhg, w_out, norm2, w_up, conv_w, conv_b, w_down, f0, final_w, *, layer, groups, tm, shift, jb,
         final, f0_base=0):
    n = x.shape[0]
    nt = n // (groups * tm)
    fw = jb * FF_BLOCK
    nff = D_FF // fw
    assert nt == 1 or nff == 1
    halo = max(HALO, (FFN_CONV - 1) * shift)
    keep = (FFN_CONV - 1) * shift
    rows = lambda w: pl.BlockSpec((tm, w), lambda g, t, f: (g * nt + t, 0))
    single = nff == 1
    wspec = lambda shape, imap: pl.BlockSpec(shape, imap, pipeline_mode=pl.Buffered(1)) if single else pl.BlockSpec(shape, imap)
    f1_spec = pl.BlockSpec((1, keep, fw), lambda g, t, f: (g, 0, f))
    in_specs = [rows(D_MODEL), rows(ML_W), rows(SSD_W), rows(HG_W),
                _layer_resident((D_MODEL, D_MODEL), layer), _layer_resident((1, D_MODEL), layer),
                wspec((None, D_MODEL, fw), lambda g, t, f: (layer, 0, f)),
                wspec((None, D_MODEL, fw), lambda g, t, f: (layer, 0, nff + f)),
                wspec((None, FFN_CONV, fw), lambda g, t, f: (layer, 0, f)),
                wspec((None, FFN_CONV, fw), lambda g, t, f: (layer, 0, nff + f)),
                wspec((None, 1, fw), lambda g, t, f: (layer, 0, f)),
                wspec((None, 1, fw), lambda g, t, f: (layer, 0, nff + f)),
                wspec((None, fw, D_MODEL), lambda g, t, f: (layer, f, 0)),
                pl.BlockSpec((1, keep, fw), lambda g, t, f: (f0_base + g, 0, f)),
                pl.BlockSpec((1, keep, fw), lambda g, t, f: (f0_base + g, 0, nff + f)),
                _resident((1, D_MODEL))]
    operands = [x, mml, mssd, mhg, w_out, norm2, w_up, w_up, conv_w, conv_w, conv_b, conv_b, w_down, f0, f0, final_w]
    acc_rows = tm if nff > 1 else 8
    return pl.pallas_call(
        functools.partial(_ffn_body, tm=tm, shift=shift, jb=jb, nff=nff, final=final),
        grid=(groups, nt, nff),
        in_specs=in_specs,
        out_specs=[rows(D_MODEL), f1_spec, f1_spec],
        out_shape=[jax.ShapeDtypeStruct((n, D_MODEL), f32),
                   jax.ShapeDtypeStruct((groups, keep, D_FF), f32),
                   jax.ShapeDtypeStruct((groups, keep, D_FF), f32)],
        scratch_shapes=[pltpu.VMEM((tm, D_MODEL), f32), pltpu.VMEM((tm, D_MODEL), bf16),
                        pltpu.VMEM((acc_rows, D_MODEL), f32), pltpu.VMEM((tm, fw), bf16),
                        pltpu.VMEM((min(4, 2 * jb), halo + tm, FF_BLOCK), f32)],
        compiler_params=_cparams("arbitrary", "arbitrary", "arbitrary"),
        name="ffn",
    )(*operands)


def _gate_rows(*parts):
    r = jnp.concatenate([p.astype(f32) for p in parts], axis=1)
    return jnp.pad(r, ((0, 0), (0, GATE_W - r.shape[1])))[:, None, :]


def kernel(x_prompt, x_sample, state_mlstm_C, state_mlstm_n, state_mlstm_m, state_ssd, state_ssd_conv,
           state_hgrn, state_ffn_conv, norm1_w, w_in, ml_ig_b, ml_fg_b, ml_norm_w, ssd_conv_w, ssd_conv_b,
           ssd_dt_bias, ssd_A_log, ssd_D, ssd_norm_w, hg_lower_bounds, hg_norm_w, w_out, norm2_w, w_up,
           ffn_conv_w, ffn_conv_b, w_down, final_norm_w):
    bp, tp, _ = x_prompt.shape
    nb, ts, _ = x_sample.shape
    assert nb == NB and ts == TS and tp % ROW_TILE == 0
    depth = w_in.shape[0]
    assert depth == DEPTH
    xp = x_prompt.reshape(bp * tp, D_MODEL)
    xs = jnp.transpose(x_sample, (1, 0, 2)).reshape(ts * nb, D_MODEL)
    lbp = hg_lower_bounds.astype(f32)
    fin_w = final_norm_w[None, :]
    zero_f = jnp.zeros((bp, FFN_CONV - 1, 2 * D_FF), f32)

    w_cat = _prep_w_in(w_in)
    wo_b, wu_b, wd_b = w_out.astype(bf16), w_up.astype(bf16), w_down.astype(bf16)
    n1, n2 = norm1_w[:, None, :], norm2_w[:, None, :]
    gate_bias = _gate_rows(ml_ig_b, ml_fg_b, ssd_dt_bias)
    alog_col = jnp.pad(ssd_A_log.astype(f32), ((0, 0), (SSD_HEAD_ROW, GATE_ROWS - SSD_HEAD_ROW - SSD_HEADS)))[:, :, None]
    d_row = jnp.repeat(ssd_D.astype(f32), HEAD_W, axis=1)[:, None, :]
    mlw, ssw, hgw = ml_norm_w[:, None, :], ssd_norm_w[:, None, :], hg_norm_w[:, None, :]
    cw, cb = ssd_conv_w, ssd_conv_b[:, None, :]
    fcw, fcb = ffn_conv_w, ffn_conv_b[:, None, :]
    head_params = jnp.broadcast_to(
        jnp.stack([ssd_A_log, ssd_D], axis=1).astype(f32).reshape(depth, 2, SSD_HEADS // 2, 2, 1),
        (depth, 2, SSD_HEADS // 2, 2, LANES))

    c_t = jnp.transpose(state_mlstm_C.astype(f32), (0, 2, 3, 4, 1))
    n_t = jnp.transpose(state_mlstm_n.astype(f32), (0, 2, 3, 1))
    m_t = jnp.transpose(state_mlstm_m.astype(f32), (0, 2, 1))
    h_t = jnp.transpose(state_ssd.astype(f32), (0, 2, 3, 4, 1))
    s_t = jnp.transpose(state_hgrn.astype(f32), (0, 2, 3, 4, 1))
    conv_halo = jnp.transpose(state_ssd_conv.astype(f32), (0, 2, 1, 3)).reshape(depth, (SSD_CONV - 1) * nb, SSD_CONV_DIM)
    f_t = jnp.transpose(state_ffn_conv.astype(f32), (0, 2, 1, 3)).reshape(depth, (FFN_CONV - 1) * nb, 2 * D_FF)

    p_states, s_sconv, s_ffn = [], [], []
    for l in range(depth):
        last = l == depth - 1

        ml, ssd, hg, gates, p_tail = _inproj(xp, n1, w_cat, l, conv=(cw, cb, tp))
        hm, p_c, p_n, p_m, hs, p_h, ho, p_s = _mixers_prompt(
            ml.reshape(bp, tp, ML_COLS), ssd.reshape(bp, tp, SSD_COLS), hg.reshape(bp, tp, HG_COLS),
            gates.reshape(bp, tp, GATE_W), gate_bias, alog_col, d_row, mlw, ssw, lbp, hgw, l)
        xp, p_fg, p_fv = _ffn(xp, hm.reshape(bp * tp, ML_W), hs.reshape(bp * tp, SSD_W), ho.reshape(bp * tp, HG_W),
                              wo_b, n2, wu_b, fcw, fcb, wd_b, zero_f, fin_w, layer=l,
                              groups=bp, tm=ROW_TILE, shift=1, jb=D_FF // FF_BLOCK, final=last)
        p_states.append((p_c, p_n, p_m[:, 0:ML_HEADS, 0],
                         p_h.reshape(bp, SSD_HEADS, HEAD_W, SSD_DSTATE),
                         p_tail[:, HALO - (SSD_CONV - 1):, :], p_s,
                         jnp.concatenate([p_fg, p_fv], axis=-1)))

        ml, ssd, hg, gates = _inproj(xs, n1, w_cat, l)
        hm, c_t, n_t, m_t = _mlstm_sample(ml, gates, gate_bias, mlw, c_t, n_t, m_t, l)
        xc = _ssd_conv_sample(ssd, conv_halo, cw, cb, l)
        hs, h_t = _ssd_sample(ssd, xc, gates, gate_bias, head_params, ssw, h_t, l)
        ho, s_t = _hgrn_sample(hg, lbp, hgw, s_t, l)
        xs, s_fg, s_fv = _ffn(xs, hm, hs, ho, wo_b, n2, wu_b, fcw, fcb, wd_b, f_t, fin_w, layer=l,
                              groups=1, tm=ts * nb, shift=nb, jb=1, final=last, f0_base=l)
        s_ffn.append(jnp.concatenate([s_fg, s_fv], axis=-1).reshape(FFN_CONV - 1, nb, 2 * D_FF))
        raw_xbc = ssd[:, SSD_W:].reshape(ts, nb, SSD_CONV_DIM)
        s_sconv.append(jnp.transpose(raw_xbc[ts - (SSD_CONV - 1):], (1, 0, 2)))

    y_prompt = xp.reshape(bp, tp, D_MODEL)
    y_sample = jnp.transpose(xs.reshape(ts, nb, D_MODEL), (1, 0, 2))
    p_out = [jnp.stack([st[i] for st in p_states]) for i in range(7)]
    s_out = [jnp.transpose(c_t, (0, 4, 1, 2, 3)), jnp.transpose(n_t, (0, 3, 1, 2)), jnp.transpose(m_t, (0, 2, 1)),
             jnp.transpose(h_t, (0, 4, 1, 2, 3)), jnp.stack(s_sconv),
             jnp.transpose(s_t, (0, 4, 1, 2, 3)), jnp.transpose(jnp.stack(s_ffn), (0, 2, 1, 3))]
    return (y_prompt, y_sample, *p_out, *s_out)
```

```python
import functools

import jax
import jax.numpy as jnp
from jax import lax
from jax.experimental import pallas as pl
from jax.experimental.pallas import tpu as pltpu

f32 = jnp.float32
bf16 = jnp.bfloat16

D_MODEL = 1024
DEPTH = 2
ML_HEADS = 6
HEAD_W = 64
ML_W = 384
SSD_HEADS = 6
SSD_W = 384
SSD_DSTATE = 128
SSD_CONV = 4
SSD_CONV_DIM = 896
HG_HEADS = 4
HG_W = 256
D_FF = 2816
FFN_CONV = 3
EPS = 1e-6
PROMPT_CHUNK = 128
HG_CHUNK = 64
LANES = 128
FF_BLOCK = 256
ROW_TILE = 512
SEQS_PER_STEP = 4
GATE_W = 128
GATE_ROWS = 16
W_IN_COLS = 4 * ML_W + (SSD_W + SSD_CONV_DIM) + 4 * HG_W + GATE_W
VMEM_LIMIT_BYTES = 56 * 1024 * 1024

NEG_INF = float("-inf")
LOG2_E = 1.4426950408889634


def _cparams(*sem):
    return pltpu.CompilerParams(dimension_semantics=sem, vmem_limit_bytes=VMEM_LIMIT_BYTES)


def _resident(shape):
    nd = len(shape)
    return pl.BlockSpec(shape, lambda *_: (0,) * nd, pipeline_mode=pl.Buffered(1))


def _layer_resident(shape, layer):
    nd = len(shape)
    return pl.BlockSpec((None,) + tuple(shape), lambda *_: (layer,) + (0,) * nd, pipeline_mode=pl.Buffered(1))


def _dot(a, b):
    return jnp.dot(a, b, preferred_element_type=f32)


def _dot_nt(a, b):
    return lax.dot_general(a, b, (((1,), (1,)), ((), ())), preferred_element_type=f32)


def _split_bf16(x, parts):
    out = []
    r = x
    for _ in range(parts):
        h = r.astype(bf16)
        out.append(h)
        r = r - h.astype(f32)
    return out


def _cumsum_rows(x, tri_b):
    acc = None
    for part in _split_bf16(x, 3):
        d = _dot(tri_b, part)
        acc = d if acc is None else acc + d
    return acc


def _group_mean(x, bd_b):
    return _dot(x.astype(bf16), bd_b) * (1.0 / HEAD_W)


def _iota2(shape, axis):
    return lax.broadcasted_iota(jnp.int32, shape, axis)


def _ones_where(mask):
    return jnp.where(mask, 1.0, 0.0).astype(bf16)


def _same_head(a, b):
    shift = HEAD_W.bit_length() - 1
    assert 1 << shift == HEAD_W
    return jnp.right_shift(a, shift) == jnp.right_shift(b, shift)


def _silu(x):
    return x * jax.nn.sigmoid(x)


ML_COLS = 4 * ML_W
SSD_COLS = SSD_W + SSD_CONV_DIM
HG_COLS = 4 * HG_W


def _inproj_body(x_ref, nw_ref, w_ref, ml_ref, ssd_ref, hg_ref, g_ref):
    x = x_ref[...]
    h = (x * lax.rsqrt(jnp.mean(x * x, axis=-1, keepdims=True) + EPS) * nw_ref[...]).astype(bf16)
    o0, o1, o2 = ML_COLS, ML_COLS + SSD_COLS, ML_COLS + SSD_COLS + HG_COLS
    ml_ref[...] = _dot(h, w_ref[:, 0:o0])
    ssd_ref[...] = _dot(h, w_ref[:, o0:o1])
    hg_ref[...] = _dot(h, w_ref[:, o1:o2])
    g_ref[...] = _dot(h, w_ref[:, o2:W_IN_COLS])


def _inproj_conv_body(x_ref, nw_ref, w_ref, cw_ref, cb_ref, ml_ref, ssd_ref, hg_ref, g_ref, tail_ref, xbuf_s, *,
                      tiles_per_seq):
    i = pl.program_id(0)
    tm = x_ref.shape[0]

    @pl.when(lax.rem(i, tiles_per_seq) == 0)
    def _():
        xbuf_s[0:HALO, :] = jnp.zeros((HALO, SSD_CONV_DIM), f32)

    x = x_ref[...]
    h = (x * lax.rsqrt(jnp.mean(x * x, axis=-1, keepdims=True) + EPS) * nw_ref[...]).astype(bf16)
    o0, o1, o2 = ML_COLS, ML_COLS + SSD_COLS, ML_COLS + SSD_COLS + HG_COLS
    zx = _dot(h, w_ref[:, o0:o1])
    ssd_ref[:, 0:SSD_W] = zx[:, 0:SSD_W]
    xbuf_s[HALO:HALO + tm, :] = zx[:, SSD_W:SSD_COLS]
    conv = cb_ref[...]
    for j in range(SSD_CONV):
        o = HALO - (SSD_CONV - 1) + j
        conv = conv + cw_ref[j:j + 1, :] * xbuf_s[o:o + tm, :]
    ssd_ref[:, SSD_W:SSD_COLS] = _silu(conv)
    tail = xbuf_s[tm:tm + HALO, :]
    tail_ref[0] = tail
    xbuf_s[0:HALO, :] = tail
    ml_ref[...] = _dot(h, w_ref[:, 0:o0])
    hg_ref[...] = _dot(h, w_ref[:, o1:o2])
    g_ref[...] = _dot(h, w_ref[:, o2:W_IN_COLS])


def _inproj(x, norm_w, w_cat, layer, conv=None):
    n = x.shape[0]
    tm = ROW_TILE
    row = lambda w: pl.BlockSpec((tm, w), lambda i: (i, 0))
    in_specs = [row(D_MODEL), _layer_resident((1, D_MODEL), layer), _layer_resident((D_MODEL, W_IN_COLS), layer)]
    out_specs = [row(ML_COLS), row(SSD_COLS), row(HG_COLS), row(GATE_W)]
    out_shape = [jax.ShapeDtypeStruct((n, w), f32) for w in (ML_COLS, SSD_COLS, HG_COLS, GATE_W)]
    if conv is None:
        return pl.pallas_call(
            _inproj_body, grid=(n // tm,), in_specs=in_specs, out_specs=out_specs, out_shape=out_shape,
            compiler_params=_cparams("arbitrary"), name="inproj",
        )(x, norm_w, w_cat)
    conv_w, conv_b, rows_per_seq = conv
    tps = rows_per_seq // tm
    assert rows_per_seq % tm == 0
    return pl.pallas_call(
        functools.partial(_inproj_conv_body, tiles_per_seq=tps),
        grid=(n // tm,),
        in_specs=in_specs + [_layer_resident((SSD_CONV, SSD_CONV_DIM), layer), _layer_resident((1, SSD_CONV_DIM), layer)],
        out_specs=out_specs + [pl.BlockSpec((1, HALO, SSD_CONV_DIM), lambda i: (i // tps, 0, 0))],
        out_shape=out_shape + [jax.ShapeDtypeStruct((n // rows_per_seq, HALO, SSD_CONV_DIM), f32)],
        scratch_shapes=[pltpu.VMEM((HALO + tm, SSD_CONV_DIM), f32)],
        compiler_params=_cparams("arbitrary"),
        name="inproj_conv",
    )(x, norm_w, w_cat, conv_w, conv_b)


W_IN_GATE_IF = ML_COLS
W_IN_SSD = W_IN_GATE_IF + 2 * ML_HEADS
W_IN_DT = W_IN_SSD + SSD_COLS
W_IN_HG = W_IN_DT + SSD_HEADS
W_IN_DIM = W_IN_HG + HG_COLS
W_PREP_ROWS = 256


def _prep_w_in_body(w_ref, out_ref):
    lane = _iota2((W_PREP_ROWS, LANES), 1)
    tile = lambda k: w_ref[:, k * LANES:(k + 1) * LANES]
    out_ref[:, 0:ML_COLS] = w_ref[:, 0:ML_COLS].astype(bf16)

    def shifted(dst0, src0, ntiles):
        t0, sh = src0 // LANES, src0 % LANES
        rolled = [pltpu.roll(tile(t0 + k), LANES - sh, axis=1) for k in range(ntiles + 1)]
        for k in range(ntiles):
            out_ref[:, dst0 + k * LANES:dst0 + (k + 1) * LANES] = jnp.where(
                lane < LANES - sh, rolled[k], rolled[k + 1]).astype(bf16)

    shifted(ML_COLS, W_IN_SSD, SSD_COLS // LANES)
    shifted(ML_COLS + SSD_COLS, W_IN_HG, HG_COLS // LANES)
    assert W_IN_GATE_IF % LANES == 0 and W_IN_DT % LANES == DT_LANE
    gates = jnp.where(lane < DT_LANE, tile(W_IN_GATE_IF // LANES),
                      jnp.where(lane < DT_LANE + SSD_HEADS, tile(W_IN_DT // LANES), 0.0))
    out_ref[:, ML_COLS + SSD_COLS + HG_COLS:W_IN_COLS] = gates.astype(bf16)


def _prep_w_in(w_in):
    depth, d, n = w_in.shape
    assert n == W_IN_DIM and d % W_PREP_ROWS == 0
    return pl.pallas_call(
        _prep_w_in_body,
        grid=(depth, d // W_PREP_ROWS),
        in_specs=[pl.BlockSpec((None, W_PREP_ROWS, W_IN_COLS), lambda l, i: (l, i, 0))],
        out_specs=pl.BlockSpec((None, W_PREP_ROWS, W_IN_COLS), lambda l, i: (l, i, 0)),
        out_shape=jax.ShapeDtypeStruct((depth, d, W_IN_COLS), bf16),
        compiler_params=_cparams("arbitrary", "arbitrary"),
        name="prep_w_in",
    )(w_in)


def _one_seq(ref, bi):
    return ref.at[pl.ds(bi, 1)]


def _interleave(chunks):
    live = list(chunks)
    while live:
        nxt = []
        for g in live:
            try:
                next(g)
                nxt.append(g)
            except StopIteration:
                pass
        live = nxt


def _mlstm_prompt_chunk(ml_ref, g_ref, gb_ref, nw_ref, out_ref, ct_s, nr_s, m_s):
    L = PROMPT_CHUNK
    row = _iota2((L, L), 0)
    col = _iota2((L, L), 1)
    causal = row <= col
    tri_b = _ones_where(causal)
    top = row < HEAD_W
    lo_half = col < HEAD_W
    bd_mask = _same_head(row, col)
    sub8 = _iota2((8, L), 0)
    lo8 = _iota2((8, L), 1) < HEAD_W
    n_mask = jnp.where(sub8 == 0, jnp.where(lo8, 1.0, 0.0), jnp.where(sub8 == 1, jnp.where(lo8, 0.0, 1.0), 0.0))

    g_t = (g_ref[0] + gb_ref[...]).T
    b_t = None
    parts = _split_bf16(jax.nn.log_sigmoid(g_t[0:GATE_ROWS, :]), 3)
    yield
    for part in parts:
        d = _dot(part, tri_b)
        b_t = d if b_t is None else b_t + d
        yield
    z_rows = b_t[ML_HEADS:ML_HEADS + 8, :] - g_t[0:8, :]
    z_nat = jnp.concatenate([z_rows, jnp.zeros((L - 8, L), f32)], axis=0).T
    m_old = m_s[...]
    m_next = m_old
    yield

    for p in range(ML_HEADS // 2):
        sl = slice(p * LANES, (p + 1) * LANES)
        q2 = ml_ref[0, :, sl]
        k2 = ml_ref[0, :, ML_W + p * LANES:ML_W + (p + 1) * LANES] * (HEAD_W ** -0.5)
        v2 = ml_ref[0, :, 2 * ML_W + p * LANES:2 * ML_W + (p + 1) * LANES]
        o2 = ml_ref[0, :, 3 * ML_W + p * LANES:3 * ML_W + (p + 1) * LANES]
        q_t = q2.T.astype(bf16)
        v_t = v2.T
        kb = k2.astype(bf16)
        k_sel = (jnp.where(lo_half, k2, 0.0).astype(bf16), jnp.where(lo_half, 0.0, k2).astype(bf16))
        sw, den, a_int, e_m, a_st, w_s = [], [], [], [], [], []
        yield
        for j in range(2):
            h = 2 * p + j
            b_row = b_t[ML_HEADS + h:ML_HEADS + h + 1, :]
            i_row = g_t[h:h + 1, :]
            m_prev = m_old[h:h + 1, :]
            dm = jnp.where(causal, b_row - z_nat[:, h:h + 1], NEG_INF)
            inter = b_row + m_prev
            raw = _dot(k_sel[j], q_t)
            yield
            m_t = jnp.maximum(inter, jnp.max(dm, axis=0, keepdims=True))
            yield
            s_w = raw * jnp.exp(dm - m_t)
            den.append(jnp.sum(s_w, axis=0, keepdims=True))
            sw.append(s_w.astype(bf16))
            yield
            a_int.append(jnp.exp(inter - m_t))
            e_m.append(jnp.exp(-m_t))
            m_new = jnp.broadcast_to(m_t[:, L - 1:L], (1, L))
            b_last = jnp.broadcast_to(b_row[:, L - 1:L], (1, L))
            a_st.append(jnp.exp(b_last + m_prev - m_new))
            w_s.append(jnp.exp(b_last - b_row + i_row - m_new))
            m_next = jnp.where(sub8 == h, m_new, m_next)
            yield
        ct = ct_s[p]
        nr = nr_s[p]
        qn = _dot(nr.astype(bf16), q_t)
        num0 = _dot(jnp.where(top, v_t, 0.0).astype(bf16), sw[0])
        yield
        num1 = _dot(jnp.where(top, 0.0, v_t).astype(bf16), sw[1])
        qc = _dot(ct.astype(bf16), q_t)
        yield
        num = num0 + num1 + jnp.where(top, a_int[0], a_int[1]) * qc
        den0 = den[0] + a_int[0] * qn[0:1, :]
        den1 = den[1] + a_int[1] * qn[1:2, :]
        hh = num / jnp.where(top, jnp.maximum(jnp.abs(den0), e_m[0]), jnp.maximum(jnp.abs(den1), e_m[1]))
        yield
        sq = hh * hh
        r0 = lax.rsqrt(jnp.sum(sq[0:HEAD_W, :], axis=0, keepdims=True) * (1.0 / HEAD_W) + EPS)
        r1 = lax.rsqrt(jnp.sum(sq[HEAD_W:LANES, :], axis=0, keepdims=True) * (1.0 / HEAD_W) + EPS)
        y = (hh * jnp.where(top, r0, r1)).T * nw_ref[:, sl] * jax.nn.sigmoid(o2)
        out_ref[0, :, sl] = y.astype(bf16)
        yield
        vw = (v_t * jnp.where(top, w_s[0], w_s[1])).astype(bf16)
        ct_s[p] = jnp.where(top, a_st[0], a_st[1]) * ct + jnp.where(bd_mask, _dot(vw, kb), 0.0)
        w8 = jnp.where(sub8 == 0, w_s[0], jnp.where(sub8 == 1, w_s[1], 0.0)).astype(bf16)
        nr_s[p] = jnp.where(sub8 == 0, a_st[0], a_st[1]) * nr + n_mask * _dot(w8, kb)
        yield
    m_s[...] = m_next


DT_LANE = 2 * ML_HEADS
SSD_GATE_ROW0 = DT_LANE // 8 * 8
SSD_HEAD_ROW = DT_LANE - SSD_GATE_ROW0
HALO = 8


def _ssd_prompt_chunk(ssd_ref, g_ref, gb_ref, al_ref, d_ref, nw_ref, out_ref, h_s):
    L = PROMPT_CHUNK
    xc = ssd_ref[0, :, SSD_W:SSD_COLS]
    nb = SSD_W
    b_b = [xc[:, nb + gi * LANES:nb + (gi + 1) * LANES].astype(bf16) for gi in range(2)]
    c_b = [xc[:, nb + (2 + gi) * LANES:nb + (3 + gi) * LANES].astype(bf16) for gi in range(2)]

    row = _iota2((L, L), 0)
    col = _iota2((L, L), 1)
    tril = col <= row
    lo_half = col < HEAD_W
    triu_b = _ones_where(row <= col)
    bd_b = _ones_where(_same_head(row, col))
    rowc = _iota2((L, 1), 0)

    g_t = (g_ref[0] + gb_ref[...]).T
    dt_t = jax.nn.softplus(g_t[SSD_GATE_ROW0:SSD_GATE_ROW0 + GATE_ROWS, :])
    yield
    cum_t = None
    for part in _split_bf16(dt_t * (-jnp.exp(al_ref[...])), 3):
        d = _dot(part, triu_b)
        cum_t = d if cum_t is None else cum_t + d
    yield
    nat = jnp.concatenate([cum_t, dt_t, cum_t * LOG2_E, jnp.zeros((L - 3 * GATE_ROWS, L), f32)], axis=0).T
    cb_g = [_dot_nt(c_b[gi], b_b[gi]) for gi in range(2)]
    yield

    for p in range(SSD_HEADS // 2):
        sl = slice(p * LANES, (p + 1) * LANES)
        x2 = xc[:, sl]
        z2 = ssd_ref[0, :, sl]
        hp = h_s[p]
        hb = hp.astype(bf16)
        grp = [(2 * p + j) // (SSD_HEADS // 2) for j in range(2)]
        ys = None
        ecum, w_s, a_last, ch = [], [], [], []
        for j in range(2):
            r = SSD_HEAD_ROW + 2 * p + j
            cum_col = nat[:, r:r + 1]
            cum_last = cum_col[L - 1:L, :]
            key_row = cum_t[r:r + 1, :] * LOG2_E - jnp.log2(dt_t[r:r + 1, :])
            lm_dt = jnp.exp2(jnp.where(tril, nat[:, 2 * GATE_ROWS + r:2 * GATE_ROWS + r + 1] - key_row, NEG_INF))
            m = (cb_g[grp[j]] * lm_dt).astype(bf16)
            yield
            xm = jnp.where(lo_half, x2, 0.0) if j == 0 else jnp.where(lo_half, 0.0, x2)
            d = _dot(m, xm.astype(bf16))
            ys = d if ys is None else ys + d
            ecum.append(jnp.exp(cum_col))
            w_s.append(jnp.exp(cum_last - cum_col) * nat[:, GATE_ROWS + r:GATE_ROWS + r + 1])
            a_last.append(jnp.exp(cum_last))
            if j == 0 or grp[1] != grp[0]:
                ch.append(_dot_nt(c_b[grp[j]], hb))
            else:
                ch.append(ch[0])
            yield
        ys = ys + jnp.where(lo_half, ecum[0] * ch[0], ecum[1] * ch[1])
        xw_t = (x2 * jnp.where(lo_half, w_s[0], w_s[1])).T.astype(bf16)
        yield
        if grp[0] == grp[1]:
            upd = _dot(xw_t, b_b[grp[0]])
        else:
            upd = jnp.where(row < HEAD_W, _dot(xw_t, b_b[grp[0]]), _dot(xw_t, b_b[grp[1]]))
        h_s[p] = jnp.where(rowc < HEAD_W, a_last[0], a_last[1]) * hp + upd
        yy = (ys + d_ref[:, sl] * x2) * _silu(z2)
        yield
        ms = _group_mean(yy * yy, bd_b)
        yield
        out_ref[0, :, sl] = (yy * lax.rsqrt(ms + EPS) * nw_ref[:, sl]).astype(bf16)


def _hg_lower_bound(lbp, layer):
    mx = jnp.max(lbp, axis=0, keepdims=True)
    e = jnp.exp(lbp - mx)
    den = jnp.sum(e, axis=0, keepdims=True)
    lb = jnp.zeros_like(den)
    for j in range(1, layer + 1):
        lb = lb + e[j:j + 1, :] / den
    return lb


def _hgrn_prompt_chunk(hg_ref, lbp_ref, nw_ref, out_ref, sbd_s, p_s, r_s, o_s, g_s, k_s, v_s, q_s, kt_s, a_s, *, layer):
    L = HG_CHUNK
    nblk = L // 8
    lb = _hg_lower_bound(lbp_ref[...], layer)
    ff = hg_ref[0, :, HG_W:2 * HG_W]
    sig = jax.nn.sigmoid(ff)
    log2_k = jnp.log2((1.0 - lb) * jax.nn.sigmoid(-ff))
    v_s[...] = hg_ref[0, :, 2 * HG_W:3 * HG_W]
    q_s[...] = hg_ref[0, :, 0:HG_W] * (HEAD_W ** -0.5)
    row = _iota2((L, L), 0)
    col = _iota2((L, L), 1)
    tri_b = _ones_where(col <= row)
    yield
    g2 = _cumsum_rows(jnp.log(lb + (1.0 - lb) * sig), tri_b) * LOG2_E
    g_s[...] = g2
    k_s[...] = g2 - log2_k
    yield

    r2 = _iota2((2 * LANES, 2 * LANES), 0)
    c2 = _iota2((2 * LANES, 2 * LANES), 1)
    bd4_b = _ones_where(_same_head(r2, c2))
    r1 = _iota2((LANES, LANES), 0)
    c1 = _iota2((LANES, LANES), 1)
    bd_mask = _same_head(r1, c1)
    bd_b = _ones_where(bd_mask)

    a_s[0:8, :] = jnp.zeros((8, HG_W), f32)
    for i in range(1, nblk):
        blk = slice(8 * i, 8 * i + 8)
        n = 8 * i
        r_i = g_s[n - 1:n, :]
        kt = jnp.exp2(r_i - k_s[0:n, :])
        first_head = jnp.bitwise_and(_iota2((n, HG_W), 1), LANES - 1) < HEAD_W
        kt_s[i - 1, 0:n, :] = jnp.where(first_head, kt, 0.0)
        kt_s[i - 1, n:L, :] = jnp.zeros((L - n, HG_W), f32)
        kt_s[i - 1, L:L + n, :] = jnp.where(first_head, 0.0, kt)
        kt_s[i - 1, L + n:2 * L, :] = jnp.zeros((L - n, HG_W), f32)
        qt = q_s[blk, :] * jnp.exp2(g_s[blk, :] - r_i)
        qt = jnp.concatenate([qt, jnp.zeros_like(qt)], axis=0).astype(bf16)
        for p in range(HG_HEADS // 2):
            sl = slice(p * LANES, (p + 1) * LANES)
            a_s[blk, sl] = _dot_nt(qt[:, sl], kt_s[i - 1, :, sl].astype(bf16))[0:8, :]
        yield

    sub = _iota2((8, HG_W), 0)
    for j in range(nblk):
        blk = slice(8 * j, 8 * j + 8)
        g_blk, gk_blk, q_blk = g_s[blk, :], k_s[blk, :], q_s[blk, :]
        for r in range(8):
            s = 8 * j + r
            p_s[8 * s:8 * s + 8, :] = q_blk * jnp.exp2(jnp.where(sub >= r, g_blk - gk_blk[r:r + 1, :], NEG_INF))
        yield
    r_s[...] = _dot(p_s[...].astype(bf16), bd4_b)
    yield

    gg = g_s[...]
    vv = v_s[...]
    qd = q_s[...] * jnp.exp2(gg)
    lo_half = _iota2((L, LANES), 1) < HEAD_W
    for p in range(HG_HEADS // 2):
        sl = slice(p * LANES, (p + 1) * LANES)
        v_cat = jnp.concatenate([jnp.where(lo_half, vv[:, sl], 0.0), jnp.where(lo_half, 0.0, vv[:, sl])], axis=0)
        o_s[:, sl] = (_dot(qd[:, sl].astype(bf16), sbd_s[p].astype(bf16))
                      + _dot(a_s[:, sl].astype(bf16), v_cat.astype(bf16)))
        yield
    for j in range(nblk):
        blk = slice(8 * j, 8 * j + 8)
        acc = o_s[blk, :]
        v_blk = v_s[blk, :]
        for r in range(8):
            s = 8 * j + r
            acc = acc + r_s[8 * s:8 * s + 8, :] * v_blk[r:r + 1, :]
        o_s[blk, :] = acc
        yield

    g_last = gg[L - 1:L, :]
    kd = jnp.exp2(g_last - k_s[...])
    zpad = jnp.zeros((LANES - L, LANES), f32)
    og = o_s[...]
    gate = hg_ref[0, :, 3 * HG_W:4 * HG_W]
    for p in range(HG_HEADS // 2):
        sl = slice(p * LANES, (p + 1) * LANES)
        kd_t = jnp.concatenate([kd[:, sl], zpad], axis=0).T.astype(bf16)
        v_pad = jnp.concatenate([vv[:, sl], zpad], axis=0).astype(bf16)
        g_t = jnp.concatenate([gg[:, sl], zpad], axis=0).T
        dec = jnp.exp2(g_t[:, L - 1:L])
        sbd_s[p] = dec * sbd_s[p] + jnp.where(bd_mask, _dot(kd_t, v_pad), 0.0)
        yield
        oo = og[:, sl]
        ms = _group_mean(oo * oo, bd_b)
        out_ref[0, :, sl] = (oo * lax.rsqrt(ms + EPS) * nw_ref[:, sl] * _silu(gate[:, sl])).astype(bf16)


def _mixers_prompt_body(ml_ref, ssd_ref, hg_ref, g_ref, gb_ref, al_ref, d_ref, mlnw_ref, ssnw_ref,
                        lbp_ref, hgnw_ref,
                        hm_ref, c1_ref, n1_ref, m1_ref, hs_ref, h1_ref, ho_ref, s1_ref,
                        ct_s, nr_s, m_s, h_s, sbd_s, p_s, r_s, o_s, gg_s, k_s, v_s, q_s, kt_s, a_s, *, layer):
    c = pl.program_id(1)
    seqs = range(ml_ref.shape[0])

    @pl.when(c == 0)
    def _():
        ct_s[...] = jnp.zeros(ct_s.shape, f32)
        nr_s[...] = jnp.zeros(nr_s.shape, f32)
        m_s[...] = jnp.zeros(m_s.shape, f32)
        h_s[...] = jnp.zeros(h_s.shape, f32)
        sbd_s[...] = jnp.zeros(sbd_s.shape, f32)

    def hgrn_chunks(bi):
        for half in range(PROMPT_CHUNK // HG_CHUNK):
            rows = pl.ds(half * HG_CHUNK, HG_CHUNK)
            yield from _hgrn_prompt_chunk(hg_ref.at[pl.ds(bi, 1), rows], lbp_ref, hgnw_ref, ho_ref.at[pl.ds(bi, 1), rows],
                                          sbd_s.at[bi], p_s.at[bi], r_s.at[bi], o_s.at[bi], gg_s.at[bi], k_s.at[bi],
                                          v_s.at[bi], q_s.at[bi], kt_s.at[bi], a_s.at[bi], layer=layer)

    chunks = []
    for bi in seqs:
        chunks.append(_ssd_prompt_chunk(_one_seq(ssd_ref, bi), _one_seq(g_ref, bi), gb_ref, al_ref, d_ref,
                                        ssnw_ref, _one_seq(hs_ref, bi), h_s.at[bi]))
        chunks.append(hgrn_chunks(bi))
        chunks.append(_mlstm_prompt_chunk(_one_seq(ml_ref, bi), _one_seq(g_ref, bi), gb_ref, mlnw_ref,
                                          _one_seq(hm_ref, bi), ct_s.at[bi], nr_s.at[bi], m_s.at[bi]))
    _interleave(chunks)

    @pl.when(c == pl.num_programs(1) - 1)
    def _():
        h1_ref[...] = h_s[...]
        for bi in seqs:
            for p in range(ML_HEADS // 2):
                c_pair = ct_s[bi, p].T
                c1_ref[bi, 2 * p] = c_pair[0:HEAD_W, 0:HEAD_W]
                c1_ref[bi, 2 * p + 1] = c_pair[HEAD_W:LANES, HEAD_W:LANES]
                n1_ref[bi, 2 * p:2 * p + 1, :] = nr_s[bi, p, 0:1, 0:HEAD_W]
                n1_ref[bi, 2 * p + 1:2 * p + 2, :] = nr_s[bi, p, 1:2, HEAD_W:LANES]
            m1_ref[bi] = m_s[bi]
            for p in range(HG_HEADS // 2):
                sbd = sbd_s[bi, p]
                s1_ref[bi, 2 * p] = sbd[0:HEAD_W, 0:HEAD_W]
                s1_ref[bi, 2 * p + 1] = sbd[HEAD_W:LANES, HEAD_W:LANES]


def _mixers_prompt(ml, ssd, hg, gates, gate_bias, alog_col, d_row, ml_nw, ssd_nw, lbp, hg_nw, layer):
    b, t, _ = ml.shape
    L = PROMPT_CHUNK
    ns = SEQS_PER_STEP
    assert b % ns == 0 and t % L == 0
    ml_pairs, ssd_pairs, hg_pairs = ML_HEADS // 2, SSD_HEADS // 2, HG_HEADS // 2
    tok = lambda w: pl.BlockSpec((ns, L, w), lambda i, c: (i, c, 0))
    per_seq = lambda *shape: pl.BlockSpec((ns,) + shape, lambda i, c: (i,) + (0,) * len(shape))
    hg_rows = pltpu.VMEM((ns, HG_CHUNK, HG_W), f32)
    hg_pairs_buf = pltpu.VMEM((ns, 8 * HG_CHUNK, HG_W), f32)
    return pl.pallas_call(
        functools.partial(_mixers_prompt_body, layer=layer),
        grid=(b // ns, t // L),
        in_specs=[tok(ML_COLS), tok(SSD_COLS), tok(HG_COLS), tok(GATE_W),
                  _layer_resident((1, GATE_W), layer), _layer_resident((GATE_ROWS, 1), layer),
                  _layer_resident((1, SSD_W), layer), _layer_resident((1, ML_W), layer),
                  _layer_resident((1, SSD_W), layer), _resident((DEPTH, HG_W)), _layer_resident((1, HG_W), layer)],
        out_specs=[tok(ML_W), per_seq(ML_HEADS, HEAD_W, HEAD_W), per_seq(ML_HEADS, HEAD_W), per_seq(8, LANES),
                   tok(SSD_W), per_seq(ssd_pairs, LANES, SSD_DSTATE),
                   tok(HG_W), per_seq(HG_HEADS, HEAD_W, HEAD_W)],
        out_shape=[jax.ShapeDtypeStruct((b, t, ML_W), bf16),
                   jax.ShapeDtypeStruct((b, ML_HEADS, HEAD_W, HEAD_W), f32),
                   jax.ShapeDtypeStruct((b, ML_HEADS, HEAD_W), f32),
                   jax.ShapeDtypeStruct((b, 8, LANES), f32),
                   jax.ShapeDtypeStruct((b, t, SSD_W), bf16),
                   jax.ShapeDtypeStruct((b, ssd_pairs, LANES, SSD_DSTATE), f32),
                   jax.ShapeDtypeStruct((b, t, HG_W), bf16),
                   jax.ShapeDtypeStruct((b, HG_HEADS, HEAD_W, HEAD_W), f32)],
        scratch_shapes=[pltpu.VMEM((ns, ml_pairs, LANES, LANES), f32), pltpu.VMEM((ns, ml_pairs, 8, LANES), f32),
                        pltpu.VMEM((ns, 8, LANES), f32),
                        pltpu.VMEM((ns, ssd_pairs, LANES, SSD_DSTATE), f32),
                        pltpu.VMEM((ns, hg_pairs, LANES, LANES), f32), hg_pairs_buf, hg_pairs_buf,
                        hg_rows, hg_rows, hg_rows, hg_rows, hg_rows,
                        pltpu.VMEM((ns, HG_CHUNK // 8 - 1, 2 * HG_CHUNK, HG_W), f32), hg_rows],
        compiler_params=_cparams("arbitrary", "arbitrary"),
        name="mixers_prompt",
    )(ml, ssd, hg, gates, gate_bias, alog_col, d_row, ml_nw, ssd_nw, lbp, hg_nw)


NB = 128
TS = 8


def _sample_call(body, name, grid, in_specs, operands, out_specs, out_shapes, scratch, n_state, n_fresh=0):
    first = len(operands) - n_state
    return pl.pallas_call(
        body, grid=grid, in_specs=in_specs, out_specs=out_specs, out_shape=out_shapes,
        scratch_shapes=scratch, input_output_aliases={first + i: 1 + i for i in range(n_fresh, n_state)},
        compiler_params=_cparams(*(["arbitrary"] * len(grid))), name=name,
    )(*operands)


def _mlstm_sample_body(q_ref, k_ref, v_ref, o_ref, g_ref, gb_ref, nw_ref, c0_ref, n0_ref, m0_ref,
                       out_ref, c1_ref, n1_ref, m1_ref, qt_s, kt_s, vt_s, gt_s, ht_s, *, layer, all_layers):
    p = pl.program_id(0)
    if all_layers:
        for l in range(DEPTH):
            if l != layer:
                c1_ref[l] = c0_ref[l]
        c0_ref, c1_ref = c0_ref.at[layer], c1_ref.at[layer]
    n_cur = [n0_ref[hl] for hl in range(2)]
    m_cur = [m0_ref[pl.ds(2 * p + hl, 1), :] for hl in range(2)]

    for t in range(TS):
        c_src = c0_ref if t == 0 else c1_ref
        rows = slice(t * NB, (t + 1) * NB)
        qt_s[...] = q_ref[rows, :].T
        kt_s[...] = k_ref[rows, :].T * (HEAD_W ** -0.5)
        vt_s[...] = v_ref[rows, :].T
        gt_s[...] = (g_ref[rows, :] + gb_ref[...]).T
        for hl in range(2):
            h = 2 * p + hl
            hs = slice(hl * HEAD_W, (hl + 1) * HEAD_W)
            i_t = gt_s[pl.ds(h, 1), :]
            lf = jax.nn.log_sigmoid(gt_s[pl.ds(ML_HEADS + h, 1), :])
            m_new = jnp.maximum(lf + m_cur[hl], i_t)
            a = jnp.exp(lf + m_cur[hl] - m_new)
            w = jnp.exp(i_t - m_new)
            n_new = a * n_cur[hl] + w * kt_s[hs, :]
            v_h = vt_s[hs, :]

            def body(d, num, hl=hl, a=a, w=w, v_h=v_h, c_src=c_src):
                kd = kt_s[pl.ds(hl * HEAD_W + d, 1), :] * w
                qd = qt_s[pl.ds(hl * HEAD_W + d, 1), :]
                cd = a * c_src[hl, d] + kd * v_h
                c1_ref[hl, d] = cd
                return num + qd * cd

            num = lax.fori_loop(0, HEAD_W, body, jnp.zeros((HEAD_W, NB), f32), unroll=4)
            den = jnp.sum(qt_s[hs, :] * n_new, axis=0, keepdims=True)
            hh = num / jnp.maximum(jnp.abs(den), jnp.exp(-m_new))
            ms = jnp.mean(hh * hh, axis=0, keepdims=True)
            ht_s[hs, :] = hh * lax.rsqrt(ms + EPS)
            n_cur[hl] = n_new
            m_cur[hl] = m_new
        out_ref[rows, :] = (ht_s[...].T * nw_ref[...] * jax.nn.sigmoid(o_ref[rows, :])).astype(bf16)

    for hl in range(2):
        n1_ref[hl] = n_cur[hl]
        m1_ref[pl.ds(2 * p + hl, 1), :] = m_cur[hl]


def _mlstm_sample(ml, gates, gate_bias, norm_w, c0t, n0t, m0t, layer):
    n = ml.shape[0]
    npair = ML_HEADS // 2
    blk = lambda off: pl.BlockSpec((n, LANES), lambda p, off=off: (0, off + p))
    all_layers = layer == 0
    if all_layers:
        c_spec = pl.BlockSpec((DEPTH, 2, HEAD_W, HEAD_W, NB), lambda p: (0, p, 0, 0, 0))
    else:
        c_spec = pl.BlockSpec((None, 2, HEAD_W, HEAD_W, NB), lambda p: (layer, p, 0, 0, 0))
    n_spec = pl.BlockSpec((None, 2, HEAD_W, NB), lambda p: (layer, p, 0, 0))
    m_spec = pl.BlockSpec((None, ML_HEADS, NB), lambda p: (layer, 0, 0))
    return _sample_call(
        functools.partial(_mlstm_sample_body, layer=layer, all_layers=all_layers), "mlstm_sample", (npair,),
        [blk(0), blk(npair), blk(2 * npair), blk(3 * npair), _resident((n, GATE_W)),
         _layer_resident((1, GATE_W), layer), pl.BlockSpec((None, 1, LANES), lambda p: (layer, 0, p)),
         c_spec, n_spec, m_spec],
        [ml, ml, ml, ml, gates, gate_bias, norm_w, c0t, n0t, m0t],
        [pl.BlockSpec((n, LANES), lambda p: (0, p)), c_spec, n_spec, m_spec],
        [jax.ShapeDtypeStruct((n, ML_W), bf16), jax.ShapeDtypeStruct(c0t.shape, f32),
         jax.ShapeDtypeStruct(n0t.shape, f32), jax.ShapeDtypeStruct(m0t.shape, f32)],
        [pltpu.VMEM((LANES, NB), f32)] * 5, 3, n_fresh=1 if all_layers else 0)


def _ssd_conv_sample_body(ssd_ref, halo_ref, cw_ref, cb_ref, out_ref, xp_s):
    nh = (SSD_CONV - 1) * NB
    n = TS * NB
    xp_s[0:nh, :] = halo_ref[...]
    xp_s[nh:nh + n, :] = ssd_ref[:, SSD_W:SSD_COLS]
    conv = cb_ref[...]
    for j in range(SSD_CONV):
        conv = conv + cw_ref[j:j + 1, :] * xp_s[j * NB:j * NB + n, :]
    out_ref[...] = _silu(conv)


def _ssd_conv_sample(ssd, halo, conv_w, conv_b, layer):
    n = ssd.shape[0]
    nh = (SSD_CONV - 1) * NB
    return pl.pallas_call(
        _ssd_conv_sample_body,
        grid=(1,),
        in_specs=[_resident((n, SSD_COLS)), _layer_resident((nh, SSD_CONV_DIM), layer),
                  _layer_resident((SSD_CONV, SSD_CONV_DIM), layer), _layer_resident((1, SSD_CONV_DIM), layer)],
        out_specs=pl.BlockSpec((n, SSD_CONV_DIM), lambda i: (0, 0)),
        out_shape=jax.ShapeDtypeStruct((n, SSD_CONV_DIM), f32),
        scratch_shapes=[pltpu.VMEM((nh + n, SSD_CONV_DIM), f32)],
        compiler_params=_cparams("arbitrary"),
        name="ssd_conv_sample",
    )(ssd, halo, conv_w, conv_b)


def _ssd_sample_body(z_ref, x_ref, b0_ref, b1_ref, c0_ref, c1_ref, g_ref, gb_ref, hp_ref, nw_ref, h0_ref,
                     out_ref, h1_ref, xt_s, bt_s, ct_s, gt_s, dd_s, yt_s):
    p = pl.program_id(0)
    b_refs = (b0_ref, b1_ref)
    c_refs = (c0_ref, c1_ref)

    for t in range(TS):
        rows = slice(t * NB, (t + 1) * NB)
        xt_s[t] = x_ref[rows, :].T
        gt_s[...] = (g_ref[rows, :] + gb_ref[...]).T
        for hl in range(2):
            bt_s[t, hl] = b_refs[hl][rows, :].T
            ct_s[t, hl] = c_refs[hl][rows, :].T
            dt = jax.nn.softplus(gt_s[pl.ds(DT_LANE + 2 * p + hl, 1), :])
            dd_s[t, 2 * hl:2 * hl + 1, :] = dt
            dd_s[t, 2 * hl + 1:2 * hl + 2, :] = jnp.exp(dt * (-jnp.exp(hp_ref[0, hl:hl + 1, :])))

    for hl in range(2):
        def body(pp, carry, hl=hl):
            row = pl.ds(hl * HEAD_W + pp, 1)
            h = h0_ref[hl, pp]
            for t in range(TS):
                xrow = xt_s[t, row, :] * dd_s[t, 2 * hl:2 * hl + 1, :]
                h = dd_s[t, 2 * hl + 1:2 * hl + 2, :] * h + xrow * bt_s[t, hl]
                yt_s[t, row, :] = jnp.sum(ct_s[t, hl] * h, axis=0, keepdims=True)
            h1_ref[hl, pp] = h
            return carry

        lax.fori_loop(0, HEAD_W, body, 0, unroll=4)

    for t in range(TS):
        rows = slice(t * NB, (t + 1) * NB)
        zt = z_ref[rows, :].T
        for hl in range(2):
            hs = slice(hl * HEAD_W, (hl + 1) * HEAD_W)
            yy = (yt_s[t, hs, :] + hp_ref[1, hl:hl + 1, :] * xt_s[t, hs, :]) * _silu(zt[hs, :])
            ms = jnp.mean(yy * yy, axis=0, keepdims=True)
            yt_s[t, hs, :] = yy * lax.rsqrt(ms + EPS)
        out_ref[rows, :] = (yt_s[t].T * nw_ref[...]).astype(bf16)


def _ssd_sample(ssd, xc, gates, gate_bias, head_params, norm_w, h0t, layer):
    n = ssd.shape[0]
    npair = SSD_HEADS // 2
    hpg = SSD_HEADS // 2
    nxb = SSD_W // LANES
    blk = lambda f: pl.BlockSpec((n, LANES), f)
    h_spec = pl.BlockSpec((None, 2, HEAD_W, SSD_DSTATE, NB), lambda p: (layer, p, 0, 0, 0))
    return _sample_call(
        _ssd_sample_body, "ssd_sample", (npair,),
        [blk(lambda p: (0, p)), blk(lambda p: (0, p)),
         blk(lambda p: (0, nxb + (2 * p) // hpg)), blk(lambda p: (0, nxb + (2 * p + 1) // hpg)),
         blk(lambda p: (0, nxb + 2 + (2 * p) // hpg)), blk(lambda p: (0, nxb + 2 + (2 * p + 1) // hpg)),
         _resident((n, GATE_W)), _layer_resident((1, GATE_W), layer),
         pl.BlockSpec((None, 2, None, 2, LANES), lambda p: (layer, 0, p, 0, 0)),
         pl.BlockSpec((None, 1, LANES), lambda p: (layer, 0, p)), h_spec],
        [ssd, xc, xc, xc, xc, xc, gates, gate_bias, head_params, norm_w, h0t],
        [pl.BlockSpec((n, LANES), lambda p: (0, p)), h_spec],
        [jax.ShapeDtypeStruct((n, SSD_W), bf16), jax.ShapeDtypeStruct(h0t.shape, f32)],
        [pltpu.VMEM((TS, LANES, NB), f32), pltpu.VMEM((TS, 2, SSD_DSTATE, NB), f32),
         pltpu.VMEM((TS, 2, SSD_DSTATE, NB), f32), pltpu.VMEM((LANES, NB), f32), pltpu.VMEM((TS, 8, NB), f32),
         pltpu.VMEM((TS, LANES, NB), f32)], 1)


def _hgrn_sample_body(q_ref, f_ref, i_ref, gate_ref, lbp_ref, nw_ref, s0_ref, out_ref, s1_ref,
                      qt_s, kt_s, ft_s, vt_s, ot_s, *, layer, all_layers):
    lb = _hg_lower_bound(lbp_ref[...], layer)
    if all_layers:
        for l in range(DEPTH):
            if l != layer:
                s1_ref[l] = s0_ref[l]
        s0_ref, s1_ref = s0_ref.at[layer], s1_ref.at[layer]

    for t in range(TS):
        s_src = s0_ref if t == 0 else s1_ref
        rows = slice(t * NB, (t + 1) * NB)
        ff = f_ref[rows, :]
        qt_s[...] = (q_ref[rows, :] * (HEAD_W ** -0.5)).T
        kt_s[...] = ((1.0 - lb) * jax.nn.sigmoid(-ff)).T
        ft_s[...] = (lb + (1.0 - lb) * jax.nn.sigmoid(ff)).T
        vt_s[...] = i_ref[rows, :].T
        for hl in range(2):
            hs = slice(hl * HEAD_W, (hl + 1) * HEAD_W)
            v_h = vt_s[hs, :]

            def body(k, o, hl=hl, v_h=v_h, s_src=s_src):
                r = pl.ds(hl * HEAD_W + k, 1)
                sk = ft_s[r, :] * s_src[hl, k] + kt_s[r, :] * v_h
                s1_ref[hl, k] = sk
                return o + qt_s[r, :] * sk

            o = lax.fori_loop(0, HEAD_W, body, jnp.zeros((HEAD_W, NB), f32), unroll=4)
            ms = jnp.mean(o * o, axis=0, keepdims=True)
            ot_s[hs, :] = o * lax.rsqrt(ms + EPS)
        out_ref[rows, :] = (ot_s[...].T * nw_ref[...] * _silu(gate_ref[rows, :])).astype(bf16)


def _hgrn_sample(hg, lbp, norm_w, s0t, layer):
    n = hg.shape[0]
    npair = HG_HEADS // 2
    blk = lambda off: pl.BlockSpec((n, LANES), lambda p, off=off: (0, off + p))
    all_layers = layer == 0
    if all_layers:
        s_spec = pl.BlockSpec((DEPTH, 2, HEAD_W, HEAD_W, NB), lambda p: (0, p, 0, 0, 0))
    else:
        s_spec = pl.BlockSpec((None, 2, HEAD_W, HEAD_W, NB), lambda p: (layer, p, 0, 0, 0))
    return _sample_call(
        functools.partial(_hgrn_sample_body, layer=layer, all_layers=all_layers), "hgrn_sample", (npair,),
        [blk(0), blk(npair), blk(2 * npair), blk(3 * npair),
         pl.BlockSpec((DEPTH, LANES), lambda p: (0, p)),
         pl.BlockSpec((None, 1, LANES), lambda p: (layer, 0, p)), s_spec],
        [hg, hg, hg, hg, lbp, norm_w, s0t],
        [pl.BlockSpec((n, LANES), lambda p: (0, p)), s_spec],
        [jax.ShapeDtypeStruct((n, HG_W), bf16), jax.ShapeDtypeStruct(s0t.shape, f32)],
        [pltpu.VMEM((LANES, NB), f32)] * 5, 1, n_fresh=1 if all_layers else 0)


def _ffn_body(x_ref, mml_ref, mssd_ref, mhg_ref, wout_ref, n2_ref, wug_ref, wuv_ref, cwg_ref, cwv_ref,
              cbg_ref, cbv_ref, wdn_ref, f0g_ref, f0v_ref, fn_ref, y_ref, f1_ref,
              x1_s, h2_s, acc_s, act_s, ubuf_s, *, tm, shift, jb, nff, final):
    t = pl.program_id(1)
    f = pl.program_id(2)
    halo = max(HALO, (FFN_CONV - 1) * shift)
    keep = (FFN_CONV - 1) * shift
    fw = jb * FF_BLOCK
    step0 = 0 if nff == 1 else pl.multiple_of(f * fw, LANES)

    def hist(half, start, width):
        return pl.ds(half * D_FF + step0 + start, width)

    def when(cond, fn):
        if nff == 1:
            fn()
        else:
            pl.when(cond)(fn)

    @pl.when(t == 0)
    def _():
        f1_ref[0, :, hist(0, 0, fw)] = f0g_ref[0]
        f1_ref[0, :, hist(1, 0, fw)] = f0v_ref[0]

    def first():
        x = x_ref[...]
        mix = jnp.concatenate([mml_ref[...], mssd_ref[...], mhg_ref[...]], axis=1)
        x1 = x + _dot(mix, wout_ref[...])
        x1_s[...] = x1
        h2_s[...] = (x1 * lax.rsqrt(jnp.mean(x1 * x1, axis=-1, keepdims=True) + EPS) * n2_ref[...]).astype(bf16)

    when(f == 0, first)
    h2 = h2_s[...]

    def conv(wu_ref, cw_ref, cb_ref, half, cols, ubuf):
        fcols = hist(half, cols.start, FF_BLOCK)
        u = _dot(h2, wu_ref[:, cols])
        ubuf[halo - keep:halo, :] = f1_ref[0, :, fcols]
        ubuf[halo:halo + tm, :] = u
        yv = (cb_ref[:, cols] + cw_ref[2:3, cols] * u
              + cw_ref[1:2, cols] * ubuf[halo - shift:halo - shift + tm, :]
              + cw_ref[0:1, cols] * ubuf[halo - 2 * shift:halo - 2 * shift + tm, :])
        f1_ref[0, :, fcols] = ubuf[halo + tm - keep:halo + tm, :]
        return yv

    nbuf = ubuf_s.shape[0]
    for j in range(jb):
        cols = slice(j * FF_BLOCK, (j + 1) * FF_BLOCK)
        gq = conv(wug_ref, cwg_ref, cbg_ref, 0, cols, ubuf_s.at[(2 * j) % nbuf])
        vq = conv(wuv_ref, cwv_ref, cbv_ref, 1, cols, ubuf_s.at[(2 * j + 1) % nbuf])
        act_s[:, cols] = (_silu(gq) * vq).astype(bf16)
    contrib = _dot(act_s[...], wdn_ref[...])

    def finish(down):
        x2 = x1_s[...] + down
        if final:
            x2 = x2 * lax.rsqrt(jnp.mean(x2 * x2, axis=-1, keepdims=True) + EPS) * fn_ref[...]
        y_ref[...] = x2

    if nff == 1:
        finish(contrib)
    else:
        @pl.when(f == 0)
        def _():
            acc_s[...] = contrib

        @pl.when(f != 0)
        def _():
            acc_s[...] = acc_s[...] + contrib

        pl.when(f == nff - 1)(lambda: finish(acc_s[...]))


def _ffn(x, mml, mssd, mhg, w_out, norm2, w_up, conv_w, conv_b, w_down, f0, final_w, *, layer, groups, tm, shift, jb,
         final, f0_base=0):
    n = x.shape[0]
    nt = n // (groups * tm)
    fw = jb * FF_BLOCK
    nff = D_FF // fw
    assert nt == 1 or nff == 1
    halo = max(HALO, (FFN_CONV - 1) * shift)
    keep = (FFN_CONV - 1) * shift
    rows = lambda w: pl.BlockSpec((tm, w), lambda g, t, f: (g * nt + t, 0))
    single = nff == 1
    wspec = lambda shape, imap: pl.BlockSpec(shape, imap, pipeline_mode=pl.Buffered(1)) if single else pl.BlockSpec(shape, imap)
    f1_spec = pl.BlockSpec((1, keep, 2 * D_FF), lambda g, t, f: (g, 0, 0))
    in_specs = [rows(D_MODEL), rows(ML_W), rows(SSD_W), rows(HG_W),
                _layer_resident((D_MODEL, D_MODEL), layer), _layer_resident((1, D_MODEL), layer),
                wspec((None, D_MODEL, fw), lambda g, t, f: (layer, 0, f)),
                wspec((None, D_MODEL, fw), lambda g, t, f: (layer, 0, nff + f)),
                wspec((None, FFN_CONV, fw), lambda g, t, f: (layer, 0, f)),
                wspec((None, FFN_CONV, fw), lambda g, t, f: (layer, 0, nff + f)),
                wspec((None, 1, fw), lambda g, t, f: (layer, 0, f)),
                wspec((None, 1, fw), lambda g, t, f: (layer, 0, nff + f)),
                wspec((None, fw, D_MODEL), lambda g, t, f: (layer, f, 0)),
                pl.BlockSpec((1, keep, fw), lambda g, t, f: (f0_base + g, 0, f)),
                pl.BlockSpec((1, keep, fw), lambda g, t, f: (f0_base + g, 0, nff + f)),
                _resident((1, D_MODEL))]
    operands = [x, mml, mssd, mhg, w_out, norm2, w_up, w_up, conv_w, conv_w, conv_b, conv_b, w_down, f0, f0, final_w]
    acc_rows = tm if nff > 1 else 8
    return pl.pallas_call(
        functools.partial(_ffn_body, tm=tm, shift=shift, jb=jb, nff=nff, final=final),
        grid=(groups, nt, nff),
        in_specs=in_specs,
        out_specs=[rows(D_MODEL), f1_spec],
        out_shape=[jax.ShapeDtypeStruct((n, D_MODEL), f32),
                   jax.ShapeDtypeStruct((groups, keep, 2 * D_FF), f32)],
        scratch_shapes=[pltpu.VMEM((tm, D_MODEL), f32), pltpu.VMEM((tm, D_MODEL), bf16),
                        pltpu.VMEM((acc_rows, D_MODEL), f32), pltpu.VMEM((tm, fw), bf16),
                        pltpu.VMEM((min(4, 2 * jb), halo + tm, FF_BLOCK), f32)],
        compiler_params=_cparams("arbitrary", "arbitrary", "arbitrary"),
        name="ffn",
    )(*operands)


def _gate_rows(*parts):
    r = jnp.concatenate([p.astype(f32) for p in parts], axis=1)
    return jnp.pad(r, ((0, 0), (0, GATE_W - r.shape[1])))[:, None, :]


def kernel(x_prompt, x_sample, state_mlstm_C, state_mlstm_n, state_mlstm_m, state_ssd, state_ssd_conv,
           state_hgrn, state_ffn_conv, norm1_w, w_in, ml_ig_b, ml_fg_b, ml_norm_w, ssd_conv_w, ssd_conv_b,
           ssd_dt_bias, ssd_A_log, ssd_D, ssd_norm_w, hg_lower_bounds, hg_norm_w, w_out, norm2_w, w_up,
           ffn_conv_w, ffn_conv_b, w_down, final_norm_w):
    bp, tp, _ = x_prompt.shape
    nb, ts, _ = x_sample.shape
    assert nb == NB and ts == TS and tp % ROW_TILE == 0
    depth = w_in.shape[0]
    assert depth == DEPTH
    xp = x_prompt.reshape(bp * tp, D_MODEL)
    xs = jnp.transpose(x_sample, (1, 0, 2)).reshape(ts * nb, D_MODEL)
    lbp = hg_lower_bounds.astype(f32)
    fin_w = final_norm_w[None, :]
    zero_f = jnp.zeros((bp, FFN_CONV - 1, 2 * D_FF), f32)

    w_cat = _prep_w_in(w_in)
    wo_b, wu_b, wd_b = w_out.astype(bf16), w_up.astype(bf16), w_down.astype(bf16)
    n1, n2 = norm1_w[:, None, :], norm2_w[:, None, :]
    gate_bias = _gate_rows(ml_ig_b, ml_fg_b, ssd_dt_bias)
    alog_col = jnp.pad(ssd_A_log.astype(f32), ((0, 0), (SSD_HEAD_ROW, GATE_ROWS - SSD_HEAD_ROW - SSD_HEADS)))[:, :, None]
    d_row = jnp.repeat(ssd_D.astype(f32), HEAD_W, axis=1)[:, None, :]
    mlw, ssw, hgw = ml_norm_w[:, None, :], ssd_norm_w[:, None, :], hg_norm_w[:, None, :]
    cw, cb = ssd_conv_w, ssd_conv_b[:, None, :]
    fcw, fcb = ffn_conv_w, ffn_conv_b[:, None, :]
    head_params = jnp.broadcast_to(
        jnp.stack([ssd_A_log, ssd_D], axis=1).astype(f32).reshape(depth, 2, SSD_HEADS // 2, 2, 1),
        (depth, 2, SSD_HEADS // 2, 2, LANES))

    c_t = jnp.transpose(state_mlstm_C.astype(f32), (0, 2, 3, 4, 1))
    n_t = jnp.transpose(state_mlstm_n.astype(f32), (0, 2, 3, 1))
    m_t = jnp.transpose(state_mlstm_m.astype(f32), (0, 2, 1))
    h_t = jnp.transpose(state_ssd.astype(f32), (0, 2, 3, 4, 1))
    s_t = jnp.transpose(state_hgrn.astype(f32), (0, 2, 3, 4, 1))
    conv_halo = jnp.transpose(state_ssd_conv.astype(f32), (0, 2, 1, 3)).reshape(depth, (SSD_CONV - 1) * nb, SSD_CONV_DIM)
    f_t = jnp.transpose(state_ffn_conv.astype(f32), (0, 2, 1, 3)).reshape(depth, (FFN_CONV - 1) * nb, 2 * D_FF)

    p_states, s_sconv, s_ffn = [], [], []
    for l in range(depth):
        last = l == depth - 1

        ml, ssd, hg, gates, p_tail = _inproj(xp, n1, w_cat, l, conv=(cw, cb, tp))
        hm, p_c, p_n, p_m, hs, p_h, ho, p_s = _mixers_prompt(
            ml.reshape(bp, tp, ML_COLS), ssd.reshape(bp, tp, SSD_COLS), hg.reshape(bp, tp, HG_COLS),
            gates.reshape(bp, tp, GATE_W), gate_bias, alog_col, d_row, mlw, ssw, lbp, hgw, l)
        xp, p_f = _ffn(xp, hm.reshape(bp * tp, ML_W), hs.reshape(bp * tp, SSD_W), ho.reshape(bp * tp, HG_W),
                       wo_b, n2, wu_b, fcw, fcb, wd_b, zero_f, fin_w, layer=l,
                       groups=bp, tm=ROW_TILE, shift=1, jb=D_FF // FF_BLOCK, final=last)
        p_states.append((p_c, p_n, p_m[:, 0:ML_HEADS, 0],
                         p_h.reshape(bp, SSD_HEADS, HEAD_W, SSD_DSTATE),
                         p_tail[:, HALO - (SSD_CONV - 1):, :], p_s, p_f))

        ml, ssd, hg, gates = _inproj(xs, n1, w_cat, l)
        hm, c_t, n_t, m_t = _mlstm_sample(ml, gates, gate_bias, mlw, c_t, n_t, m_t, l)
        xc = _ssd_conv_sample(ssd, conv_halo, cw, cb, l)
        hs, h_t = _ssd_sample(ssd, xc, gates, gate_bias, head_params, ssw, h_t, l)
        ho, s_t = _hgrn_sample(hg, lbp, hgw, s_t, l)
        xs, s_f = _ffn(xs, hm, hs, ho, wo_b, n2, wu_b, fcw, fcb, wd_b, f_t, fin_w, layer=l,
                       groups=1, tm=ts * nb, shift=nb, jb=1, final=last, f0_base=l)
        s_ffn.append(s_f.reshape(FFN_CONV - 1, nb, 2 * D_FF))
        raw_xbc = ssd[:, SSD_W:].reshape(ts, nb, SSD_CONV_DIM)
        s_sconv.append(jnp.transpose(raw_xbc[ts - (SSD_CONV - 1):], (1, 0, 2)))

    y_prompt = xp.reshape(bp, tp, D_MODEL)
    y_sample = jnp.transpose(xs.reshape(ts, nb, D_MODEL), (1, 0, 2))
    p_out = [jnp.stack([st[i] for st in p_states]) for i in range(7)]
    s_out = [jnp.transpose(c_t, (0, 4, 1, 2, 3)), jnp.transpose(n_t, (0, 3, 1, 2)), jnp.transpose(m_t, (0, 2, 1)),
             jnp.transpose(h_t, (0, 4, 1, 2, 3)), jnp.stack(s_sconv),
             jnp.transpose(s_t, (0, 4, 1, 2, 3)), jnp.transpose(jnp.stack(s_ffn), (0, 2, 1, 3))]
    return (y_prompt, y_sample, *p_out, *s_out)
```
